```python
import math
import jax
import jax.numpy as jnp
from jax import lax
import numpy as np


D_MODEL = 2048
BATCH = 2
SEQ = 4096
DEPTH = 2
DEC_BATCH = 1
DEC_SEQ = 16384
PAST_LEN = 128

HEAD_DIM = 128
A_HEADS = 8
DILATED_CFGS = ((128, 1), (512, 4), (2048, 16))
SWA_BLOCK = 64
B_GROUPS = 8
B_GROUP_DIM = 128
CHUNK = 128
C_HEADS = 12
C_HALF = 64
DIFF_Q_BLOCK = 128
D_GROUPS = 4
D_GROUP_DIM = 128
A_W = A_HEADS * HEAD_DIM
B_W = B_GROUPS * B_GROUP_DIM
C_QK_W = C_HEADS * 2 * C_HALF
C_V_W = C_HEADS * 2 * C_HALF
D_W = D_GROUPS * D_GROUP_DIM
MIX_IN = 3 * A_W + 2 * B_W
MIX_OUT = A_W + B_W
ROPE_THETA = 500000.0
ROPE_FRACTION = 4
MEM_LEN = 256
CA_HEADS = 4
CA_HEAD_DIM = 128
CA_W = CA_HEADS * CA_HEAD_DIM
D_FF = 5632
CONV_W = 3
EPS = 1e-6
N_EVEN = (DEPTH + 1) // 2
N_ODD = DEPTH // 2

kernel_name = 'hybrid_dilated_gmlp_diffattn_fnet_encoder'


def _rmsnorm(x, g):
    xf = x.astype(jnp.float32)
    y = xf * lax.rsqrt(jnp.mean(xf * xf, axis=-1, keepdims=True) + EPS)
    return (y * g.astype(jnp.float32)).astype(x.dtype)


def _rope_partial(x, pos):
    rd = x.shape[-1] // ROPE_FRACTION
    half = rd // 2
    inv = ROPE_THETA ** (-jnp.arange(half, dtype=jnp.float32) / half)
    ang = pos.astype(jnp.float32)[:, None] * inv[None, :]
    cos = jnp.cos(ang)[:, None, :]
    sin = jnp.sin(ang)[:, None, :]
    x1 = x[..., :half].astype(jnp.float32)
    x2 = x[..., half:rd].astype(jnp.float32)
    rot = jnp.concatenate([x1 * cos - x2 * sin, x2 * cos + x1 * sin], axis=-1).astype(x.dtype)
    return jnp.concatenate([rot, x[..., rd:]], axis=-1)


def _dilated_window_attention(q, k, v):
    Bn, S, H, dh = q.shape
    scale = dh ** -0.5
    ms, nums, dens = [], [], []
    for window, dil in DILATED_CFGS:
        radius = window // (2 * dil)
        L = S // dil
        nblk = -(-L // SWA_BLOCK)
        Lp = nblk * SWA_BLOCK

        def to_sub(t):
            t = t.reshape(Bn, L, dil, H, dh)
            t = jnp.pad(t, ((0, 0), (0, Lp - L), (0, 0), (0, 0), (0, 0)))
            return t.reshape(Bn, nblk, SWA_BLOCK, dil, H, dh)

        def neigh(t):
            z = jnp.zeros_like(t[:, :1])
            prev = jnp.concatenate([z, t[:, :-1]], axis=1)
            nxt = jnp.concatenate([t[:, 1:], z], axis=1)
            return jnp.concatenate([prev, t, nxt], axis=2)

        qs = to_sub(q)
        kn = neigh(to_sub(k))
        vn = neigh(to_sub(v))
        qpos = jnp.arange(SWA_BLOCK)
        kpos = jnp.arange(3 * SWA_BLOCK) - SWA_BLOCK
        kabs = (jnp.arange(nblk) * SWA_BLOCK)[:, None] + kpos[None, :]
        band = jnp.abs(kpos[None, :] - qpos[:, None]) <= radius
        mask = band[None] & ((kabs >= 0) & (kabs < L))[:, None, :]
        s = jnp.einsum('bnqrhd,bnkrhd->bnrhqk', qs, kn,
                       preferred_element_type=jnp.float32) * scale
        s = jnp.where(mask[None, :, None, None], s, -jnp.inf)
        m = jnp.max(s, axis=-1)
        p = jnp.exp(s - m[..., None])
        l = jnp.sum(p, axis=-1)
        num = jnp.einsum('bnrhqk,bnkrhd->bnqrhd', p, vn.astype(jnp.float32))
        num = num.reshape(Bn, Lp, dil, H, dh)[:, :L].reshape(Bn, S, H, dh)
        m = m.transpose(0, 1, 4, 2, 3).reshape(Bn, Lp, dil, H)[:, :L].reshape(Bn, S, H)
        l = l.transpose(0, 1, 4, 2, 3).reshape(Bn, Lp, dil, H)[:, :L].reshape(Bn, S, H)
        ms.append(m)
        nums.append(num)
        dens.append(l)
    m_all = jnp.stack(ms)
    w = jnp.exp(m_all - jnp.max(m_all, axis=0, keepdims=True))
    den = jnp.sum(w * jnp.stack(dens), axis=0)
    num = jnp.sum(w[..., None] * jnp.stack(nums), axis=0)
    return (num / den[..., None]).astype(q.dtype)


def _chunked_spatial_gating(u, v, w_s, b_s, g_ln):
    Bn, S, G, c = v.shape
    vf = v.astype(jnp.float32)
    mu = jnp.mean(vf, axis=-1, keepdims=True)
    var = jnp.mean(jnp.square(vf - mu), axis=-1, keepdims=True)
    vn = (vf - mu) * lax.rsqrt(var + EPS) * g_ln.astype(jnp.float32)
    vc = vn.reshape(Bn, S // CHUNK, CHUNK, G, c)
    sv = jnp.einsum('gij,bnjgc->bnigc', w_s.astype(jnp.float32), vc) \
        + b_s.astype(jnp.float32).T[None, None, :, :, None]
    return (u.astype(jnp.float32) * sv.reshape(Bn, S, G, c)).astype(u.dtype)


def _diff_attention(q, k, v, lam, subln_g, layer_idx):
    Bn, S, H, _, dc = q.shape
    scale = dc ** -0.5
    lam_init = 0.8 - 0.6 * math.exp(-0.3 * layer_idx)
    lamf = lam.astype(jnp.float32)
    lam_full = jnp.exp(jnp.sum(lamf[0] * lamf[1])) - jnp.exp(jnp.sum(lamf[2] * lamf[3])) + lam_init
    nqb = S // DIFF_Q_BLOCK
    qb = q.reshape(Bn, nqb, DIFF_Q_BLOCK, H, 2, dc).transpose(1, 0, 2, 3, 4, 5)
    vf = v.astype(jnp.float32)

    def one_block(qblk):
        s = jnp.einsum('bqhtd,bkhtd->bthqk', qblk, k,
                       preferred_element_type=jnp.float32) * scale
        p = jax.nn.softmax(s, axis=-1)
        a = p[:, 0] - lam_full * p[:, 1]
        return jnp.einsum('bhqk,bkhd->bqhd', a, vf)

    o = lax.map(one_block, qb)
    o = o.transpose(1, 0, 2, 3, 4).reshape(Bn, S, H, 2 * dc)
    o = _rmsnorm(o, subln_g) * (1.0 - lam_init)
    return o.astype(v.dtype)


def _fourier_mix(z, w_f):
    f = jnp.fft.fft2(z.astype(jnp.float32), axes=(1, 3), norm='ortho').real
    return jnp.einsum('bsgc,gce->bsge', f, w_f.astype(jnp.float32)).astype(z.dtype)


def _even_mixer(proj, pos, w_s, b_s, g_ln):
    Bn, S, _ = proj.shape
    qa = _rope_partial(proj[..., 0:A_W].reshape(Bn, S, A_HEADS, HEAD_DIM), pos)
    ka = _rope_partial(proj[..., A_W:2 * A_W].reshape(Bn, S, A_HEADS, HEAD_DIM), pos)
    va = proj[..., 2 * A_W:3 * A_W].reshape(Bn, S, A_HEADS, HEAD_DIM)
    ub = jax.nn.gelu(proj[..., 3 * A_W:3 * A_W + B_W]).reshape(Bn, S, B_GROUPS, B_GROUP_DIM)
    vb = jax.nn.gelu(proj[..., 3 * A_W + B_W:]).reshape(Bn, S, B_GROUPS, B_GROUP_DIM)
    ya = _dilated_window_attention(qa, ka, va).reshape(Bn, S, A_W)
    yb = _chunked_spatial_gating(ub, vb, w_s, b_s, g_ln).reshape(Bn, S, B_W)
    return jnp.concatenate([ya, yb], axis=-1)


def _odd_mixer(proj, pos, lam, subln_g, w_f, layer_idx):
    Bn, S, _ = proj.shape
    qc = proj[..., 0:C_QK_W].reshape(Bn, S, C_HEADS * 2, C_HALF)
    kc = proj[..., C_QK_W:2 * C_QK_W].reshape(Bn, S, C_HEADS * 2, C_HALF)
    qc = _rope_partial(qc, pos).reshape(Bn, S, C_HEADS, 2, C_HALF)
    kc = _rope_partial(kc, pos).reshape(Bn, S, C_HEADS, 2, C_HALF)
    vc = proj[..., 2 * C_QK_W:2 * C_QK_W + C_V_W].reshape(Bn, S, C_HEADS, 2 * C_HALF)
    zd = proj[..., 2 * C_QK_W + C_V_W:].reshape(Bn, S, D_GROUPS, D_GROUP_DIM)
    yc = _diff_attention(qc, kc, vc, lam, subln_g, layer_idx).reshape(Bn, S, C_V_W)
    yd = _fourier_mix(zd, w_f).reshape(Bn, S, D_W)
    return jnp.concatenate([yc, yd], axis=-1)


def _memory_cross_attention(h, mem, g_mem, wq, wkv, wo):
    Bn, S, _ = h.shape
    M = mem.shape[1]
    m = _rmsnorm(mem, g_mem)
    q = (h @ wq).reshape(Bn, S, CA_HEADS, CA_HEAD_DIM)
    kv = (m @ wkv).reshape(Bn, M, 2, CA_HEADS, CA_HEAD_DIM)
    s = jnp.einsum('bqhd,bkhd->bhqk', q, kv[:, :, 0],
                   preferred_element_type=jnp.float32) * (CA_HEAD_DIM ** -0.5)
    p = jax.nn.softmax(s, axis=-1)
    o = jnp.einsum('bhqk,bkhd->bqhd', p, kv[:, :, 1].astype(jnp.float32)).astype(h.dtype)
    return o.reshape(Bn, S, CA_W) @ wo


def _conv_gated_ffn(h, w_up, conv_w, conv_b, w_down):
    S = h.shape[1]
    z = h @ w_up
    pad = CONV_W // 2
    zp = jnp.pad(z, ((0, 0), (pad, pad), (0, 0)))
    z = sum(zp[:, j:j + S] * conv_w[j] for j in range(CONV_W)) + conv_b
    gate = z[..., :D_FF]
    val = z[..., D_FF:]
    return (jax.nn.gelu(gate) * val) @ w_down


def _trunk(x, mem, norm_gains, w_in, w_out, b_w_spatial, b_b_spatial, b_ln_gain,
           c_lambda, c_subln_gain, d_w_fourier, mem_norm_gain, ca_w_q, ca_w_kv, ca_w_o,
           ffn_w_up, ffn_conv_w, ffn_conv_b, ffn_w_down):
    S = x.shape[1]
    pos = jnp.arange(S)
    for layer in range(DEPTH):
        g = norm_gains[layer]
        proj = _rmsnorm(x, g[0]) @ w_in[layer]
        if layer % 2 == 0:
            e = layer // 2
            mix = _even_mixer(proj, pos, b_w_spatial[e], b_b_spatial[e], b_ln_gain[e])
        else:
            o = layer // 2
            mix = _odd_mixer(proj, pos, c_lambda[o], c_subln_gain[o], d_w_fourier[o], layer)
        x = x + _rmsnorm(mix @ w_out[layer], g[1])
        ca = _memory_cross_attention(_rmsnorm(x, g[2]), mem, mem_norm_gain[layer],
                                     ca_w_q[layer], ca_w_kv[layer], ca_w_o[layer])
        x = x + _rmsnorm(ca, g[3])
        ff = _conv_gated_ffn(_rmsnorm(x, g[4]), ffn_w_up[layer], ffn_conv_w[layer],
                             ffn_conv_b[layer], ffn_w_down[layer])
        x = x + _rmsnorm(ff, g[5])
    return x


def setup_inputs(seed: int = 0) -> dict:
    key = jax.random.key(seed)
    ks = jax.random.split(key, 24)
    f32 = jnp.float32

    def nrm(k, shape, scale):
        return jax.random.normal(k, shape, f32) * scale

    return {
        'x_prompt': nrm(ks[0], (BATCH, SEQ, D_MODEL), 1.0),
        'x_sample': nrm(ks[1], (DEC_BATCH, DEC_SEQ, D_MODEL), 1.0),
        'mem_prompt': nrm(ks[2], (BATCH, MEM_LEN, D_MODEL), 1.0),
        'mem_sample': nrm(ks[3], (DEC_BATCH, MEM_LEN, D_MODEL), 1.0),
        'norm_gains': 1.0 + nrm(ks[4], (DEPTH, 6, D_MODEL), 0.02),
        'w_in': nrm(ks[5], (DEPTH, D_MODEL, MIX_IN), D_MODEL ** -0.5),
        'w_out': nrm(ks[6], (DEPTH, MIX_OUT, D_MODEL), MIX_OUT ** -0.5),
        'b_w_spatial': nrm(ks[7], (N_EVEN, B_GROUPS, CHUNK, CHUNK), CHUNK ** -0.5),
        'b_b_spatial': 1.0 + nrm(ks[8], (N_EVEN, B_GROUPS, CHUNK), 0.02),
        'b_ln_gain': 1.0 + nrm(ks[9], (N_EVEN, B_GROUPS, B_GROUP_DIM), 0.02),
        'c_lambda': nrm(ks[10], (N_ODD, 4, C_HALF), 0.1),
        'c_subln_gain': 1.0 + nrm(ks[11], (N_ODD, 2 * C_HALF), 0.02),
        'd_w_fourier': nrm(ks[12], (N_ODD, D_GROUPS, D_GROUP_DIM, D_GROUP_DIM), D_GROUP_DIM ** -0.5),
        'mem_norm_gain': 1.0 + nrm(ks[13], (DEPTH, D_MODEL), 0.02),
        'ca_w_q': nrm(ks[14], (DEPTH, D_MODEL, CA_W), D_MODEL ** -0.5),
        'ca_w_kv': nrm(ks[15], (DEPTH, D_MODEL, 2 * CA_W), D_MODEL ** -0.5),
        'ca_w_o': nrm(ks[16], (DEPTH, CA_W, D_MODEL), CA_W ** -0.5),
        'ffn_w_up': nrm(ks[17], (DEPTH, D_MODEL, 2 * D_FF), D_MODEL ** -0.5),
        'ffn_conv_w': nrm(ks[18], (DEPTH, CONV_W, 2 * D_FF), CONV_W ** -0.5),
        'ffn_conv_b': nrm(ks[19], (DEPTH, 2 * D_FF), 0.02),
        'ffn_w_down': nrm(ks[20], (DEPTH, D_FF, D_MODEL), D_FF ** -0.5),
    }


def reference(x_prompt, x_sample, mem_prompt, mem_sample, norm_gains, w_in, w_out,
              b_w_spatial, b_b_spatial, b_ln_gain, c_lambda, c_subln_gain, d_w_fourier,
              mem_norm_gain, ca_w_q, ca_w_kv, ca_w_o, ffn_w_up, ffn_conv_w, ffn_conv_b,
              ffn_w_down):
    y_prompt = _trunk(x_prompt, mem_prompt, norm_gains, w_in, w_out, b_w_spatial, b_b_spatial,
                      b_ln_gain, c_lambda, c_subln_gain, d_w_fourier, mem_norm_gain, ca_w_q,
                      ca_w_kv, ca_w_o, ffn_w_up, ffn_conv_w, ffn_conv_b, ffn_w_down)
    y_sample = _trunk(x_sample, mem_sample, norm_gains, w_in, w_out, b_w_spatial, b_b_spatial,
                      b_ln_gain, c_lambda, c_subln_gain, d_w_fourier, mem_norm_gain, ca_w_q,
                      ca_w_kv, ca_w_o, ffn_w_up, ffn_conv_w, ffn_conv_b, ffn_w_down)
    return (y_prompt, y_sample)
```

```python
import functools
import math

import jax
import jax.numpy as jnp
from jax import lax
from jax.experimental import pallas as pl
from jax.experimental.pallas import tpu as pltpu

F32 = jnp.float32
BF16 = jnp.bfloat16

EPS = 1e-6
ROPE_THETA = 500000.0
ROPE_FRACTION = 4
LANES = 128
HEAD_DIM = 128
A_HEADS = 8
DILATED_CFGS = ((128, 1), (512, 4), (2048, 16))
BAND_RADIUS = 64
B_GROUPS = 8
CHUNK = 128
C_HEADS = 12
C_HALF = 64
D_GROUPS = 4
CA_HEADS = 4
CONV_W = 3
NEG_BIG = -1e30
MIB = 1024 * 1024


def _params(semantics, vmem_mib):
    return pltpu.CompilerParams(dimension_semantics=semantics, vmem_limit_bytes=vmem_mib * MIB)


def _rms(x, g):
    return x * lax.rsqrt(jnp.mean(x * x, axis=-1, keepdims=True) + EPS) * g


def _dot_nt(a, b):
    return lax.dot_general(a, b, (((1,), (1,)), ((), ())), preferred_element_type=F32)


def _norm_mm_kernel(x_ref, g_ref, w_ref, o_ref, xn_ref):
    @pl.when(pl.program_id(1) == 0)
    def _():
        xn_ref[...] = _rms(x_ref[...], g_ref[...]).astype(BF16)

    o_ref[...] = jnp.dot(xn_ref[...], w_ref[...], preferred_element_type=F32).astype(o_ref.dtype)


def _norm_matmul(x, g, w, out_dtype, tm=512, tn=512):
    t, k = x.shape
    n = w.shape[1]
    tm = min(tm, t)
    return pl.pallas_call(
        _norm_mm_kernel,
        grid=(t // tm, n // tn),
        in_specs=[pl.BlockSpec((tm, k), lambda i, j: (i, 0)),
                  pl.BlockSpec((1, k), lambda i, j: (0, 0)),
                  pl.BlockSpec((k, tn), lambda i, j: (0, j))],
        out_specs=pl.BlockSpec((tm, tn), lambda i, j: (i, j)),
        out_shape=jax.ShapeDtypeStruct((t, n), out_dtype),
        scratch_shapes=[pltpu.VMEM((tm, k), BF16)],
        compiler_params=_params(("parallel", "arbitrary"), 40),
        name="norm_matmul",
    )(x, g.reshape(1, k), w)


def _mm_norm_res_kernel(a_ref, w_ref, g_ref, r_ref, o_ref):
    y = jnp.dot(a_ref[...], w_ref[...], preferred_element_type=F32)
    o_ref[...] = r_ref[...] + _rms(y, g_ref[...])


def _matmul_norm_res(a, w, g, res, tm=512):
    t, k = a.shape
    n = w.shape[1]
    return pl.pallas_call(
        _mm_norm_res_kernel,
        grid=(t // tm,),
        in_specs=[pl.BlockSpec((tm, k), lambda i: (i, 0)),
                  pl.BlockSpec((k, n), lambda i: (0, 0)),
                  pl.BlockSpec((1, n), lambda i: (0, 0)),
                  pl.BlockSpec((tm, n), lambda i: (i, 0))],
        out_specs=pl.BlockSpec((tm, n), lambda i: (i, 0)),
        out_shape=jax.ShapeDtypeStruct((t, n), F32),
        compiler_params=_params(("parallel",), 48),
        name="matmul_norm_res",
    )(a, w, g.reshape(1, n), res)


def _rope_tables(seq, head_w):
    rd = head_w // ROPE_FRACTION
    half = rd // 2
    inv = ROPE_THETA ** (-jnp.arange(half, dtype=F32) / half)
    ang = jnp.arange(seq, dtype=F32)[:, None] * inv[None, :]
    cos = jnp.cos(ang)
    sin = jnp.sin(ang)
    rest = head_w - rd
    c_head = jnp.concatenate([cos, cos, jnp.ones((seq, rest), F32)], axis=-1)
    s_head = jnp.concatenate([-sin, sin, jnp.zeros((seq, rest), F32)], axis=-1)
    reps = LANES // head_w
    return jnp.tile(c_head, (1, reps)), jnp.tile(s_head, (1, reps))


def _qkv_prep_kernel(x_ref, c_ref, s_ref, o_ref, *, head_w, q_scale, n_blocks):
    part = pl.program_id(1)
    half = head_w // ROPE_FRACTION // 2

    def rope(scale):
        c = c_ref[...]
        s = s_ref[...]
        lane = lax.broadcasted_iota(jnp.int32, c.shape, 1) & (head_w - 1)
        first = lane < half
        for b in range(n_blocks):
            x = x_ref[:, b * LANES:(b + 1) * LANES]
            partner = jnp.where(first, pltpu.roll(x, LANES - half, axis=1), pltpu.roll(x, half, axis=1))
            y = x * c + partner * s
            if scale != 1.0:
                y = y * scale
            o_ref[:, b * LANES:(b + 1) * LANES] = y.astype(BF16)

    @pl.when(part == 0)
    def _():
        rope(q_scale)

    @pl.when(part == 1)
    def _():
        rope(1.0)

    @pl.when(part == 2)
    def _():
        o_ref[...] = x_ref[...].astype(BF16)


def _qkv_prep(proj, seq, part_w, head_w, q_scale, tm=256):
    t = proj.shape[0]
    cos_t, sin_t = _rope_tables(seq, head_w)
    spb = seq // tm
    kern = functools.partial(_qkv_prep_kernel, head_w=head_w, q_scale=q_scale, n_blocks=part_w // LANES)
    return pl.pallas_call(
        kern,
        grid=(t // tm, 3),
        in_specs=[pl.BlockSpec((tm, part_w), lambda i, p: (i, p)),
                  pl.BlockSpec((tm, LANES), lambda i, p: (i % spb, 0)),
                  pl.BlockSpec((tm, LANES), lambda i, p: (i % spb, 0))],
        out_specs=pl.BlockSpec((tm, part_w), lambda i, p: (i, p)),
        out_shape=jax.ShapeDtypeStruct((t, 3 * part_w), BF16),
        compiler_params=_params(("parallel", "arbitrary"), 32),
        name="qkv_prep",
    )(proj, cos_t, sin_t)


def _dil_attn_kernel(q_ref, k_ref, v_ref, o_ref, lse_ref, *, bq, sub_len):
    qi = pl.program_id(2)
    win = bq + 2 * BAND_RADIUS
    q0 = qi * bq
    ws = pl.multiple_of(jnp.clip(q0 - BAND_RADIUS, 0, sub_len - win), BAND_RADIUS)
    q = q_ref[0]
    kw = k_ref[0, pl.ds(ws, win), :]
    vw = v_ref[0, pl.ds(ws, win), :]
    s = _dot_nt(q, kw)
    qpos = q0 + lax.broadcasted_iota(jnp.int32, (bq, win), 0)
    kpos = ws + lax.broadcasted_iota(jnp.int32, (bq, win), 1)
    rel = kpos - qpos
    s = jnp.where((rel <= BAND_RADIUS) & (rel >= -BAND_RADIUS), s, NEG_BIG)
    m = jnp.max(s, axis=-1, keepdims=True)
    p = jnp.exp(s - m)
    l = jnp.sum(p, axis=-1, keepdims=True)
    o = jnp.dot(p.astype(BF16), vw, preferred_element_type=F32)
    o_ref[0] = o / l
    lse_ref[0] = jnp.broadcast_to(m + jnp.log(l), (bq, LANES))


def _dilated_attention_one(qkv, batch, seq, dil, bq=128):
    a_w = A_HEADS * HEAD_DIM
    sub_len = seq // dil
    x = qkv.reshape(batch, sub_len, dil * 3 * a_w)
    nb = 3 * A_HEADS
    kern = functools.partial(_dil_attn_kernel, bq=bq, sub_len=sub_len)
    o, lse = pl.pallas_call(
        kern,
        grid=(batch, dil * A_HEADS, sub_len // bq),
        in_specs=[
            pl.BlockSpec((1, bq, LANES), lambda b, c, i: (b, i, (c // A_HEADS) * nb + c % A_HEADS)),
            pl.BlockSpec((1, sub_len, LANES),
                         lambda b, c, i: (b, 0, (c // A_HEADS) * nb + A_HEADS + c % A_HEADS)),
            pl.BlockSpec((1, sub_len, LANES),
                         lambda b, c, i: (b, 0, (c // A_HEADS) * nb + 2 * A_HEADS + c % A_HEADS)),
        ],
        out_specs=[pl.BlockSpec((1, bq, LANES), lambda b, c, i: (b, i, c)),
                   pl.BlockSpec((1, bq, LANES), lambda b, c, i: (b, i, c))],
        out_shape=[jax.ShapeDtypeStruct((batch, sub_len, dil * a_w), F32),
                   jax.ShapeDtypeStruct((batch, sub_len, dil * a_w), F32)],
        compiler_params=_params(("parallel", "parallel", "arbitrary"), 40),
        name=f"dilated_attn_d{dil}",
    )(x, x, x)
    return o.reshape(batch * seq, a_w), lse.reshape(batch * seq, a_w)


def _combine_kernel(o0, l0, o1, l1, o2, l2, y_ref):
    a, b, c = l0[...], l1[...], l2[...]
    m = jnp.maximum(jnp.maximum(a, b), c)
    wa, wb, wc = jnp.exp(a - m), jnp.exp(b - m), jnp.exp(c - m)
    num = wa * o0[...] + wb * o1[...] + wc * o2[...]
    y_ref[...] = (num / (wa + wb + wc)).astype(y_ref.dtype)


def _combine_configs(parts, tm=512):
    t, w = parts[0][0].shape
    flat = [a for pair in parts for a in pair]
    spec = pl.BlockSpec((tm, w), lambda i: (i, 0))
    return pl.pallas_call(
        _combine_kernel,
        grid=(t // tm,),
        in_specs=[spec] * 6,
        out_specs=spec,
        out_shape=jax.ShapeDtypeStruct((t, w), BF16),
        compiler_params=_params(("parallel",), 40),
        name="dilated_combine",
    )(*flat)


def _gating_kernel(u_ref, v_ref, w_ref, b_ref, g_ref, o_ref, *, n_chunks):
    w = w_ref[0]
    bias = b_ref[0]
    gain = g_ref[0]
    for c in range(n_chunks):
        rows = pl.ds(c * CHUNK, CHUNK)
        v = jax.nn.gelu(v_ref[rows, :])
        mu = jnp.mean(v, axis=-1, keepdims=True)
        d = v - mu
        var = jnp.mean(d * d, axis=-1, keepdims=True)
        vn = d * lax.rsqrt(var + EPS) * gain
        sv = jnp.dot(w, vn.astype(BF16), preferred_element_type=F32) + bias
        o_ref[rows, :] = (jax.nn.gelu(u_ref[rows, :]) * sv).astype(o_ref.dtype)


def _spatial_gating(proj, w_s, b_s, g_ln, u_col, v_col, tm=512):
    t = proj.shape[0]
    groups = w_s.shape[0]
    bias = jnp.broadcast_to(b_s[:, :, None], (groups, CHUNK, LANES))
    kern = functools.partial(_gating_kernel, n_chunks=tm // CHUNK)
    return pl.pallas_call(
        kern,
        grid=(t // tm, groups),
        in_specs=[pl.BlockSpec((tm, LANES), lambda i, g: (i, u_col + g)),
                  pl.BlockSpec((tm, LANES), lambda i, g: (i, v_col + g)),
                  pl.BlockSpec((1, CHUNK, CHUNK), lambda i, g: (g, 0, 0)),
                  pl.BlockSpec((1, CHUNK, LANES), lambda i, g: (g, 0, 0)),
                  pl.BlockSpec((1, 1, LANES), lambda i, g: (g, 0, 0))],
        out_specs=pl.BlockSpec((tm, LANES), lambda i, g: (i, g)),
        out_shape=jax.ShapeDtypeStruct((t, groups * LANES), BF16),
        compiler_params=_params(("parallel", "parallel"), 32),
        name="spatial_gating",
    )(proj, proj, w_s.astype(BF16), bias, g_ln.reshape(groups, 1, LANES))


def _diff_attn_kernel(q_ref, k_ref, v_ref, lam_ref, g_ref, o_ref, *, bq, bk, seq, lam_init):
    q = q_ref[0]
    lane = lax.broadcasted_iota(jnp.int32, q.shape, 1)
    zero = jnp.zeros_like(q)
    q1 = jnp.where(lane < C_HALF, q, zero)
    q2 = jnp.where(lane < C_HALF, zero, q)

    def update(s, m, l, acc, vc):
        m_new = jnp.maximum(m, jnp.max(s, axis=-1, keepdims=True))
        alpha = jnp.exp(m - m_new)
        p = jnp.exp(s - m_new)
        l_new = alpha * l + jnp.sum(p, axis=-1, keepdims=True)
        acc_new = alpha * acc + jnp.dot(p.astype(BF16), vc, preferred_element_type=F32)
        return m_new, l_new, acc_new

    def body(j, carry):
        m1, l1, a1, m2, l2, a2 = carry
        rows = pl.ds(pl.multiple_of(j * bk, bk), bk)
        kc = k_ref[0, rows, :]
        vc = v_ref[0, rows, :]
        m1, l1, a1 = update(_dot_nt(q1, kc), m1, l1, a1, vc)
        m2, l2, a2 = update(_dot_nt(q2, kc), m2, l2, a2, vc)
        return m1, l1, a1, m2, l2, a2

    col = jnp.full((bq, 1), NEG_BIG, F32)
    zcol = jnp.zeros((bq, 1), F32)
    zacc = jnp.zeros((bq, LANES), F32)
    m1, l1, a1, m2, l2, a2 = lax.fori_loop(0, seq // bk, body, (col, zcol, zacc, col, zcol, zacc))

    lam = lam_ref[...]
    lam_full = (jnp.exp(jnp.sum(lam[0:1] * lam[1:2], keepdims=True))
                - jnp.exp(jnp.sum(lam[2:3] * lam[3:4], keepdims=True)) + lam_init)
    o = a1 / l1 - lam_full * (a2 / l2)
    o_ref[0] = (_rms(o, g_ref[...]) * (1.0 - lam_init)).astype(o_ref.dtype)


def _diff_attention(qkv, lam, subln_g, batch, seq, layer_idx, bq=256, bk=512):
    c_w = C_HEADS * 2 * C_HALF
    lam_init = 0.8 - 0.6 * math.exp(-0.3 * layer_idx)
    x = qkv.reshape(batch, seq, 3 * c_w)
    kern = functools.partial(_diff_attn_kernel, bq=bq, bk=bk, seq=seq, lam_init=lam_init)
    out = pl.pallas_call(
        kern,
        grid=(batch, C_HEADS, seq // bq),
        in_specs=[pl.BlockSpec((1, bq, LANES), lambda b, h, i: (b, i, h)),
                  pl.BlockSpec((1, seq, LANES), lambda b, h, i: (b, 0, C_HEADS + h)),
                  pl.BlockSpec((1, seq, LANES), lambda b, h, i: (b, 0, 2 * C_HEADS + h)),
                  pl.BlockSpec((4, C_HALF), lambda b, h, i: (0, 0)),
                  pl.BlockSpec((1, LANES), lambda b, h, i: (0, 0))],
        out_specs=pl.BlockSpec((1, bq, LANES), lambda b, h, i: (b, i, h)),
        out_shape=jax.ShapeDtypeStruct((batch, seq, c_w), BF16),
        compiler_params=_params(("parallel", "parallel", "arbitrary"), 40),
        name="diff_attn",
    )(x, x, x, lam, subln_g.reshape(1, LANES))
    return out.reshape(batch * seq, c_w)


def _fourier_wprep_kernel(c_ref, s_ref, w_ref, o_ref, *, scale):
    w = w_ref[0]
    mr = jnp.dot(c_ref[...], w, preferred_element_type=F32, precision=lax.Precision.HIGHEST)
    mi = jnp.dot(s_ref[...], w, preferred_element_type=F32, precision=lax.Precision.HIGHEST)
    o_ref[0, :, :LANES] = (mr * scale).astype(BF16)
    o_ref[0, :, LANES:] = (-mi * scale).astype(BF16)


def _fourier_weights(w_f, seq):
    groups, c, _ = w_f.shape
    idx = jnp.arange(c, dtype=jnp.int32)
    ang = (2.0 * math.pi / c) * ((idx[:, None] * idx[None, :]) % c).astype(F32)
    kern = functools.partial(_fourier_wprep_kernel, scale=1.0 / math.sqrt(seq * c))
    return pl.pallas_call(
        kern,
        grid=(groups,),
        in_specs=[pl.BlockSpec((c, c), lambda g: (0, 0)),
                  pl.BlockSpec((c, c), lambda g: (0, 0)),
                  pl.BlockSpec((1, c, c), lambda g: (g, 0, 0))],
        out_specs=pl.BlockSpec((1, c, 2 * c), lambda g: (g, 0, 0)),
        out_shape=jax.ShapeDtypeStruct((groups, c, 2 * c), BF16),
        compiler_params=_params(("parallel",), 32),
        name="fourier_wprep",
    )(jnp.cos(ang), jnp.sin(ang), w_f)


def _chan_mix_kernel(z_ref, m_ref, o_ref):
    o_ref[...] = jnp.dot(z_ref[...].astype(BF16), m_ref[0], preferred_element_type=F32).astype(BF16)


def _fourier_channel_mix(proj, m_c, z_col, tm=512):
    t = proj.shape[0]
    groups = m_c.shape[0]
    return pl.pallas_call(
        _chan_mix_kernel,
        grid=(t // tm, groups),
        in_specs=[pl.BlockSpec((tm, LANES), lambda i, g: (i, z_col + g)),
                  pl.BlockSpec((1, LANES, 2 * LANES), lambda i, g: (g, 0, 0))],
        out_specs=pl.BlockSpec((tm, 2 * LANES), lambda i, g: (i, g)),
        out_shape=jax.ShapeDtypeStruct((t, groups * 2 * LANES), BF16),
        compiler_params=_params(("parallel", "parallel"), 32),
        name="fourier_channel_mix",
    )(proj, m_c)


def _dft_stage1_kernel(u_ref, fr_ref, fi_ref, y_ref, *, pairs):
    u = u_ref[0]
    p = jnp.dot(fr_ref[...], u, preferred_element_type=F32)
    q = jnp.dot(fi_ref[...], u, preferred_element_type=F32)
    for j in range(pairs):
        re = slice(2 * j * LANES, (2 * j + 1) * LANES)
        im = slice((2 * j + 1) * LANES, (2 * j + 2) * LANES)
        y_ref[0, :, re] = (p[:, re] - q[:, im]).astype(BF16)
        y_ref[0, :, im] = (q[:, re] + p[:, im]).astype(BF16)


def _dft_stage2_kernel(y_ref, gc_ref, gs_ref, o_ref, *, groups):
    y = y_ref[0, 0]
    a = jnp.dot(gc_ref[0], y, preferred_element_type=F32)
    b = jnp.dot(gs_ref[0], y, preferred_element_type=F32)
    for g in range(groups):
        re = slice(2 * g * LANES, (2 * g + 1) * LANES)
        im = slice((2 * g + 1) * LANES, (2 * g + 2) * LANES)
        o_ref[0, :, g * LANES:(g + 1) * LANES] = (a[:, re] + b[:, im]).astype(o_ref.dtype)


def _seq_dft_real(u, batch, seq, groups, tc=4096):
    n1 = 128 if seq >= 16384 else 64
    n2 = seq // n1
    wc = groups * 2 * LANES
    i1 = jnp.arange(n1, dtype=jnp.int32)
    ang1 = (2.0 * math.pi / n1) * ((i1[:, None] * i1[None, :]) % n1).astype(F32)
    fr = jnp.cos(ang1).astype(BF16)
    fi = (-jnp.sin(ang1)).astype(BF16)
    cols = n2 * wc
    tc = min(tc, cols)
    y = pl.pallas_call(
        functools.partial(_dft_stage1_kernel, pairs=tc // (2 * LANES)),
        grid=(batch, cols // tc),
        in_specs=[pl.BlockSpec((1, n1, tc), lambda b, j: (b, 0, j)),
                  pl.BlockSpec((n1, n1), lambda b, j: (0, 0)),
                  pl.BlockSpec((n1, n1), lambda b, j: (0, 0))],
        out_specs=pl.BlockSpec((1, n1, tc), lambda b, j: (b, 0, j)),
        out_shape=jax.ShapeDtypeStruct((batch, n1, cols), BF16),
        compiler_params=_params(("parallel", "parallel"), 32),
        name="dft_stage1",
    )(u.reshape(batch, n1, cols), fr, fi)
    i2 = jnp.arange(n2, dtype=jnp.int32)
    tw = (i2[None, None, :] * (i1[:, None, None] + n1 * i2[None, :, None])) % seq
    ang2 = (2.0 * math.pi / seq) * tw.astype(F32)
    gc = jnp.cos(ang2).astype(BF16)
    gs = jnp.sin(ang2).astype(BF16)
    out = pl.pallas_call(
        functools.partial(_dft_stage2_kernel, groups=groups),
        grid=(batch, n1),
        in_specs=[pl.BlockSpec((1, 1, n2, wc), lambda b, k: (b, k, 0, 0)),
                  pl.BlockSpec((1, n2, n2), lambda b, k: (k, 0, 0)),
                  pl.BlockSpec((1, n2, n2), lambda b, k: (k, 0, 0))],
        out_specs=pl.BlockSpec((1, n2, groups * LANES), lambda b, k: (b, 0, k)),
        out_shape=jax.ShapeDtypeStruct((batch, n2, n1 * groups * LANES), BF16),
        compiler_params=_params(("parallel", "parallel"), 32),
        name="dft_stage2",
    )(y.reshape(batch, n1, n2, wc), gc, gs)
    return out.reshape(batch * seq, groups * LANES)


def _cross_attn_kernel(q_ref, kv_ref, o_ref, *, heads, scale):
    ca_w = heads * HEAD_DIM
    for h in range(heads):
        cols = slice(h * HEAD_DIM, (h + 1) * HEAD_DIM)
        q = q_ref[0, :, cols]
        k = kv_ref[0, :, cols]
        v = kv_ref[0, :, ca_w + h * HEAD_DIM:ca_w + (h + 1) * HEAD_DIM]
        s = _dot_nt(q, k) * scale
        m = jnp.max(s, axis=-1, keepdims=True)
        p = jnp.exp(s - m)
        l = jnp.sum(p, axis=-1, keepdims=True)
        o = jnp.dot(p.astype(BF16), v, preferred_element_type=F32) / l
        o_ref[0, :, cols] = o.astype(o_ref.dtype)


def _cross_attention(q, kv, batch, seq, tm=512):
    ca_w = CA_HEADS * HEAD_DIM
    mem_len = kv.shape[0] // batch
    kern = functools.partial(_cross_attn_kernel, heads=CA_HEADS, scale=HEAD_DIM ** -0.5)
    out = pl.pallas_call(
        kern,
        grid=(batch, seq // tm),
        in_specs=[pl.BlockSpec((1, tm, ca_w), lambda b, i: (b, i, 0)),
                  pl.BlockSpec((1, mem_len, 2 * ca_w), lambda b, i: (b, 0, 0))],
        out_specs=pl.BlockSpec((1, tm, ca_w), lambda b, i: (b, i, 0)),
        out_shape=jax.ShapeDtypeStruct((batch, seq, ca_w), BF16),
        compiler_params=_params(("parallel", "parallel"), 32),
        name="cross_attn",
    )(q.reshape(batch, seq, ca_w), kv.reshape(batch, mem_len, 2 * ca_w))
    return out.reshape(batch * seq, ca_w)


HALO = 16


def _ffn_kernel(x_ref, xp_ref, xn_ref, g4_ref, wg_ref, wv_ref, cwg_ref, cwv_ref, cbg_ref, cbv_ref,
                wd_ref, g5_ref, o_ref, xe_ref, zg_ref, zv_ref, acc_ref, *, tm, tiles_per_seq, nf):
    i = pl.program_id(0)
    f = pl.program_id(1)

    @pl.when(f == 0)
    def _():
        g4 = g4_ref[...]
        pos = i % tiles_per_seq
        prev = _rms(xp_ref[...], g4) * jnp.where(pos == 0, 0.0, 1.0)
        nxt = _rms(xn_ref[...], g4) * jnp.where(pos == tiles_per_seq - 1, 0.0, 1.0)
        xe_ref[0:HALO, :] = prev.astype(BF16)
        xe_ref[HALO:HALO + tm, :] = _rms(x_ref[...], g4).astype(BF16)
        xe_ref[HALO + tm:, :] = nxt.astype(BF16)

    xe = xe_ref[...]
    zg_ref[...] = jnp.dot(xe, wg_ref[...], preferred_element_type=F32)
    zv_ref[...] = jnp.dot(xe, wv_ref[...], preferred_element_type=F32)

    def conv(z_ref, cw_ref, cb_ref):
        cw = cw_ref[...]
        return (z_ref[HALO - 1:HALO - 1 + tm, :] * cw[0:1] + z_ref[HALO:HALO + tm, :] * cw[1:2]
                + z_ref[HALO + 1:HALO + 1 + tm, :] * cw[2:3] + cb_ref[...])

    h = jax.nn.gelu(conv(zg_ref, cwg_ref, cbg_ref)) * conv(zv_ref, cwv_ref, cbv_ref)
    part = jnp.dot(h.astype(BF16), wd_ref[...], preferred_element_type=F32)

    @pl.when(f == 0)
    def _():
        acc_ref[...] = part

    @pl.when(f > 0)
    def _():
        acc_ref[...] += part

    @pl.when(f == nf - 1)
    def _():
        o_ref[...] = x_ref[...] + _rms(acc_ref[...], g5_ref[...])


def _conv_ffn(x, seq, g4, w_up, conv_w, conv_b, w_down, g5, tm=512, tf=512):
    t, d = x.shape
    d_ff = w_down.shape[0]
    nf = d_ff // tf
    hb = tm // HALO
    last_hb = t // HALO - 1
    kern = functools.partial(_ffn_kernel, tm=tm, tiles_per_seq=seq // tm, nf=nf)
    cb = conv_b.reshape(1, 2 * d_ff)
    return pl.pallas_call(
        kern,
        grid=(t // tm, nf),
        in_specs=[pl.BlockSpec((tm, d), lambda i, f: (i, 0)),
                  pl.BlockSpec((HALO, d), lambda i, f: (jnp.maximum(i * hb - 1, 0), 0)),
                  pl.BlockSpec((HALO, d), lambda i, f: (jnp.minimum((i + 1) * hb, last_hb), 0)),
                  pl.BlockSpec((1, d), lambda i, f: (0, 0)),
                  pl.BlockSpec((d, tf), lambda i, f: (0, f)),
                  pl.BlockSpec((d, tf), lambda i, f: (0, nf + f)),
                  pl.BlockSpec((CONV_W, tf), lambda i, f: (0, f)),
                  pl.BlockSpec((CONV_W, tf), lambda i, f: (0, nf + f)),
                  pl.BlockSpec((1, tf), lambda i, f: (0, f)),
                  pl.BlockSpec((1, tf), lambda i, f: (0, nf + f)),
                  pl.BlockSpec((tf, d), lambda i, f: (f, 0)),
                  pl.BlockSpec((1, d), lambda i, f: (0, 0))],
        out_specs=pl.BlockSpec((tm, d), lambda i, f: (i, 0)),
        out_shape=jax.ShapeDtypeStruct((t, d), F32),
        scratch_shapes=[pltpu.VMEM((tm + 2 * HALO, d), BF16),
                        pltpu.VMEM((tm + 2 * HALO, tf), F32),
                        pltpu.VMEM((tm + 2 * HALO, tf), F32),
                        pltpu.VMEM((tm, d), F32)],
        compiler_params=_params(("parallel", "arbitrary"), 52),
        name="conv_ffn",
    )(x, x, x, g4.reshape(1, d), w_up, w_up, conv_w, conv_w, cb, cb, w_down, g5.reshape(1, d))


def _trunk(x3, mem3, p):
    batch, seq, d = x3.shape
    x = x3.reshape(batch * seq, d)
    mem = mem3.reshape(-1, d)
    depth = p["w_in"].shape[0]
    a_w = A_HEADS * HEAD_DIM
    c_w = C_HEADS * 2 * C_HALF
    for layer in range(depth):
        g = p["norm_gains"][layer]
        proj = _norm_matmul(x, g[0], p["w_in"][layer], F32)
        if layer % 2 == 0:
            e = layer // 2
            qkv = _qkv_prep(proj, seq, a_w, HEAD_DIM, HEAD_DIM ** -0.5)
            parts = [_dilated_attention_one(qkv, batch, seq, dil) for _, dil in DILATED_CFGS]
            ya = _combine_configs(parts)
            yb = _spatial_gating(proj, p["b_w_spatial"][e], p["b_b_spatial"][e], p["b_ln_gain"][e],
                                 3 * A_HEADS, 3 * A_HEADS + B_GROUPS)
            mix = jnp.concatenate([ya, yb], axis=-1)
        else:
            o = layer // 2
            qkv = _qkv_prep(proj, seq, c_w, C_HALF, C_HALF ** -0.5)
            yc = _diff_attention(qkv, p["c_lambda"][o], p["c_subln_gain"][o], batch, seq, layer)
            m_c = _fourier_weights(p["d_w_fourier"][o], seq)
            u = _fourier_channel_mix(proj, m_c, 3 * c_w // LANES)
            yd = _seq_dft_real(u, batch, seq, D_GROUPS)
            mix = jnp.concatenate([yc, yd], axis=-1)
        x = _matmul_norm_res(mix, p["w_out"][layer], g[1], x)
        q = _norm_matmul(x, g[2], p["ca_w_q"][layer], BF16)
        kv = _norm_matmul(mem, p["mem_norm_gain"][layer], p["ca_w_kv"][layer], BF16)
        ca = _cross_attention(q, kv, batch, seq)
        x = _matmul_norm_res(ca, p["ca_w_o"][layer], g[3], x)
        x = _conv_ffn(x, seq, g[4], p["ffn_w_up"][layer], p["ffn_conv_w"][layer], p["ffn_conv_b"][layer],
                      p["ffn_w_down"][layer], g[5])
    return x.reshape(batch, seq, d)


def kernel(x_prompt, x_sample, mem_prompt, mem_sample, norm_gains, w_in, w_out, b_w_spatial, b_b_spatial,
           b_ln_gain, c_lambda, c_subln_gain, d_w_fourier, mem_norm_gain, ca_w_q, ca_w_kv, ca_w_o, ffn_w_up,
           ffn_conv_w, ffn_conv_b, ffn_w_down):
    p = dict(norm_gains=norm_gains, w_in=w_in.astype(BF16), w_out=w_out.astype(BF16), b_w_spatial=b_w_spatial,
             b_b_spatial=b_b_spatial, b_ln_gain=b_ln_gain, c_lambda=c_lambda, c_subln_gain=c_subln_gain,
             d_w_fourier=d_w_fourier, mem_norm_gain=mem_norm_gain, ca_w_q=ca_w_q.astype(BF16),
             ca_w_kv=ca_w_kv.astype(BF16), ca_w_o=ca_w_o.astype(BF16), ffn_w_up=ffn_w_up.astype(BF16),
             ffn_conv_w=ffn_conv_w, ffn_conv_b=ffn_conv_b, ffn_w_down=ffn_w_down.astype(BF16))
    return _trunk(x_prompt, mem_prompt, p), _trunk(x_sample, mem_sample, p)
```

```python
import functools
import math

import jax
import jax.numpy as jnp
from jax import lax
from jax.experimental import pallas as pl
from jax.experimental.pallas import tpu as pltpu

F32 = jnp.float32
BF16 = jnp.bfloat16

EPS = 1e-6
ROPE_THETA = 500000.0
ROPE_FRACTION = 4
LANES = 128
HEAD_DIM = 128
A_HEADS = 8
DILATED_CFGS = ((128, 1), (512, 4), (2048, 16))
BAND_RADIUS = 64
B_GROUPS = 8
CHUNK = 128
C_HEADS = 12
C_HALF = 64
D_GROUPS = 4
CA_HEADS = 4
CONV_W = 3
NEG_BIG = -1e30
MIB = 1024 * 1024


def _params(semantics, vmem_mib):
    return pltpu.CompilerParams(dimension_semantics=semantics, vmem_limit_bytes=vmem_mib * MIB)


def _rms(x, g):
    return x * lax.rsqrt(jnp.mean(x * x, axis=-1, keepdims=True) + EPS) * g


def _dot_nt(a, b):
    return lax.dot_general(a, b, (((1,), (1,)), ((), ())), preferred_element_type=F32)


def _norm_mm_kernel(x_ref, g_ref, w_ref, o_ref, xn_ref):
    @pl.when(pl.program_id(1) == 0)
    def _():
        xn_ref[...] = _rms(x_ref[...], g_ref[...]).astype(BF16)

    o_ref[...] = jnp.dot(xn_ref[...], w_ref[...], preferred_element_type=F32).astype(o_ref.dtype)


def _norm_matmul(x, g, w, out_dtype, tm=512, tn=512):
    t, k = x.shape
    n = w.shape[1]
    tm = min(tm, t)
    return pl.pallas_call(
        _norm_mm_kernel,
        grid=(t // tm, n // tn),
        in_specs=[pl.BlockSpec((tm, k), lambda i, j: (i, 0)),
                  pl.BlockSpec((1, k), lambda i, j: (0, 0)),
                  pl.BlockSpec((k, tn), lambda i, j: (0, j))],
        out_specs=pl.BlockSpec((tm, tn), lambda i, j: (i, j)),
        out_shape=jax.ShapeDtypeStruct((t, n), out_dtype),
        scratch_shapes=[pltpu.VMEM((tm, k), BF16)],
        compiler_params=_params(("parallel", "arbitrary"), 40),
        name="norm_matmul",
    )(x, g.reshape(1, k), w)


def _mm_norm_res_kernel(a_ref, w_ref, g_ref, r_ref, o_ref):
    y = jnp.dot(a_ref[...], w_ref[...], preferred_element_type=F32)
    o_ref[...] = r_ref[...] + _rms(y, g_ref[...])


def _matmul_norm_res(a, w, g, res, tm=512):
    t, k = a.shape
    n = w.shape[1]
    return pl.pallas_call(
        _mm_norm_res_kernel,
        grid=(t // tm,),
        in_specs=[pl.BlockSpec((tm, k), lambda i: (i, 0)),
                  pl.BlockSpec((k, n), lambda i: (0, 0)),
                  pl.BlockSpec((1, n), lambda i: (0, 0)),
                  pl.BlockSpec((tm, n), lambda i: (i, 0))],
        out_specs=pl.BlockSpec((tm, n), lambda i: (i, 0)),
        out_shape=jax.ShapeDtypeStruct((t, n), F32),
        compiler_params=_params(("parallel",), 48),
        name="matmul_norm_res",
    )(a, w, g.reshape(1, n), res)


def _rope_tables(seq, head_w):
    rd = head_w // ROPE_FRACTION
    half = rd // 2
    inv = ROPE_THETA ** (-jnp.arange(half, dtype=F32) / half)
    ang = jnp.arange(seq, dtype=F32)[:, None] * inv[None, :]
    cos = jnp.cos(ang)
    sin = jnp.sin(ang)
    rest = head_w - rd
    c_head = jnp.concatenate([cos, cos, jnp.ones((seq, rest), F32)], axis=-1)
    s_head = jnp.concatenate([-sin, sin, jnp.zeros((seq, rest), F32)], axis=-1)
    reps = LANES // head_w
    return jnp.tile(c_head, (1, reps)), jnp.tile(s_head, (1, reps))


def _qkv_prep_kernel(x_ref, c_ref, s_ref, o_ref, *, head_w, q_scale, n_blocks):
    part = pl.program_id(1)
    half = head_w // ROPE_FRACTION // 2

    def rope(scale):
        c = c_ref[...]
        s = s_ref[...]
        lane = lax.broadcasted_iota(jnp.int32, c.shape, 1) & (head_w - 1)
        first = lane < half
        for b in range(n_blocks):
            x = x_ref[:, b * LANES:(b + 1) * LANES]
            partner = jnp.where(first, pltpu.roll(x, LANES - half, axis=1), pltpu.roll(x, half, axis=1))
            y = x * c + partner * s
            if scale != 1.0:
                y = y * scale
            o_ref[:, b * LANES:(b + 1) * LANES] = y.astype(BF16)

    @pl.when(part == 0)
    def _():
        rope(q_scale)

    @pl.when(part == 1)
    def _():
        rope(1.0)

    @pl.when(part == 2)
    def _():
        o_ref[...] = x_ref[...].astype(BF16)


def _qkv_prep(proj, seq, part_w, head_w, q_scale, tm=256):
    t = proj.shape[0]
    cos_t, sin_t = _rope_tables(seq, head_w)
    spb = seq // tm
    kern = functools.partial(_qkv_prep_kernel, head_w=head_w, q_scale=q_scale, n_blocks=part_w // LANES)
    return pl.pallas_call(
        kern,
        grid=(t // tm, 3),
        in_specs=[pl.BlockSpec((tm, part_w), lambda i, p: (i, p)),
                  pl.BlockSpec((tm, LANES), lambda i, p: (i % spb, 0)),
                  pl.BlockSpec((tm, LANES), lambda i, p: (i % spb, 0))],
        out_specs=pl.BlockSpec((tm, part_w), lambda i, p: (i, p)),
        out_shape=jax.ShapeDtypeStruct((t, 3 * part_w), BF16),
        compiler_params=_params(("parallel", "arbitrary"), 32),
        name="qkv_prep",
    )(proj, cos_t, sin_t)


def _dil_attn_kernel(q_ref, k_ref, v_ref, o_ref, lse_ref, *, bq, sub_len):
    qi = pl.program_id(2)
    win = bq + 2 * BAND_RADIUS
    q0 = qi * bq
    ws = pl.multiple_of(jnp.clip(q0 - BAND_RADIUS, 0, sub_len - win), BAND_RADIUS)
    q = q_ref[0]
    kw = k_ref[0, pl.ds(ws, win), :]
    vw = v_ref[0, pl.ds(ws, win), :]
    s = _dot_nt(q, kw)
    qpos = q0 + lax.broadcasted_iota(jnp.int32, (bq, win), 0)
    kpos = ws + lax.broadcasted_iota(jnp.int32, (bq, win), 1)
    rel = kpos - qpos
    s = jnp.where((rel <= BAND_RADIUS) & (rel >= -BAND_RADIUS), s, NEG_BIG)
    m = jnp.max(s, axis=-1, keepdims=True)
    p = jnp.exp(s - m)
    l = jnp.sum(p, axis=-1, keepdims=True)
    o = jnp.dot(p.astype(BF16), vw, preferred_element_type=F32)
    o_ref[0] = o / l
    lse_ref[0] = jnp.broadcast_to(m + jnp.log(l), (bq, LANES))


def _dilated_attention_one(qkv, batch, seq, dil, bq=128):
    a_w = A_HEADS * HEAD_DIM
    sub_len = seq // dil
    x = qkv.reshape(batch, sub_len, dil * 3 * a_w)
    nb = 3 * A_HEADS
    kern = functools.partial(_dil_attn_kernel, bq=bq, sub_len=sub_len)
    o, lse = pl.pallas_call(
        kern,
        grid=(batch, dil * A_HEADS, sub_len // bq),
        in_specs=[
            pl.BlockSpec((1, bq, LANES), lambda b, c, i: (b, i, (c // A_HEADS) * nb + c % A_HEADS)),
            pl.BlockSpec((1, sub_len, LANES),
                         lambda b, c, i: (b, 0, (c // A_HEADS) * nb + A_HEADS + c % A_HEADS)),
            pl.BlockSpec((1, sub_len, LANES),
                         lambda b, c, i: (b, 0, (c // A_HEADS) * nb + 2 * A_HEADS + c % A_HEADS)),
        ],
        out_specs=[pl.BlockSpec((1, bq, LANES), lambda b, c, i: (b, i, c)),
                   pl.BlockSpec((1, bq, LANES), lambda b, c, i: (b, i, c))],
        out_shape=[jax.ShapeDtypeStruct((batch, sub_len, dil * a_w), F32),
                   jax.ShapeDtypeStruct((batch, sub_len, dil * a_w), F32)],
        compiler_params=_params(("parallel", "parallel", "arbitrary"), 40),
        name=f"dilated_attn_d{dil}",
    )(x, x, x)
    return o.reshape(batch * seq, a_w), lse.reshape(batch * seq, a_w)


def _combine_kernel(o0, l0, o1, l1, o2, l2, y_ref):
    a, b, c = l0[...], l1[...], l2[...]
    m = jnp.maximum(jnp.maximum(a, b), c)
    wa, wb, wc = jnp.exp(a - m), jnp.exp(b - m), jnp.exp(c - m)
    num = wa * o0[...] + wb * o1[...] + wc * o2[...]
    y_ref[...] = (num / (wa + wb + wc)).astype(y_ref.dtype)


def _combine_configs(parts, tm=512):
    t, w = parts[0][0].shape
    flat = [a for pair in parts for a in pair]
    spec = pl.BlockSpec((tm, w), lambda i: (i, 0))
    return pl.pallas_call(
        _combine_kernel,
        grid=(t // tm,),
        in_specs=[spec] * 6,
        out_specs=spec,
        out_shape=jax.ShapeDtypeStruct((t, w), BF16),
        compiler_params=_params(("parallel",), 40),
        name="dilated_combine",
    )(*flat)


def _gating_kernel(u_ref, v_ref, w_ref, b_ref, g_ref, o_ref, *, n_chunks):
    w = w_ref[0]
    bias = b_ref[0]
    gain = g_ref[0]
    for c in range(n_chunks):
        rows = pl.ds(c * CHUNK, CHUNK)
        v = jax.nn.gelu(v_ref[rows, :])
        mu = jnp.mean(v, axis=-1, keepdims=True)
        d = v - mu
        var = jnp.mean(d * d, axis=-1, keepdims=True)
        vn = d * lax.rsqrt(var + EPS) * gain
        sv = jnp.dot(w, vn.astype(BF16), preferred_element_type=F32) + bias
        o_ref[rows, :] = (jax.nn.gelu(u_ref[rows, :]) * sv).astype(o_ref.dtype)


def _spatial_gating(proj, w_s, b_s, g_ln, u_col, v_col, tm=512):
    t = proj.shape[0]
    groups = w_s.shape[0]
    bias = jnp.broadcast_to(b_s[:, :, None], (groups, CHUNK, LANES))
    kern = functools.partial(_gating_kernel, n_chunks=tm // CHUNK)
    return pl.pallas_call(
        kern,
        grid=(t // tm, groups),
        in_specs=[pl.BlockSpec((tm, LANES), lambda i, g: (i, u_col + g)),
                  pl.BlockSpec((tm, LANES), lambda i, g: (i, v_col + g)),
                  pl.BlockSpec((1, CHUNK, CHUNK), lambda i, g: (g, 0, 0)),
                  pl.BlockSpec((1, CHUNK, LANES), lambda i, g: (g, 0, 0)),
                  pl.BlockSpec((1, 1, LANES), lambda i, g: (g, 0, 0))],
        out_specs=pl.BlockSpec((tm, LANES), lambda i, g: (i, g)),
        out_shape=jax.ShapeDtypeStruct((t, groups * LANES), BF16),
        compiler_params=_params(("parallel", "parallel"), 32),
        name="spatial_gating",
    )(proj, proj, w_s.astype(BF16), bias, g_ln.reshape(groups, 1, LANES))


def _diff_prep_kernel(x_ref, c_ref, s_ref, o_ref, *, q_scale, n_blocks):
    part = pl.program_id(1)
    half = C_HALF // ROPE_FRACTION // 2

    def rope(scale):
        c = c_ref[...]
        s = s_ref[...]
        first = (lax.broadcasted_iota(jnp.int32, c.shape, 1) & (C_HALF - 1)) < half
        for b in range(n_blocks):
            x = x_ref[:, b * LANES:(b + 1) * LANES]
            partner = jnp.where(first, pltpu.roll(x, LANES - half, axis=1), pltpu.roll(x, half, axis=1))
            o_ref[:, b * LANES:(b + 1) * LANES] = ((x * c + partner * s) * scale).astype(BF16)

    @pl.when(part == 0)
    def _():
        rope(q_scale)

    @pl.when(part == 1)
    def _():
        rope(1.0)

    def widen(first_head):
        ones = jnp.ones((x_ref.shape[0], LANES), BF16)
        for h in range(n_blocks // 2):
            src = (first_head + h) * LANES
            o_ref[:, 2 * h * LANES:(2 * h + 1) * LANES] = x_ref[:, src:src + LANES].astype(BF16)
            o_ref[:, (2 * h + 1) * LANES:(2 * h + 2) * LANES] = ones

    @pl.when(part == 2)
    def _():
        widen(0)

    @pl.when(part == 3)
    def _():
        widen(n_blocks // 2)


def _diff_prep(proj, seq, q_scale, tm=256):
    t = proj.shape[0]
    c_w = C_HEADS * 2 * C_HALF
    cos_t, sin_t = _rope_tables(seq, C_HALF)
    spb = seq // tm
    kern = functools.partial(_diff_prep_kernel, q_scale=q_scale, n_blocks=c_w // LANES)
    return pl.pallas_call(
        kern,
        grid=(t // tm, 4),
        in_specs=[pl.BlockSpec((tm, c_w), lambda i, p: (i, jnp.minimum(p, 2))),
                  pl.BlockSpec((tm, LANES), lambda i, p: (i % spb, 0)),
                  pl.BlockSpec((tm, LANES), lambda i, p: (i % spb, 0))],
        out_specs=pl.BlockSpec((tm, c_w), lambda i, p: (i, p)),
        out_shape=jax.ShapeDtypeStruct((t, 4 * c_w), BF16),
        compiler_params=_params(("parallel", "arbitrary"), 32),
        name="diff_prep",
    )(proj, cos_t, sin_t)


def _diff_attn_kernel(q_ref, k_ref, v_ref, lam_ref, g_ref, o_ref, s_ref, mc_ref, m_ref, acc_ref,
                      *, bq, bk, seq, lam_init):
    n = seq // bk
    tiles = bk // LANES
    q = q_ref[0]
    lane = lax.broadcasted_iota(jnp.int32, q.shape, 1)
    zero = jnp.zeros_like(q)
    qs = (jnp.where(lane < C_HALF, q, zero), jnp.where(lane < C_HALF, zero, q))

    def scores(j, slot):
        kc = k_ref[0, pl.ds(pl.multiple_of(j * bk, bk), bk), :]
        for br in range(2):
            s = _dot_nt(qs[br], kc)
            s_ref[slot, br] = s
            mc = s[:, 0:LANES]
            for t in range(1, tiles):
                mc = jnp.maximum(mc, s[:, t * LANES:(t + 1) * LANES])
            mc_ref[slot, br] = jnp.broadcast_to(jnp.max(mc, axis=-1, keepdims=True), (bq, LANES))

    def accumulate(j, slot):
        vc = v_ref[0, pl.ds(pl.multiple_of(j * bk, bk), bk), :]
        for br in range(2):
            m_old = m_ref[br]
            m_new = jnp.maximum(m_old, mc_ref[slot, br])
            alpha = jnp.exp2(m_old - m_new)
            m_ref[br] = m_new
            s = s_ref[slot, br]
            p = jnp.concatenate(
                [jnp.exp2(s[:, t * LANES:(t + 1) * LANES] - m_new).astype(BF16) for t in range(tiles)], axis=1)
            pv = jnp.dot(p, vc, preferred_element_type=F32)
            acc_ref[br, :, 0:LANES] = alpha * acc_ref[br, :, 0:LANES] + pv[:, 0:LANES]
            acc_ref[br, :, LANES:] = alpha * acc_ref[br, :, LANES:] + pv[:, LANES:]

    m_ref[...] = jnp.full(m_ref.shape, NEG_BIG, F32)
    acc_ref[...] = jnp.zeros(acc_ref.shape, F32)
    scores(0, 0)

    def body(i, carry):
        j = 2 * i
        scores(j + 1, 1)
        accumulate(j, 0)
        scores(jnp.minimum(j + 2, n - 1), 0)
        accumulate(j + 1, 1)
        return carry

    lax.fori_loop(0, n // 2, body, 0)

    lam = lam_ref[...]
    lam_full = (jnp.exp(jnp.sum(lam[0:1] * lam[1:2], keepdims=True))
                - jnp.exp(jnp.sum(lam[2:3] * lam[3:4], keepdims=True)) + lam_init)
    a1 = acc_ref[0]
    a2 = acc_ref[1]
    o = a1[:, 0:LANES] / a1[:, LANES:] - lam_full * (a2[:, 0:LANES] / a2[:, LANES:])
    o_ref[0] = (_rms(o, g_ref[...]) * (1.0 - lam_init)).astype(o_ref.dtype)


def _diff_attention(qkv, lam, subln_g, batch, seq, layer_idx, bq=256, bk=512):
    c_w = C_HEADS * 2 * C_HALF
    lam_init = 0.8 - 0.6 * math.exp(-0.3 * layer_idx)
    x = qkv.reshape(batch, seq, 4 * c_w)
    assert (seq // bk) % 2 == 0
    kern = functools.partial(_diff_attn_kernel, bq=bq, bk=bk, seq=seq, lam_init=lam_init)
    out = pl.pallas_call(
        kern,
        grid=(batch, C_HEADS, seq // bq),
        in_specs=[pl.BlockSpec((1, bq, LANES), lambda b, h, i: (b, i, h)),
                  pl.BlockSpec((1, seq, LANES), lambda b, h, i: (b, 0, C_HEADS + h)),
                  pl.BlockSpec((1, seq, 2 * LANES), lambda b, h, i: (b, 0, C_HEADS + h)),
                  pl.BlockSpec((4, C_HALF), lambda b, h, i: (0, 0)),
                  pl.BlockSpec((1, LANES), lambda b, h, i: (0, 0))],
        out_specs=pl.BlockSpec((1, bq, LANES), lambda b, h, i: (b, i, h)),
        out_shape=jax.ShapeDtypeStruct((batch, seq, c_w), BF16),
        scratch_shapes=[pltpu.VMEM((2, 2, bq, bk), F32),
                        pltpu.VMEM((2, 2, bq, LANES), F32),
                        pltpu.VMEM((2, bq, LANES), F32),
                        pltpu.VMEM((2, bq, 2 * LANES), F32)],
        compiler_params=_params(("parallel", "parallel", "arbitrary"), 48),
        name="diff_attn",
    )(x, x, x, lam, subln_g.reshape(1, LANES))
    return out.reshape(batch * seq, c_w)


def _fourier_wprep_kernel(c_ref, s_ref, w_ref, o_ref, *, scale):
    w = w_ref[0]
    mr = jnp.dot(c_ref[...], w, preferred_element_type=F32, precision=lax.Precision.HIGHEST)
    mi = jnp.dot(s_ref[...], w, preferred_element_type=F32, precision=lax.Precision.HIGHEST)
    o_ref[0, :, :LANES] = (mr * scale).astype(BF16)
    o_ref[0, :, LANES:] = (-mi * scale).astype(BF16)


def _fourier_weights(w_f, seq):
    groups, c, _ = w_f.shape
    idx = jnp.arange(c, dtype=jnp.int32)
    ang = (2.0 * math.pi / c) * ((idx[:, None] * idx[None, :]) % c).astype(F32)
    kern = functools.partial(_fourier_wprep_kernel, scale=1.0 / math.sqrt(seq * c))
    return pl.pallas_call(
        kern,
        grid=(groups,),
        in_specs=[pl.BlockSpec((c, c), lambda g: (0, 0)),
                  pl.BlockSpec((c, c), lambda g: (0, 0)),
                  pl.BlockSpec((1, c, c), lambda g: (g, 0, 0))],
        out_specs=pl.BlockSpec((1, c, 2 * c), lambda g: (g, 0, 0)),
        out_shape=jax.ShapeDtypeStruct((groups, c, 2 * c), BF16),
        compiler_params=_params(("parallel",), 32),
        name="fourier_wprep",
    )(jnp.cos(ang), jnp.sin(ang), w_f)


def _chan_mix_kernel(z_ref, m_ref, o_ref):
    o_ref[...] = jnp.dot(z_ref[...].astype(BF16), m_ref[0], preferred_element_type=F32).astype(BF16)


def _fourier_channel_mix(proj, m_c, z_col, tm=512):
    t = proj.shape[0]
    groups = m_c.shape[0]
    return pl.pallas_call(
        _chan_mix_kernel,
        grid=(t // tm, groups),
        in_specs=[pl.BlockSpec((tm, LANES), lambda i, g: (i, z_col + g)),
                  pl.BlockSpec((1, LANES, 2 * LANES), lambda i, g: (g, 0, 0))],
        out_specs=pl.BlockSpec((tm, 2 * LANES), lambda i, g: (i, g)),
        out_shape=jax.ShapeDtypeStruct((t, groups * 2 * LANES), BF16),
        compiler_params=_params(("parallel", "parallel"), 32),
        name="fourier_channel_mix",
    )(proj, m_c)


def _dft_stage1_kernel(u_ref, fr_ref, fi_ref, y_ref, *, pairs):
    u = u_ref[0]
    p = jnp.dot(fr_ref[...], u, preferred_element_type=F32)
    q = jnp.dot(fi_ref[...], u, preferred_element_type=F32)
    for j in range(pairs):
        re = slice(2 * j * LANES, (2 * j + 1) * LANES)
        im = slice((2 * j + 1) * LANES, (2 * j + 2) * LANES)
        y_ref[0, :, re] = (p[:, re] - q[:, im]).astype(BF16)
        y_ref[0, :, im] = (q[:, re] + p[:, im]).astype(BF16)


def _dft_stage2_kernel(y_ref, gc_ref, gs_ref, o_ref, *, groups):
    y = y_ref[0, 0]
    a = jnp.dot(gc_ref[0], y, preferred_element_type=F32)
    b = jnp.dot(gs_ref[0], y, preferred_element_type=F32)
    for g in range(groups):
        re = slice(2 * g * LANES, (2 * g + 1) * LANES)
        im = slice((2 * g + 1) * LANES, (2 * g + 2) * LANES)
        o_ref[0, :, g * LANES:(g + 1) * LANES] = (a[:, re] + b[:, im]).astype(o_ref.dtype)


def _seq_dft_real(u, batch, seq, groups, tc=4096):
    n1 = 128 if seq >= 16384 else 64
    n2 = seq // n1
    wc = groups * 2 * LANES
    i1 = jnp.arange(n1, dtype=jnp.int32)
    ang1 = (2.0 * math.pi / n1) * ((i1[:, None] * i1[None, :]) % n1).astype(F32)
    fr = jnp.cos(ang1).astype(BF16)
    fi = (-jnp.sin(ang1)).astype(BF16)
    cols = n2 * wc
    tc = min(tc, cols)
    y = pl.pallas_call(
        functools.partial(_dft_stage1_kernel, pairs=tc // (2 * LANES)),
        grid=(batch, cols // tc),
        in_specs=[pl.BlockSpec((1, n1, tc), lambda b, j: (b, 0, j)),
                  pl.BlockSpec((n1, n1), lambda b, j: (0, 0)),
                  pl.BlockSpec((n1, n1), lambda b, j: (0, 0))],
        out_specs=pl.BlockSpec((1, n1, tc), lambda b, j: (b, 0, j)),
        out_shape=jax.ShapeDtypeStruct((batch, n1, cols), BF16),
        compiler_params=_params(("parallel", "parallel"), 32),
        name="dft_stage1",
    )(u.reshape(batch, n1, cols), fr, fi)
    i2 = jnp.arange(n2, dtype=jnp.int32)
    tw = (i2[None, None, :] * (i1[:, None, None] + n1 * i2[None, :, None])) % seq
    ang2 = (2.0 * math.pi / seq) * tw.astype(F32)
    gc = jnp.cos(ang2).astype(BF16)
    gs = jnp.sin(ang2).astype(BF16)
    out = pl.pallas_call(
        functools.partial(_dft_stage2_kernel, groups=groups),
        grid=(batch, n1),
        in_specs=[pl.BlockSpec((1, 1, n2, wc), lambda b, k: (b, k, 0, 0)),
                  pl.BlockSpec((1, n2, n2), lambda b, k: (k, 0, 0)),
                  pl.BlockSpec((1, n2, n2), lambda b, k: (k, 0, 0))],
        out_specs=pl.BlockSpec((1, n2, groups * LANES), lambda b, k: (b, 0, k)),
        out_shape=jax.ShapeDtypeStruct((batch, n2, n1 * groups * LANES), BF16),
        compiler_params=_params(("parallel", "parallel"), 32),
        name="dft_stage2",
    )(y.reshape(batch, n1, n2, wc), gc, gs)
    return out.reshape(batch * seq, groups * LANES)


def _cross_attn_kernel(q_ref, kv_ref, o_ref, *, heads, scale):
    ca_w = heads * HEAD_DIM
    for h in range(heads):
        cols = slice(h * HEAD_DIM, (h + 1) * HEAD_DIM)
        q = q_ref[0, :, cols]
        k = kv_ref[0, :, cols]
        v = kv_ref[0, :, ca_w + h * HEAD_DIM:ca_w + (h + 1) * HEAD_DIM]
        s = _dot_nt(q, k) * scale
        m = jnp.max(s, axis=-1, keepdims=True)
        p = jnp.exp(s - m)
        l = jnp.sum(p, axis=-1, keepdims=True)
        o = jnp.dot(p.astype(BF16), v, preferred_element_type=F32) / l
        o_ref[0, :, cols] = o.astype(o_ref.dtype)


def _cross_attention(q, kv, batch, seq, tm=512):
    ca_w = CA_HEADS * HEAD_DIM
    mem_len = kv.shape[0] // batch
    kern = functools.partial(_cross_attn_kernel, heads=CA_HEADS, scale=HEAD_DIM ** -0.5)
    out = pl.pallas_call(
        kern,
        grid=(batch, seq // tm),
        in_specs=[pl.BlockSpec((1, tm, ca_w), lambda b, i: (b, i, 0)),
                  pl.BlockSpec((1, mem_len, 2 * ca_w), lambda b, i: (b, 0, 0))],
        out_specs=pl.BlockSpec((1, tm, ca_w), lambda b, i: (b, i, 0)),
        out_shape=jax.ShapeDtypeStruct((batch, seq, ca_w), BF16),
        compiler_params=_params(("parallel", "parallel"), 32),
        name="cross_attn",
    )(q.reshape(batch, seq, ca_w), kv.reshape(batch, mem_len, 2 * ca_w))
    return out.reshape(batch * seq, ca_w)


HALO = 16


def _ffn_kernel(x_ref, xp_ref, xn_ref, g4_ref, wg_ref, wv_ref, cwg_ref, cwv_ref, cbg_ref, cbv_ref,
                wd_ref, g5_ref, o_ref, xe_ref, zg_ref, zv_ref, acc_ref, *, tm, tiles_per_seq, nf):
    i = pl.program_id(0)
    f = pl.program_id(1)

    @pl.when(f == 0)
    def _():
        g4 = g4_ref[...]
        pos = i % tiles_per_seq
        prev = _rms(xp_ref[...], g4) * jnp.where(pos == 0, 0.0, 1.0)
        nxt = _rms(xn_ref[...], g4) * jnp.where(pos == tiles_per_seq - 1, 0.0, 1.0)
        xe_ref[0:HALO, :] = prev.astype(BF16)
        xe_ref[HALO:HALO + tm, :] = _rms(x_ref[...], g4).astype(BF16)
        xe_ref[HALO + tm:, :] = nxt.astype(BF16)

    xe = xe_ref[...]
    zg_ref[...] = jnp.dot(xe, wg_ref[...], preferred_element_type=F32)
    zv_ref[...] = jnp.dot(xe, wv_ref[...], preferred_element_type=F32)

    def conv(z_ref, cw_ref, cb_ref):
        cw = cw_ref[...]
        return (z_ref[HALO - 1:HALO - 1 + tm, :] * cw[0:1] + z_ref[HALO:HALO + tm, :] * cw[1:2]
                + z_ref[HALO + 1:HALO + 1 + tm, :] * cw[2:3] + cb_ref[...])

    h = jax.nn.gelu(conv(zg_ref, cwg_ref, cbg_ref)) * conv(zv_ref, cwv_ref, cbv_ref)
    part = jnp.dot(h.astype(BF16), wd_ref[...], preferred_element_type=F32)

    @pl.when(f == 0)
    def _():
        acc_ref[...] = part

    @pl.when(f > 0)
    def _():
        acc_ref[...] += part

    @pl.when(f == nf - 1)
    def _():
        o_ref[...] = x_ref[...] + _rms(acc_ref[...], g5_ref[...])


def _conv_ffn(x, seq, g4, w_up, conv_w, conv_b, w_down, g5, tm=512, tf=512):
    t, d = x.shape
    d_ff = w_down.shape[0]
    nf = d_ff // tf
    hb = tm // HALO
    last_hb = t // HALO - 1
    kern = functools.partial(_ffn_kernel, tm=tm, tiles_per_seq=seq // tm, nf=nf)
    cb = conv_b.reshape(1, 2 * d_ff)
    return pl.pallas_call(
        kern,
        grid=(t // tm, nf),
        in_specs=[pl.BlockSpec((tm, d), lambda i, f: (i, 0)),
                  pl.BlockSpec((HALO, d), lambda i, f: (jnp.maximum(i * hb - 1, 0), 0)),
                  pl.BlockSpec((HALO, d), lambda i, f: (jnp.minimum((i + 1) * hb, last_hb), 0)),
                  pl.BlockSpec((1, d), lambda i, f: (0, 0)),
                  pl.BlockSpec((d, tf), lambda i, f: (0, f)),
                  pl.BlockSpec((d, tf), lambda i, f: (0, nf + f)),
                  pl.BlockSpec((CONV_W, tf), lambda i, f: (0, f)),
                  pl.BlockSpec((CONV_W, tf), lambda i, f: (0, nf + f)),
                  pl.BlockSpec((1, tf), lambda i, f: (0, f)),
                  pl.BlockSpec((1, tf), lambda i, f: (0, nf + f)),
                  pl.BlockSpec((tf, d), lambda i, f: (f, 0)),
                  pl.BlockSpec((1, d), lambda i, f: (0, 0))],
        out_specs=pl.BlockSpec((tm, d), lambda i, f: (i, 0)),
        out_shape=jax.ShapeDtypeStruct((t, d), F32),
        scratch_shapes=[pltpu.VMEM((tm + 2 * HALO, d), BF16),
                        pltpu.VMEM((tm + 2 * HALO, tf), F32),
                        pltpu.VMEM((tm + 2 * HALO, tf), F32),
                        pltpu.VMEM((tm, d), F32)],
        compiler_params=_params(("parallel", "arbitrary"), 52),
        name="conv_ffn",
    )(x, x, x, g4.reshape(1, d), w_up, w_up, conv_w, conv_w, cb, cb, w_down, g5.reshape(1, d))


def _trunk(x3, mem3, p):
    batch, seq, d = x3.shape
    x = x3.reshape(batch * seq, d)
    mem = mem3.reshape(-1, d)
    depth = p["w_in"].shape[0]
    a_w = A_HEADS * HEAD_DIM
    c_w = C_HEADS * 2 * C_HALF
    for layer in range(depth):
        g = p["norm_gains"][layer]
        proj = _norm_matmul(x, g[0], p["w_in"][layer], F32)
        if layer % 2 == 0:
            e = layer // 2
            qkv = _qkv_prep(proj, seq, a_w, HEAD_DIM, HEAD_DIM ** -0.5)
            parts = [_dilated_attention_one(qkv, batch, seq, dil) for _, dil in DILATED_CFGS]
            ya = _combine_configs(parts)
            yb = _spatial_gating(proj, p["b_w_spatial"][e], p["b_b_spatial"][e], p["b_ln_gain"][e],
                                 3 * A_HEADS, 3 * A_HEADS + B_GROUPS)
            mix = jnp.concatenate([ya, yb], axis=-1)
        else:
            o = layer // 2
            qkv = _diff_prep(proj, seq, C_HALF ** -0.5 * math.log2(math.e))
            yc = _diff_attention(qkv, p["c_lambda"][o], p["c_subln_gain"][o], batch, seq, layer)
            m_c = _fourier_weights(p["d_w_fourier"][o], seq)
            u = _fourier_channel_mix(proj, m_c, 3 * c_w // LANES)
            yd = _seq_dft_real(u, batch, seq, D_GROUPS)
            mix = jnp.concatenate([yc, yd], axis=-1)
        x = _matmul_norm_res(mix, p["w_out"][layer], g[1], x)
        q = _norm_matmul(x, g[2], p["ca_w_q"][layer], BF16)
        kv = _norm_matmul(mem, p["mem_norm_gain"][layer], p["ca_w_kv"][layer], BF16)
        ca = _cross_attention(q, kv, batch, seq)
        x = _matmul_norm_res(ca, p["ca_w_o"][layer], g[3], x)
        x = _conv_ffn(x, seq, g[4], p["ffn_w_up"][layer], p["ffn_conv_w"][layer], p["ffn_conv_b"][layer],
                      p["ffn_w_down"][layer], g[5])
    return x.reshape(batch, seq, d)


def kernel(x_prompt, x_sample, mem_prompt, mem_sample, norm_gains, w_in, w_out, b_w_spatial, b_b_spatial,
           b_ln_gain, c_lambda, c_subln_gain, d_w_fourier, mem_norm_gain, ca_w_q, ca_w_kv, ca_w_o, ffn_w_up,
           ffn_conv_w, ffn_conv_b, ffn_w_down):
    p = dict(norm_gains=norm_gains, w_in=w_in.astype(BF16), w_out=w_out.astype(BF16), b_w_spatial=b_w_spatial,
             b_b_spatial=b_b_spatial, b_ln_gain=b_ln_gain, c_lambda=c_lambda, c_subln_gain=c_subln_gain,
             d_w_fourier=d_w_fourier, mem_norm_gain=mem_norm_gain, ca_w_q=ca_w_q.astype(BF16),
             ca_w_kv=ca_w_kv.astype(BF16), ca_w_o=ca_w_o.astype(BF16), ffn_w_up=ffn_w_up.astype(BF16),
             ffn_conv_w=ffn_conv_w, ffn_conv_b=ffn_conv_b, ffn_w_down=ffn_w_down.astype(BF16))
    return _trunk(x_prompt, mem_prompt, p), _trunk(x_sample, mem_sample, p)
```

```python
import functools
import math

import jax
import jax.numpy as jnp
from jax import lax
from jax.experimental import pallas as pl
from jax.experimental.pallas import tpu as pltpu

F32 = jnp.float32
BF16 = jnp.bfloat16

EPS = 1e-6
ROPE_THETA = 500000.0
ROPE_FRACTION = 4
LANES = 128
HEAD_DIM = 128
A_HEADS = 8
DILATED_CFGS = ((128, 1), (512, 4), (2048, 16))
BAND_RADIUS = 64
B_GROUPS = 8
CHUNK = 128
C_HEADS = 12
C_HALF = 64
D_GROUPS = 4
CA_HEADS = 4
CONV_W = 3
NEG_BIG = -1e30
MIB = 1024 * 1024


def _params(semantics, vmem_mib):
    return pltpu.CompilerParams(dimension_semantics=semantics, vmem_limit_bytes=vmem_mib * MIB)


def _rms(x, g):
    return x * lax.rsqrt(jnp.mean(x * x, axis=-1, keepdims=True) + EPS) * g


def _dot_nt(a, b):
    return lax.dot_general(a, b, (((1,), (1,)), ((), ())), preferred_element_type=F32)


def _norm_mm_kernel(x_ref, g_ref, w_ref, o_ref, xn_ref):
    @pl.when(pl.program_id(1) == 0)
    def _():
        xn_ref[...] = _rms(x_ref[...], g_ref[...]).astype(BF16)

    o_ref[...] = jnp.dot(xn_ref[...], w_ref[...], preferred_element_type=F32).astype(o_ref.dtype)


def _norm_matmul(x, g, w, out_dtype, tm=512, tn=512):
    t, k = x.shape
    n = w.shape[1]
    tm = min(tm, t)
    return pl.pallas_call(
        _norm_mm_kernel,
        grid=(t // tm, n // tn),
        in_specs=[pl.BlockSpec((tm, k), lambda i, j: (i, 0)),
                  pl.BlockSpec((1, k), lambda i, j: (0, 0)),
                  pl.BlockSpec((k, tn), lambda i, j: (0, j))],
        out_specs=pl.BlockSpec((tm, tn), lambda i, j: (i, j)),
        out_shape=jax.ShapeDtypeStruct((t, n), out_dtype),
        scratch_shapes=[pltpu.VMEM((tm, k), BF16)],
        compiler_params=_params(("parallel", "arbitrary"), 40),
        name="norm_matmul",
    )(x, g.reshape(1, k), w)


def _mm_norm_res_kernel(a_ref, w_ref, g_ref, r_ref, o_ref):
    y = jnp.dot(a_ref[...], w_ref[...], preferred_element_type=F32)
    o_ref[...] = r_ref[...] + _rms(y, g_ref[...])


def _matmul_norm_res(a, w, g, res, tm=512):
    t, k = a.shape
    n = w.shape[1]
    return pl.pallas_call(
        _mm_norm_res_kernel,
        grid=(t // tm,),
        in_specs=[pl.BlockSpec((tm, k), lambda i: (i, 0)),
                  pl.BlockSpec((k, n), lambda i: (0, 0)),
                  pl.BlockSpec((1, n), lambda i: (0, 0)),
                  pl.BlockSpec((tm, n), lambda i: (i, 0))],
        out_specs=pl.BlockSpec((tm, n), lambda i: (i, 0)),
        out_shape=jax.ShapeDtypeStruct((t, n), F32),
        compiler_params=_params(("parallel",), 48),
        name="matmul_norm_res",
    )(a, w, g.reshape(1, n), res)


def _rope_tables(seq, head_w):
    rd = head_w // ROPE_FRACTION
    half = rd // 2
    inv = ROPE_THETA ** (-jnp.arange(half, dtype=F32) / half)
    ang = jnp.arange(seq, dtype=F32)[:, None] * inv[None, :]
    cos = jnp.cos(ang)
    sin = jnp.sin(ang)
    rest = head_w - rd
    c_head = jnp.concatenate([cos, cos, jnp.ones((seq, rest), F32)], axis=-1)
    s_head = jnp.concatenate([-sin, sin, jnp.zeros((seq, rest), F32)], axis=-1)
    reps = LANES // head_w
    return jnp.tile(c_head, (1, reps)), jnp.tile(s_head, (1, reps))


def _rope_block(x, c, s, head_w):
    half = head_w // ROPE_FRACTION // 2
    first = (lax.broadcasted_iota(jnp.int32, x.shape, 1) & (head_w - 1)) < half
    partner = jnp.where(first, pltpu.roll(x, LANES - half, axis=1), pltpu.roll(x, half, axis=1))
    return x * c + partner * s


def _even_prep_kernel(x_ref, c_ref, s_ref, *refs, q_scale, n_blocks):
    out_refs, y_ref = refs[:-1], refs[-1]
    part = pl.program_id(1)
    tm = x_ref.shape[0]

    def rope(scale):
        c = c_ref[...]
        s = s_ref[...]
        for b in range(n_blocks):
            cols = slice(b * LANES, (b + 1) * LANES)
            y_ref[b] = _rope_block(x_ref[:, cols], c, s, HEAD_DIM) * scale

    @pl.when(part == 0)
    def _():
        rope(q_scale)

    @pl.when(part == 1)
    def _():
        rope(1.0)

    @pl.when(part == 2)
    def _():
        for b in range(n_blocks):
            y_ref[b] = x_ref[:, b * LANES:(b + 1) * LANES]

    for (_, dil), o_ref in zip(DILATED_CFGS, out_refs):
        for r in range(dil):
            for b in range(n_blocks):
                rows = y_ref[b, pl.ds(r, tm // dil, stride=dil), :]
                o_ref[0, r, :, b * LANES:(b + 1) * LANES] = rows.astype(BF16)


def _even_prep(proj, batch, seq, q_scale, tm=256):
    a_w = A_HEADS * HEAD_DIM
    cos_t, sin_t = _rope_tables(seq, HEAD_DIM)
    spb = seq // tm
    kern = functools.partial(_even_prep_kernel, q_scale=q_scale, n_blocks=a_w // LANES)
    return pl.pallas_call(
        kern,
        grid=(batch * spb, 3),
        in_specs=[pl.BlockSpec((tm, a_w), lambda i, p: (i, p)),
                  pl.BlockSpec((tm, LANES), lambda i, p: (i % spb, 0)),
                  pl.BlockSpec((tm, LANES), lambda i, p: (i % spb, 0))],
        out_specs=[pl.BlockSpec((1, dil, tm // dil, a_w), lambda i, p: (i // spb, 0, i % spb, p))
                   for _, dil in DILATED_CFGS],
        out_shape=[jax.ShapeDtypeStruct((batch, dil, seq // dil, 3 * a_w), BF16) for _, dil in DILATED_CFGS],
        scratch_shapes=[pltpu.VMEM((a_w // LANES, tm, LANES), F32)],
        compiler_params=_params(("parallel", "arbitrary"), 32),
        name="even_prep",
    )(proj, cos_t, sin_t)


def _band_attn_kernel(q_ref, kp_ref, kc_ref, kn_ref, vp_ref, vc_ref, vn_ref, o_ref, lse_ref, *, bq, sub_len):
    q0 = pl.program_id(1) * bq
    win = bq + 2 * BAND_RADIUS
    qpos = q0 + lax.broadcasted_iota(jnp.int32, (bq, win), 0)
    kpos = q0 - BAND_RADIUS + lax.broadcasted_iota(jnp.int32, (bq, win), 1)
    rel = kpos - qpos
    ok = (rel <= BAND_RADIUS) & (rel >= -BAND_RADIUS) & (kpos >= 0) & (kpos < sub_len)
    bias = jnp.where(ok, 0.0, NEG_BIG)
    for h in range(A_HEADS):
        cols = slice(h * HEAD_DIM, (h + 1) * HEAD_DIM)
        kw = jnp.concatenate([kp_ref[0, :, cols], kc_ref[0, :, cols], kn_ref[0, :, cols]], axis=0)
        vw = jnp.concatenate([vp_ref[0, :, cols], vc_ref[0, :, cols], vn_ref[0, :, cols]], axis=0)
        s = _dot_nt(q_ref[0, :, cols], kw) + bias
        m = jnp.max(s, axis=-1, keepdims=True)
        p = jnp.exp(s - m)
        l = jnp.sum(p, axis=-1, keepdims=True)
        o = jnp.dot(p.astype(BF16), vw, preferred_element_type=F32)
        o_ref[0, :, cols] = o / l
        lse_ref[0, :, cols] = jnp.broadcast_to(m + jnp.log(l), (bq, HEAD_DIM))


def _band_attention(qkv, bq=256):
    n_seq, sub_len, _ = qkv.shape
    a_w = A_HEADS * HEAD_DIM
    hb = bq // BAND_RADIUS
    last = sub_len // BAND_RADIUS - 1
    kern = functools.partial(_band_attn_kernel, bq=bq, sub_len=sub_len)

    def halo_specs(col):
        return [pl.BlockSpec((1, BAND_RADIUS, a_w), lambda s, i: (s, jnp.maximum(i * hb - 1, 0), col)),
                pl.BlockSpec((1, bq, a_w), lambda s, i: (s, i, col)),
                pl.BlockSpec((1, BAND_RADIUS, a_w), lambda s, i: (s, jnp.minimum((i + 1) * hb, last), col))]

    out_spec = pl.BlockSpec((1, bq, a_w), lambda s, i: (s, i, 0))
    return pl.pallas_call(
        kern,
        grid=(n_seq, sub_len // bq),
        in_specs=[pl.BlockSpec((1, bq, a_w), lambda s, i: (s, i, 0))] + halo_specs(1) + halo_specs(2),
        out_specs=[out_spec, out_spec],
        out_shape=[jax.ShapeDtypeStruct((n_seq, sub_len, a_w), F32)] * 2,
        compiler_params=_params(("parallel", "parallel"), 40),
        name="band_attn",
    )(*([qkv] * 7))


def _combine_kernel(o1, l1, o4, l4, o16, l16, y_ref, so4, sl4, so16, sl16):
    tm = y_ref.shape[0]
    for dil, src, dst in ((4, o4, so4), (4, l4, sl4), (16, o16, so16), (16, l16, sl16)):
        for r in range(dil):
            for h in range(A_HEADS):
                dst[h, pl.ds(r, tm // dil, stride=dil), :] = src[0, r, :, h * HEAD_DIM:(h + 1) * HEAD_DIM]
    for h in range(A_HEADS):
        cols = slice(h * HEAD_DIM, (h + 1) * HEAD_DIM)
        a, b, c = l1[0, 0, :, cols], sl4[h], sl16[h]
        m = jnp.maximum(jnp.maximum(a, b), c)
        wa, wb, wc = jnp.exp(a - m), jnp.exp(b - m), jnp.exp(c - m)
        num = wa * o1[0, 0, :, cols] + wb * so4[h] + wc * so16[h]
        y_ref[:, cols] = (num / (wa + wb + wc)).astype(y_ref.dtype)


def _combine_configs(parts, batch, seq, tm=256):
    a_w = A_HEADS * HEAD_DIM
    spb = seq // tm
    flat, specs = [], []
    for (_, dil), pair in zip(DILATED_CFGS, parts):
        for arr in pair:
            flat.append(arr.reshape(batch, dil, seq // dil, a_w))
            specs.append(pl.BlockSpec((1, dil, tm // dil, a_w), lambda i: (i // spb, 0, i % spb, 0)))
    return pl.pallas_call(
        _combine_kernel,
        grid=(batch * spb,),
        in_specs=specs,
        out_specs=pl.BlockSpec((tm, a_w), lambda i: (i, 0)),
        out_shape=jax.ShapeDtypeStruct((batch * seq, a_w), BF16),
        scratch_shapes=[pltpu.VMEM((A_HEADS, tm, HEAD_DIM), F32)] * 4,
        compiler_params=_params(("parallel",), 40),
        name="dilated_combine",
    )(*flat)


def _gating_kernel(u_ref, v_ref, w_ref, b_ref, g_ref, o_ref, *, n_chunks):
    w = w_ref[0]
    bias = b_ref[0]
    gain = g_ref[0]
    for c in range(n_chunks):
        rows = pl.ds(c * CHUNK, CHUNK)
        v = jax.nn.gelu(v_ref[rows, :])
        mu = jnp.mean(v, axis=-1, keepdims=True)
        d = v - mu
        var = jnp.mean(d * d, axis=-1, keepdims=True)
        vn = d * lax.rsqrt(var + EPS) * gain
        sv = jnp.dot(w, vn.astype(BF16), preferred_element_type=F32) + bias
        o_ref[rows, :] = (jax.nn.gelu(u_ref[rows, :]) * sv).astype(o_ref.dtype)


def _spatial_gating(proj, w_s, b_s, g_ln, u_col, v_col, tm=512):
    t = proj.shape[0]
    groups = w_s.shape[0]
    bias = jnp.broadcast_to(b_s[:, :, None], (groups, CHUNK, LANES))
    kern = functools.partial(_gating_kernel, n_chunks=tm // CHUNK)
    return pl.pallas_call(
        kern,
        grid=(t // tm, groups),
        in_specs=[pl.BlockSpec((tm, LANES), lambda i, g: (i, u_col + g)),
                  pl.BlockSpec((tm, LANES), lambda i, g: (i, v_col + g)),
                  pl.BlockSpec((1, CHUNK, CHUNK), lambda i, g: (g, 0, 0)),
                  pl.BlockSpec((1, CHUNK, LANES), lambda i, g: (g, 0, 0)),
                  pl.BlockSpec((1, 1, LANES), lambda i, g: (g, 0, 0))],
        out_specs=pl.BlockSpec((tm, LANES), lambda i, g: (i, g)),
        out_shape=jax.ShapeDtypeStruct((t, groups * LANES), BF16),
        compiler_params=_params(("parallel", "parallel"), 32),
        name="spatial_gating",
    )(proj, proj, w_s.astype(BF16), bias, g_ln.reshape(groups, 1, LANES))


V_ROWS = 2 * C_HALF + 16
DIFF_BK = 512


def _diff_prep_kernel(x_ref, c_ref, s_ref, qt_ref, k_ref, vt_ref, *, q_scale, n_blocks):
    part = pl.program_id(1)
    tm = x_ref.shape[0]

    @pl.when(part == 0)
    def _():
        c = c_ref[...]
        s = s_ref[...]
        for h in range(n_blocks):
            cols = slice(h * LANES, (h + 1) * LANES)
            qt_ref[0, h] = (_rope_block(x_ref[:, cols], c, s, C_HALF) * q_scale).T.astype(BF16)

    @pl.when(part == 1)
    def _():
        c = c_ref[...]
        s = s_ref[...]
        for h in range(n_blocks):
            cols = slice(h * LANES, (h + 1) * LANES)
            k_ref[:, cols] = _rope_block(x_ref[:, cols], c, s, C_HALF).astype(BF16)

    @pl.when(part == 2)
    def _():
        ones = jnp.ones((V_ROWS - LANES, tm), BF16)
        for h in range(n_blocks):
            vt_ref[0, h, 0, 0:LANES, :] = x_ref[:, h * LANES:(h + 1) * LANES].T.astype(BF16)
            vt_ref[0, h, 0, LANES:, :] = ones


def _diff_prep(proj, batch, seq, q_scale, bk, tm=256):
    c_w = C_HEADS * 2 * C_HALF
    cos_t, sin_t = _rope_tables(seq, C_HALF)
    spb = seq // tm
    per_chunk = bk // tm
    kern = functools.partial(_diff_prep_kernel, q_scale=q_scale, n_blocks=C_HEADS)
    return pl.pallas_call(
        kern,
        grid=(batch * spb, 3),
        in_specs=[pl.BlockSpec((tm, c_w), lambda i, p: (i, p)),
                  pl.BlockSpec((tm, LANES), lambda i, p: (i % spb, 0)),
                  pl.BlockSpec((tm, LANES), lambda i, p: (i % spb, 0))],
        out_specs=[pl.BlockSpec((1, C_HEADS, LANES, tm), lambda i, p: (i // spb, 0, 0, i % spb)),
                   pl.BlockSpec((tm, c_w), lambda i, p: (i, 0)),
                   pl.BlockSpec((1, C_HEADS, 1, V_ROWS, tm),
                                lambda i, p: (i // spb, 0, (i % spb) // per_chunk, 0, (i % spb) % per_chunk))],
        out_shape=[jax.ShapeDtypeStruct((batch, C_HEADS, LANES, seq), BF16),
                   jax.ShapeDtypeStruct((batch * seq, c_w), BF16),
                   jax.ShapeDtypeStruct((batch, C_HEADS, seq // bk, V_ROWS, bk), BF16)],
        compiler_params=_params(("parallel", "arbitrary"), 40),
        name="diff_prep",
    )(proj, cos_t, sin_t)


def _diff_attn_kernel(qt_ref, k_ref, vt_ref, lam_ref, g_ref, o_ref, s_ref, mc_ref, m_ref, acc_ref,
                      *, bq, bk, seq, lam_init, unroll):
    n = seq // bk
    sub = 8
    qt = qt_ref[0, 0]
    row = lax.broadcasted_iota(jnp.int32, qt.shape, 0)
    zero = jnp.zeros_like(qt)
    qs = (jnp.where(row < C_HALF, qt, zero), jnp.where(row < C_HALF, zero, qt))

    def scores(j, slot):
        kc = k_ref[0, pl.ds(pl.multiple_of(j * bk, bk), bk), :]
        for br in range(2):
            s = jnp.dot(kc, qs[br], preferred_element_type=F32)
            s_ref[slot, br] = s
            mc = jnp.max(s.reshape(bk // sub, sub, bq), axis=0)
            mc_ref[slot, br] = jnp.broadcast_to(jnp.max(mc, axis=0, keepdims=True), (sub, bq))

    def accumulate(j, slot):
        vt = vt_ref[0, 0, j]
        for br in range(2):
            m_old = m_ref[br]
            m_new = jnp.maximum(m_old, mc_ref[slot, br])
            alpha = jnp.exp2(m_old - m_new)
            m_ref[br] = m_new
            s3 = s_ref[slot, br].reshape(bk // sub, sub, bq)
            p = jnp.exp2(s3 - m_new[None]).reshape(bk, bq).astype(BF16)
            pv = jnp.dot(vt, p, preferred_element_type=F32)
            acc3 = acc_ref[br].reshape(V_ROWS // sub, sub, bq)
            acc_ref[br] = (alpha[None] * acc3).reshape(V_ROWS, bq) + pv

    m_ref[...] = jnp.full(m_ref.shape, NEG_BIG, F32)
    acc_ref[...] = jnp.zeros(acc_ref.shape, F32)
    scores(0, 0)

    def body(i, carry):
        j = unroll * i
        for u in range(unroll):
            scores(jnp.minimum(j + u + 1, n - 1), (u + 1) % 2)
            accumulate(j + u, u % 2)
        return carry

    lax.fori_loop(0, n // unroll, body, 0)

    lam = lam_ref[...]
    lam_full = (jnp.exp(jnp.sum(lam[0:1] * lam[1:2], keepdims=True))
                - jnp.exp(jnp.sum(lam[2:3] * lam[3:4], keepdims=True)) + lam_init)

    def normalised(a):
        num = a[0:LANES].reshape(LANES // sub, sub, bq)
        return (num / a[LANES:LANES + sub][None]).reshape(LANES, bq)

    o = normalised(acc_ref[0]) - lam_full * normalised(acc_ref[1])
    y = o * lax.rsqrt(jnp.mean(o * o, axis=0, keepdims=True) + EPS) * g_ref[...] * (1.0 - lam_init)
    o_ref[0] = y.T.astype(o_ref.dtype)


def _diff_attention(qt, k, vt, lam, subln_g, batch, seq, layer_idx, bq=512, bk=512, unroll=4):
    c_w = C_HEADS * 2 * C_HALF
    lam_init = 0.8 - 0.6 * math.exp(-0.3 * layer_idx)
    n = seq // bk
    assert unroll % 2 == 0 and n % unroll == 0 and vt.shape[2] == n
    kern = functools.partial(_diff_attn_kernel, bq=bq, bk=bk, seq=seq, lam_init=lam_init, unroll=unroll)
    out = pl.pallas_call(
        kern,
        grid=(batch, C_HEADS, seq // bq),
        in_specs=[pl.BlockSpec((1, 1, LANES, bq), lambda b, h, i: (b, h, 0, i)),
                  pl.BlockSpec((1, seq, LANES), lambda b, h, i: (b, 0, h)),
                  pl.BlockSpec((1, 1, n, V_ROWS, bk), lambda b, h, i: (b, h, 0, 0, 0)),
                  pl.BlockSpec((4, C_HALF), lambda b, h, i: (0, 0)),
                  pl.BlockSpec((LANES, bq), lambda b, h, i: (0, 0))],
        out_specs=pl.BlockSpec((1, bq, LANES), lambda b, h, i: (b, i, h)),
        out_shape=jax.ShapeDtypeStruct((batch, seq, c_w), BF16),
        scratch_shapes=[pltpu.VMEM((2, 2, bk, bq), F32),
                        pltpu.VMEM((2, 2, 8, bq), F32),
                        pltpu.VMEM((2, 8, bq), F32),
                        pltpu.VMEM((2, V_ROWS, bq), F32)],
        compiler_params=_params(("parallel", "parallel", "arbitrary"), 48),
        name="diff_attn",
    )(qt, k.reshape(batch, seq, c_w), vt, lam, jnp.broadcast_to(subln_g[:, None], (LANES, bq)))
    return out.reshape(batch * seq, c_w)


def _fourier_wprep_kernel(c_ref, s_ref, w_ref, o_ref, *, scale):
    w = w_ref[0]
    mr = jnp.dot(c_ref[...], w, preferred_element_type=F32, precision=lax.Precision.HIGHEST)
    mi = jnp.dot(s_ref[...], w, preferred_element_type=F32, precision=lax.Precision.HIGHEST)
    o_ref[0, :, :LANES] = (mr * scale).astype(BF16)
    o_ref[0, :, LANES:] = (-mi * scale).astype(BF16)


def _fourier_weights(w_f, seq):
    groups, c, _ = w_f.shape
    idx = jnp.arange(c, dtype=jnp.int32)
    ang = (2.0 * math.pi / c) * ((idx[:, None] * idx[None, :]) % c).astype(F32)
    kern = functools.partial(_fourier_wprep_kernel, scale=1.0 / math.sqrt(seq * c))
    return pl.pallas_call(
        kern,
        grid=(groups,),
        in_specs=[pl.BlockSpec((c, c), lambda g: (0, 0)),
                  pl.BlockSpec((c, c), lambda g: (0, 0)),
                  pl.BlockSpec((1, c, c), lambda g: (g, 0, 0))],
        out_specs=pl.BlockSpec((1, c, 2 * c), lambda g: (g, 0, 0)),
        out_shape=jax.ShapeDtypeStruct((groups, c, 2 * c), BF16),
        compiler_params=_params(("parallel",), 32),
        name="fourier_wprep",
    )(jnp.cos(ang), jnp.sin(ang), w_f)


def _chan_mix_kernel(z_ref, m_ref, o_ref):
    o_ref[...] = jnp.dot(z_ref[...].astype(BF16), m_ref[0], preferred_element_type=F32).astype(BF16)


def _fourier_channel_mix(proj, m_c, z_col, tm=512):
    t = proj.shape[0]
    groups = m_c.shape[0]
    return pl.pallas_call(
        _chan_mix_kernel,
        grid=(t // tm, groups),
        in_specs=[pl.BlockSpec((tm, LANES), lambda i, g: (i, z_col + g)),
                  pl.BlockSpec((1, LANES, 2 * LANES), lambda i, g: (g, 0, 0))],
        out_specs=pl.BlockSpec((tm, 2 * LANES), lambda i, g: (i, g)),
        out_shape=jax.ShapeDtypeStruct((t, groups * 2 * LANES), BF16),
        compiler_params=_params(("parallel", "parallel"), 32),
        name="fourier_channel_mix",
    )(proj, m_c)


def _dft_stage1_kernel(u_ref, fr_ref, fi_ref, y_ref, *, pairs):
    u = u_ref[0]
    p = jnp.dot(fr_ref[...], u, preferred_element_type=F32)
    q = jnp.dot(fi_ref[...], u, preferred_element_type=F32)
    for j in range(pairs):
        re = slice(2 * j * LANES, (2 * j + 1) * LANES)
        im = slice((2 * j + 1) * LANES, (2 * j + 2) * LANES)
        y_ref[0, :, re] = (p[:, re] - q[:, im]).astype(BF16)
        y_ref[0, :, im] = (q[:, re] + p[:, im]).astype(BF16)


def _dft_stage2_kernel(y_ref, gc_ref, gs_ref, o_ref, *, groups):
    y = y_ref[0, 0]
    a = jnp.dot(gc_ref[0], y, preferred_element_type=F32)
    b = jnp.dot(gs_ref[0], y, preferred_element_type=F32)
    for g in range(groups):
        re = slice(2 * g * LANES, (2 * g + 1) * LANES)
        im = slice((2 * g + 1) * LANES, (2 * g + 2) * LANES)
        o_ref[0, :, g * LANES:(g + 1) * LANES] = (a[:, re] + b[:, im]).astype(o_ref.dtype)


def _seq_dft_real(u, batch, seq, groups, tc=4096):
    n1 = 128 if seq >= 16384 else 64
    n2 = seq // n1
    wc = groups * 2 * LANES
    i1 = jnp.arange(n1, dtype=jnp.int32)
    ang1 = (2.0 * math.pi / n1) * ((i1[:, None] * i1[None, :]) % n1).astype(F32)
    fr = jnp.cos(ang1).astype(BF16)
    fi = (-jnp.sin(ang1)).astype(BF16)
    cols = n2 * wc
    tc = min(tc, cols)
    y = pl.pallas_call(
        functools.partial(_dft_stage1_kernel, pairs=tc // (2 * LANES)),
        grid=(batch, cols // tc),
        in_specs=[pl.BlockSpec((1, n1, tc), lambda b, j: (b, 0, j)),
                  pl.BlockSpec((n1, n1), lambda b, j: (0, 0)),
                  pl.BlockSpec((n1, n1), lambda b, j: (0, 0))],
        out_specs=pl.BlockSpec((1, n1, tc), lambda b, j: (b, 0, j)),
        out_shape=jax.ShapeDtypeStruct((batch, n1, cols), BF16),
        compiler_params=_params(("parallel", "parallel"), 32),
        name="dft_stage1",
    )(u.reshape(batch, n1, cols), fr, fi)
    i2 = jnp.arange(n2, dtype=jnp.int32)
    tw = (i2[None, None, :] * (i1[:, None, None] + n1 * i2[None, :, None])) % seq
    ang2 = (2.0 * math.pi / seq) * tw.astype(F32)
    gc = jnp.cos(ang2).astype(BF16)
    gs = jnp.sin(ang2).astype(BF16)
    out = pl.pallas_call(
        functools.partial(_dft_stage2_kernel, groups=groups),
        grid=(batch, n1),
        in_specs=[pl.BlockSpec((1, 1, n2, wc), lambda b, k: (b, k, 0, 0)),
                  pl.BlockSpec((1, n2, n2), lambda b, k: (k, 0, 0)),
                  pl.BlockSpec((1, n2, n2), lambda b, k: (k, 0, 0))],
        out_specs=pl.BlockSpec((1, n2, groups * LANES), lambda b, k: (b, 0, k)),
        out_shape=jax.ShapeDtypeStruct((batch, n2, n1 * groups * LANES), BF16),
        compiler_params=_params(("parallel", "parallel"), 32),
        name="dft_stage2",
    )(y.reshape(batch, n1, n2, wc), gc, gs)
    return out.reshape(batch * seq, groups * LANES)


def _cross_attn_kernel(q_ref, kv_ref, o_ref, *, heads, scale):
    ca_w = heads * HEAD_DIM
    for h in range(heads):
        cols = slice(h * HEAD_DIM, (h + 1) * HEAD_DIM)
        q = q_ref[0, :, cols]
        k = kv_ref[0, :, cols]
        v = kv_ref[0, :, ca_w + h * HEAD_DIM:ca_w + (h + 1) * HEAD_DIM]
        s = _dot_nt(q, k) * scale
        m = jnp.max(s, axis=-1, keepdims=True)
        p = jnp.exp(s - m)
        l = jnp.sum(p, axis=-1, keepdims=True)
        o = jnp.dot(p.astype(BF16), v, preferred_element_type=F32) / l
        o_ref[0, :, cols] = o.astype(o_ref.dtype)


def _cross_attention(q, kv, batch, seq, tm=512):
    ca_w = CA_HEADS * HEAD_DIM
    mem_len = kv.shape[0] // batch
    kern = functools.partial(_cross_attn_kernel, heads=CA_HEADS, scale=HEAD_DIM ** -0.5)
    out = pl.pallas_call(
        kern,
        grid=(batch, seq // tm),
        in_specs=[pl.BlockSpec((1, tm, ca_w), lambda b, i: (b, i, 0)),
                  pl.BlockSpec((1, mem_len, 2 * ca_w), lambda b, i: (b, 0, 0))],
        out_specs=pl.BlockSpec((1, tm, ca_w), lambda b, i: (b, i, 0)),
        out_shape=jax.ShapeDtypeStruct((batch, seq, ca_w), BF16),
        compiler_params=_params(("parallel", "parallel"), 32),
        name="cross_attn",
    )(q.reshape(batch, seq, ca_w), kv.reshape(batch, mem_len, 2 * ca_w))
    return out.reshape(batch * seq, ca_w)


HALO = 16


def _ffn_kernel(x_ref, xp_ref, xn_ref, g4_ref, wg_ref, wv_ref, cwg_ref, cwv_ref, cbg_ref, cbv_ref,
                wd_ref, g5_ref, o_ref, xe_ref, zg_ref, zv_ref, acc_ref, *, tm, tiles_per_seq, nf):
    i = pl.program_id(0)
    f = pl.program_id(1)

    @pl.when(f == 0)
    def _():
        g4 = g4_ref[...]
        pos = i % tiles_per_seq
        prev = _rms(xp_ref[...], g4) * jnp.where(pos == 0, 0.0, 1.0)
        nxt = _rms(xn_ref[...], g4) * jnp.where(pos == tiles_per_seq - 1, 0.0, 1.0)
        xe_ref[0:HALO, :] = prev.astype(BF16)
        xe_ref[HALO:HALO + tm, :] = _rms(x_ref[...], g4).astype(BF16)
        xe_ref[HALO + tm:, :] = nxt.astype(BF16)

    xe = xe_ref[...]
    zg_ref[...] = jnp.dot(xe, wg_ref[...], preferred_element_type=F32)
    zv_ref[...] = jnp.dot(xe, wv_ref[...], preferred_element_type=F32)

    def conv(z_ref, cw_ref, cb_ref):
        cw = cw_ref[...]
        return (z_ref[HALO - 1:HALO - 1 + tm, :] * cw[0:1] + z_ref[HALO:HALO + tm, :] * cw[1:2]
                + z_ref[HALO + 1:HALO + 1 + tm, :] * cw[2:3] + cb_ref[...])

    h = jax.nn.gelu(conv(zg_ref, cwg_ref, cbg_ref)) * conv(zv_ref, cwv_ref, cbv_ref)
    part = jnp.dot(h.astype(BF16), wd_ref[...], preferred_element_type=F32)

    @pl.when(f == 0)
    def _():
        acc_ref[...] = part

    @pl.when(f > 0)
    def _():
        acc_ref[...] += part

    @pl.when(f == nf - 1)
    def _():
        o_ref[...] = x_ref[...] + _rms(acc_ref[...], g5_ref[...])


def _conv_ffn(x, seq, g4, w_up, conv_w, conv_b, w_down, g5, tm=512, tf=512):
    t, d = x.shape
    d_ff = w_down.shape[0]
    nf = d_ff // tf
    hb = tm // HALO
    last_hb = t // HALO - 1
    kern = functools.partial(_ffn_kernel, tm=tm, tiles_per_seq=seq // tm, nf=nf)
    cb = conv_b.reshape(1, 2 * d_ff)
    return pl.pallas_call(
        kern,
        grid=(t // tm, nf),
        in_specs=[pl.BlockSpec((tm, d), lambda i, f: (i, 0)),
                  pl.BlockSpec((HALO, d), lambda i, f: (jnp.maximum(i * hb - 1, 0), 0)),
                  pl.BlockSpec((HALO, d), lambda i, f: (jnp.minimum((i + 1) * hb, last_hb), 0)),
                  pl.BlockSpec((1, d), lambda i, f: (0, 0)),
                  pl.BlockSpec((d, tf), lambda i, f: (0, f)),
                  pl.BlockSpec((d, tf), lambda i, f: (0, nf + f)),
                  pl.BlockSpec((CONV_W, tf), lambda i, f: (0, f)),
                  pl.BlockSpec((CONV_W, tf), lambda i, f: (0, nf + f)),
                  pl.BlockSpec((1, tf), lambda i, f: (0, f)),
                  pl.BlockSpec((1, tf), lambda i, f: (0, nf + f)),
                  pl.BlockSpec((tf, d), lambda i, f: (f, 0)),
                  pl.BlockSpec((1, d), lambda i, f: (0, 0))],
        out_specs=pl.BlockSpec((tm, d), lambda i, f: (i, 0)),
        out_shape=jax.ShapeDtypeStruct((t, d), F32),
        scratch_shapes=[pltpu.VMEM((tm + 2 * HALO, d), BF16),
                        pltpu.VMEM((tm + 2 * HALO, tf), F32),
                        pltpu.VMEM((tm + 2 * HALO, tf), F32),
                        pltpu.VMEM((tm, d), F32)],
        compiler_params=_params(("parallel", "arbitrary"), 52),
        name="conv_ffn",
    )(x, x, x, g4.reshape(1, d), w_up, w_up, conv_w, conv_w, cb, cb, w_down, g5.reshape(1, d))


def _trunk(x3, mem3, p):
    batch, seq, d = x3.shape
    x = x3.reshape(batch * seq, d)
    mem = mem3.reshape(-1, d)
    depth = p["w_in"].shape[0]
    a_w = A_HEADS * HEAD_DIM
    c_w = C_HEADS * 2 * C_HALF
    for layer in range(depth):
        g = p["norm_gains"][layer]
        proj = _norm_matmul(x, g[0], p["w_in"][layer], F32)
        if layer % 2 == 0:
            e = layer // 2
            qkvs = _even_prep(proj, batch, seq, HEAD_DIM ** -0.5)
            parts = [_band_attention(qkv.reshape(batch * dil, seq // dil, 3 * a_w))
                     for (_, dil), qkv in zip(DILATED_CFGS, qkvs)]
            ya = _combine_configs(parts, batch, seq)
            yb = _spatial_gating(proj, p["b_w_spatial"][e], p["b_b_spatial"][e], p["b_ln_gain"][e],
                                 3 * A_HEADS, 3 * A_HEADS + B_GROUPS)
            mix = jnp.concatenate([ya, yb], axis=-1)
        else:
            o = layer // 2
            qt, kk, vt = _diff_prep(proj, batch, seq, C_HALF ** -0.5 * math.log2(math.e), DIFF_BK)
            yc = _diff_attention(qt, kk, vt, p["c_lambda"][o], p["c_subln_gain"][o], batch, seq, layer,
                                 bk=DIFF_BK)
            m_c = _fourier_weights(p["d_w_fourier"][o], seq)
            u = _fourier_channel_mix(proj, m_c, 3 * c_w // LANES)
            yd = _seq_dft_real(u, batch, seq, D_GROUPS)
            mix = jnp.concatenate([yc, yd], axis=-1)
        x = _matmul_norm_res(mix, p["w_out"][layer], g[1], x)
        q = _norm_matmul(x, g[2], p["ca_w_q"][layer], BF16)
        kv = _norm_matmul(mem, p["mem_norm_gain"][layer], p["ca_w_kv"][layer], BF16)
        ca = _cross_attention(q, kv, batch, seq)
        x = _matmul_norm_res(ca, p["ca_w_o"][layer], g[3], x)
        x = _conv_ffn(x, seq, g[4], p["ffn_w_up"][layer], p["ffn_conv_w"][layer], p["ffn_conv_b"][layer],
                      p["ffn_w_down"][layer], g[5])
    return x.reshape(batch, seq, d)


def kernel(x_prompt, x_sample, mem_prompt, mem_sample, norm_gains, w_in, w_out, b_w_spatial, b_b_spatial,
           b_ln_gain, c_lambda, c_subln_gain, d_w_fourier, mem_norm_gain, ca_w_q, ca_w_kv, ca_w_o, ffn_w_up,
           ffn_conv_w, ffn_conv_b, ffn_w_down):
    p = dict(norm_gains=norm_gains, w_in=w_in.astype(BF16), w_out=w_out.astype(BF16), b_w_spatial=b_w_spatial,
             b_b_spatial=b_b_spatial, b_ln_gain=b_ln_gain, c_lambda=c_lambda, c_subln_gain=c_subln_gain,
             d_w_fourier=d_w_fourier, mem_norm_gain=mem_norm_gain, ca_w_q=ca_w_q.astype(BF16),
             ca_w_kv=ca_w_kv.astype(BF16), ca_w_o=ca_w_o.astype(BF16), ffn_w_up=ffn_w_up.astype(BF16),
             ffn_conv_w=ffn_conv_w, ffn_conv_b=ffn_conv_b, ffn_w_down=ffn_w_down.astype(BF16))
    return _trunk(x_prompt, mem_prompt, p), _trunk(x_sample, mem_sample, p)
```

```python
import functools
import math

import jax
import jax.numpy as jnp
from jax import lax
from jax.experimental import pallas as pl
from jax.experimental.pallas import tpu as pltpu

F32 = jnp.float32
BF16 = jnp.bfloat16

EPS = 1e-6
ROPE_THETA = 500000.0
ROPE_FRACTION = 4
LANES = 128
HEAD_DIM = 128
A_HEADS = 8
DILATED_CFGS = ((128, 1), (512, 4), (2048, 16))
BAND_RADIUS = 64
B_GROUPS = 8
CHUNK = 128
C_HEADS = 12
C_HALF = 64
D_GROUPS = 4
CA_HEADS = 4
CONV_W = 3
NEG_BIG = -1e30
MIB = 1024 * 1024


def _params(semantics, vmem_mib):
    return pltpu.CompilerParams(dimension_semantics=semantics, vmem_limit_bytes=vmem_mib * MIB)


def _rms(x, g):
    return x * lax.rsqrt(jnp.mean(x * x, axis=-1, keepdims=True) + EPS) * g


def _dot_nt(a, b):
    return lax.dot_general(a, b, (((1,), (1,)), ((), ())), preferred_element_type=F32)


def _norm_mm_kernel(x_ref, g_ref, w_ref, o_ref, xn_ref):
    @pl.when(pl.program_id(1) == 0)
    def _():
        xn_ref[...] = _rms(x_ref[...], g_ref[...]).astype(BF16)

    o_ref[...] = jnp.dot(xn_ref[...], w_ref[...], preferred_element_type=F32).astype(o_ref.dtype)


def _norm_matmul(x, g, w, out_dtype, tm=1024, tn=512):
    t, k = x.shape
    n = w.shape[1]
    tm = min(tm, t)
    return pl.pallas_call(
        _norm_mm_kernel,
        grid=(t // tm, n // tn),
        in_specs=[pl.BlockSpec((tm, k), lambda i, j: (i, 0)),
                  pl.BlockSpec((1, k), lambda i, j: (0, 0)),
                  pl.BlockSpec((k, tn), lambda i, j: (0, j))],
        out_specs=pl.BlockSpec((tm, tn), lambda i, j: (i, j)),
        out_shape=jax.ShapeDtypeStruct((t, n), out_dtype),
        scratch_shapes=[pltpu.VMEM((tm, k), BF16)],
        compiler_params=_params(("parallel", "arbitrary"), 40),
        name="norm_matmul",
    )(x, g.reshape(1, k), w)


def _mm_norm_res_kernel(a_ref, w_ref, g_ref, r_ref, o_ref):
    y = jnp.dot(a_ref[...], w_ref[...], preferred_element_type=F32)
    o_ref[...] = r_ref[...] + _rms(y, g_ref[...])


def _matmul_norm_res(a, w, g, res, tm=512):
    t, k = a.shape
    n = w.shape[1]
    return pl.pallas_call(
        _mm_norm_res_kernel,
        grid=(t // tm,),
        in_specs=[pl.BlockSpec((tm, k), lambda i: (i, 0)),
                  pl.BlockSpec((k, n), lambda i: (0, 0)),
                  pl.BlockSpec((1, n), lambda i: (0, 0)),
                  pl.BlockSpec((tm, n), lambda i: (i, 0))],
        out_specs=pl.BlockSpec((tm, n), lambda i: (i, 0)),
        out_shape=jax.ShapeDtypeStruct((t, n), F32),
        compiler_params=_params(("parallel",), 48),
        name="matmul_norm_res",
    )(a, w, g.reshape(1, n), res)


def _rope_tables(seq, head_w):
    rd = head_w // ROPE_FRACTION
    half = rd // 2
    inv = ROPE_THETA ** (-jnp.arange(half, dtype=F32) / half)
    ang = jnp.arange(seq, dtype=F32)[:, None] * inv[None, :]
    cos = jnp.cos(ang)
    sin = jnp.sin(ang)
    rest = head_w - rd
    c_head = jnp.concatenate([cos, cos, jnp.ones((seq, rest), F32)], axis=-1)
    s_head = jnp.concatenate([-sin, sin, jnp.zeros((seq, rest), F32)], axis=-1)
    reps = LANES // head_w
    return jnp.tile(c_head, (1, reps)), jnp.tile(s_head, (1, reps))


def _rope_block(x, c, s, head_w):
    half = head_w // ROPE_FRACTION // 2
    first = (lax.broadcasted_iota(jnp.int32, x.shape, 1) & (head_w - 1)) < half
    partner = jnp.where(first, pltpu.roll(x, LANES - half, axis=1), pltpu.roll(x, half, axis=1))
    return x * c + partner * s


def _even_prep_kernel(x_ref, c_ref, s_ref, *refs, q_scale, n_blocks):
    out_refs, y_ref = refs[:-1], refs[-1]
    part = pl.program_id(1)
    tm = x_ref.shape[0]

    def rope(scale):
        c = c_ref[...]
        s = s_ref[...]
        for b in range(n_blocks):
            cols = slice(b * LANES, (b + 1) * LANES)
            y_ref[b] = _rope_block(x_ref[:, cols], c, s, HEAD_DIM) * scale

    @pl.when(part == 0)
    def _():
        rope(q_scale)

    @pl.when(part == 1)
    def _():
        rope(1.0)

    @pl.when(part == 2)
    def _():
        for b in range(n_blocks):
            y_ref[b] = x_ref[:, b * LANES:(b + 1) * LANES]

    for (_, dil), o_ref in zip(DILATED_CFGS, out_refs):
        for r in range(dil):
            for b in range(n_blocks):
                rows = y_ref[b, pl.ds(r, tm // dil, stride=dil), :]
                o_ref[0, r, :, b * LANES:(b + 1) * LANES] = rows.astype(BF16)


def _even_prep(proj, batch, seq, q_scale, tm=256):
    a_w = A_HEADS * HEAD_DIM
    cos_t, sin_t = _rope_tables(seq, HEAD_DIM)
    spb = seq // tm
    kern = functools.partial(_even_prep_kernel, q_scale=q_scale, n_blocks=a_w // LANES)
    return pl.pallas_call(
        kern,
        grid=(batch * spb, 3),
        in_specs=[pl.BlockSpec((tm, a_w), lambda i, p: (i, p)),
                  pl.BlockSpec((tm, LANES), lambda i, p: (i % spb, 0)),
                  pl.BlockSpec((tm, LANES), lambda i, p: (i % spb, 0))],
        out_specs=[pl.BlockSpec((1, dil, tm // dil, a_w), lambda i, p: (i // spb, 0, i % spb, p))
                   for _, dil in DILATED_CFGS],
        out_shape=[jax.ShapeDtypeStruct((batch, dil, seq // dil, 3 * a_w), BF16) for _, dil in DILATED_CFGS],
        scratch_shapes=[pltpu.VMEM((a_w // LANES, tm, LANES), F32)],
        compiler_params=_params(("parallel", "arbitrary"), 32),
        name="even_prep",
    )(proj, cos_t, sin_t)


def _band_attn_kernel(q_ref, kp_ref, kc_ref, kn_ref, vp_ref, vc_ref, vn_ref, o_ref, lse_ref, *, bq, sub_len):
    q0 = pl.program_id(1) * bq
    win = bq + 2 * BAND_RADIUS
    qpos = q0 + lax.broadcasted_iota(jnp.int32, (bq, win), 0)
    kpos = q0 - BAND_RADIUS + lax.broadcasted_iota(jnp.int32, (bq, win), 1)
    rel = kpos - qpos
    ok = (rel <= BAND_RADIUS) & (rel >= -BAND_RADIUS) & (kpos >= 0) & (kpos < sub_len)
    bias = jnp.where(ok, 0.0, NEG_BIG)
    def scores(h):
        cols = slice(h * HEAD_DIM, (h + 1) * HEAD_DIM)
        kw = jnp.concatenate([kp_ref[0, :, cols], kc_ref[0, :, cols], kn_ref[0, :, cols]], axis=0)
        return _dot_nt(q_ref[0, :, cols], kw) + bias

    s_next = scores(0)
    for h in range(A_HEADS):
        cols = slice(h * HEAD_DIM, (h + 1) * HEAD_DIM)
        s = s_next
        if h + 1 < A_HEADS:
            s_next = scores(h + 1)
        vw = jnp.concatenate([vp_ref[0, :, cols], vc_ref[0, :, cols], vn_ref[0, :, cols]], axis=0)
        m = jnp.max(s, axis=-1, keepdims=True)
        p = jnp.exp(s - m)
        l = jnp.sum(p, axis=-1, keepdims=True)
        o = jnp.dot(p.astype(BF16), vw, preferred_element_type=F32)
        o_ref[0, :, cols] = o / l
        lse_ref[0, :, cols] = jnp.broadcast_to(m + jnp.log(l), (bq, HEAD_DIM))


def _band_attention(qkv, bq=256):
    n_seq, sub_len, _ = qkv.shape
    a_w = A_HEADS * HEAD_DIM
    hb = bq // BAND_RADIUS
    last = sub_len // BAND_RADIUS - 1
    kern = functools.partial(_band_attn_kernel, bq=bq, sub_len=sub_len)

    def halo_specs(col):
        return [pl.BlockSpec((1, BAND_RADIUS, a_w), lambda s, i: (s, jnp.maximum(i * hb - 1, 0), col)),
                pl.BlockSpec((1, bq, a_w), lambda s, i: (s, i, col)),
                pl.BlockSpec((1, BAND_RADIUS, a_w), lambda s, i: (s, jnp.minimum((i + 1) * hb, last), col))]

    out_spec = pl.BlockSpec((1, bq, a_w), lambda s, i: (s, i, 0))
    return pl.pallas_call(
        kern,
        grid=(n_seq, sub_len // bq),
        in_specs=[pl.BlockSpec((1, bq, a_w), lambda s, i: (s, i, 0))] + halo_specs(1) + halo_specs(2),
        out_specs=[out_spec, out_spec],
        out_shape=[jax.ShapeDtypeStruct((n_seq, sub_len, a_w), F32)] * 2,
        compiler_params=_params(("parallel", "parallel"), 40),
        name="band_attn",
    )(*([qkv] * 7))


def _combine_kernel(o1, l1, o4, l4, o16, l16, y_ref, so4, sl4, so16, sl16):
    tm = y_ref.shape[0]
    for dil, src, dst in ((4, o4, so4), (4, l4, sl4), (16, o16, so16), (16, l16, sl16)):
        for r in range(dil):
            for h in range(A_HEADS):
                dst[h, pl.ds(r, tm // dil, stride=dil), :] = src[0, r, :, h * HEAD_DIM:(h + 1) * HEAD_DIM]
    for h in range(A_HEADS):
        cols = slice(h * HEAD_DIM, (h + 1) * HEAD_DIM)
        a, b, c = l1[0, 0, :, cols], sl4[h], sl16[h]
        m = jnp.maximum(jnp.maximum(a, b), c)
        wa, wb, wc = jnp.exp(a - m), jnp.exp(b - m), jnp.exp(c - m)
        num = wa * o1[0, 0, :, cols] + wb * so4[h] + wc * so16[h]
        y_ref[:, cols] = (num / (wa + wb + wc)).astype(y_ref.dtype)


def _combine_configs(parts, batch, seq, tm=256):
    a_w = A_HEADS * HEAD_DIM
    spb = seq // tm
    flat, specs = [], []
    for (_, dil), pair in zip(DILATED_CFGS, parts):
        for arr in pair:
            flat.append(arr.reshape(batch, dil, seq // dil, a_w))
            specs.append(pl.BlockSpec((1, dil, tm // dil, a_w), lambda i: (i // spb, 0, i % spb, 0)))
    return pl.pallas_call(
        _combine_kernel,
        grid=(batch * spb,),
        in_specs=specs,
        out_specs=pl.BlockSpec((tm, a_w), lambda i: (i, 0)),
        out_shape=jax.ShapeDtypeStruct((batch * seq, a_w), BF16),
        scratch_shapes=[pltpu.VMEM((A_HEADS, tm, HEAD_DIM), F32)] * 4,
        compiler_params=_params(("parallel",), 40),
        name="dilated_combine",
    )(*flat)


def _gating_kernel(u_ref, v_ref, w_ref, b_ref, g_ref, o_ref, *, n_chunks):
    w = w_ref[0]
    bias = b_ref[0]
    gain = g_ref[0]
    for c in range(n_chunks):
        rows = pl.ds(c * CHUNK, CHUNK)
        v = jax.nn.gelu(v_ref[rows, :])
        mu = jnp.mean(v, axis=-1, keepdims=True)
        d = v - mu
        var = jnp.mean(d * d, axis=-1, keepdims=True)
        vn = d * lax.rsqrt(var + EPS) * gain
        sv = jnp.dot(w, vn.astype(BF16), preferred_element_type=F32) + bias
        o_ref[rows, :] = (jax.nn.gelu(u_ref[rows, :]) * sv).astype(o_ref.dtype)


def _spatial_gating(proj, w_s, b_s, g_ln, u_col, v_col, tm=512):
    t = proj.shape[0]
    groups = w_s.shape[0]
    bias = jnp.broadcast_to(b_s[:, :, None], (groups, CHUNK, LANES))
    kern = functools.partial(_gating_kernel, n_chunks=tm // CHUNK)
    return pl.pallas_call(
        kern,
        grid=(t // tm, groups),
        in_specs=[pl.BlockSpec((tm, LANES), lambda i, g: (i, u_col + g)),
                  pl.BlockSpec((tm, LANES), lambda i, g: (i, v_col + g)),
                  pl.BlockSpec((1, CHUNK, CHUNK), lambda i, g: (g, 0, 0)),
                  pl.BlockSpec((1, CHUNK, LANES), lambda i, g: (g, 0, 0)),
                  pl.BlockSpec((1, 1, LANES), lambda i, g: (g, 0, 0))],
        out_specs=pl.BlockSpec((tm, LANES), lambda i, g: (i, g)),
        out_shape=jax.ShapeDtypeStruct((t, groups * LANES), BF16),
        compiler_params=_params(("parallel", "parallel"), 32),
        name="spatial_gating",
    )(proj, proj, w_s.astype(BF16), bias, g_ln.reshape(groups, 1, LANES))


V_ROWS = 2 * C_HALF + 16
DIFF_BK = 512


def _diff_prep_kernel(x_ref, c_ref, s_ref, qt_ref, k_ref, vt_ref, *, q_scale, n_blocks):
    part = pl.program_id(1)
    tm = x_ref.shape[0]

    @pl.when(part == 0)
    def _():
        c = c_ref[...]
        s = s_ref[...]
        for h in range(n_blocks):
            cols = slice(h * LANES, (h + 1) * LANES)
            qt_ref[0, h] = (_rope_block(x_ref[:, cols], c, s, C_HALF) * q_scale).T.astype(BF16)

    @pl.when(part == 1)
    def _():
        c = c_ref[...]
        s = s_ref[...]
        for h in range(n_blocks):
            cols = slice(h * LANES, (h + 1) * LANES)
            k_ref[:, cols] = _rope_block(x_ref[:, cols], c, s, C_HALF).astype(BF16)

    @pl.when(part == 2)
    def _():
        ones = jnp.ones((V_ROWS - LANES, tm), BF16)
        for h in range(n_blocks):
            vt_ref[0, h, 0, 0:LANES, :] = x_ref[:, h * LANES:(h + 1) * LANES].T.astype(BF16)
            vt_ref[0, h, 0, LANES:, :] = ones


def _diff_prep(proj, batch, seq, q_scale, bk, tm=256):
    c_w = C_HEADS * 2 * C_HALF
    cos_t, sin_t = _rope_tables(seq, C_HALF)
    spb = seq // tm
    per_chunk = bk // tm
    kern = functools.partial(_diff_prep_kernel, q_scale=q_scale, n_blocks=C_HEADS)
    return pl.pallas_call(
        kern,
        grid=(batch * spb, 3),
        in_specs=[pl.BlockSpec((tm, c_w), lambda i, p: (i, p)),
                  pl.BlockSpec((tm, LANES), lambda i, p: (i % spb, 0)),
                  pl.BlockSpec((tm, LANES), lambda i, p: (i % spb, 0))],
        out_specs=[pl.BlockSpec((1, C_HEADS, LANES, tm), lambda i, p: (i // spb, 0, 0, i % spb)),
                   pl.BlockSpec((tm, c_w), lambda i, p: (i, 0)),
                   pl.BlockSpec((1, C_HEADS, 1, V_ROWS, tm),
                                lambda i, p: (i // spb, 0, (i % spb) // per_chunk, 0, (i % spb) % per_chunk))],
        out_shape=[jax.ShapeDtypeStruct((batch, C_HEADS, LANES, seq), BF16),
                   jax.ShapeDtypeStruct((batch * seq, c_w), BF16),
                   jax.ShapeDtypeStruct((batch, C_HEADS, seq // bk, V_ROWS, bk), BF16)],
        compiler_params=_params(("parallel", "arbitrary"), 40),
        name="diff_prep",
    )(proj, cos_t, sin_t)


def _diff_attn_kernel(qt_ref, k_ref, vt_ref, lam_ref, g_ref, o_ref, s_ref, mc_ref, m_ref, acc_ref,
                      *, bq, bk, seq, lam_init, unroll):
    n = seq // bk
    sub = 8
    qt = qt_ref[0, 0]
    row = lax.broadcasted_iota(jnp.int32, qt.shape, 0)
    zero = jnp.zeros_like(qt)
    qs = (jnp.where(row < C_HALF, qt, zero), jnp.where(row < C_HALF, zero, qt))

    def scores(j, slot):
        kc = k_ref[0, pl.ds(pl.multiple_of(j * bk, bk), bk), :]
        for br in range(2):
            s = jnp.dot(kc, qs[br], preferred_element_type=F32)
            s_ref[slot, br] = s
            mc = jnp.max(s.reshape(bk // sub, sub, bq), axis=0)
            mc_ref[slot, br] = jnp.broadcast_to(jnp.max(mc, axis=0, keepdims=True), (sub, bq))

    def accumulate(j, slot):
        vt = vt_ref[0, 0, j]
        for br in range(2):
            m_old = m_ref[br]
            m_new = jnp.maximum(m_old, mc_ref[slot, br])
            alpha = jnp.exp2(m_old - m_new)
            m_ref[br] = m_new
            s3 = s_ref[slot, br].reshape(bk // sub, sub, bq)
            p = jnp.exp2(s3 - m_new[None]).reshape(bk, bq).astype(BF16)
            pv = jnp.dot(vt, p, preferred_element_type=F32)
            acc3 = acc_ref[br].reshape(V_ROWS // sub, sub, bq)
            acc_ref[br] = (alpha[None] * acc3).reshape(V_ROWS, bq) + pv

    m_ref[...] = jnp.full(m_ref.shape, NEG_BIG, F32)
    acc_ref[...] = jnp.zeros(acc_ref.shape, F32)
    scores(0, 0)

    def body(i, carry):
        j = unroll * i
        for u in range(unroll):
            scores(jnp.minimum(j + u + 1, n - 1), (u + 1) % 2)
            accumulate(j + u, u % 2)
        return carry

    lax.fori_loop(0, n // unroll, body, 0)

    lam = lam_ref[...]
    lam_full = (jnp.exp(jnp.sum(lam[0:1] * lam[1:2], keepdims=True))
                - jnp.exp(jnp.sum(lam[2:3] * lam[3:4], keepdims=True)) + lam_init)

    def normalised(a):
        num = a[0:LANES].reshape(LANES // sub, sub, bq)
        return (num / a[LANES:LANES + sub][None]).reshape(LANES, bq)

    o = normalised(acc_ref[0]) - lam_full * normalised(acc_ref[1])
    y = o * lax.rsqrt(jnp.mean(o * o, axis=0, keepdims=True) + EPS) * g_ref[...] * (1.0 - lam_init)
    o_ref[0] = y.T.astype(o_ref.dtype)


def _diff_attention(qt, k, vt, lam, subln_g, batch, seq, layer_idx, bq=512, bk=512, unroll=4):
    c_w = C_HEADS * 2 * C_HALF
    lam_init = 0.8 - 0.6 * math.exp(-0.3 * layer_idx)
    n = seq // bk
    assert unroll % 2 == 0 and n % unroll == 0 and vt.shape[2] == n
    kern = functools.partial(_diff_attn_kernel, bq=bq, bk=bk, seq=seq, lam_init=lam_init, unroll=unroll)
    out = pl.pallas_call(
        kern,
        grid=(batch, C_HEADS, seq // bq),
        in_specs=[pl.BlockSpec((1, 1, LANES, bq), lambda b, h, i: (b, h, 0, i)),
                  pl.BlockSpec((1, seq, LANES), lambda b, h, i: (b, 0, h)),
                  pl.BlockSpec((1, 1, n, V_ROWS, bk), lambda b, h, i: (b, h, 0, 0, 0)),
                  pl.BlockSpec((4, C_HALF), lambda b, h, i: (0, 0)),
                  pl.BlockSpec((LANES, bq), lambda b, h, i: (0, 0))],
        out_specs=pl.BlockSpec((1, bq, LANES), lambda b, h, i: (b, i, h)),
        out_shape=jax.ShapeDtypeStruct((batch, seq, c_w), BF16),
        scratch_shapes=[pltpu.VMEM((2, 2, bk, bq), F32),
                        pltpu.VMEM((2, 2, 8, bq), F32),
                        pltpu.VMEM((2, 8, bq), F32),
                        pltpu.VMEM((2, V_ROWS, bq), F32)],
        compiler_params=_params(("parallel", "parallel", "arbitrary"), 48),
        name="diff_attn",
    )(qt, k.reshape(batch, seq, c_w), vt, lam, jnp.broadcast_to(subln_g[:, None], (LANES, bq)))
    return out.reshape(batch * seq, c_w)


def _fourier_wprep_kernel(c_ref, s_ref, w_ref, o_ref, *, scale):
    w = w_ref[0]
    mr = jnp.dot(c_ref[...], w, preferred_element_type=F32, precision=lax.Precision.HIGHEST)
    mi = jnp.dot(s_ref[...], w, preferred_element_type=F32, precision=lax.Precision.HIGHEST)
    o_ref[0, :, :LANES] = (mr * scale).astype(BF16)
    o_ref[0, :, LANES:] = (-mi * scale).astype(BF16)


def _fourier_weights(w_f, seq):
    groups, c, _ = w_f.shape
    idx = jnp.arange(c, dtype=jnp.int32)
    ang = (2.0 * math.pi / c) * ((idx[:, None] * idx[None, :]) % c).astype(F32)
    kern = functools.partial(_fourier_wprep_kernel, scale=1.0 / math.sqrt(seq * c))
    return pl.pallas_call(
        kern,
        grid=(groups,),
        in_specs=[pl.BlockSpec((c, c), lambda g: (0, 0)),
                  pl.BlockSpec((c, c), lambda g: (0, 0)),
                  pl.BlockSpec((1, c, c), lambda g: (g, 0, 0))],
        out_specs=pl.BlockSpec((1, c, 2 * c), lambda g: (g, 0, 0)),
        out_shape=jax.ShapeDtypeStruct((groups, c, 2 * c), BF16),
        compiler_params=_params(("parallel",), 32),
        name="fourier_wprep",
    )(jnp.cos(ang), jnp.sin(ang), w_f)


def _chan_mix_kernel(z_ref, m_ref, o_ref):
    o_ref[...] = jnp.dot(z_ref[...].astype(BF16), m_ref[0], preferred_element_type=F32).astype(BF16)


def _fourier_channel_mix(proj, m_c, z_col, tm=512):
    t = proj.shape[0]
    groups = m_c.shape[0]
    return pl.pallas_call(
        _chan_mix_kernel,
        grid=(t // tm, groups),
        in_specs=[pl.BlockSpec((tm, LANES), lambda i, g: (i, z_col + g)),
                  pl.BlockSpec((1, LANES, 2 * LANES), lambda i, g: (g, 0, 0))],
        out_specs=pl.BlockSpec((tm, 2 * LANES), lambda i, g: (i, g)),
        out_shape=jax.ShapeDtypeStruct((t, groups * 2 * LANES), BF16),
        compiler_params=_params(("parallel", "parallel"), 32),
        name="fourier_channel_mix",
    )(proj, m_c)


def _dft_stage1_kernel(u_ref, fr_ref, fi_ref, y_ref, *, pairs):
    u = u_ref[0]
    p = jnp.dot(fr_ref[...], u, preferred_element_type=F32)
    q = jnp.dot(fi_ref[...], u, preferred_element_type=F32)
    for j in range(pairs):
        re = slice(2 * j * LANES, (2 * j + 1) * LANES)
        im = slice((2 * j + 1) * LANES, (2 * j + 2) * LANES)
        y_ref[0, :, re] = (p[:, re] - q[:, im]).astype(BF16)
        y_ref[0, :, im] = (q[:, re] + p[:, im]).astype(BF16)


def _dft_stage2_kernel(y_ref, gc_ref, gs_ref, o_ref, *, groups):
    y = y_ref[0, 0]
    a = jnp.dot(gc_ref[0], y, preferred_element_type=F32)
    b = jnp.dot(gs_ref[0], y, preferred_element_type=F32)
    for g in range(groups):
        re = slice(2 * g * LANES, (2 * g + 1) * LANES)
        im = slice((2 * g + 1) * LANES, (2 * g + 2) * LANES)
        o_ref[0, :, g * LANES:(g + 1) * LANES] = (a[:, re] + b[:, im]).astype(o_ref.dtype)


def _seq_dft_real(u, batch, seq, groups, tc=4096):
    n1 = 128 if seq >= 16384 else 64
    n2 = seq // n1
    wc = groups * 2 * LANES
    i1 = jnp.arange(n1, dtype=jnp.int32)
    ang1 = (2.0 * math.pi / n1) * ((i1[:, None] * i1[None, :]) % n1).astype(F32)
    fr = jnp.cos(ang1).astype(BF16)
    fi = (-jnp.sin(ang1)).astype(BF16)
    cols = n2 * wc
    tc = min(tc, cols)
    y = pl.pallas_call(
        functools.partial(_dft_stage1_kernel, pairs=tc // (2 * LANES)),
        grid=(batch, cols // tc),
        in_specs=[pl.BlockSpec((1, n1, tc), lambda b, j: (b, 0, j)),
                  pl.BlockSpec((n1, n1), lambda b, j: (0, 0)),
                  pl.BlockSpec((n1, n1), lambda b, j: (0, 0))],
        out_specs=pl.BlockSpec((1, n1, tc), lambda b, j: (b, 0, j)),
        out_shape=jax.ShapeDtypeStruct((batch, n1, cols), BF16),
        compiler_params=_params(("parallel", "parallel"), 32),
        name="dft_stage1",
    )(u.reshape(batch, n1, cols), fr, fi)
    i2 = jnp.arange(n2, dtype=jnp.int32)
    tw = (i2[None, None, :] * (i1[:, None, None] + n1 * i2[None, :, None])) % seq
    ang2 = (2.0 * math.pi / seq) * tw.astype(F32)
    gc = jnp.cos(ang2).astype(BF16)
    gs = jnp.sin(ang2).astype(BF16)
    out = pl.pallas_call(
        functools.partial(_dft_stage2_kernel, groups=groups),
        grid=(batch, n1),
        in_specs=[pl.BlockSpec((1, 1, n2, wc), lambda b, k: (b, k, 0, 0)),
                  pl.BlockSpec((1, n2, n2), lambda b, k: (k, 0, 0)),
                  pl.BlockSpec((1, n2, n2), lambda b, k: (k, 0, 0))],
        out_specs=pl.BlockSpec((1, n2, groups * LANES), lambda b, k: (b, 0, k)),
        out_shape=jax.ShapeDtypeStruct((batch, n2, n1 * groups * LANES), BF16),
        compiler_params=_params(("parallel", "parallel"), 32),
        name="dft_stage2",
    )(y.reshape(batch, n1, n2, wc), gc, gs)
    return out.reshape(batch * seq, groups * LANES)


def _cross_attn_kernel(q_ref, kv_ref, o_ref, *, heads, scale):
    ca_w = heads * HEAD_DIM
    for h in range(heads):
        cols = slice(h * HEAD_DIM, (h + 1) * HEAD_DIM)
        q = q_ref[0, :, cols]
        k = kv_ref[0, :, cols]
        v = kv_ref[0, :, ca_w + h * HEAD_DIM:ca_w + (h + 1) * HEAD_DIM]
        s = _dot_nt(q, k) * scale
        m = jnp.max(s, axis=-1, keepdims=True)
        p = jnp.exp(s - m)
        l = jnp.sum(p, axis=-1, keepdims=True)
        o = jnp.dot(p.astype(BF16), v, preferred_element_type=F32) / l
        o_ref[0, :, cols] = o.astype(o_ref.dtype)


def _cross_attention(q, kv, batch, seq, tm=512):
    ca_w = CA_HEADS * HEAD_DIM
    mem_len = kv.shape[0] // batch
    kern = functools.partial(_cross_attn_kernel, heads=CA_HEADS, scale=HEAD_DIM ** -0.5)
    out = pl.pallas_call(
        kern,
        grid=(batch, seq // tm),
        in_specs=[pl.BlockSpec((1, tm, ca_w), lambda b, i: (b, i, 0)),
                  pl.BlockSpec((1, mem_len, 2 * ca_w), lambda b, i: (b, 0, 0))],
        out_specs=pl.BlockSpec((1, tm, ca_w), lambda b, i: (b, i, 0)),
        out_shape=jax.ShapeDtypeStruct((batch, seq, ca_w), BF16),
        compiler_params=_params(("parallel", "parallel"), 32),
        name="cross_attn",
    )(q.reshape(batch, seq, ca_w), kv.reshape(batch, mem_len, 2 * ca_w))
    return out.reshape(batch * seq, ca_w)


HALO = 16


def _ffn_kernel(x_ref, xp_ref, xn_ref, g4_ref, wg_ref, wv_ref, cwg_ref, cwv_ref, cbg_ref, cbv_ref,
                wd_ref, g5_ref, o_ref, xe_ref, zg_ref, zv_ref, acc_ref, *, tm, tc, tiles_per_seq, nf):
    i = pl.program_id(0)
    f = pl.program_id(1)

    @pl.when(f == 0)
    def _():
        g4 = g4_ref[...]
        pos = i % tiles_per_seq
        prev = _rms(xp_ref[...], g4) * jnp.where(pos == 0, 0.0, 1.0)
        nxt = _rms(xn_ref[...], g4) * jnp.where(pos == tiles_per_seq - 1, 0.0, 1.0)
        xe_ref[0:HALO, :] = prev.astype(BF16)
        xe_ref[HALO:HALO + tm, :] = _rms(x_ref[...], g4).astype(BF16)
        xe_ref[HALO + tm:, :] = nxt.astype(BF16)
        acc_ref[...] = jnp.zeros(acc_ref.shape, F32)

    def conv(z_ref, c, cw, cb):
        return (z_ref[c, HALO - 1:HALO - 1 + tm, :] * cw[0:1] + z_ref[c, HALO:HALO + tm, :] * cw[1:2]
                + z_ref[c, HALO + 1:HALO + 1 + tm, :] * cw[2:3] + cb)

    xe = xe_ref[...]
    n_sub = zg_ref.shape[0]
    for c in range(n_sub):
        cols = slice(c * tc, (c + 1) * tc)
        zg_ref[c] = jnp.dot(xe, wg_ref[:, cols], preferred_element_type=F32)
        zv_ref[c] = jnp.dot(xe, wv_ref[:, cols], preferred_element_type=F32)
    for c in range(n_sub):
        cols = slice(c * tc, (c + 1) * tc)
        h = (jax.nn.gelu(conv(zg_ref, c, cwg_ref[:, cols], cbg_ref[:, cols]))
             * conv(zv_ref, c, cwv_ref[:, cols], cbv_ref[:, cols]))
        acc_ref[...] += jnp.dot(h.astype(BF16), wd_ref[cols, :], preferred_element_type=F32)

    @pl.when(f == nf - 1)
    def _():
        o_ref[...] = x_ref[...] + _rms(acc_ref[...], g5_ref[...])


def _conv_ffn(x, seq, g4, w_up, conv_w, conv_b, w_down, g5, tm=512, tf=512, tc=256):
    t, d = x.shape
    d_ff = w_down.shape[0]
    nf = d_ff // tf
    hb = tm // HALO
    last_hb = t // HALO - 1
    kern = functools.partial(_ffn_kernel, tm=tm, tc=tc, tiles_per_seq=seq // tm, nf=nf)
    cb = conv_b.reshape(1, 2 * d_ff)
    return pl.pallas_call(
        kern,
        grid=(t // tm, nf),
        in_specs=[pl.BlockSpec((tm, d), lambda i, f: (i, 0)),
                  pl.BlockSpec((HALO, d), lambda i, f: (jnp.maximum(i * hb - 1, 0), 0)),
                  pl.BlockSpec((HALO, d), lambda i, f: (jnp.minimum((i + 1) * hb, last_hb), 0)),
                  pl.BlockSpec((1, d), lambda i, f: (0, 0)),
                  pl.BlockSpec((d, tf), lambda i, f: (0, f)),
                  pl.BlockSpec((d, tf), lambda i, f: (0, nf + f)),
                  pl.BlockSpec((CONV_W, tf), lambda i, f: (0, f)),
                  pl.BlockSpec((CONV_W, tf), lambda i, f: (0, nf + f)),
                  pl.BlockSpec((1, tf), lambda i, f: (0, f)),
                  pl.BlockSpec((1, tf), lambda i, f: (0, nf + f)),
                  pl.BlockSpec((tf, d), lambda i, f: (f, 0)),
                  pl.BlockSpec((1, d), lambda i, f: (0, 0))],
        out_specs=pl.BlockSpec((tm, d), lambda i, f: (i, 0)),
        out_shape=jax.ShapeDtypeStruct((t, d), F32),
        scratch_shapes=[pltpu.VMEM((tm + 2 * HALO, d), BF16),
                        pltpu.VMEM((tf // tc, tm + 2 * HALO, tc), F32),
                        pltpu.VMEM((tf // tc, tm + 2 * HALO, tc), F32),
                        pltpu.VMEM((tm, d), F32)],
        compiler_params=_params(("parallel", "arbitrary"), 52),
        name="conv_ffn",
    )(x, x, x, g4.reshape(1, d), w_up, w_up, conv_w, conv_w, cb, cb, w_down, g5.reshape(1, d))


def _trunk(x3, mem3, p):
    batch, seq, d = x3.shape
    x = x3.reshape(batch * seq, d)
    mem = mem3.reshape(-1, d)
    depth = p["w_in"].shape[0]
    a_w = A_HEADS * HEAD_DIM
    c_w = C_HEADS * 2 * C_HALF
    for layer in range(depth):
        g = p["norm_gains"][layer]
        proj = _norm_matmul(x, g[0], p["w_in"][layer], F32)
        if layer % 2 == 0:
            e = layer // 2
            qkvs = _even_prep(proj, batch, seq, HEAD_DIM ** -0.5)
            parts = [_band_attention(qkv.reshape(batch * dil, seq // dil, 3 * a_w))
                     for (_, dil), qkv in zip(DILATED_CFGS, qkvs)]
            ya = _combine_configs(parts, batch, seq)
            yb = _spatial_gating(proj, p["b_w_spatial"][e], p["b_b_spatial"][e], p["b_ln_gain"][e],
                                 3 * A_HEADS, 3 * A_HEADS + B_GROUPS)
            mix = jnp.concatenate([ya, yb], axis=-1)
        else:
            o = layer // 2
            qt, kk, vt = _diff_prep(proj, batch, seq, C_HALF ** -0.5 * math.log2(math.e), DIFF_BK)
            yc = _diff_attention(qt, kk, vt, p["c_lambda"][o], p["c_subln_gain"][o], batch, seq, layer,
                                 bk=DIFF_BK)
            m_c = _fourier_weights(p["d_w_fourier"][o], seq)
            u = _fourier_channel_mix(proj, m_c, 3 * c_w // LANES)
            yd = _seq_dft_real(u, batch, seq, D_GROUPS)
            mix = jnp.concatenate([yc, yd], axis=-1)
        x = _matmul_norm_res(mix, p["w_out"][layer], g[1], x)
        q = _norm_matmul(x, g[2], p["ca_w_q"][layer], BF16)
        kv = _norm_matmul(mem, p["mem_norm_gain"][layer], p["ca_w_kv"][layer], BF16)
        ca = _cross_attention(q, kv, batch, seq)
        x = _matmul_norm_res(ca, p["ca_w_o"][layer], g[3], x)
        x = _conv_ffn(x, seq, g[4], p["ffn_w_up"][layer], p["ffn_conv_w"][layer], p["ffn_conv_b"][layer],
                      p["ffn_w_down"][layer], g[5])
    return x.reshape(batch, seq, d)


def kernel(x_prompt, x_sample, mem_prompt, mem_sample, norm_gains, w_in, w_out, b_w_spatial, b_b_spatial,
           b_ln_gain, c_lambda, c_subln_gain, d_w_fourier, mem_norm_gain, ca_w_q, ca_w_kv, ca_w_o, ffn_w_up,
           ffn_conv_w, ffn_conv_b, ffn_w_down):
    p = dict(norm_gains=norm_gains, w_in=w_in.astype(BF16), w_out=w_out.astype(BF16), b_w_spatial=b_w_spatial,
             b_b_spatial=b_b_spatial, b_ln_gain=b_ln_gain, c_lambda=c_lambda, c_subln_gain=c_subln_gain,
             d_w_fourier=d_w_fourier, mem_norm_gain=mem_norm_gain, ca_w_q=ca_w_q.astype(BF16),
             ca_w_kv=ca_w_kv.astype(BF16), ca_w_o=ca_w_o.astype(BF16), ffn_w_up=ffn_w_up.astype(BF16),
             ffn_conv_w=ffn_conv_w, ffn_conv_b=ffn_conv_b, ffn_w_down=ffn_w_down.astype(BF16))
    return _trunk(x_prompt, mem_prompt, p), _trunk(x_sample, mem_sample, p)
```

```python
import functools
import math

import jax
import jax.numpy as jnp
from jax import lax
from jax.experimental import pallas as pl
from jax.experimental.pallas import tpu as pltpu

F32 = jnp.float32
BF16 = jnp.bfloat16

EPS = 1e-6
ROPE_THETA = 500000.0
ROPE_FRACTION = 4
LANES = 128
HEAD_DIM = 128
A_HEADS = 8
DILATED_CFGS = ((128, 1), (512, 4), (2048, 16))
BAND_RADIUS = 64
B_GROUPS = 8
CHUNK = 128
C_HEADS = 12
C_HALF = 64
D_GROUPS = 4
CA_HEADS = 4
CONV_W = 3
NEG_BIG = -1e30
MIB = 1024 * 1024


def _params(semantics, vmem_mib):
    return pltpu.CompilerParams(dimension_semantics=semantics, vmem_limit_bytes=vmem_mib * MIB)


def _rms(x, g):
    return x * lax.rsqrt(jnp.mean(x * x, axis=-1, keepdims=True) + EPS) * g


def _dot_nt(a, b):
    return lax.dot_general(a, b, (((1,), (1,)), ((), ())), preferred_element_type=F32)


def _norm_mm_kernel(x_ref, g_ref, w_ref, o_ref, xn_ref):
    @pl.when(pl.program_id(1) == 0)
    def _():
        xn_ref[...] = _rms(x_ref[...], g_ref[...]).astype(BF16)

    o_ref[...] = jnp.dot(xn_ref[...], w_ref[...], preferred_element_type=F32).astype(o_ref.dtype)


def _norm_matmul(x, g, w, layer, out_dtype, tm=1024, tn=512):
    t, k = x.shape
    n = w.shape[2]
    tm = min(tm, t)
    return pl.pallas_call(
        _norm_mm_kernel,
        grid=(t // tm, n // tn),
        in_specs=[pl.BlockSpec((tm, k), lambda i, j: (i, 0)),
                  pl.BlockSpec((1, k), lambda i, j: (0, 0)),
                  pl.BlockSpec((None, k, tn), lambda i, j: (layer, 0, j))],
        out_specs=pl.BlockSpec((tm, tn), lambda i, j: (i, j)),
        out_shape=jax.ShapeDtypeStruct((t, n), out_dtype),
        scratch_shapes=[pltpu.VMEM((tm, k), BF16)],
        compiler_params=_params(("parallel", "arbitrary"), 40),
        name="norm_matmul",
    )(x, g.reshape(1, k), w)


def _mm_norm_res_kernel(a_ref, w_ref, g_ref, r_ref, o_ref):
    y = jnp.dot(a_ref[...], w_ref[...], preferred_element_type=F32)
    o_ref[...] = r_ref[...] + _rms(y, g_ref[...])


def _matmul_norm_res(a, w, layer, g, res, tm=512):
    t, k = a.shape
    n = w.shape[2]
    return pl.pallas_call(
        _mm_norm_res_kernel,
        grid=(t // tm,),
        in_specs=[pl.BlockSpec((tm, k), lambda i: (i, 0)),
                  pl.BlockSpec((None, k, n), lambda i: (layer, 0, 0)),
                  pl.BlockSpec((1, n), lambda i: (0, 0)),
                  pl.BlockSpec((tm, n), lambda i: (i, 0))],
        out_specs=pl.BlockSpec((tm, n), lambda i: (i, 0)),
        out_shape=jax.ShapeDtypeStruct((t, n), F32),
        compiler_params=_params(("parallel",), 48),
        name="matmul_norm_res",
    )(a, w, g.reshape(1, n), res)


def _rope_tables(seq, head_w):
    rd = head_w // ROPE_FRACTION
    half = rd // 2
    inv = ROPE_THETA ** (-jnp.arange(half, dtype=F32) / half)
    ang = jnp.arange(seq, dtype=F32)[:, None] * inv[None, :]
    cos = jnp.cos(ang)
    sin = jnp.sin(ang)
    rest = head_w - rd
    c_head = jnp.concatenate([cos, cos, jnp.ones((seq, rest), F32)], axis=-1)
    s_head = jnp.concatenate([-sin, sin, jnp.zeros((seq, rest), F32)], axis=-1)
    reps = LANES // head_w
    return jnp.tile(c_head, (1, reps)), jnp.tile(s_head, (1, reps))


def _rope_block(x, c, s, head_w):
    half = head_w // ROPE_FRACTION // 2
    first = (lax.broadcasted_iota(jnp.int32, x.shape, 1) & (head_w - 1)) < half
    partner = jnp.where(first, pltpu.roll(x, LANES - half, axis=1), pltpu.roll(x, half, axis=1))
    return x * c + partner * s


def _even_prep_kernel(x_ref, c_ref, s_ref, *refs, q_scale, n_blocks):
    out_refs, y_ref = refs[:-1], refs[-1]
    part = pl.program_id(1)
    tm = x_ref.shape[0]

    def rope(scale):
        c = c_ref[...]
        s = s_ref[...]
        for b in range(n_blocks):
            cols = slice(b * LANES, (b + 1) * LANES)
            y_ref[b] = _rope_block(x_ref[:, cols].astype(F32), c, s, HEAD_DIM) * scale

    @pl.when(part == 0)
    def _():
        rope(q_scale)

    @pl.when(part == 1)
    def _():
        rope(1.0)

    @pl.when(part == 2)
    def _():
        for b in range(n_blocks):
            y_ref[b] = x_ref[:, b * LANES:(b + 1) * LANES].astype(F32)

    for (_, dil), o_ref in zip(DILATED_CFGS, out_refs):
        for r in range(dil):
            for b in range(n_blocks):
                rows = y_ref[b, pl.ds(r, tm // dil, stride=dil), :]
                o_ref[0, r, :, b * LANES:(b + 1) * LANES] = rows.astype(BF16)


def _even_prep(proj, batch, seq, q_scale, tm=256):
    a_w = A_HEADS * HEAD_DIM
    cos_t, sin_t = _rope_tables(seq, HEAD_DIM)
    spb = seq // tm
    kern = functools.partial(_even_prep_kernel, q_scale=q_scale, n_blocks=a_w // LANES)
    return pl.pallas_call(
        kern,
        grid=(batch * spb, 3),
        in_specs=[pl.BlockSpec((tm, a_w), lambda i, p: (i, p)),
                  pl.BlockSpec((tm, LANES), lambda i, p: (i % spb, 0)),
                  pl.BlockSpec((tm, LANES), lambda i, p: (i % spb, 0))],
        out_specs=[pl.BlockSpec((1, dil, tm // dil, a_w), lambda i, p: (i // spb, 0, i % spb, p))
                   for _, dil in DILATED_CFGS],
        out_shape=[jax.ShapeDtypeStruct((batch, dil, seq // dil, 3 * a_w), BF16) for _, dil in DILATED_CFGS],
        scratch_shapes=[pltpu.VMEM((a_w // LANES, tm, LANES), F32)],
        compiler_params=_params(("parallel", "arbitrary"), 32),
        name="even_prep",
    )(proj, cos_t, sin_t)


def _band_attn_kernel(q_ref, kp_ref, kc_ref, kn_ref, vp_ref, vc_ref, vn_ref, o_ref, lse_ref, *, bq, sub_len):
    q0 = pl.program_id(1) * bq
    win = bq + 2 * BAND_RADIUS
    qpos = q0 + lax.broadcasted_iota(jnp.int32, (bq, win), 0)
    kpos = q0 - BAND_RADIUS + lax.broadcasted_iota(jnp.int32, (bq, win), 1)
    rel = kpos - qpos
    ok = (rel <= BAND_RADIUS) & (rel >= -BAND_RADIUS) & (kpos >= 0) & (kpos < sub_len)
    bias = jnp.where(ok, 0.0, NEG_BIG)
    def scores(h):
        cols = slice(h * HEAD_DIM, (h + 1) * HEAD_DIM)
        kw = jnp.concatenate([kp_ref[0, :, cols], kc_ref[0, :, cols], kn_ref[0, :, cols]], axis=0)
        return _dot_nt(q_ref[0, :, cols], kw) + bias

    s_next = scores(0)
    for h in range(A_HEADS):
        cols = slice(h * HEAD_DIM, (h + 1) * HEAD_DIM)
        s = s_next
        if h + 1 < A_HEADS:
            s_next = scores(h + 1)
        vw = jnp.concatenate([vp_ref[0, :, cols], vc_ref[0, :, cols], vn_ref[0, :, cols]], axis=0)
        m = jnp.max(s, axis=-1, keepdims=True)
        p = jnp.exp(s - m)
        l = jnp.sum(p, axis=-1, keepdims=True)
        o = jnp.dot(p.astype(BF16), vw, preferred_element_type=F32)
        o_ref[0, :, cols] = o / l
        lse_ref[0, :, cols] = jnp.broadcast_to(m + jnp.log(l), (bq, HEAD_DIM))


def _band_attention(qkv, bq=256):
    n_seq, sub_len, _ = qkv.shape
    a_w = A_HEADS * HEAD_DIM
    hb = bq // BAND_RADIUS
    last = sub_len // BAND_RADIUS - 1
    kern = functools.partial(_band_attn_kernel, bq=bq, sub_len=sub_len)

    def halo_specs(col):
        return [pl.BlockSpec((1, BAND_RADIUS, a_w), lambda s, i: (s, jnp.maximum(i * hb - 1, 0), col)),
                pl.BlockSpec((1, bq, a_w), lambda s, i: (s, i, col)),
                pl.BlockSpec((1, BAND_RADIUS, a_w), lambda s, i: (s, jnp.minimum((i + 1) * hb, last), col))]

    out_spec = pl.BlockSpec((1, bq, a_w), lambda s, i: (s, i, 0))
    return pl.pallas_call(
        kern,
        grid=(n_seq, sub_len // bq),
        in_specs=[pl.BlockSpec((1, bq, a_w), lambda s, i: (s, i, 0))] + halo_specs(1) + halo_specs(2),
        out_specs=[out_spec, out_spec],
        out_shape=[jax.ShapeDtypeStruct((n_seq, sub_len, a_w), F32)] * 2,
        compiler_params=_params(("parallel", "parallel"), 40),
        name="band_attn",
    )(*([qkv] * 7))


def _combine_kernel(o1, l1, o4, l4, o16, l16, y_ref, so4, sl4, so16, sl16):
    tm = y_ref.shape[0]
    for dil, src, dst in ((4, o4, so4), (4, l4, sl4), (16, o16, so16), (16, l16, sl16)):
        for r in range(dil):
            for h in range(A_HEADS):
                dst[h, pl.ds(r, tm // dil, stride=dil), :] = src[0, r, :, h * HEAD_DIM:(h + 1) * HEAD_DIM]
    for h in range(A_HEADS):
        cols = slice(h * HEAD_DIM, (h + 1) * HEAD_DIM)
        a, b, c = l1[0, 0, :, cols], sl4[h], sl16[h]
        m = jnp.maximum(jnp.maximum(a, b), c)
        wa, wb, wc = jnp.exp(a - m), jnp.exp(b - m), jnp.exp(c - m)
        num = wa * o1[0, 0, :, cols] + wb * so4[h] + wc * so16[h]
        y_ref[:, cols] = (num / (wa + wb + wc)).astype(y_ref.dtype)


def _combine_configs(parts, batch, seq, tm=256):
    a_w = A_HEADS * HEAD_DIM
    spb = seq // tm
    flat, specs = [], []
    for (_, dil), pair in zip(DILATED_CFGS, parts):
        for arr in pair:
            flat.append(arr.reshape(batch, dil, seq // dil, a_w))
            specs.append(pl.BlockSpec((1, dil, tm // dil, a_w), lambda i: (i // spb, 0, i % spb, 0)))
    return pl.pallas_call(
        _combine_kernel,
        grid=(batch * spb,),
        in_specs=specs,
        out_specs=pl.BlockSpec((tm, a_w), lambda i: (i, 0)),
        out_shape=jax.ShapeDtypeStruct((batch * seq, a_w), BF16),
        scratch_shapes=[pltpu.VMEM((A_HEADS, tm, HEAD_DIM), F32)] * 4,
        compiler_params=_params(("parallel",), 40),
        name="dilated_combine",
    )(*flat)


def _gating_kernel(u_ref, v_ref, w_ref, b_ref, g_ref, o_ref, *, n_chunks):
    w = w_ref[0]
    bias = b_ref[0]
    gain = g_ref[0]
    for c in range(n_chunks):
        rows = pl.ds(c * CHUNK, CHUNK)
        v = jax.nn.gelu(v_ref[rows, :].astype(F32))
        mu = jnp.mean(v, axis=-1, keepdims=True)
        d = v - mu
        var = jnp.mean(d * d, axis=-1, keepdims=True)
        vn = d * lax.rsqrt(var + EPS) * gain
        sv = jnp.dot(w, vn.astype(BF16), preferred_element_type=F32) + bias
        o_ref[rows, :] = (jax.nn.gelu(u_ref[rows, :].astype(F32)) * sv).astype(o_ref.dtype)


def _spatial_gating(proj, w_s, b_s, g_ln, u_col, v_col, tm=512):
    t = proj.shape[0]
    groups = w_s.shape[0]
    bias = jnp.broadcast_to(b_s[:, :, None], (groups, CHUNK, LANES))
    kern = functools.partial(_gating_kernel, n_chunks=tm // CHUNK)
    return pl.pallas_call(
        kern,
        grid=(t // tm, groups),
        in_specs=[pl.BlockSpec((tm, LANES), lambda i, g: (i, u_col + g)),
                  pl.BlockSpec((tm, LANES), lambda i, g: (i, v_col + g)),
                  pl.BlockSpec((1, CHUNK, CHUNK), lambda i, g: (g, 0, 0)),
                  pl.BlockSpec((1, CHUNK, LANES), lambda i, g: (g, 0, 0)),
                  pl.BlockSpec((1, 1, LANES), lambda i, g: (g, 0, 0))],
        out_specs=pl.BlockSpec((tm, LANES), lambda i, g: (i, g)),
        out_shape=jax.ShapeDtypeStruct((t, groups * LANES), BF16),
        compiler_params=_params(("parallel", "parallel"), 32),
        name="spatial_gating",
    )(proj, proj, w_s.astype(BF16), bias, g_ln.reshape(groups, 1, LANES))


V_ROWS = 2 * C_HALF + 16
DIFF_BK = 512


def _diff_prep_kernel(x_ref, c_ref, s_ref, qt_ref, k_ref, vt_ref, *, q_scale, n_blocks):
    part = pl.program_id(1)
    tm = x_ref.shape[0]

    @pl.when(part == 0)
    def _():
        c = c_ref[...]
        s = s_ref[...]
        for h in range(n_blocks):
            cols = slice(h * LANES, (h + 1) * LANES)
            qt_ref[0, h] = (_rope_block(x_ref[:, cols].astype(F32), c, s, C_HALF) * q_scale).T.astype(BF16)

    @pl.when(part == 1)
    def _():
        c = c_ref[...]
        s = s_ref[...]
        for h in range(n_blocks):
            cols = slice(h * LANES, (h + 1) * LANES)
            k_ref[:, cols] = _rope_block(x_ref[:, cols].astype(F32), c, s, C_HALF).astype(BF16)

    @pl.when(part == 2)
    def _():
        ones = jnp.ones((V_ROWS - LANES, tm), BF16)
        for h in range(n_blocks):
            vt_ref[0, h, 0, 0:LANES, :] = x_ref[:, h * LANES:(h + 1) * LANES].astype(F32).T.astype(BF16)
            vt_ref[0, h, 0, LANES:, :] = ones


def _diff_prep(proj, batch, seq, q_scale, bk, tm=256):
    c_w = C_HEADS * 2 * C_HALF
    cos_t, sin_t = _rope_tables(seq, C_HALF)
    spb = seq // tm
    per_chunk = bk // tm
    kern = functools.partial(_diff_prep_kernel, q_scale=q_scale, n_blocks=C_HEADS)
    return pl.pallas_call(
        kern,
        grid=(batch * spb, 3),
        in_specs=[pl.BlockSpec((tm, c_w), lambda i, p: (i, p)),
                  pl.BlockSpec((tm, LANES), lambda i, p: (i % spb, 0)),
                  pl.BlockSpec((tm, LANES), lambda i, p: (i % spb, 0))],
        out_specs=[pl.BlockSpec((1, C_HEADS, LANES, tm), lambda i, p: (i // spb, 0, 0, i % spb)),
                   pl.BlockSpec((tm, c_w), lambda i, p: (i, 0)),
                   pl.BlockSpec((1, C_HEADS, 1, V_ROWS, tm),
                                lambda i, p: (i // spb, 0, (i % spb) // per_chunk, 0, (i % spb) % per_chunk))],
        out_shape=[jax.ShapeDtypeStruct((batch, C_HEADS, LANES, seq), BF16),
                   jax.ShapeDtypeStruct((batch * seq, c_w), BF16),
                   jax.ShapeDtypeStruct((batch, C_HEADS, seq // bk, V_ROWS, bk), BF16)],
        compiler_params=_params(("parallel", "arbitrary"), 40),
        name="diff_prep",
    )(proj, cos_t, sin_t)


def _diff_attn_kernel(qt_ref, k_ref, vt_ref, lam_ref, g_ref, o_ref, s_ref, mc_ref, m_ref, acc_ref,
                      *, bq, bk, seq, lam_init, unroll):
    n = seq // bk
    sub = 8
    qt = qt_ref[0, 0]
    row = lax.broadcasted_iota(jnp.int32, qt.shape, 0)
    zero = jnp.zeros_like(qt)
    qs = (jnp.where(row < C_HALF, qt, zero), jnp.where(row < C_HALF, zero, qt))

    def scores(j, slot):
        kc = k_ref[0, pl.ds(pl.multiple_of(j * bk, bk), bk), :]
        for br in range(2):
            s = jnp.dot(kc, qs[br], preferred_element_type=F32)
            s_ref[slot, br] = s
            mc = jnp.max(s.reshape(bk // sub, sub, bq), axis=0)
            mc_ref[slot, br] = jnp.broadcast_to(jnp.max(mc, axis=0, keepdims=True), (sub, bq))

    def accumulate(j, slot):
        vt = vt_ref[0, 0, j]
        for br in range(2):
            m_old = m_ref[br]
            m_new = jnp.maximum(m_old, mc_ref[slot, br])
            alpha = jnp.exp2(m_old - m_new)
            m_ref[br] = m_new
            s3 = s_ref[slot, br].reshape(bk // sub, sub, bq)
            p = jnp.exp2(s3 - m_new[None]).reshape(bk, bq).astype(BF16)
            pv = jnp.dot(vt, p, preferred_element_type=F32)
            acc3 = acc_ref[br].reshape(V_ROWS // sub, sub, bq)
            acc_ref[br] = (alpha[None] * acc3).reshape(V_ROWS, bq) + pv

    m_ref[...] = jnp.full(m_ref.shape, NEG_BIG, F32)
    acc_ref[...] = jnp.zeros(acc_ref.shape, F32)
    scores(0, 0)

    def body(i, carry):
        j = unroll * i
        for u in range(unroll):
            scores(jnp.minimum(j + u + 1, n - 1), (u + 1) % 2)
            accumulate(j + u, u % 2)
        return carry

    lax.fori_loop(0, n // unroll, body, 0)

    lam = lam_ref[...]
    lam_full = (jnp.exp(jnp.sum(lam[0:1] * lam[1:2], keepdims=True))
                - jnp.exp(jnp.sum(lam[2:3] * lam[3:4], keepdims=True)) + lam_init)

    def normalised(a):
        num = a[0:LANES].reshape(LANES // sub, sub, bq)
        return (num / a[LANES:LANES + sub][None]).reshape(LANES, bq)

    o = normalised(acc_ref[0]) - lam_full * normalised(acc_ref[1])
    y = o * lax.rsqrt(jnp.mean(o * o, axis=0, keepdims=True) + EPS) * g_ref[...] * (1.0 - lam_init)
    o_ref[0] = y.T.astype(o_ref.dtype)


def _diff_attention(qt, k, vt, lam, subln_g, batch, seq, layer_idx, bq=512, bk=512, unroll=4):
    c_w = C_HEADS * 2 * C_HALF
    lam_init = 0.8 - 0.6 * math.exp(-0.3 * layer_idx)
    n = seq // bk
    assert unroll % 2 == 0 and n % unroll == 0 and vt.shape[2] == n
    kern = functools.partial(_diff_attn_kernel, bq=bq, bk=bk, seq=seq, lam_init=lam_init, unroll=unroll)
    out = pl.pallas_call(
        kern,
        grid=(batch, C_HEADS, seq // bq),
        in_specs=[pl.BlockSpec((1, 1, LANES, bq), lambda b, h, i: (b, h, 0, i)),
                  pl.BlockSpec((1, seq, LANES), lambda b, h, i: (b, 0, h)),
                  pl.BlockSpec((1, 1, n, V_ROWS, bk), lambda b, h, i: (b, h, 0, 0, 0)),
                  pl.BlockSpec((4, C_HALF), lambda b, h, i: (0, 0)),
                  pl.BlockSpec((LANES, bq), lambda b, h, i: (0, 0))],
        out_specs=pl.BlockSpec((1, bq, LANES), lambda b, h, i: (b, i, h)),
        out_shape=jax.ShapeDtypeStruct((batch, seq, c_w), BF16),
        scratch_shapes=[pltpu.VMEM((2, 2, bk, bq), F32),
                        pltpu.VMEM((2, 2, 8, bq), F32),
                        pltpu.VMEM((2, 8, bq), F32),
                        pltpu.VMEM((2, V_ROWS, bq), F32)],
        compiler_params=_params(("parallel", "parallel", "arbitrary"), 48),
        name="diff_attn",
    )(qt, k.reshape(batch, seq, c_w), vt, lam, jnp.broadcast_to(subln_g[:, None], (LANES, bq)))
    return out.reshape(batch * seq, c_w)


def _fourier_wprep_kernel(c_ref, s_ref, w_ref, o_ref, *, scale):
    w = w_ref[0]
    mr = jnp.dot(c_ref[...], w, preferred_element_type=F32, precision=lax.Precision.HIGHEST)
    mi = jnp.dot(s_ref[...], w, preferred_element_type=F32, precision=lax.Precision.HIGHEST)
    o_ref[0, :, :LANES] = (mr * scale).astype(BF16)
    o_ref[0, :, LANES:] = (-mi * scale).astype(BF16)


def _fourier_weights(w_f, seq):
    groups, c, _ = w_f.shape
    idx = jnp.arange(c, dtype=jnp.int32)
    ang = (2.0 * math.pi / c) * ((idx[:, None] * idx[None, :]) % c).astype(F32)
    kern = functools.partial(_fourier_wprep_kernel, scale=1.0 / math.sqrt(seq * c))
    return pl.pallas_call(
        kern,
        grid=(groups,),
        in_specs=[pl.BlockSpec((c, c), lambda g: (0, 0)),
                  pl.BlockSpec((c, c), lambda g: (0, 0)),
                  pl.BlockSpec((1, c, c), lambda g: (g, 0, 0))],
        out_specs=pl.BlockSpec((1, c, 2 * c), lambda g: (g, 0, 0)),
        out_shape=jax.ShapeDtypeStruct((groups, c, 2 * c), BF16),
        compiler_params=_params(("parallel",), 32),
        name="fourier_wprep",
    )(jnp.cos(ang), jnp.sin(ang), w_f)


def _chan_mix_kernel(z_ref, m_ref, o_ref):
    o_ref[...] = jnp.dot(z_ref[...], m_ref[0], preferred_element_type=F32).astype(BF16)


def _fourier_channel_mix(proj, m_c, z_col, tm=512):
    t = proj.shape[0]
    groups = m_c.shape[0]
    return pl.pallas_call(
        _chan_mix_kernel,
        grid=(t // tm, groups),
        in_specs=[pl.BlockSpec((tm, LANES), lambda i, g: (i, z_col + g)),
                  pl.BlockSpec((1, LANES, 2 * LANES), lambda i, g: (g, 0, 0))],
        out_specs=pl.BlockSpec((tm, 2 * LANES), lambda i, g: (i, g)),
        out_shape=jax.ShapeDtypeStruct((t, groups * 2 * LANES), BF16),
        compiler_params=_params(("parallel", "parallel"), 32),
        name="fourier_channel_mix",
    )(proj, m_c)


def _dft_stage1_kernel(u_ref, fr_ref, fi_ref, y_ref, *, pairs):
    u = u_ref[0]
    p = jnp.dot(fr_ref[...], u, preferred_element_type=F32)
    q = jnp.dot(fi_ref[...], u, preferred_element_type=F32)
    for j in range(pairs):
        re = slice(2 * j * LANES, (2 * j + 1) * LANES)
        im = slice((2 * j + 1) * LANES, (2 * j + 2) * LANES)
        y_ref[0, :, re] = (p[:, re] - q[:, im]).astype(BF16)
        y_ref[0, :, im] = (q[:, re] + p[:, im]).astype(BF16)


def _dft_stage2_kernel(y_ref, gc_ref, gs_ref, o_ref, *, groups):
    y = y_ref[0, 0]
    a = jnp.dot(gc_ref[0], y, preferred_element_type=F32)
    b = jnp.dot(gs_ref[0], y, preferred_element_type=F32)
    for g in range(groups):
        re = slice(2 * g * LANES, (2 * g + 1) * LANES)
        im = slice((2 * g + 1) * LANES, (2 * g + 2) * LANES)
        o_ref[0, :, g * LANES:(g + 1) * LANES] = (a[:, re] + b[:, im]).astype(o_ref.dtype)


def _seq_dft_real(u, batch, seq, groups, tc=4096):
    n1 = 128 if seq >= 16384 else 64
    n2 = seq // n1
    wc = groups * 2 * LANES
    i1 = jnp.arange(n1, dtype=jnp.int32)
    ang1 = (2.0 * math.pi / n1) * ((i1[:, None] * i1[None, :]) % n1).astype(F32)
    fr = jnp.cos(ang1).astype(BF16)
    fi = (-jnp.sin(ang1)).astype(BF16)
    cols = n2 * wc
    tc = min(tc, cols)
    y = pl.pallas_call(
        functools.partial(_dft_stage1_kernel, pairs=tc // (2 * LANES)),
        grid=(batch, cols // tc),
        in_specs=[pl.BlockSpec((1, n1, tc), lambda b, j: (b, 0, j)),
                  pl.BlockSpec((n1, n1), lambda b, j: (0, 0)),
                  pl.BlockSpec((n1, n1), lambda b, j: (0, 0))],
        out_specs=pl.BlockSpec((1, n1, tc), lambda b, j: (b, 0, j)),
        out_shape=jax.ShapeDtypeStruct((batch, n1, cols), BF16),
        compiler_params=_params(("parallel", "parallel"), 32),
        name="dft_stage1",
    )(u.reshape(batch, n1, cols), fr, fi)
    i2 = jnp.arange(n2, dtype=jnp.int32)
    tw = (i2[None, None, :] * (i1[:, None, None] + n1 * i2[None, :, None])) % seq
    ang2 = (2.0 * math.pi / seq) * tw.astype(F32)
    gc = jnp.cos(ang2).astype(BF16)
    gs = jnp.sin(ang2).astype(BF16)
    out = pl.pallas_call(
        functools.partial(_dft_stage2_kernel, groups=groups),
        grid=(batch, n1),
        in_specs=[pl.BlockSpec((1, 1, n2, wc), lambda b, k: (b, k, 0, 0)),
                  pl.BlockSpec((1, n2, n2), lambda b, k: (k, 0, 0)),
                  pl.BlockSpec((1, n2, n2), lambda b, k: (k, 0, 0))],
        out_specs=pl.BlockSpec((1, n2, groups * LANES), lambda b, k: (b, 0, k)),
        out_shape=jax.ShapeDtypeStruct((batch, n2, n1 * groups * LANES), BF16),
        compiler_params=_params(("parallel", "parallel"), 32),
        name="dft_stage2",
    )(y.reshape(batch, n1, n2, wc), gc, gs)
    return out.reshape(batch * seq, groups * LANES)


def _cross_attn_kernel(q_ref, kv_ref, o_ref, *, heads, scale):
    ca_w = heads * HEAD_DIM
    for h in range(heads):
        cols = slice(h * HEAD_DIM, (h + 1) * HEAD_DIM)
        q = q_ref[0, :, cols]
        k = kv_ref[0, :, cols]
        v = kv_ref[0, :, ca_w + h * HEAD_DIM:ca_w + (h + 1) * HEAD_DIM]
        s = _dot_nt(q, k) * scale
        m = jnp.max(s, axis=-1, keepdims=True)
        p = jnp.exp(s - m)
        l = jnp.sum(p, axis=-1, keepdims=True)
        o = jnp.dot(p.astype(BF16), v, preferred_element_type=F32) / l
        o_ref[0, :, cols] = o.astype(o_ref.dtype)


def _cross_attention(q, kv, batch, seq, tm=512):
    ca_w = CA_HEADS * HEAD_DIM
    mem_len = kv.shape[0] // batch
    kern = functools.partial(_cross_attn_kernel, heads=CA_HEADS, scale=HEAD_DIM ** -0.5)
    out = pl.pallas_call(
        kern,
        grid=(batch, seq // tm),
        in_specs=[pl.BlockSpec((1, tm, ca_w), lambda b, i: (b, i, 0)),
                  pl.BlockSpec((1, mem_len, 2 * ca_w), lambda b, i: (b, 0, 0))],
        out_specs=pl.BlockSpec((1, tm, ca_w), lambda b, i: (b, i, 0)),
        out_shape=jax.ShapeDtypeStruct((batch, seq, ca_w), BF16),
        compiler_params=_params(("parallel", "parallel"), 32),
        name="cross_attn",
    )(q.reshape(batch, seq, ca_w), kv.reshape(batch, mem_len, 2 * ca_w))
    return out.reshape(batch * seq, ca_w)


HALO = 16


def _ffn_kernel(x_ref, xp_ref, xn_ref, g4_ref, wg_ref, wv_ref, cwg_ref, cwv_ref, cbg_ref, cbv_ref,
                wd_ref, g5_ref, o_ref, xe_ref, zg_ref, zv_ref, *, tm, tc, tiles_per_seq, nf):
    acc_ref = o_ref
    i = pl.program_id(0)
    f = pl.program_id(1)

    @pl.when(f == 0)
    def _():
        g4 = g4_ref[...]
        pos = i % tiles_per_seq
        prev = _rms(xp_ref[...], g4) * jnp.where(pos == 0, 0.0, 1.0)
        nxt = _rms(xn_ref[...], g4) * jnp.where(pos == tiles_per_seq - 1, 0.0, 1.0)
        xe_ref[0:HALO, :] = prev.astype(BF16)
        xe_ref[HALO:HALO + tm, :] = _rms(x_ref[...], g4).astype(BF16)
        xe_ref[HALO + tm:, :] = nxt.astype(BF16)
        acc_ref[...] = jnp.zeros(acc_ref.shape, F32)

    def conv(z_ref, c, cw, cb):
        return (z_ref[c, HALO - 1:HALO - 1 + tm, :] * cw[0:1] + z_ref[c, HALO:HALO + tm, :] * cw[1:2]
                + z_ref[c, HALO + 1:HALO + 1 + tm, :] * cw[2:3] + cb)

    xe = xe_ref[...]
    n_sub = zg_ref.shape[0]
    for c in range(n_sub):
        cols = slice(c * tc, (c + 1) * tc)
        zg_ref[c] = jnp.dot(xe, wg_ref[:, cols], preferred_element_type=F32)
        zv_ref[c] = jnp.dot(xe, wv_ref[:, cols], preferred_element_type=F32)
    for c in range(n_sub):
        cols = slice(c * tc, (c + 1) * tc)
        h = (jax.nn.gelu(conv(zg_ref, c, cwg_ref[:, cols], cbg_ref[:, cols]))
             * conv(zv_ref, c, cwv_ref[:, cols], cbv_ref[:, cols]))
        acc_ref[...] += jnp.dot(h.astype(BF16), wd_ref[cols, :], preferred_element_type=F32)

    @pl.when(f == nf - 1)
    def _():
        o_ref[...] = x_ref[...] + _rms(acc_ref[...], g5_ref[...])


def _conv_ffn(x, seq, g4, w_up, conv_w, conv_b, w_down, layer, g5, tm=512, tf=512, tc=256):
    t, d = x.shape
    d_ff = w_down.shape[1]
    nf = d_ff // tf
    hb = tm // HALO
    last_hb = t // HALO - 1
    kern = functools.partial(_ffn_kernel, tm=tm, tc=tc, tiles_per_seq=seq // tm, nf=nf)
    cb = conv_b.reshape(1, 2 * d_ff)
    return pl.pallas_call(
        kern,
        grid=(t // tm, nf),
        in_specs=[pl.BlockSpec((tm, d), lambda i, f: (i, 0)),
                  pl.BlockSpec((HALO, d), lambda i, f: (jnp.maximum(i * hb - 1, 0), 0)),
                  pl.BlockSpec((HALO, d), lambda i, f: (jnp.minimum((i + 1) * hb, last_hb), 0)),
                  pl.BlockSpec((1, d), lambda i, f: (0, 0)),
                  pl.BlockSpec((None, d, tf), lambda i, f: (layer, 0, f)),
                  pl.BlockSpec((None, d, tf), lambda i, f: (layer, 0, nf + f)),
                  pl.BlockSpec((CONV_W, tf), lambda i, f: (0, f)),
                  pl.BlockSpec((CONV_W, tf), lambda i, f: (0, nf + f)),
                  pl.BlockSpec((1, tf), lambda i, f: (0, f)),
                  pl.BlockSpec((1, tf), lambda i, f: (0, nf + f)),
                  pl.BlockSpec((None, tf, d), lambda i, f: (layer, f, 0)),
                  pl.BlockSpec((1, d), lambda i, f: (0, 0))],
        out_specs=pl.BlockSpec((tm, d), lambda i, f: (i, 0)),
        out_shape=jax.ShapeDtypeStruct((t, d), F32),
        scratch_shapes=[pltpu.VMEM((tm + 2 * HALO, d), BF16),
                        pltpu.VMEM((tf // tc, tm + 2 * HALO, tc), F32),
                        pltpu.VMEM((tf // tc, tm + 2 * HALO, tc), F32)],
        compiler_params=_params(("parallel", "arbitrary"), 48),
        name="conv_ffn",
    )(x, x, x, g4.reshape(1, d), w_up, w_up, conv_w, conv_w, cb, cb, w_down, g5.reshape(1, d))


def _trunk(x3, mem3, p):
    batch, seq, d = x3.shape
    x = x3.reshape(batch * seq, d)
    mem = mem3.reshape(-1, d)
    depth = p["w_in"].shape[0]
    a_w = A_HEADS * HEAD_DIM
    c_w = C_HEADS * 2 * C_HALF
    for layer in range(depth):
        g = p["norm_gains"][layer]
        proj = _norm_matmul(x, g[0], p["w_in"], layer, BF16)
        if layer % 2 == 0:
            e = layer // 2
            qkvs = _even_prep(proj, batch, seq, HEAD_DIM ** -0.5)
            parts = [_band_attention(qkv.reshape(batch * dil, seq // dil, 3 * a_w))
                     for (_, dil), qkv in zip(DILATED_CFGS, qkvs)]
            ya = _combine_configs(parts, batch, seq)
            yb = _spatial_gating(proj, p["b_w_spatial"][e], p["b_b_spatial"][e], p["b_ln_gain"][e],
                                 3 * A_HEADS, 3 * A_HEADS + B_GROUPS)
            mix = jnp.concatenate([ya, yb], axis=-1)
        else:
            o = layer // 2
            qt, kk, vt = _diff_prep(proj, batch, seq, C_HALF ** -0.5 * math.log2(math.e), DIFF_BK)
            yc = _diff_attention(qt, kk, vt, p["c_lambda"][o], p["c_subln_gain"][o], batch, seq, layer,
                                 bk=DIFF_BK)
            m_c = _fourier_weights(p["d_w_fourier"][o], seq)
            u = _fourier_channel_mix(proj, m_c, 3 * c_w // LANES)
            yd = _seq_dft_real(u, batch, seq, D_GROUPS)
            mix = jnp.concatenate([yc, yd], axis=-1)
        x = _matmul_norm_res(mix, p["w_out"], layer, g[1], x)
        q = _norm_matmul(x, g[2], p["ca_w_q"], layer, BF16)
        kv = _norm_matmul(mem, p["mem_norm_gain"][layer], p["ca_w_kv"], layer, BF16)
        ca = _cross_attention(q, kv, batch, seq)
        x = _matmul_norm_res(ca, p["ca_w_o"], layer, g[3], x)
        x = _conv_ffn(x, seq, g[4], p["ffn_w_up"], p["ffn_conv_w"][layer], p["ffn_conv_b"][layer],
                      p["ffn_w_down"], layer, g[5])
    return x.reshape(batch, seq, d)


def kernel(x_prompt, x_sample, mem_prompt, mem_sample, norm_gains, w_in, w_out, b_w_spatial, b_b_spatial,
           b_ln_gain, c_lambda, c_subln_gain, d_w_fourier, mem_norm_gain, ca_w_q, ca_w_kv, ca_w_o, ffn_w_up,
           ffn_conv_w, ffn_conv_b, ffn_w_down):
    p = dict(norm_gains=norm_gains, w_in=w_in.astype(BF16), w_out=w_out.astype(BF16), b_w_spatial=b_w_spatial,
             b_b_spatial=b_b_spatial, b_ln_gain=b_ln_gain, c_lambda=c_lambda, c_subln_gain=c_subln_gain,
             d_w_fourier=d_w_fourier, mem_norm_gain=mem_norm_gain, ca_w_q=ca_w_q.astype(BF16),
             ca_w_kv=ca_w_kv.astype(BF16), ca_w_o=ca_w_o.astype(BF16), ffn_w_up=ffn_w_up.astype(BF16),
             ffn_conv_w=ffn_conv_w, ffn_conv_b=ffn_conv_b, ffn_w_down=ffn_w_down.astype(BF16))
    return _trunk(x_prompt, mem_prompt, p), _trunk(x_sample, mem_sample, p)
```

```python
import functools
import math

import jax
import jax.numpy as jnp
from jax import lax
from jax.experimental import pallas as pl
from jax.experimental.pallas import tpu as pltpu

F32 = jnp.float32
BF16 = jnp.bfloat16

EPS = 1e-6
ROPE_THETA = 500000.0
ROPE_FRACTION = 4
LANES = 128
HEAD_DIM = 128
A_HEADS = 8
DILATED_CFGS = ((128, 1), (512, 4), (2048, 16))
BAND_RADIUS = 64
B_GROUPS = 8
CHUNK = 128
C_HEADS = 12
C_HALF = 64
D_GROUPS = 4
CA_HEADS = 4
CONV_W = 3
NEG_BIG = -1e30
MIB = 1024 * 1024


def _params(semantics, vmem_mib):
    return pltpu.CompilerParams(dimension_semantics=semantics, vmem_limit_bytes=vmem_mib * MIB)


def _rms(x, g):
    return x * lax.rsqrt(jnp.mean(x * x, axis=-1, keepdims=True) + EPS) * g


def _dot_nt(a, b):
    return lax.dot_general(a, b, (((1,), (1,)), ((), ())), preferred_element_type=F32)


def _norm_mm_kernel(x_ref, g_ref, w_ref, o_ref, xn_ref):
    @pl.when(pl.program_id(1) == 0)
    def _():
        xn_ref[...] = _rms(x_ref[...], g_ref[...]).astype(BF16)

    o_ref[...] = jnp.dot(xn_ref[...], w_ref[...], preferred_element_type=F32).astype(o_ref.dtype)


def _norm_matmul(x, g, w, layer, out_dtype, tm=1024, tn=512):
    t, k = x.shape
    n = w.shape[2]
    tm = min(tm, t)
    return pl.pallas_call(
        _norm_mm_kernel,
        grid=(t // tm, n // tn),
        in_specs=[pl.BlockSpec((tm, k), lambda i, j: (i, 0)),
                  pl.BlockSpec((1, k), lambda i, j: (0, 0)),
                  pl.BlockSpec((None, k, tn), lambda i, j: (layer, 0, j))],
        out_specs=pl.BlockSpec((tm, tn), lambda i, j: (i, j)),
        out_shape=jax.ShapeDtypeStruct((t, n), out_dtype),
        scratch_shapes=[pltpu.VMEM((tm, k), BF16)],
        compiler_params=_params(("parallel", "arbitrary"), 40),
        name="norm_matmul",
    )(x, g.reshape(1, k), w)


def _mm_norm_res_kernel(a_ref, w_ref, g_ref, r_ref, o_ref):
    y = jnp.dot(a_ref[...], w_ref[...], preferred_element_type=F32)
    o_ref[...] = r_ref[...] + _rms(y, g_ref[...])


def _matmul_norm_res(a, w, layer, g, res, tm=512):
    t, k = a.shape
    n = w.shape[2]
    return pl.pallas_call(
        _mm_norm_res_kernel,
        grid=(t // tm,),
        in_specs=[pl.BlockSpec((tm, k), lambda i: (i, 0)),
                  pl.BlockSpec((None, k, n), lambda i: (layer, 0, 0)),
                  pl.BlockSpec((1, n), lambda i: (0, 0)),
                  pl.BlockSpec((tm, n), lambda i: (i, 0))],
        out_specs=pl.BlockSpec((tm, n), lambda i: (i, 0)),
        out_shape=jax.ShapeDtypeStruct((t, n), F32),
        compiler_params=_params(("parallel",), 48),
        name="matmul_norm_res",
    )(a, w, g.reshape(1, n), res)


def _rope_tables(seq, head_w):
    rd = head_w // ROPE_FRACTION
    half = rd // 2
    inv = ROPE_THETA ** (-jnp.arange(half, dtype=F32) / half)
    ang = jnp.arange(seq, dtype=F32)[:, None] * inv[None, :]
    cos = jnp.cos(ang)
    sin = jnp.sin(ang)
    rest = head_w - rd
    c_head = jnp.concatenate([cos, cos, jnp.ones((seq, rest), F32)], axis=-1)
    s_head = jnp.concatenate([-sin, sin, jnp.zeros((seq, rest), F32)], axis=-1)
    reps = LANES // head_w
    return jnp.tile(c_head, (1, reps)), jnp.tile(s_head, (1, reps))


def _rope_block(x, c, s, head_w):
    half = head_w // ROPE_FRACTION // 2
    first = (lax.broadcasted_iota(jnp.int32, x.shape, 1) & (head_w - 1)) < half
    partner = jnp.where(first, pltpu.roll(x, LANES - half, axis=1), pltpu.roll(x, half, axis=1))
    return x * c + partner * s


def _even_prep_kernel(x_ref, c_ref, s_ref, *refs, q_scale, n_blocks):
    out_refs, y_ref = refs[:-1], refs[-1]
    part = pl.program_id(1)
    tm = x_ref.shape[0]

    def rope(scale):
        c = c_ref[...]
        s = s_ref[...]
        for b in range(n_blocks):
            cols = slice(b * LANES, (b + 1) * LANES)
            y_ref[b] = _rope_block(x_ref[:, cols].astype(F32), c, s, HEAD_DIM) * scale

    @pl.when(part == 0)
    def _():
        rope(q_scale)

    @pl.when(part == 1)
    def _():
        rope(1.0)

    @pl.when(part == 2)
    def _():
        for b in range(n_blocks):
            y_ref[b] = x_ref[:, b * LANES:(b + 1) * LANES].astype(F32)

    for (_, dil), o_ref in zip(DILATED_CFGS, out_refs):
        for r in range(dil):
            for b in range(n_blocks):
                rows = y_ref[b, pl.ds(r, tm // dil, stride=dil), :]
                o_ref[0, r, :, b * LANES:(b + 1) * LANES] = rows.astype(BF16)


def _even_prep(proj, batch, seq, q_scale, tm=256):
    a_w = A_HEADS * HEAD_DIM
    cos_t, sin_t = _rope_tables(seq, HEAD_DIM)
    spb = seq // tm
    kern = functools.partial(_even_prep_kernel, q_scale=q_scale, n_blocks=a_w // LANES)
    return pl.pallas_call(
        kern,
        grid=(batch * spb, 3),
        in_specs=[pl.BlockSpec((tm, a_w), lambda i, p: (i, p)),
                  pl.BlockSpec((tm, LANES), lambda i, p: (i % spb, 0)),
                  pl.BlockSpec((tm, LANES), lambda i, p: (i % spb, 0))],
        out_specs=[pl.BlockSpec((1, dil, tm // dil, a_w), lambda i, p: (i // spb, 0, i % spb, p))
                   for _, dil in DILATED_CFGS],
        out_shape=[jax.ShapeDtypeStruct((batch, dil, seq // dil, 3 * a_w), BF16) for _, dil in DILATED_CFGS],
        scratch_shapes=[pltpu.VMEM((a_w // LANES, tm, LANES), F32)],
        compiler_params=_params(("parallel", "arbitrary"), 32),
        name="even_prep",
    )(proj, cos_t, sin_t)


def _band_attn_kernel(q_ref, kp_ref, kc_ref, kn_ref, vp_ref, vc_ref, vn_ref, o_ref, lse_ref, *, bq, sub_len):
    q0 = pl.program_id(1) * bq
    win = bq + 2 * BAND_RADIUS
    qpos = q0 + lax.broadcasted_iota(jnp.int32, (bq, win), 0)
    kpos = q0 - BAND_RADIUS + lax.broadcasted_iota(jnp.int32, (bq, win), 1)
    rel = kpos - qpos
    ok = (rel <= BAND_RADIUS) & (rel >= -BAND_RADIUS) & (kpos >= 0) & (kpos < sub_len)
    bias = jnp.where(ok, 0.0, NEG_BIG)
    def scores(h):
        cols = slice(h * HEAD_DIM, (h + 1) * HEAD_DIM)
        kw = jnp.concatenate([kp_ref[0, :, cols], kc_ref[0, :, cols], kn_ref[0, :, cols]], axis=0)
        return _dot_nt(q_ref[0, :, cols], kw) + bias

    s_next = scores(0)
    for h in range(A_HEADS):
        cols = slice(h * HEAD_DIM, (h + 1) * HEAD_DIM)
        s = s_next
        if h + 1 < A_HEADS:
            s_next = scores(h + 1)
        vw = jnp.concatenate([vp_ref[0, :, cols], vc_ref[0, :, cols], vn_ref[0, :, cols]], axis=0)
        m = jnp.max(s, axis=-1, keepdims=True)
        p = jnp.exp(s - m)
        l = jnp.sum(p, axis=-1, keepdims=True)
        o = jnp.dot(p.astype(BF16), vw, preferred_element_type=F32)
        o_ref[0, :, cols] = o / l
        lse_ref[0, :, cols] = jnp.broadcast_to(m + jnp.log(l), (bq, HEAD_DIM))


def _band_attention(qkv, bq=256):
    n_seq, sub_len, _ = qkv.shape
    a_w = A_HEADS * HEAD_DIM
    hb = bq // BAND_RADIUS
    last = sub_len // BAND_RADIUS - 1
    kern = functools.partial(_band_attn_kernel, bq=bq, sub_len=sub_len)

    def halo_specs(col):
        return [pl.BlockSpec((1, BAND_RADIUS, a_w), lambda s, i: (s, jnp.maximum(i * hb - 1, 0), col)),
                pl.BlockSpec((1, bq, a_w), lambda s, i: (s, i, col)),
                pl.BlockSpec((1, BAND_RADIUS, a_w), lambda s, i: (s, jnp.minimum((i + 1) * hb, last), col))]

    out_spec = pl.BlockSpec((1, bq, a_w), lambda s, i: (s, i, 0))
    return pl.pallas_call(
        kern,
        grid=(n_seq, sub_len // bq),
        in_specs=[pl.BlockSpec((1, bq, a_w), lambda s, i: (s, i, 0))] + halo_specs(1) + halo_specs(2),
        out_specs=[out_spec, out_spec],
        out_shape=[jax.ShapeDtypeStruct((n_seq, sub_len, a_w), F32)] * 2,
        compiler_params=_params(("parallel", "parallel"), 40),
        name="band_attn",
    )(*([qkv] * 7))


def _combine_kernel(o1, l1, o4, l4, o16, l16, y_ref, so4, sl4, so16, sl16):
    tm = y_ref.shape[0]
    for dil, src, dst in ((4, o4, so4), (4, l4, sl4), (16, o16, so16), (16, l16, sl16)):
        for r in range(dil):
            for h in range(A_HEADS):
                dst[h, pl.ds(r, tm // dil, stride=dil), :] = src[0, r, :, h * HEAD_DIM:(h + 1) * HEAD_DIM]
    for h in range(A_HEADS):
        cols = slice(h * HEAD_DIM, (h + 1) * HEAD_DIM)
        a, b, c = l1[0, 0, :, cols], sl4[h], sl16[h]
        m = jnp.maximum(jnp.maximum(a, b), c)
        wa, wb, wc = jnp.exp(a - m), jnp.exp(b - m), jnp.exp(c - m)
        num = wa * o1[0, 0, :, cols] + wb * so4[h] + wc * so16[h]
        y_ref[:, cols] = (num / (wa + wb + wc)).astype(y_ref.dtype)


def _combine_configs(parts, batch, seq, tm=256):
    a_w = A_HEADS * HEAD_DIM
    spb = seq // tm
    flat, specs = [], []
    for (_, dil), pair in zip(DILATED_CFGS, parts):
        for arr in pair:
            flat.append(arr.reshape(batch, dil, seq // dil, a_w))
            specs.append(pl.BlockSpec((1, dil, tm // dil, a_w), lambda i: (i // spb, 0, i % spb, 0)))
    return pl.pallas_call(
        _combine_kernel,
        grid=(batch * spb,),
        in_specs=specs,
        out_specs=pl.BlockSpec((tm, a_w), lambda i: (i, 0)),
        out_shape=jax.ShapeDtypeStruct((batch * seq, a_w), BF16),
        scratch_shapes=[pltpu.VMEM((A_HEADS, tm, HEAD_DIM), F32)] * 4,
        compiler_params=_params(("parallel",), 40),
        name="dilated_combine",
    )(*flat)


def _gating_kernel(u_ref, v_ref, w_ref, b_ref, g_ref, o_ref, *, n_chunks):
    w = w_ref[0]
    bias = b_ref[0]
    gain = g_ref[0]
    for c in range(n_chunks):
        rows = pl.ds(c * CHUNK, CHUNK)
        v = jax.nn.gelu(v_ref[rows, :].astype(F32))
        mu = jnp.mean(v, axis=-1, keepdims=True)
        d = v - mu
        var = jnp.mean(d * d, axis=-1, keepdims=True)
        vn = d * lax.rsqrt(var + EPS) * gain
        sv = jnp.dot(w, vn.astype(BF16), preferred_element_type=F32) + bias
        o_ref[rows, :] = (jax.nn.gelu(u_ref[rows, :].astype(F32)) * sv).astype(o_ref.dtype)


def _spatial_gating(proj, w_s, b_s, g_ln, u_col, v_col, tm=2048):
    t = proj.shape[0]
    tm = min(tm, t)
    groups = w_s.shape[0]
    bias = jnp.broadcast_to(b_s[:, :, None], (groups, CHUNK, LANES))
    kern = functools.partial(_gating_kernel, n_chunks=tm // CHUNK)
    return pl.pallas_call(
        kern,
        grid=(t // tm, groups),
        in_specs=[pl.BlockSpec((tm, LANES), lambda i, g: (i, u_col + g)),
                  pl.BlockSpec((tm, LANES), lambda i, g: (i, v_col + g)),
                  pl.BlockSpec((1, CHUNK, CHUNK), lambda i, g: (g, 0, 0)),
                  pl.BlockSpec((1, CHUNK, LANES), lambda i, g: (g, 0, 0)),
                  pl.BlockSpec((1, 1, LANES), lambda i, g: (g, 0, 0))],
        out_specs=pl.BlockSpec((tm, LANES), lambda i, g: (i, g)),
        out_shape=jax.ShapeDtypeStruct((t, groups * LANES), BF16),
        compiler_params=_params(("parallel", "parallel"), 32),
        name="spatial_gating",
    )(proj, proj, w_s.astype(BF16), bias, g_ln.reshape(groups, 1, LANES))


V_ROWS = 2 * C_HALF + 16
DIFF_BK = 512


def _diff_prep_kernel(x_ref, c_ref, s_ref, qt_ref, k_ref, vt_ref, *, q_scale, n_blocks):
    part = pl.program_id(1)
    tm = x_ref.shape[0]

    @pl.when(part == 0)
    def _():
        c = c_ref[...]
        s = s_ref[...]
        for h in range(n_blocks):
            cols = slice(h * LANES, (h + 1) * LANES)
            qt_ref[0, h] = (_rope_block(x_ref[:, cols].astype(F32), c, s, C_HALF) * q_scale).T.astype(BF16)

    @pl.when(part == 1)
    def _():
        c = c_ref[...]
        s = s_ref[...]
        for h in range(n_blocks):
            cols = slice(h * LANES, (h + 1) * LANES)
            k_ref[:, cols] = _rope_block(x_ref[:, cols].astype(F32), c, s, C_HALF).astype(BF16)

    @pl.when(part == 2)
    def _():
        ones = jnp.ones((V_ROWS - LANES, tm), BF16)
        for h in range(n_blocks):
            vt_ref[0, h, 0, 0:LANES, :] = x_ref[:, h * LANES:(h + 1) * LANES].astype(F32).T.astype(BF16)
            vt_ref[0, h, 0, LANES:, :] = ones


def _diff_prep(proj, batch, seq, q_scale, bk, tm=256):
    c_w = C_HEADS * 2 * C_HALF
    cos_t, sin_t = _rope_tables(seq, C_HALF)
    spb = seq // tm
    per_chunk = bk // tm
    kern = functools.partial(_diff_prep_kernel, q_scale=q_scale, n_blocks=C_HEADS)
    return pl.pallas_call(
        kern,
        grid=(batch * spb, 3),
        in_specs=[pl.BlockSpec((tm, c_w), lambda i, p: (i, p)),
                  pl.BlockSpec((tm, LANES), lambda i, p: (i % spb, 0)),
                  pl.BlockSpec((tm, LANES), lambda i, p: (i % spb, 0))],
        out_specs=[pl.BlockSpec((1, C_HEADS, LANES, tm), lambda i, p: (i // spb, 0, 0, i % spb)),
                   pl.BlockSpec((tm, c_w), lambda i, p: (i, 0)),
                   pl.BlockSpec((1, C_HEADS, 1, V_ROWS, tm),
                                lambda i, p: (i // spb, 0, (i % spb) // per_chunk, 0, (i % spb) % per_chunk))],
        out_shape=[jax.ShapeDtypeStruct((batch, C_HEADS, LANES, seq), BF16),
                   jax.ShapeDtypeStruct((batch * seq, c_w), BF16),
                   jax.ShapeDtypeStruct((batch, C_HEADS, seq // bk, V_ROWS, bk), BF16)],
        compiler_params=_params(("parallel", "arbitrary"), 40),
        name="diff_prep",
    )(proj, cos_t, sin_t)


def _diff_attn_kernel(qt_ref, k_ref, vt_ref, lam_ref, g_ref, o_ref, s_ref, mc_ref, m_ref, acc_ref,
                      *, bq, bk, seq, lam_init, unroll):
    n = seq // bk
    sub = 8
    qt = qt_ref[0, 0]
    row = lax.broadcasted_iota(jnp.int32, qt.shape, 0)
    zero = jnp.zeros_like(qt)
    qs = (jnp.where(row < C_HALF, qt, zero), jnp.where(row < C_HALF, zero, qt))

    def scores(j, slot):
        kc = k_ref[0, pl.ds(pl.multiple_of(j * bk, bk), bk), :]
        for br in range(2):
            s = jnp.dot(kc, qs[br], preferred_element_type=F32)
            s_ref[slot, br] = s
            mc = jnp.max(s.reshape(bk // sub, sub, bq), axis=0)
            mc_ref[slot, br] = jnp.broadcast_to(jnp.max(mc, axis=0, keepdims=True), (sub, bq))

    def accumulate(j, slot):
        vt = vt_ref[0, 0, j]
        for br in range(2):
            m_old = m_ref[br]
            m_new = jnp.maximum(m_old, mc_ref[slot, br])
            alpha = jnp.exp2(m_old - m_new)
            m_ref[br] = m_new
            s3 = s_ref[slot, br].reshape(bk // sub, sub, bq)
            p = jnp.exp2(s3 - m_new[None]).reshape(bk, bq).astype(BF16)
            pv = jnp.dot(vt, p, preferred_element_type=F32)
            acc3 = acc_ref[br].reshape(V_ROWS // sub, sub, bq)
            acc_ref[br] = (alpha[None] * acc3).reshape(V_ROWS, bq) + pv

    m_ref[...] = jnp.full(m_ref.shape, NEG_BIG, F32)
    acc_ref[...] = jnp.zeros(acc_ref.shape, F32)
    scores(0, 0)

    def body(i, carry):
        j = unroll * i
        for u in range(unroll):
            scores(jnp.minimum(j + u + 1, n - 1), (u + 1) % 2)
            accumulate(j + u, u % 2)
        return carry

    lax.fori_loop(0, n // unroll, body, 0)

    lam = lam_ref[...]
    lam_full = (jnp.exp(jnp.sum(lam[0:1] * lam[1:2], keepdims=True))
                - jnp.exp(jnp.sum(lam[2:3] * lam[3:4], keepdims=True)) + lam_init)

    def normalised(a):
        num = a[0:LANES].reshape(LANES // sub, sub, bq)
        return (num / a[LANES:LANES + sub][None]).reshape(LANES, bq)

    o = normalised(acc_ref[0]) - lam_full * normalised(acc_ref[1])
    y = o * lax.rsqrt(jnp.mean(o * o, axis=0, keepdims=True) + EPS) * g_ref[...] * (1.0 - lam_init)
    o_ref[0] = y.T.astype(o_ref.dtype)


def _diff_attention(qt, k, vt, lam, subln_g, batch, seq, layer_idx, bq=512, bk=512, unroll=8):
    c_w = C_HEADS * 2 * C_HALF
    lam_init = 0.8 - 0.6 * math.exp(-0.3 * layer_idx)
    n = seq // bk
    assert unroll % 2 == 0 and n % unroll == 0 and vt.shape[2] == n
    kern = functools.partial(_diff_attn_kernel, bq=bq, bk=bk, seq=seq, lam_init=lam_init, unroll=unroll)
    out = pl.pallas_call(
        kern,
        grid=(batch, C_HEADS, seq // bq),
        in_specs=[pl.BlockSpec((1, 1, LANES, bq), lambda b, h, i: (b, h, 0, i)),
                  pl.BlockSpec((1, seq, LANES), lambda b, h, i: (b, 0, h)),
                  pl.BlockSpec((1, 1, n, V_ROWS, bk), lambda b, h, i: (b, h, 0, 0, 0)),
                  pl.BlockSpec((4, C_HALF), lambda b, h, i: (0, 0)),
                  pl.BlockSpec((LANES, bq), lambda b, h, i: (0, 0))],
        out_specs=pl.BlockSpec((1, bq, LANES), lambda b, h, i: (b, i, h)),
        out_shape=jax.ShapeDtypeStruct((batch, seq, c_w), BF16),
        scratch_shapes=[pltpu.VMEM((2, 2, bk, bq), F32),
                        pltpu.VMEM((2, 2, 8, bq), F32),
                        pltpu.VMEM((2, 8, bq), F32),
                        pltpu.VMEM((2, V_ROWS, bq), F32)],
        compiler_params=_params(("parallel", "parallel", "arbitrary"), 48),
        name="diff_attn",
    )(qt, k.reshape(batch, seq, c_w), vt, lam, jnp.broadcast_to(subln_g[:, None], (LANES, bq)))
    return out.reshape(batch * seq, c_w)


def _fourier_wprep_kernel(c_ref, s_ref, w_ref, o_ref, *, scale):
    w = w_ref[0]
    mr = jnp.dot(c_ref[...], w, preferred_element_type=F32, precision=lax.Precision.HIGHEST)
    mi = jnp.dot(s_ref[...], w, preferred_element_type=F32, precision=lax.Precision.HIGHEST)
    o_ref[0, :, :LANES] = (mr * scale).astype(BF16)
    o_ref[0, :, LANES:] = (-mi * scale).astype(BF16)


def _fourier_weights(w_f, seq):
    groups, c, _ = w_f.shape
    idx = jnp.arange(c, dtype=jnp.int32)
    ang = (2.0 * math.pi / c) * ((idx[:, None] * idx[None, :]) % c).astype(F32)
    kern = functools.partial(_fourier_wprep_kernel, scale=1.0 / math.sqrt(seq * c))
    return pl.pallas_call(
        kern,
        grid=(groups,),
        in_specs=[pl.BlockSpec((c, c), lambda g: (0, 0)),
                  pl.BlockSpec((c, c), lambda g: (0, 0)),
                  pl.BlockSpec((1, c, c), lambda g: (g, 0, 0))],
        out_specs=pl.BlockSpec((1, c, 2 * c), lambda g: (g, 0, 0)),
        out_shape=jax.ShapeDtypeStruct((groups, c, 2 * c), BF16),
        compiler_params=_params(("parallel",), 32),
        name="fourier_wprep",
    )(jnp.cos(ang), jnp.sin(ang), w_f)


def _chan_mix_kernel(z_ref, m_ref, o_ref):
    o_ref[...] = jnp.dot(z_ref[...], m_ref[0], preferred_element_type=F32).astype(BF16)


def _fourier_channel_mix(proj, m_c, z_col, tm=2048):
    t = proj.shape[0]
    tm = min(tm, t)
    groups = m_c.shape[0]
    return pl.pallas_call(
        _chan_mix_kernel,
        grid=(t // tm, groups),
        in_specs=[pl.BlockSpec((tm, LANES), lambda i, g: (i, z_col + g)),
                  pl.BlockSpec((1, LANES, 2 * LANES), lambda i, g: (g, 0, 0))],
        out_specs=pl.BlockSpec((tm, 2 * LANES), lambda i, g: (i, g)),
        out_shape=jax.ShapeDtypeStruct((t, groups * 2 * LANES), BF16),
        compiler_params=_params(("parallel", "parallel"), 32),
        name="fourier_channel_mix",
    )(proj, m_c)


def _dft_stage1_kernel(u_ref, fr_ref, fi_ref, y_ref, *, pairs):
    u = u_ref[0]
    p = jnp.dot(fr_ref[...], u, preferred_element_type=F32)
    q = jnp.dot(fi_ref[...], u, preferred_element_type=F32)
    for j in range(pairs):
        re = slice(2 * j * LANES, (2 * j + 1) * LANES)
        im = slice((2 * j + 1) * LANES, (2 * j + 2) * LANES)
        y_ref[0, :, re] = (p[:, re] - q[:, im]).astype(BF16)
        y_ref[0, :, im] = (q[:, re] + p[:, im]).astype(BF16)


def _dft_stage2_kernel(y_ref, gc_ref, gs_ref, o_ref, *, groups):
    for t in range(y_ref.shape[1]):
        y = y_ref[0, t]
        a = jnp.dot(gc_ref[t], y, preferred_element_type=F32)
        b = jnp.dot(gs_ref[t], y, preferred_element_type=F32)
        for g in range(groups):
            re = slice(2 * g * LANES, (2 * g + 1) * LANES)
            im = slice((2 * g + 1) * LANES, (2 * g + 2) * LANES)
            out_cols = slice((t * groups + g) * LANES, (t * groups + g + 1) * LANES)
            o_ref[0, :, out_cols] = (a[:, re] + b[:, im]).astype(o_ref.dtype)


def _seq_dft_real(u, batch, seq, groups, tc=4096, kb=4):
    n1 = 128 if seq >= 16384 else 64
    n2 = seq // n1
    wc = groups * 2 * LANES
    i1 = jnp.arange(n1, dtype=jnp.int32)
    ang1 = (2.0 * math.pi / n1) * ((i1[:, None] * i1[None, :]) % n1).astype(F32)
    fr = jnp.cos(ang1).astype(BF16)
    fi = (-jnp.sin(ang1)).astype(BF16)
    cols = n2 * wc
    tc = min(tc, cols)
    y = pl.pallas_call(
        functools.partial(_dft_stage1_kernel, pairs=tc // (2 * LANES)),
        grid=(batch, cols // tc),
        in_specs=[pl.BlockSpec((1, n1, tc), lambda b, j: (b, 0, j)),
                  pl.BlockSpec((n1, n1), lambda b, j: (0, 0)),
                  pl.BlockSpec((n1, n1), lambda b, j: (0, 0))],
        out_specs=pl.BlockSpec((1, n1, tc), lambda b, j: (b, 0, j)),
        out_shape=jax.ShapeDtypeStruct((batch, n1, cols), BF16),
        compiler_params=_params(("parallel", "parallel"), 32),
        name="dft_stage1",
    )(u.reshape(batch, n1, cols), fr, fi)
    i2 = jnp.arange(n2, dtype=jnp.int32)
    tw = (i2[None, None, :] * (i1[:, None, None] + n1 * i2[None, :, None])) % seq
    ang2 = (2.0 * math.pi / seq) * tw.astype(F32)
    gc = jnp.cos(ang2).astype(BF16)
    gs = jnp.sin(ang2).astype(BF16)
    out = pl.pallas_call(
        functools.partial(_dft_stage2_kernel, groups=groups),
        grid=(batch, n1 // kb),
        in_specs=[pl.BlockSpec((1, kb, n2, wc), lambda b, k: (b, k, 0, 0)),
                  pl.BlockSpec((kb, n2, n2), lambda b, k: (k, 0, 0)),
                  pl.BlockSpec((kb, n2, n2), lambda b, k: (k, 0, 0))],
        out_specs=pl.BlockSpec((1, n2, kb * groups * LANES), lambda b, k: (b, 0, k)),
        out_shape=jax.ShapeDtypeStruct((batch, n2, n1 * groups * LANES), BF16),
        compiler_params=_params(("parallel", "parallel"), 32),
        name="dft_stage2",
    )(y.reshape(batch, n1, n2, wc), gc, gs)
    return out.reshape(batch * seq, groups * LANES)


def _mix_out_cross_attn_kernel(mix_ref, x_ref, wout_ref, g1_ref, g2_ref, wq_ref, kv_ref, wo_ref, g3_ref, o_ref,
                               *, heads, scale, halves):
    ca_w = heads * HEAD_DIM
    hr = mix_ref.shape[0] // halves
    rows = [slice(r * hr, (r + 1) * hr) for r in range(halves)]
    ys = [jnp.dot(mix_ref[r, :], wout_ref[...], preferred_element_type=F32) for r in rows]
    x1s = [x_ref[r, :] + _rms(y, g1_ref[...]) for r, y in zip(rows, ys)]
    qs = [jnp.dot(_rms(x1, g2_ref[...]).astype(BF16), wq_ref[...], preferred_element_type=F32).astype(BF16)
          for x1 in x1s]
    cas = []
    for q in qs:
        outs = []
        for h in range(heads):
            cols = slice(h * HEAD_DIM, (h + 1) * HEAD_DIM)
            s = _dot_nt(q[:, cols], kv_ref[0, :, cols]) * scale
            m = jnp.max(s, axis=-1, keepdims=True)
            p = jnp.exp(s - m)
            l = jnp.sum(p, axis=-1, keepdims=True)
            v = kv_ref[0, :, ca_w + h * HEAD_DIM:ca_w + (h + 1) * HEAD_DIM]
            outs.append((jnp.dot(p.astype(BF16), v, preferred_element_type=F32) / l).astype(BF16))
        cas.append(jnp.concatenate(outs, axis=1))
    zs = [jnp.dot(ca, wo_ref[...], preferred_element_type=F32) for ca in cas]
    for r, x1, z in zip(rows, x1s, zs):
        o_ref[r, :] = x1 + _rms(z, g3_ref[...])


def _mix_out_cross_attn(mix, x, w_out, w_q, w_o, layer, g1, g2, g3, kv, batch, seq, tm=512):
    t, d = x.shape
    ca_w = CA_HEADS * HEAD_DIM
    mem_len = kv.shape[0] // batch
    spb = seq // tm
    kern = functools.partial(_mix_out_cross_attn_kernel, heads=CA_HEADS, scale=HEAD_DIM ** -0.5, halves=2)
    once = pl.Buffered(1)
    vec = lambda: pl.BlockSpec((1, d), lambda i: (0, 0))
    return pl.pallas_call(
        kern,
        grid=(t // tm,),
        in_specs=[pl.BlockSpec((tm, d), lambda i: (i, 0)),
                  pl.BlockSpec((tm, d), lambda i: (i, 0)),
                  pl.BlockSpec((None, d, d), lambda i: (layer, 0, 0), pipeline_mode=once),
                  vec(), vec(),
                  pl.BlockSpec((None, d, ca_w), lambda i: (layer, 0, 0), pipeline_mode=once),
                  pl.BlockSpec((1, mem_len, 2 * ca_w), lambda i: (i // spb, 0, 0)),
                  pl.BlockSpec((None, ca_w, d), lambda i: (layer, 0, 0), pipeline_mode=once),
                  vec()],
        out_specs=pl.BlockSpec((tm, d), lambda i: (i, 0)),
        out_shape=jax.ShapeDtypeStruct((t, d), F32),
        compiler_params=_params(("parallel",), 56),
        name="mix_out_cross_attn",
    )(mix, x, w_out, g1.reshape(1, d), g2.reshape(1, d), w_q, kv.reshape(batch, mem_len, 2 * ca_w), w_o,
      g3.reshape(1, d))


HALO = 16


def _ffn_kernel(x_ref, xp_ref, xn_ref, g4_ref, wg_ref, wv_ref, cwg_ref, cwv_ref, cbg_ref, cbv_ref,
                wd_ref, g5_ref, o_ref, xe_ref, zg_ref, zv_ref, *, tm, tc, tiles_per_seq, nf):
    acc_ref = o_ref
    i = pl.program_id(0)
    f = pl.program_id(1)

    @pl.when(f == 0)
    def _():
        g4 = g4_ref[...]
        pos = i % tiles_per_seq
        prev = _rms(xp_ref[...], g4) * jnp.where(pos == 0, 0.0, 1.0)
        nxt = _rms(xn_ref[...], g4) * jnp.where(pos == tiles_per_seq - 1, 0.0, 1.0)
        xe_ref[0:HALO, :] = prev.astype(BF16)
        xe_ref[HALO:HALO + tm, :] = _rms(x_ref[...], g4).astype(BF16)
        xe_ref[HALO + tm:, :] = nxt.astype(BF16)
        acc_ref[...] = jnp.zeros(acc_ref.shape, F32)

    def conv(z_ref, c, cw, cb, lo, rows):
        lo = HALO + lo
        return (z_ref[c, lo - 1:lo - 1 + rows, :] * cw[0:1] + z_ref[c, lo:lo + rows, :] * cw[1:2]
                + z_ref[c, lo + 1:lo + 1 + rows, :] * cw[2:3] + cb)

    xe = xe_ref[...]
    n_sub = zg_ref.shape[0]
    for c in range(n_sub):
        cols = slice(c * tc, (c + 1) * tc)
        zg_ref[c] = jnp.dot(xe, wg_ref[:, cols], preferred_element_type=F32)
        zv_ref[c] = jnp.dot(xe, wv_ref[:, cols], preferred_element_type=F32)
    for c in range(n_sub):
        cols = slice(c * tc, (c + 1) * tc)
        pieces = 2 if c == n_sub - 1 else 1
        rows = tm // pieces
        for r in range(pieces):
            lo = r * rows
            h = (jax.nn.gelu(conv(zg_ref, c, cwg_ref[:, cols], cbg_ref[:, cols], lo, rows))
                 * conv(zv_ref, c, cwv_ref[:, cols], cbv_ref[:, cols], lo, rows))
            acc_ref[lo:lo + rows, :] += jnp.dot(h.astype(BF16), wd_ref[cols, :], preferred_element_type=F32)

    @pl.when(f == nf - 1)
    def _():
        o_ref[...] = x_ref[...] + _rms(acc_ref[...], g5_ref[...])


def _conv_ffn(x, seq, g4, w_up, conv_w, conv_b, w_down, layer, g5, tm=512, tf=512, tc=256):
    t, d = x.shape
    d_ff = w_down.shape[1]
    nf = d_ff // tf
    hb = tm // HALO
    last_hb = t // HALO - 1
    kern = functools.partial(_ffn_kernel, tm=tm, tc=tc, tiles_per_seq=seq // tm, nf=nf)
    cb = conv_b.reshape(1, 2 * d_ff)
    return pl.pallas_call(
        kern,
        grid=(t // tm, nf),
        in_specs=[pl.BlockSpec((tm, d), lambda i, f: (i, 0)),
                  pl.BlockSpec((HALO, d), lambda i, f: (jnp.maximum(i * hb - 1, 0), 0)),
                  pl.BlockSpec((HALO, d), lambda i, f: (jnp.minimum((i + 1) * hb, last_hb), 0)),
                  pl.BlockSpec((1, d), lambda i, f: (0, 0)),
                  pl.BlockSpec((None, d, tf), lambda i, f: (layer, 0, f)),
                  pl.BlockSpec((None, d, tf), lambda i, f: (layer, 0, nf + f)),
                  pl.BlockSpec((CONV_W, tf), lambda i, f: (0, f)),
                  pl.BlockSpec((CONV_W, tf), lambda i, f: (0, nf + f)),
                  pl.BlockSpec((1, tf), lambda i, f: (0, f)),
                  pl.BlockSpec((1, tf), lambda i, f: (0, nf + f)),
                  pl.BlockSpec((None, tf, d), lambda i, f: (layer, f, 0)),
                  pl.BlockSpec((1, d), lambda i, f: (0, 0))],
        out_specs=pl.BlockSpec((tm, d), lambda i, f: (i, 0)),
        out_shape=jax.ShapeDtypeStruct((t, d), F32),
        scratch_shapes=[pltpu.VMEM((tm + 2 * HALO, d), BF16),
                        pltpu.VMEM((tf // tc, tm + 2 * HALO, tc), F32),
                        pltpu.VMEM((tf // tc, tm + 2 * HALO, tc), F32)],
        compiler_params=_params(("parallel", "arbitrary"), 48),
        name="conv_ffn",
    )(x, x, x, g4.reshape(1, d), w_up, w_up, conv_w, conv_w, cb, cb, w_down, g5.reshape(1, d))


def _trunk(x3, mem3, p):
    batch, seq, d = x3.shape
    x = x3.reshape(batch * seq, d)
    mem = mem3.reshape(-1, d)
    depth = p["w_in"].shape[0]
    a_w = A_HEADS * HEAD_DIM
    c_w = C_HEADS * 2 * C_HALF
    for layer in range(depth):
        g = p["norm_gains"][layer]
        proj = _norm_matmul(x, g[0], p["w_in"], layer, BF16)
        if layer % 2 == 0:
            e = layer // 2
            qkvs = _even_prep(proj, batch, seq, HEAD_DIM ** -0.5)
            parts = [_band_attention(qkv.reshape(batch * dil, seq // dil, 3 * a_w))
                     for (_, dil), qkv in zip(DILATED_CFGS, qkvs)]
            ya = _combine_configs(parts, batch, seq)
            yb = _spatial_gating(proj, p["b_w_spatial"][e], p["b_b_spatial"][e], p["b_ln_gain"][e],
                                 3 * A_HEADS, 3 * A_HEADS + B_GROUPS)
            mix = jnp.concatenate([ya, yb], axis=-1)
        else:
            o = layer // 2
            qt, kk, vt = _diff_prep(proj, batch, seq, C_HALF ** -0.5 * math.log2(math.e), DIFF_BK)
            yc = _diff_attention(qt, kk, vt, p["c_lambda"][o], p["c_subln_gain"][o], batch, seq, layer,
                                 bk=DIFF_BK)
            m_c = _fourier_weights(p["d_w_fourier"][o], seq)
            u = _fourier_channel_mix(proj, m_c, 3 * c_w // LANES)
            yd = _seq_dft_real(u, batch, seq, D_GROUPS)
            mix = jnp.concatenate([yc, yd], axis=-1)
        kv = _norm_matmul(mem, p["mem_norm_gain"][layer], p["ca_w_kv"], layer, BF16)
        x = _mix_out_cross_attn(mix, x, p["w_out"], p["ca_w_q"], p["ca_w_o"], layer, g[1], g[2], g[3], kv,
                                batch, seq)
        x = _conv_ffn(x, seq, g[4], p["ffn_w_up"], p["ffn_conv_w"][layer], p["ffn_conv_b"][layer],
                      p["ffn_w_down"], layer, g[5])
    return x.reshape(batch, seq, d)


def kernel(x_prompt, x_sample, mem_prompt, mem_sample, norm_gains, w_in, w_out, b_w_spatial, b_b_spatial,
           b_ln_gain, c_lambda, c_subln_gain, d_w_fourier, mem_norm_gain, ca_w_q, ca_w_kv, ca_w_o, ffn_w_up,
           ffn_conv_w, ffn_conv_b, ffn_w_down):
    p = dict(norm_gains=norm_gains, w_in=w_in.astype(BF16), w_out=w_out.astype(BF16), b_w_spatial=b_w_spatial,
             b_b_spatial=b_b_spatial, b_ln_gain=b_ln_gain, c_lambda=c_lambda, c_subln_gain=c_subln_gain,
             d_w_fourier=d_w_fourier, mem_norm_gain=mem_norm_gain, ca_w_q=ca_w_q.astype(BF16),
             ca_w_kv=ca_w_kv.astype(BF16), ca_w_o=ca_w_o.astype(BF16), ffn_w_up=ffn_w_up.astype(BF16),
             ffn_conv_w=ffn_conv_w, ffn_conv_b=ffn_conv_b, ffn_w_down=ffn_w_down.astype(BF16))
    return _trunk(x_prompt, mem_prompt, p), _trunk(x_sample, mem_sample, p)
```

```python
import functools
import math

import jax
import jax.numpy as jnp
from jax import lax
from jax.experimental import pallas as pl
from jax.experimental.pallas import tpu as pltpu

F32 = jnp.float32
BF16 = jnp.bfloat16

EPS = 1e-6
ROPE_THETA = 500000.0
ROPE_FRACTION = 4
LANES = 128
HEAD_DIM = 128
A_HEADS = 8
DILATED_CFGS = ((128, 1), (512, 4), (2048, 16))
BAND_RADIUS = 64
B_GROUPS = 8
CHUNK = 128
C_HEADS = 12
C_HALF = 64
D_GROUPS = 4
CA_HEADS = 4
CONV_W = 3
NEG_BIG = -1e30
MIB = 1024 * 1024


def _params(semantics, vmem_mib):
    return pltpu.CompilerParams(dimension_semantics=semantics, vmem_limit_bytes=vmem_mib * MIB)


def _rms(x, g):
    return x * lax.rsqrt(jnp.mean(x * x, axis=-1, keepdims=True) + EPS) * g


def _dot_nt(a, b):
    return lax.dot_general(a, b, (((1,), (1,)), ((), ())), preferred_element_type=F32)


def _norm_mm_kernel(x_ref, g_ref, w_ref, o_ref, xn_ref):
    @pl.when(pl.program_id(1) == 0)
    def _():
        xn_ref[...] = _rms(x_ref[...], g_ref[...]).astype(BF16)

    o_ref[...] = jnp.dot(xn_ref[...], w_ref[...], preferred_element_type=F32).astype(o_ref.dtype)


def _norm_matmul(x, g, w, layer, out_dtype, tm=1024, tn=512):
    t, k = x.shape
    n = w.shape[2]
    tm = min(tm, t)
    return pl.pallas_call(
        _norm_mm_kernel,
        grid=(t // tm, n // tn),
        in_specs=[pl.BlockSpec((tm, k), lambda i, j: (i, 0)),
                  pl.BlockSpec((1, k), lambda i, j: (0, 0)),
                  pl.BlockSpec((None, k, tn), lambda i, j: (layer, 0, j))],
        out_specs=pl.BlockSpec((tm, tn), lambda i, j: (i, j)),
        out_shape=jax.ShapeDtypeStruct((t, n), out_dtype),
        scratch_shapes=[pltpu.VMEM((tm, k), BF16)],
        compiler_params=_params(("parallel", "arbitrary"), 40),
        name="norm_matmul",
    )(x, g.reshape(1, k), w)


def _mm_norm_res_kernel(a_ref, w_ref, g_ref, r_ref, o_ref):
    y = jnp.dot(a_ref[...], w_ref[...], preferred_element_type=F32)
    o_ref[...] = r_ref[...] + _rms(y, g_ref[...])


def _matmul_norm_res(a, w, layer, g, res, tm=512):
    t, k = a.shape
    n = w.shape[2]
    return pl.pallas_call(
        _mm_norm_res_kernel,
        grid=(t // tm,),
        in_specs=[pl.BlockSpec((tm, k), lambda i: (i, 0)),
                  pl.BlockSpec((None, k, n), lambda i: (layer, 0, 0)),
                  pl.BlockSpec((1, n), lambda i: (0, 0)),
                  pl.BlockSpec((tm, n), lambda i: (i, 0))],
        out_specs=pl.BlockSpec((tm, n), lambda i: (i, 0)),
        out_shape=jax.ShapeDtypeStruct((t, n), F32),
        compiler_params=_params(("parallel",), 48),
        name="matmul_norm_res",
    )(a, w, g.reshape(1, n), res)


def _rope_tables(seq, head_w):
    rd = head_w // ROPE_FRACTION
    half = rd // 2
    inv = ROPE_THETA ** (-jnp.arange(half, dtype=F32) / half)
    ang = jnp.arange(seq, dtype=F32)[:, None] * inv[None, :]
    cos = jnp.cos(ang)
    sin = jnp.sin(ang)
    rest = head_w - rd
    c_head = jnp.concatenate([cos, cos, jnp.ones((seq, rest), F32)], axis=-1)
    s_head = jnp.concatenate([-sin, sin, jnp.zeros((seq, rest), F32)], axis=-1)
    reps = LANES // head_w
    return jnp.tile(c_head, (1, reps)), jnp.tile(s_head, (1, reps))


def _rope_block(x, c, s, head_w):
    half = head_w // ROPE_FRACTION // 2
    first = (lax.broadcasted_iota(jnp.int32, x.shape, 1) & (head_w - 1)) < half
    partner = jnp.where(first, pltpu.roll(x, LANES - half, axis=1), pltpu.roll(x, half, axis=1))
    return x * c + partner * s


def _even_prep_kernel(x_ref, c_ref, s_ref, *refs, q_scale, n_blocks):
    out_refs, y_ref = refs[:-1], refs[-1]
    part = pl.program_id(1)
    tm = x_ref.shape[0]

    def rope(scale):
        c = c_ref[...]
        s = s_ref[...]
        for b in range(n_blocks):
            cols = slice(b * LANES, (b + 1) * LANES)
            y_ref[b] = _rope_block(x_ref[:, cols].astype(F32), c, s, HEAD_DIM) * scale

    @pl.when(part == 0)
    def _():
        rope(q_scale)

    @pl.when(part == 1)
    def _():
        rope(1.0)

    @pl.when(part == 2)
    def _():
        for b in range(n_blocks):
            y_ref[b] = x_ref[:, b * LANES:(b + 1) * LANES].astype(F32)

    for (_, dil), o_ref in zip(DILATED_CFGS, out_refs):
        for r in range(dil):
            for b in range(n_blocks):
                rows = y_ref[b, pl.ds(r, tm // dil, stride=dil), :]
                o_ref[0, r, :, b * LANES:(b + 1) * LANES] = rows.astype(BF16)


def _even_prep(proj, batch, seq, q_scale, tm=256):
    a_w = A_HEADS * HEAD_DIM
    cos_t, sin_t = _rope_tables(seq, HEAD_DIM)
    spb = seq // tm
    kern = functools.partial(_even_prep_kernel, q_scale=q_scale, n_blocks=a_w // LANES)
    return pl.pallas_call(
        kern,
        grid=(batch * spb, 3),
        in_specs=[pl.BlockSpec((tm, a_w), lambda i, p: (i, p)),
                  pl.BlockSpec((tm, LANES), lambda i, p: (i % spb, 0)),
                  pl.BlockSpec((tm, LANES), lambda i, p: (i % spb, 0))],
        out_specs=[pl.BlockSpec((1, dil, tm // dil, a_w), lambda i, p: (i // spb, 0, i % spb, p))
                   for _, dil in DILATED_CFGS],
        out_shape=[jax.ShapeDtypeStruct((batch, dil, seq // dil, 3 * a_w), BF16) for _, dil in DILATED_CFGS],
        scratch_shapes=[pltpu.VMEM((a_w // LANES, tm, LANES), F32)],
        compiler_params=_params(("parallel", "arbitrary"), 32),
        name="even_prep",
    )(proj, cos_t, sin_t)


def _band_attn_kernel(q_ref, kp_ref, kc_ref, kn_ref, vp_ref, vc_ref, vn_ref, o_ref, lse_ref, *, bq, sub_len):
    q0 = pl.program_id(1) * bq
    sq = 2 * BAND_RADIUS
    win = sq + 2 * BAND_RADIUS
    n_sub = bq // sq
    rel = (lax.broadcasted_iota(jnp.int32, (sq, win), 1) - BAND_RADIUS
           - lax.broadcasted_iota(jnp.int32, (sq, win), 0))
    band = (rel <= BAND_RADIUS) & (rel >= -BAND_RADIUS)
    biases = []
    for u in range(n_sub):
        kpos = q0 + u * sq - BAND_RADIUS + lax.broadcasted_iota(jnp.int32, (sq, win), 1)
        ok = band & (kpos >= 0) & (kpos < sub_len)
        biases.append(jnp.where(ok, 0.0, NEG_BIG))

    def window(p_ref, c_ref, n_ref, cols, u):
        parts = []
        if u == 0:
            parts.append(p_ref[0, :, cols])
        lo = max(u * sq - BAND_RADIUS, 0)
        hi = min(u * sq + sq + BAND_RADIUS, bq)
        parts.append(c_ref[0, lo:hi, cols])
        if u == n_sub - 1:
            parts.append(n_ref[0, :, cols])
        return jnp.concatenate(parts, axis=0)

    tiles = [(h, u) for h in range(A_HEADS) for u in range(n_sub)]

    def scores(h, u):
        cols = slice(h * HEAD_DIM, (h + 1) * HEAD_DIM)
        q = q_ref[0, u * sq:(u + 1) * sq, cols]
        return _dot_nt(q, window(kp_ref, kc_ref, kn_ref, cols, u)) + biases[u]

    s_next = scores(*tiles[0])
    for n, (h, u) in enumerate(tiles):
        cols = slice(h * HEAD_DIM, (h + 1) * HEAD_DIM)
        rows = slice(u * sq, (u + 1) * sq)
        s = s_next
        if n + 1 < len(tiles):
            s_next = scores(*tiles[n + 1])
        m = jnp.max(s, axis=-1, keepdims=True)
        p = jnp.exp(s - m)
        l = jnp.sum(p, axis=-1, keepdims=True)
        o = jnp.dot(p.astype(BF16), window(vp_ref, vc_ref, vn_ref, cols, u), preferred_element_type=F32)
        o_ref[0, rows, cols] = o / l
        lse_ref[0, rows, cols] = jnp.broadcast_to(m + jnp.log(l), (sq, HEAD_DIM))


def _band_attention(qkv, bq=256):
    n_seq, sub_len, _ = qkv.shape
    a_w = A_HEADS * HEAD_DIM
    hb = bq // BAND_RADIUS
    last = sub_len // BAND_RADIUS - 1
    kern = functools.partial(_band_attn_kernel, bq=bq, sub_len=sub_len)

    def halo_specs(col):
        return [pl.BlockSpec((1, BAND_RADIUS, a_w), lambda s, i: (s, jnp.maximum(i * hb - 1, 0), col)),
                pl.BlockSpec((1, bq, a_w), lambda s, i: (s, i, col)),
                pl.BlockSpec((1, BAND_RADIUS, a_w), lambda s, i: (s, jnp.minimum((i + 1) * hb, last), col))]

    out_spec = pl.BlockSpec((1, bq, a_w), lambda s, i: (s, i, 0))
    return pl.pallas_call(
        kern,
        grid=(n_seq, sub_len // bq),
        in_specs=[pl.BlockSpec((1, bq, a_w), lambda s, i: (s, i, 0))] + halo_specs(1) + halo_specs(2),
        out_specs=[out_spec, out_spec],
        out_shape=[jax.ShapeDtypeStruct((n_seq, sub_len, a_w), F32)] * 2,
        compiler_params=_params(("parallel", "parallel"), 40),
        name="band_attn",
    )(*([qkv] * 7))


def _combine_kernel(o1, l1, o4, l4, o16, l16, y_ref, so4, sl4, so16, sl16):
    tm = y_ref.shape[0]
    for dil, src, dst in ((4, o4, so4), (4, l4, sl4), (16, o16, so16), (16, l16, sl16)):
        for r in range(dil):
            for h in range(A_HEADS):
                dst[h, pl.ds(r, tm // dil, stride=dil), :] = src[0, r, :, h * HEAD_DIM:(h + 1) * HEAD_DIM]
    for h in range(A_HEADS):
        cols = slice(h * HEAD_DIM, (h + 1) * HEAD_DIM)
        a, b, c = l1[0, 0, :, cols], sl4[h], sl16[h]
        m = jnp.maximum(jnp.maximum(a, b), c)
        wa, wb, wc = jnp.exp(a - m), jnp.exp(b - m), jnp.exp(c - m)
        num = wa * o1[0, 0, :, cols] + wb * so4[h] + wc * so16[h]
        y_ref[:, cols] = (num / (wa + wb + wc)).astype(y_ref.dtype)


def _combine_configs(parts, batch, seq, tm=256):
    a_w = A_HEADS * HEAD_DIM
    spb = seq // tm
    flat, specs = [], []
    for (_, dil), pair in zip(DILATED_CFGS, parts):
        for arr in pair:
            flat.append(arr.reshape(batch, dil, seq // dil, a_w))
            specs.append(pl.BlockSpec((1, dil, tm // dil, a_w), lambda i: (i // spb, 0, i % spb, 0)))
    return pl.pallas_call(
        _combine_kernel,
        grid=(batch * spb,),
        in_specs=specs,
        out_specs=pl.BlockSpec((tm, a_w), lambda i: (i, 0)),
        out_shape=jax.ShapeDtypeStruct((batch * seq, a_w), BF16),
        scratch_shapes=[pltpu.VMEM((A_HEADS, tm, HEAD_DIM), F32)] * 4,
        compiler_params=_params(("parallel",), 40),
        name="dilated_combine",
    )(*flat)


def _gating_kernel(u_ref, v_ref, w_ref, b_ref, g_ref, o_ref, *, n_chunks):
    w = w_ref[0]
    bias = b_ref[0]
    gain = g_ref[0]
    for c in range(n_chunks):
        rows = pl.ds(c * CHUNK, CHUNK)
        v = jax.nn.gelu(v_ref[rows, :].astype(F32))
        mu = jnp.mean(v, axis=-1, keepdims=True)
        d = v - mu
        var = jnp.mean(d * d, axis=-1, keepdims=True)
        vn = d * lax.rsqrt(var + EPS) * gain
        sv = jnp.dot(w, vn.astype(BF16), preferred_element_type=F32) + bias
        o_ref[rows, :] = (jax.nn.gelu(u_ref[rows, :].astype(F32)) * sv).astype(o_ref.dtype)


def _spatial_gating(proj, w_s, b_s, g_ln, u_col, v_col, tm=2048):
    t = proj.shape[0]
    tm = min(tm, t)
    groups = w_s.shape[0]
    bias = jnp.broadcast_to(b_s[:, :, None], (groups, CHUNK, LANES))
    kern = functools.partial(_gating_kernel, n_chunks=tm // CHUNK)
    return pl.pallas_call(
        kern,
        grid=(t // tm, groups),
        in_specs=[pl.BlockSpec((tm, LANES), lambda i, g: (i, u_col + g)),
                  pl.BlockSpec((tm, LANES), lambda i, g: (i, v_col + g)),
                  pl.BlockSpec((1, CHUNK, CHUNK), lambda i, g: (g, 0, 0)),
                  pl.BlockSpec((1, CHUNK, LANES), lambda i, g: (g, 0, 0)),
                  pl.BlockSpec((1, 1, LANES), lambda i, g: (g, 0, 0))],
        out_specs=pl.BlockSpec((tm, LANES), lambda i, g: (i, g)),
        out_shape=jax.ShapeDtypeStruct((t, groups * LANES), BF16),
        compiler_params=_params(("parallel", "parallel"), 32),
        name="spatial_gating",
    )(proj, proj, w_s.astype(BF16), bias, g_ln.reshape(groups, 1, LANES))


V_ROWS = 2 * C_HALF + 16
DIFF_BK = 512


def _diff_prep_kernel(x_ref, c_ref, s_ref, qt_ref, k_ref, vt_ref, *, q_scale, n_blocks):
    part = pl.program_id(1)
    tm = x_ref.shape[0]

    @pl.when(part == 0)
    def _():
        c = c_ref[...]
        s = s_ref[...]
        for h in range(n_blocks):
            cols = slice(h * LANES, (h + 1) * LANES)
            qt_ref[0, h] = (_rope_block(x_ref[:, cols].astype(F32), c, s, C_HALF) * q_scale).T.astype(BF16)

    @pl.when(part == 1)
    def _():
        c = c_ref[...]
        s = s_ref[...]
        for h in range(n_blocks):
            cols = slice(h * LANES, (h + 1) * LANES)
            k_ref[:, cols] = _rope_block(x_ref[:, cols].astype(F32), c, s, C_HALF).astype(BF16)

    @pl.when(part == 2)
    def _():
        ones = jnp.ones((V_ROWS - LANES, tm), BF16)
        for h in range(n_blocks):
            vt_ref[0, h, 0, 0:LANES, :] = x_ref[:, h * LANES:(h + 1) * LANES].astype(F32).T.astype(BF16)
            vt_ref[0, h, 0, LANES:, :] = ones


def _diff_prep(proj, batch, seq, q_scale, bk, tm=256):
    c_w = C_HEADS * 2 * C_HALF
    cos_t, sin_t = _rope_tables(seq, C_HALF)
    spb = seq // tm
    per_chunk = bk // tm
    kern = functools.partial(_diff_prep_kernel, q_scale=q_scale, n_blocks=C_HEADS)
    return pl.pallas_call(
        kern,
        grid=(batch * spb, 3),
        in_specs=[pl.BlockSpec((tm, c_w), lambda i, p: (i, p)),
                  pl.BlockSpec((tm, LANES), lambda i, p: (i % spb, 0)),
                  pl.BlockSpec((tm, LANES), lambda i, p: (i % spb, 0))],
        out_specs=[pl.BlockSpec((1, C_HEADS, LANES, tm), lambda i, p: (i // spb, 0, 0, i % spb)),
                   pl.BlockSpec((tm, c_w), lambda i, p: (i, 0)),
                   pl.BlockSpec((1, C_HEADS, 1, V_ROWS, tm),
                                lambda i, p: (i // spb, 0, (i % spb) // per_chunk, 0, (i % spb) % per_chunk))],
        out_shape=[jax.ShapeDtypeStruct((batch, C_HEADS, LANES, seq), BF16),
                   jax.ShapeDtypeStruct((batch * seq, c_w), BF16),
                   jax.ShapeDtypeStruct((batch, C_HEADS, seq // bk, V_ROWS, bk), BF16)],
        compiler_params=_params(("parallel", "arbitrary"), 40),
        name="diff_prep",
    )(proj, cos_t, sin_t)


def _diff_attn_kernel(qt_ref, k_ref, vt_ref, lam_ref, g_ref, o_ref, s_ref, mc_ref, m_ref, acc_ref,
                      *, bq, bk, seq, lam_init, unroll):
    n = seq // bk
    sub = 8
    qt = qt_ref[0, 0]
    row = lax.broadcasted_iota(jnp.int32, qt.shape, 0)
    zero = jnp.zeros_like(qt)
    qs = (jnp.where(row < C_HALF, qt, zero), jnp.where(row < C_HALF, zero, qt))

    def scores(j, slot):
        kc = k_ref[0, pl.ds(pl.multiple_of(j * bk, bk), bk), :]
        for br in range(2):
            s = jnp.dot(kc, qs[br], preferred_element_type=F32)
            s_ref[slot, br] = s
            mc = jnp.max(s.reshape(bk // sub, sub, bq), axis=0)
            mc_ref[slot, br] = jnp.broadcast_to(jnp.max(mc, axis=0, keepdims=True), (sub, bq))

    def accumulate(j, slot):
        vt = vt_ref[0, 0, j]
        for br in range(2):
            m_old = m_ref[br]
            m_new = jnp.maximum(m_old, mc_ref[slot, br])
            alpha = jnp.exp2(m_old - m_new)
            m_ref[br] = m_new
            s3 = s_ref[slot, br].reshape(bk // sub, sub, bq)
            p = jnp.exp2(s3 - m_new[None]).reshape(bk, bq).astype(BF16)
            pv = jnp.dot(vt, p, preferred_element_type=F32)
            acc3 = acc_ref[br].reshape(V_ROWS // sub, sub, bq)
            acc_ref[br] = (alpha[None] * acc3).reshape(V_ROWS, bq) + pv

    m_ref[...] = jnp.full(m_ref.shape, NEG_BIG, F32)
    acc_ref[...] = jnp.zeros(acc_ref.shape, F32)
    scores(0, 0)

    def body(i, carry):
        j = unroll * i
        for u in range(unroll):
            scores(jnp.minimum(j + u + 1, n - 1), (u + 1) % 2)
            accumulate(j + u, u % 2)
        return carry

    lax.fori_loop(0, n // unroll, body, 0)

    lam = lam_ref[...]
    lam_full = (jnp.exp(jnp.sum(lam[0:1] * lam[1:2], keepdims=True))
                - jnp.exp(jnp.sum(lam[2:3] * lam[3:4], keepdims=True)) + lam_init)

    def normalised(a):
        num = a[0:LANES].reshape(LANES // sub, sub, bq)
        return (num / a[LANES:LANES + sub][None]).reshape(LANES, bq)

    o = normalised(acc_ref[0]) - lam_full * normalised(acc_ref[1])
    y = o * lax.rsqrt(jnp.mean(o * o, axis=0, keepdims=True) + EPS) * g_ref[...] * (1.0 - lam_init)
    o_ref[0] = y.T.astype(o_ref.dtype)


def _diff_attention(qt, k, vt, lam, subln_g, batch, seq, layer_idx, bq=512, bk=512, unroll=8):
    c_w = C_HEADS * 2 * C_HALF
    lam_init = 0.8 - 0.6 * math.exp(-0.3 * layer_idx)
    n = seq // bk
    assert unroll % 2 == 0 and n % unroll == 0 and vt.shape[2] == n
    kern = functools.partial(_diff_attn_kernel, bq=bq, bk=bk, seq=seq, lam_init=lam_init, unroll=unroll)
    out = pl.pallas_call(
        kern,
        grid=(batch, C_HEADS, seq // bq),
        in_specs=[pl.BlockSpec((1, 1, LANES, bq), lambda b, h, i: (b, h, 0, i)),
                  pl.BlockSpec((1, seq, LANES), lambda b, h, i: (b, 0, h)),
                  pl.BlockSpec((1, 1, n, V_ROWS, bk), lambda b, h, i: (b, h, 0, 0, 0)),
                  pl.BlockSpec((4, C_HALF), lambda b, h, i: (0, 0)),
                  pl.BlockSpec((LANES, bq), lambda b, h, i: (0, 0))],
        out_specs=pl.BlockSpec((1, bq, LANES), lambda b, h, i: (b, i, h)),
        out_shape=jax.ShapeDtypeStruct((batch, seq, c_w), BF16),
        scratch_shapes=[pltpu.VMEM((2, 2, bk, bq), F32),
                        pltpu.VMEM((2, 2, 8, bq), F32),
                        pltpu.VMEM((2, 8, bq), F32),
                        pltpu.VMEM((2, V_ROWS, bq), F32)],
        compiler_params=_params(("parallel", "parallel", "arbitrary"), 48),
        name="diff_attn",
    )(qt, k.reshape(batch, seq, c_w), vt, lam, jnp.broadcast_to(subln_g[:, None], (LANES, bq)))
    return out.reshape(batch * seq, c_w)


def _fourier_wprep_kernel(c_ref, s_ref, w_ref, o_ref, *, scale):
    w = w_ref[0]
    mr = jnp.dot(c_ref[...], w, preferred_element_type=F32, precision=lax.Precision.HIGHEST)
    mi = jnp.dot(s_ref[...], w, preferred_element_type=F32, precision=lax.Precision.HIGHEST)
    o_ref[0, :, :LANES] = (mr * scale).astype(BF16)
    o_ref[0, :, LANES:] = (-mi * scale).astype(BF16)


def _fourier_weights(w_f, seq):
    groups, c, _ = w_f.shape
    idx = jnp.arange(c, dtype=jnp.int32)
    ang = (2.0 * math.pi / c) * ((idx[:, None] * idx[None, :]) % c).astype(F32)
    kern = functools.partial(_fourier_wprep_kernel, scale=1.0 / math.sqrt(seq * c))
    return pl.pallas_call(
        kern,
        grid=(groups,),
        in_specs=[pl.BlockSpec((c, c), lambda g: (0, 0)),
                  pl.BlockSpec((c, c), lambda g: (0, 0)),
                  pl.BlockSpec((1, c, c), lambda g: (g, 0, 0))],
        out_specs=pl.BlockSpec((1, c, 2 * c), lambda g: (g, 0, 0)),
        out_shape=jax.ShapeDtypeStruct((groups, c, 2 * c), BF16),
        compiler_params=_params(("parallel",), 32),
        name="fourier_wprep",
    )(jnp.cos(ang), jnp.sin(ang), w_f)


def _chan_mix_kernel(z_ref, m_ref, o_ref):
    o_ref[...] = jnp.dot(z_ref[...], m_ref[0], preferred_element_type=F32).astype(BF16)


def _fourier_channel_mix(proj, m_c, z_col, tm=2048):
    t = proj.shape[0]
    tm = min(tm, t)
    groups = m_c.shape[0]
    return pl.pallas_call(
        _chan_mix_kernel,
        grid=(t // tm, groups),
        in_specs=[pl.BlockSpec((tm, LANES), lambda i, g: (i, z_col + g)),
                  pl.BlockSpec((1, LANES, 2 * LANES), lambda i, g: (g, 0, 0))],
        out_specs=pl.BlockSpec((tm, 2 * LANES), lambda i, g: (i, g)),
        out_shape=jax.ShapeDtypeStruct((t, groups * 2 * LANES), BF16),
        compiler_params=_params(("parallel", "parallel"), 32),
        name="fourier_channel_mix",
    )(proj, m_c)


def _dft_stage1_kernel(u_ref, fr_ref, fi_ref, y_ref, *, pairs):
    u = u_ref[0]
    p = jnp.dot(fr_ref[...], u, preferred_element_type=F32)
    q = jnp.dot(fi_ref[...], u, preferred_element_type=F32)
    for j in range(pairs):
        re = slice(2 * j * LANES, (2 * j + 1) * LANES)
        im = slice((2 * j + 1) * LANES, (2 * j + 2) * LANES)
        y_ref[0, :, re] = (p[:, re] - q[:, im]).astype(BF16)
        y_ref[0, :, im] = (q[:, re] + p[:, im]).astype(BF16)


def _dft_stage2_kernel(y_ref, gc_ref, gs_ref, o_ref, *, groups):
    for t in range(y_ref.shape[1]):
        y = y_ref[0, t]
        a = jnp.dot(gc_ref[t], y, preferred_element_type=F32)
        b = jnp.dot(gs_ref[t], y, preferred_element_type=F32)
        for g in range(groups):
            re = slice(2 * g * LANES, (2 * g + 1) * LANES)
            im = slice((2 * g + 1) * LANES, (2 * g + 2) * LANES)
            out_cols = slice((t * groups + g) * LANES, (t * groups + g + 1) * LANES)
            o_ref[0, :, out_cols] = (a[:, re] + b[:, im]).astype(o_ref.dtype)


def _seq_dft_real(u, batch, seq, groups, tc=4096, kb=4):
    n1 = 128 if seq >= 16384 else 64
    n2 = seq // n1
    wc = groups * 2 * LANES
    i1 = jnp.arange(n1, dtype=jnp.int32)
    ang1 = (2.0 * math.pi / n1) * ((i1[:, None] * i1[None, :]) % n1).astype(F32)
    fr = jnp.cos(ang1).astype(BF16)
    fi = (-jnp.sin(ang1)).astype(BF16)
    cols = n2 * wc
    tc = min(tc, cols)
    y = pl.pallas_call(
        functools.partial(_dft_stage1_kernel, pairs=tc // (2 * LANES)),
        grid=(batch, cols // tc),
        in_specs=[pl.BlockSpec((1, n1, tc), lambda b, j: (b, 0, j)),
                  pl.BlockSpec((n1, n1), lambda b, j: (0, 0)),
                  pl.BlockSpec((n1, n1), lambda b, j: (0, 0))],
        out_specs=pl.BlockSpec((1, n1, tc), lambda b, j: (b, 0, j)),
        out_shape=jax.ShapeDtypeStruct((batch, n1, cols), BF16),
        compiler_params=_params(("parallel", "parallel"), 32),
        name="dft_stage1",
    )(u.reshape(batch, n1, cols), fr, fi)
    i2 = jnp.arange(n2, dtype=jnp.int32)
    tw = (i2[None, None, :] * (i1[:, None, None] + n1 * i2[None, :, None])) % seq
    ang2 = (2.0 * math.pi / seq) * tw.astype(F32)
    gc = jnp.cos(ang2).astype(BF16)
    gs = jnp.sin(ang2).astype(BF16)
    out = pl.pallas_call(
        functools.partial(_dft_stage2_kernel, groups=groups),
        grid=(batch, n1 // kb),
        in_specs=[pl.BlockSpec((1, kb, n2, wc), lambda b, k: (b, k, 0, 0)),
                  pl.BlockSpec((kb, n2, n2), lambda b, k: (k, 0, 0)),
                  pl.BlockSpec((kb, n2, n2), lambda b, k: (k, 0, 0))],
        out_specs=pl.BlockSpec((1, n2, kb * groups * LANES), lambda b, k: (b, 0, k)),
        out_shape=jax.ShapeDtypeStruct((batch, n2, n1 * groups * LANES), BF16),
        compiler_params=_params(("parallel", "parallel"), 32),
        name="dft_stage2",
    )(y.reshape(batch, n1, n2, wc), gc, gs)
    return out.reshape(batch * seq, groups * LANES)


def _mix_out_cross_attn_kernel(mix_ref, x_ref, wout_ref, g1_ref, g2_ref, wq_ref, kv_ref, wo_ref, g3_ref, o_ref,
                               *, heads, scale, halves):
    ca_w = heads * HEAD_DIM
    hr = mix_ref.shape[0] // halves
    rows = [slice(r * hr, (r + 1) * hr) for r in range(halves)]
    ys = [jnp.dot(mix_ref[r, :], wout_ref[...], preferred_element_type=F32) for r in rows]
    x1s = [x_ref[r, :] + _rms(y, g1_ref[...]) for r, y in zip(rows, ys)]
    qs = [jnp.dot(_rms(x1, g2_ref[...]).astype(BF16), wq_ref[...], preferred_element_type=F32).astype(BF16)
          for x1 in x1s]
    cas = []
    for q in qs:
        outs = []
        for h in range(heads):
            cols = slice(h * HEAD_DIM, (h + 1) * HEAD_DIM)
            s = _dot_nt(q[:, cols], kv_ref[0, :, cols]) * scale
            m = jnp.max(s, axis=-1, keepdims=True)
            p = jnp.exp(s - m)
            l = jnp.sum(p, axis=-1, keepdims=True)
            v = kv_ref[0, :, ca_w + h * HEAD_DIM:ca_w + (h + 1) * HEAD_DIM]
            outs.append((jnp.dot(p.astype(BF16), v, preferred_element_type=F32) / l).astype(BF16))
        cas.append(jnp.concatenate(outs, axis=1))
    zs = [jnp.dot(ca, wo_ref[...], preferred_element_type=F32) for ca in cas]
    for r, x1, z in zip(rows, x1s, zs):
        o_ref[r, :] = x1 + _rms(z, g3_ref[...])


def _mix_out_cross_attn(mix, x, w_out, w_q, w_o, layer, g1, g2, g3, kv, batch, seq, tm=512):
    t, d = x.shape
    ca_w = CA_HEADS * HEAD_DIM
    mem_len = kv.shape[0] // batch
    spb = seq // tm
    kern = functools.partial(_mix_out_cross_attn_kernel, heads=CA_HEADS, scale=HEAD_DIM ** -0.5, halves=2)
    once = pl.Buffered(1)
    vec = lambda: pl.BlockSpec((1, d), lambda i: (0, 0))
    return pl.pallas_call(
        kern,
        grid=(t // tm,),
        in_specs=[pl.BlockSpec((tm, d), lambda i: (i, 0)),
                  pl.BlockSpec((tm, d), lambda i: (i, 0)),
                  pl.BlockSpec((None, d, d), lambda i: (layer, 0, 0), pipeline_mode=once),
                  vec(), vec(),
                  pl.BlockSpec((None, d, ca_w), lambda i: (layer, 0, 0), pipeline_mode=once),
                  pl.BlockSpec((1, mem_len, 2 * ca_w), lambda i: (i // spb, 0, 0)),
                  pl.BlockSpec((None, ca_w, d), lambda i: (layer, 0, 0), pipeline_mode=once),
                  vec()],
        out_specs=pl.BlockSpec((tm, d), lambda i: (i, 0)),
        out_shape=jax.ShapeDtypeStruct((t, d), F32),
        compiler_params=_params(("parallel",), 56),
        name="mix_out_cross_attn",
    )(mix, x, w_out, g1.reshape(1, d), g2.reshape(1, d), w_q, kv.reshape(batch, mem_len, 2 * ca_w), w_o,
      g3.reshape(1, d))


HALO = 16


def _ffn_kernel(x_ref, xp_ref, xn_ref, g4_ref, wg_ref, wv_ref, cwg_ref, cwv_ref, cbg_ref, cbv_ref,
                wd_ref, g5_ref, o_ref, xe_ref, zg_ref, zv_ref, *, tm, tc, tiles_per_seq, nf):
    acc_ref = o_ref
    i = pl.program_id(0)
    f = pl.program_id(1)

    @pl.when(f == 0)
    def _():
        g4 = g4_ref[...]
        pos = i % tiles_per_seq
        prev = _rms(xp_ref[...], g4) * jnp.where(pos == 0, 0.0, 1.0)
        nxt = _rms(xn_ref[...], g4) * jnp.where(pos == tiles_per_seq - 1, 0.0, 1.0)
        xe_ref[0:HALO, :] = prev.astype(BF16)
        xe_ref[HALO:HALO + tm, :] = _rms(x_ref[...], g4).astype(BF16)
        xe_ref[HALO + tm:, :] = nxt.astype(BF16)
        acc_ref[...] = jnp.zeros(acc_ref.shape, F32)

    def conv(z_ref, c, cw, cb, lo, rows):
        lo = HALO + lo
        return (z_ref[c, lo - 1:lo - 1 + rows, :] * cw[0:1] + z_ref[c, lo:lo + rows, :] * cw[1:2]
                + z_ref[c, lo + 1:lo + 1 + rows, :] * cw[2:3] + cb)

    xe = xe_ref[...]
    n_sub = zg_ref.shape[0]
    for c in range(n_sub):
        cols = slice(c * tc, (c + 1) * tc)
        zg_ref[c] = jnp.dot(xe, wg_ref[:, cols], preferred_element_type=F32)
        zv_ref[c] = jnp.dot(xe, wv_ref[:, cols], preferred_element_type=F32)
    for c in range(n_sub):
        cols = slice(c * tc, (c + 1) * tc)
        pieces = 2 if c == n_sub - 1 else 1
        rows = tm // pieces
        for r in range(pieces):
            lo = r * rows
            h = (jax.nn.gelu(conv(zg_ref, c, cwg_ref[:, cols], cbg_ref[:, cols], lo, rows))
                 * conv(zv_ref, c, cwv_ref[:, cols], cbv_ref[:, cols], lo, rows))
            acc_ref[lo:lo + rows, :] += jnp.dot(h.astype(BF16), wd_ref[cols, :], preferred_element_type=F32)

    @pl.when(f == nf - 1)
    def _():
        o_ref[...] = x_ref[...] + _rms(acc_ref[...], g5_ref[...])


def _conv_ffn(x, seq, g4, w_up, conv_w, conv_b, w_down, layer, g5, tm=1024, tf=512, tc=256):
    t, d = x.shape
    d_ff = w_down.shape[1]
    nf = d_ff // tf
    hb = tm // HALO
    last_hb = t // HALO - 1
    kern = functools.partial(_ffn_kernel, tm=tm, tc=tc, tiles_per_seq=seq // tm, nf=nf)
    cb = conv_b.reshape(1, 2 * d_ff)
    return pl.pallas_call(
        kern,
        grid=(t // tm, nf),
        in_specs=[pl.BlockSpec((tm, d), lambda i, f: (i, 0), pipeline_mode=pl.Buffered(1)),
                  pl.BlockSpec((HALO, d), lambda i, f: (jnp.maximum(i * hb - 1, 0), 0)),
                  pl.BlockSpec((HALO, d), lambda i, f: (jnp.minimum((i + 1) * hb, last_hb), 0)),
                  pl.BlockSpec((1, d), lambda i, f: (0, 0)),
                  pl.BlockSpec((None, d, tf), lambda i, f: (layer, 0, f)),
                  pl.BlockSpec((None, d, tf), lambda i, f: (layer, 0, nf + f)),
                  pl.BlockSpec((CONV_W, tf), lambda i, f: (0, f)),
                  pl.BlockSpec((CONV_W, tf), lambda i, f: (0, nf + f)),
                  pl.BlockSpec((1, tf), lambda i, f: (0, f)),
                  pl.BlockSpec((1, tf), lambda i, f: (0, nf + f)),
                  pl.BlockSpec((None, tf, d), lambda i, f: (layer, f, 0)),
                  pl.BlockSpec((1, d), lambda i, f: (0, 0))],
        out_specs=pl.BlockSpec((tm, d), lambda i, f: (i, 0)),
        out_shape=jax.ShapeDtypeStruct((t, d), F32),
        scratch_shapes=[pltpu.VMEM((tm + 2 * HALO, d), BF16),
                        pltpu.VMEM((tf // tc, tm + 2 * HALO, tc), F32),
                        pltpu.VMEM((tf // tc, tm + 2 * HALO, tc), F32)],
        compiler_params=_params(("parallel", "arbitrary"), 58),
        name="conv_ffn",
    )(x, x, x, g4.reshape(1, d), w_up, w_up, conv_w, conv_w, cb, cb, w_down, g5.reshape(1, d))


def _trunk(x3, mem3, p):
    batch, seq, d = x3.shape
    x = x3.reshape(batch * seq, d)
    mem = mem3.reshape(-1, d)
    depth = p["w_in"].shape[0]
    a_w = A_HEADS * HEAD_DIM
    c_w = C_HEADS * 2 * C_HALF
    for layer in range(depth):
        g = p["norm_gains"][layer]
        proj = _norm_matmul(x, g[0], p["w_in"], layer, BF16)
        if layer % 2 == 0:
            e = layer // 2
            qkvs = _even_prep(proj, batch, seq, HEAD_DIM ** -0.5)
            parts = [_band_attention(qkv.reshape(batch * dil, seq // dil, 3 * a_w))
                     for (_, dil), qkv in zip(DILATED_CFGS, qkvs)]
            ya = _combine_configs(parts, batch, seq)
            yb = _spatial_gating(proj, p["b_w_spatial"][e], p["b_b_spatial"][e], p["b_ln_gain"][e],
                                 3 * A_HEADS, 3 * A_HEADS + B_GROUPS)
            mix = jnp.concatenate([ya, yb], axis=-1)
        else:
            o = layer // 2
            qt, kk, vt = _diff_prep(proj, batch, seq, C_HALF ** -0.5 * math.log2(math.e), DIFF_BK)
            yc = _diff_attention(qt, kk, vt, p["c_lambda"][o], p["c_subln_gain"][o], batch, seq, layer,
                                 bk=DIFF_BK)
            m_c = _fourier_weights(p["d_w_fourier"][o], seq)
            u = _fourier_channel_mix(proj, m_c, 3 * c_w // LANES)
            yd = _seq_dft_real(u, batch, seq, D_GROUPS)
            mix = jnp.concatenate([yc, yd], axis=-1)
        kv = _norm_matmul(mem, p["mem_norm_gain"][layer], p["ca_w_kv"], layer, BF16)
        x = _mix_out_cross_attn(mix, x, p["w_out"], p["ca_w_q"], p["ca_w_o"], layer, g[1], g[2], g[3], kv,
                                batch, seq)
        x = _conv_ffn(x, seq, g[4], p["ffn_w_up"], p["ffn_conv_w"][layer], p["ffn_conv_b"][layer],
                      p["ffn_w_down"], layer, g[5])
    return x.reshape(batch, seq, d)


def kernel(x_prompt, x_sample, mem_prompt, mem_sample, norm_gains, w_in, w_out, b_w_spatial, b_b_spatial,
           b_ln_gain, c_lambda, c_subln_gain, d_w_fourier, mem_norm_gain, ca_w_q, ca_w_kv, ca_w_o, ffn_w_up,
           ffn_conv_w, ffn_conv_b, ffn_w_down):
    p = dict(norm_gains=norm_gains, w_in=w_in.astype(BF16), w_out=w_out.astype(BF16), b_w_spatial=b_w_spatial,
             b_b_spatial=b_b_spatial, b_ln_gain=b_ln_gain, c_lambda=c_lambda, c_subln_gain=c_subln_gain,
             d_w_fourier=d_w_fourier, mem_norm_gain=mem_norm_gain, ca_w_q=ca_w_q.astype(BF16),
             ca_w_kv=ca_w_kv.astype(BF16), ca_w_o=ca_w_o.astype(BF16), ffn_w_up=ffn_w_up.astype(BF16),
             ffn_conv_w=ffn_conv_w, ffn_conv_b=ffn_conv_b, ffn_w_down=ffn_w_down.astype(BF16))
    return _trunk(x_prompt, mem_prompt, p), _trunk(x_sample, mem_sample, p)
```

```python
import functools
import math

import jax
import jax.numpy as jnp
from jax import lax
from jax.experimental import pallas as pl
from jax.experimental.pallas import tpu as pltpu

F32 = jnp.float32
BF16 = jnp.bfloat16

EPS = 1e-6
ROPE_THETA = 500000.0
ROPE_FRACTION = 4
LANES = 128
HEAD_DIM = 128
A_HEADS = 8
DILATED_CFGS = ((128, 1), (512, 4), (2048, 16))
BAND_RADIUS = 64
B_GROUPS = 8
CHUNK = 128
C_HEADS = 12
C_HALF = 64
D_GROUPS = 4
CA_HEADS = 4
CONV_W = 3
NEG_BIG = -1e30
MIB = 1024 * 1024


def _params(semantics, vmem_mib):
    return pltpu.CompilerParams(dimension_semantics=semantics, vmem_limit_bytes=vmem_mib * MIB)


def _rms(x, g):
    return x * lax.rsqrt(jnp.mean(x * x, axis=-1, keepdims=True) + EPS) * g


def _dot_nt(a, b):
    return lax.dot_general(a, b, (((1,), (1,)), ((), ())), preferred_element_type=F32)


def _norm_mm_kernel(x_ref, g_ref, w_ref, o_ref, xn_ref):
    @pl.when(pl.program_id(1) == 0)
    def _():
        xn_ref[...] = _rms(x_ref[...], g_ref[...]).astype(BF16)

    o_ref[...] = jnp.dot(xn_ref[...], w_ref[...], preferred_element_type=F32).astype(o_ref.dtype)


def _norm_matmul(x, g, w, layer, out_dtype, tm=1024, tn=512):
    t, k = x.shape
    n = w.shape[2]
    tm = min(tm, t)
    return pl.pallas_call(
        _norm_mm_kernel,
        grid=(t // tm, n // tn),
        in_specs=[pl.BlockSpec((tm, k), lambda i, j: (i, 0)),
                  pl.BlockSpec((1, k), lambda i, j: (0, 0)),
                  pl.BlockSpec((None, k, tn), lambda i, j: (layer, 0, j))],
        out_specs=pl.BlockSpec((tm, tn), lambda i, j: (i, j)),
        out_shape=jax.ShapeDtypeStruct((t, n), out_dtype),
        scratch_shapes=[pltpu.VMEM((tm, k), BF16)],
        compiler_params=_params(("parallel", "arbitrary"), 40),
        name="norm_matmul",
    )(x, g.reshape(1, k), w)


def _mm_norm_res_kernel(a_ref, w_ref, g_ref, r_ref, o_ref):
    y = jnp.dot(a_ref[...], w_ref[...], preferred_element_type=F32)
    o_ref[...] = r_ref[...] + _rms(y, g_ref[...])


def _matmul_norm_res(a, w, layer, g, res, tm=512):
    t, k = a.shape
    n = w.shape[2]
    return pl.pallas_call(
        _mm_norm_res_kernel,
        grid=(t // tm,),
        in_specs=[pl.BlockSpec((tm, k), lambda i: (i, 0)),
                  pl.BlockSpec((None, k, n), lambda i: (layer, 0, 0)),
                  pl.BlockSpec((1, n), lambda i: (0, 0)),
                  pl.BlockSpec((tm, n), lambda i: (i, 0))],
        out_specs=pl.BlockSpec((tm, n), lambda i: (i, 0)),
        out_shape=jax.ShapeDtypeStruct((t, n), F32),
        compiler_params=_params(("parallel",), 48),
        name="matmul_norm_res",
    )(a, w, g.reshape(1, n), res)


def _rope_tables(seq, head_w):
    rd = head_w // ROPE_FRACTION
    half = rd // 2
    inv = ROPE_THETA ** (-jnp.arange(half, dtype=F32) / half)
    ang = jnp.arange(seq, dtype=F32)[:, None] * inv[None, :]
    cos = jnp.cos(ang)
    sin = jnp.sin(ang)
    rest = head_w - rd
    c_head = jnp.concatenate([cos, cos, jnp.ones((seq, rest), F32)], axis=-1)
    s_head = jnp.concatenate([-sin, sin, jnp.zeros((seq, rest), F32)], axis=-1)
    reps = LANES // head_w
    return jnp.tile(c_head, (1, reps)), jnp.tile(s_head, (1, reps))


def _rope_block(x, c, s, head_w):
    half = head_w // ROPE_FRACTION // 2
    first = (lax.broadcasted_iota(jnp.int32, x.shape, 1) & (head_w - 1)) < half
    partner = jnp.where(first, pltpu.roll(x, LANES - half, axis=1), pltpu.roll(x, half, axis=1))
    return x * c + partner * s


def _even_prep_kernel(x_ref, c_ref, s_ref, *refs, q_scale, n_blocks):
    out_refs, y_ref = refs[:-1], refs[-1]
    part = pl.program_id(1)
    tm = x_ref.shape[0]

    def rope(scale):
        c = c_ref[...]
        s = s_ref[...]
        for b in range(n_blocks):
            cols = slice(b * LANES, (b + 1) * LANES)
            y_ref[b] = _rope_block(x_ref[:, cols].astype(F32), c, s, HEAD_DIM) * scale

    @pl.when(part == 0)
    def _():
        rope(q_scale)

    @pl.when(part == 1)
    def _():
        rope(1.0)

    @pl.when(part == 2)
    def _():
        for b in range(n_blocks):
            y_ref[b] = x_ref[:, b * LANES:(b + 1) * LANES].astype(F32)

    for (_, dil), o_ref in zip(DILATED_CFGS, out_refs):
        for r in range(dil):
            for b in range(n_blocks):
                rows = y_ref[b, pl.ds(r, tm // dil, stride=dil), :]
                o_ref[0, r, :, b * LANES:(b + 1) * LANES] = rows.astype(BF16)


def _even_prep(proj, batch, seq, q_scale, tm=512):
    a_w = A_HEADS * HEAD_DIM
    cos_t, sin_t = _rope_tables(seq, HEAD_DIM)
    spb = seq // tm
    kern = functools.partial(_even_prep_kernel, q_scale=q_scale, n_blocks=a_w // LANES)
    return pl.pallas_call(
        kern,
        grid=(batch * spb, 3),
        in_specs=[pl.BlockSpec((tm, a_w), lambda i, p: (i, p)),
                  pl.BlockSpec((tm, LANES), lambda i, p: (i % spb, 0)),
                  pl.BlockSpec((tm, LANES), lambda i, p: (i % spb, 0))],
        out_specs=[pl.BlockSpec((1, dil, tm // dil, a_w), lambda i, p: (i // spb, 0, i % spb, p))
                   for _, dil in DILATED_CFGS],
        out_shape=[jax.ShapeDtypeStruct((batch, dil, seq // dil, 3 * a_w), BF16) for _, dil in DILATED_CFGS],
        scratch_shapes=[pltpu.VMEM((a_w // LANES, tm, LANES), F32)],
        compiler_params=_params(("parallel", "arbitrary"), 32),
        name="even_prep",
    )(proj, cos_t, sin_t)


def _band_attn_kernel(q_ref, kp_ref, kc_ref, kn_ref, vp_ref, vc_ref, vn_ref, o_ref, lse_ref, *, bq, sub_len):
    q0 = pl.program_id(1) * bq
    sq = 2 * BAND_RADIUS
    win = sq + 2 * BAND_RADIUS
    n_sub = bq // sq
    rel = (lax.broadcasted_iota(jnp.int32, (sq, win), 1) - BAND_RADIUS
           - lax.broadcasted_iota(jnp.int32, (sq, win), 0))
    band = (rel <= BAND_RADIUS) & (rel >= -BAND_RADIUS)
    biases = []
    for u in range(n_sub):
        kpos = q0 + u * sq - BAND_RADIUS + lax.broadcasted_iota(jnp.int32, (sq, win), 1)
        ok = band & (kpos >= 0) & (kpos < sub_len)
        biases.append(jnp.where(ok, 0.0, NEG_BIG))

    def window(p_ref, c_ref, n_ref, cols, u):
        parts = []
        if u == 0:
            parts.append(p_ref[0, :, cols])
        lo = max(u * sq - BAND_RADIUS, 0)
        hi = min(u * sq + sq + BAND_RADIUS, bq)
        parts.append(c_ref[0, lo:hi, cols])
        if u == n_sub - 1:
            parts.append(n_ref[0, :, cols])
        return jnp.concatenate(parts, axis=0)

    tiles = [(h, u) for h in range(A_HEADS) for u in range(n_sub)]

    def scores(h, u):
        cols = slice(h * HEAD_DIM, (h + 1) * HEAD_DIM)
        q = q_ref[0, u * sq:(u + 1) * sq, cols]
        return _dot_nt(q, window(kp_ref, kc_ref, kn_ref, cols, u)) + biases[u]

    s_next = scores(*tiles[0])
    for n, (h, u) in enumerate(tiles):
        cols = slice(h * HEAD_DIM, (h + 1) * HEAD_DIM)
        rows = slice(u * sq, (u + 1) * sq)
        s = s_next
        if n + 1 < len(tiles):
            s_next = scores(*tiles[n + 1])
        m = jnp.max(s, axis=-1, keepdims=True)
        p = jnp.exp(s - m)
        l = jnp.sum(p, axis=-1, keepdims=True)
        o = jnp.dot(p.astype(BF16), window(vp_ref, vc_ref, vn_ref, cols, u), preferred_element_type=F32)
        o_ref[0, rows, cols] = (o / l).astype(o_ref.dtype)
        lse_ref[0, rows, cols] = jnp.broadcast_to(m + jnp.log(l), (sq, HEAD_DIM))


def _band_attention(qkv, bq=256):
    n_seq, sub_len, _ = qkv.shape
    a_w = A_HEADS * HEAD_DIM
    hb = bq // BAND_RADIUS
    last = sub_len // BAND_RADIUS - 1
    kern = functools.partial(_band_attn_kernel, bq=bq, sub_len=sub_len)

    def halo_specs(col):
        return [pl.BlockSpec((1, BAND_RADIUS, a_w), lambda s, i: (s, jnp.maximum(i * hb - 1, 0), col)),
                pl.BlockSpec((1, bq, a_w), lambda s, i: (s, i, col)),
                pl.BlockSpec((1, BAND_RADIUS, a_w), lambda s, i: (s, jnp.minimum((i + 1) * hb, last), col))]

    out_spec = pl.BlockSpec((1, bq, a_w), lambda s, i: (s, i, 0))
    return pl.pallas_call(
        kern,
        grid=(n_seq, sub_len // bq),
        in_specs=[pl.BlockSpec((1, bq, a_w), lambda s, i: (s, i, 0))] + halo_specs(1) + halo_specs(2),
        out_specs=[out_spec, out_spec],
        out_shape=[jax.ShapeDtypeStruct((n_seq, sub_len, a_w), BF16),
                   jax.ShapeDtypeStruct((n_seq, sub_len, a_w), F32)],
        compiler_params=_params(("parallel", "parallel"), 40),
        name="band_attn",
    )(*([qkv] * 7))


def _combine_kernel(o1, l1, o4, l4, o16, l16, y_ref, so4, sl4, so16, sl16):
    tm = y_ref.shape[0]
    for dil, src, dst in ((4, o4, so4), (4, l4, sl4), (16, o16, so16), (16, l16, sl16)):
        for r in range(dil):
            for h in range(A_HEADS):
                dst[h, pl.ds(r, tm // dil, stride=dil), :] = (
                    src[0, r, :, h * HEAD_DIM:(h + 1) * HEAD_DIM].astype(F32))
    for h in range(A_HEADS):
        cols = slice(h * HEAD_DIM, (h + 1) * HEAD_DIM)
        a, b, c = l1[0, 0, :, cols], sl4[h], sl16[h]
        m = jnp.maximum(jnp.maximum(a, b), c)
        wa, wb, wc = jnp.exp(a - m), jnp.exp(b - m), jnp.exp(c - m)
        num = wa * o1[0, 0, :, cols].astype(F32) + wb * so4[h] + wc * so16[h]
        y_ref[:, cols] = (num / (wa + wb + wc)).astype(y_ref.dtype)


def _combine_configs(parts, batch, seq, tm=256):
    a_w = A_HEADS * HEAD_DIM
    spb = seq // tm
    flat, specs = [], []
    for (_, dil), pair in zip(DILATED_CFGS, parts):
        for arr in pair:
            flat.append(arr.reshape(batch, dil, seq // dil, a_w))
            specs.append(pl.BlockSpec((1, dil, tm // dil, a_w), lambda i: (i // spb, 0, i % spb, 0)))
    return pl.pallas_call(
        _combine_kernel,
        grid=(batch * spb,),
        in_specs=specs,
        out_specs=pl.BlockSpec((tm, a_w), lambda i: (i, 0)),
        out_shape=jax.ShapeDtypeStruct((batch * seq, a_w), BF16),
        scratch_shapes=[pltpu.VMEM((A_HEADS, tm, HEAD_DIM), F32)] * 4,
        compiler_params=_params(("parallel",), 40),
        name="dilated_combine",
    )(*flat)


def _gating_kernel(u_ref, v_ref, w_ref, b_ref, g_ref, o_ref, *, n_chunks):
    w = w_ref[0]
    bias = b_ref[0]
    gain = g_ref[0]
    for c in range(n_chunks):
        rows = pl.ds(c * CHUNK, CHUNK)
        v = jax.nn.gelu(v_ref[rows, :].astype(F32))
        mu = jnp.mean(v, axis=-1, keepdims=True)
        d = v - mu
        var = jnp.mean(d * d, axis=-1, keepdims=True)
        vn = d * lax.rsqrt(var + EPS) * gain
        sv = jnp.dot(w, vn.astype(BF16), preferred_element_type=F32) + bias
        o_ref[rows, :] = (jax.nn.gelu(u_ref[rows, :].astype(F32)) * sv).astype(o_ref.dtype)


def _spatial_gating(proj, w_s, b_s, g_ln, u_col, v_col, tm=2048):
    t = proj.shape[0]
    tm = min(tm, t)
    groups = w_s.shape[0]
    bias = jnp.broadcast_to(b_s[:, :, None], (groups, CHUNK, LANES))
    kern = functools.partial(_gating_kernel, n_chunks=tm // CHUNK)
    return pl.pallas_call(
        kern,
        grid=(t // tm, groups),
        in_specs=[pl.BlockSpec((tm, LANES), lambda i, g: (i, u_col + g)),
                  pl.BlockSpec((tm, LANES), lambda i, g: (i, v_col + g)),
                  pl.BlockSpec((1, CHUNK, CHUNK), lambda i, g: (g, 0, 0)),
                  pl.BlockSpec((1, CHUNK, LANES), lambda i, g: (g, 0, 0)),
                  pl.BlockSpec((1, 1, LANES), lambda i, g: (g, 0, 0))],
        out_specs=pl.BlockSpec((tm, LANES), lambda i, g: (i, g)),
        out_shape=jax.ShapeDtypeStruct((t, groups * LANES), BF16),
        compiler_params=_params(("parallel", "parallel"), 32),
        name="spatial_gating",
    )(proj, proj, w_s.astype(BF16), bias, g_ln.reshape(groups, 1, LANES))


V_ROWS = 2 * C_HALF + 16
DIFF_BK = 512


def _diff_prep_kernel(x_ref, c_ref, s_ref, qt_ref, k_ref, vt_ref, *, q_scale, n_blocks):
    part = pl.program_id(1)
    tm = x_ref.shape[0]

    @pl.when(part == 0)
    def _():
        c = c_ref[...]
        s = s_ref[...]
        for h in range(n_blocks):
            cols = slice(h * LANES, (h + 1) * LANES)
            qt_ref[0, h] = (_rope_block(x_ref[:, cols].astype(F32), c, s, C_HALF) * q_scale).T.astype(BF16)

    @pl.when(part == 1)
    def _():
        c = c_ref[...]
        s = s_ref[...]
        for h in range(n_blocks):
            cols = slice(h * LANES, (h + 1) * LANES)
            k_ref[:, cols] = _rope_block(x_ref[:, cols].astype(F32), c, s, C_HALF).astype(BF16)

    @pl.when(part == 2)
    def _():
        ones = jnp.ones((V_ROWS - LANES, tm), BF16)
        for h in range(n_blocks):
            vt_ref[0, h, 0, 0:LANES, :] = x_ref[:, h * LANES:(h + 1) * LANES].astype(F32).T.astype(BF16)
            vt_ref[0, h, 0, LANES:, :] = ones


def _diff_prep(proj, batch, seq, q_scale, bk, tm=512):
    c_w = C_HEADS * 2 * C_HALF
    cos_t, sin_t = _rope_tables(seq, C_HALF)
    spb = seq // tm
    per_chunk = bk // tm
    kern = functools.partial(_diff_prep_kernel, q_scale=q_scale, n_blocks=C_HEADS)
    return pl.pallas_call(
        kern,
        grid=(batch * spb, 3),
        in_specs=[pl.BlockSpec((tm, c_w), lambda i, p: (i, p)),
                  pl.BlockSpec((tm, LANES), lambda i, p: (i % spb, 0)),
                  pl.BlockSpec((tm, LANES), lambda i, p: (i % spb, 0))],
        out_specs=[pl.BlockSpec((1, C_HEADS, LANES, tm), lambda i, p: (i // spb, 0, 0, i % spb)),
                   pl.BlockSpec((tm, c_w), lambda i, p: (i, 0)),
                   pl.BlockSpec((1, C_HEADS, 1, V_ROWS, tm),
                                lambda i, p: (i // spb, 0, (i % spb) // per_chunk, 0, (i % spb) % per_chunk))],
        out_shape=[jax.ShapeDtypeStruct((batch, C_HEADS, LANES, seq), BF16),
                   jax.ShapeDtypeStruct((batch * seq, c_w), BF16),
                   jax.ShapeDtypeStruct((batch, C_HEADS, seq // bk, V_ROWS, bk), BF16)],
        compiler_params=_params(("parallel", "arbitrary"), 40),
        name="diff_prep",
    )(proj, cos_t, sin_t)


def _diff_attn_kernel(qt_ref, k_ref, vt_ref, lam_ref, g_ref, o_ref, s_ref, mc_ref, m_ref, acc_ref,
                      *, bq, bk, seq, lam_init, unroll):
    n = seq // bk
    sub = 8
    qt = qt_ref[0, 0]
    row = lax.broadcasted_iota(jnp.int32, qt.shape, 0)
    zero = jnp.zeros_like(qt)
    qs = (jnp.where(row < C_HALF, qt, zero), jnp.where(row < C_HALF, zero, qt))

    def scores(j, slot):
        kc = k_ref[0, pl.ds(pl.multiple_of(j * bk, bk), bk), :]
        for br in range(2):
            s = jnp.dot(kc, qs[br], preferred_element_type=F32)
            s_ref[slot, br] = s
            mc = jnp.max(s.reshape(bk // sub, sub, bq), axis=0)
            mc_ref[slot, br] = jnp.broadcast_to(jnp.max(mc, axis=0, keepdims=True), (sub, bq))

    def accumulate(j, slot):
        vt = vt_ref[0, 0, j]
        for br in range(2):
            m_old = m_ref[br]
            m_new = jnp.maximum(m_old, mc_ref[slot, br])
            alpha = jnp.exp2(m_old - m_new)
            m_ref[br] = m_new
            s3 = s_ref[slot, br].reshape(bk // sub, sub, bq)
            p = jnp.exp2(s3 - m_new[None]).reshape(bk, bq).astype(BF16)
            pv = jnp.dot(vt, p, preferred_element_type=F32)
            acc3 = acc_ref[br].reshape(V_ROWS // sub, sub, bq)
            acc_ref[br] = (alpha[None] * acc3).reshape(V_ROWS, bq) + pv

    m_ref[...] = jnp.full(m_ref.shape, NEG_BIG, F32)
    acc_ref[...] = jnp.zeros(acc_ref.shape, F32)
    scores(0, 0)

    def trip(j, last):
        for u in range(unroll):
            if not (last and u == unroll - 1):
                scores(j + u + 1, (u + 1) % 2)
            accumulate(j + u, u % 2)

    def body(i, carry):
        trip(unroll * i, False)
        return carry

    lax.fori_loop(0, n // unroll - 1, body, 0)
    trip(n - unroll, True)

    lam = lam_ref[...]
    lam_full = (jnp.exp(jnp.sum(lam[0:1] * lam[1:2], keepdims=True))
                - jnp.exp(jnp.sum(lam[2:3] * lam[3:4], keepdims=True)) + lam_init)

    def normalised(a):
        num = a[0:LANES].reshape(LANES // sub, sub, bq)
        return (num / a[LANES:LANES + sub][None]).reshape(LANES, bq)

    o = normalised(acc_ref[0]) - lam_full * normalised(acc_ref[1])
    y = o * lax.rsqrt(jnp.mean(o * o, axis=0, keepdims=True) + EPS) * g_ref[...] * (1.0 - lam_init)
    o_ref[0] = y.T.astype(o_ref.dtype)


def _diff_attention(qt, k, vt, lam, subln_g, batch, seq, layer_idx, bq=512, bk=512, unroll=8):
    c_w = C_HEADS * 2 * C_HALF
    lam_init = 0.8 - 0.6 * math.exp(-0.3 * layer_idx)
    n = seq // bk
    unroll = min(unroll, n)
    assert unroll % 2 == 0 and n % unroll == 0 and vt.shape[2] == n
    kern = functools.partial(_diff_attn_kernel, bq=bq, bk=bk, seq=seq, lam_init=lam_init, unroll=unroll)
    out = pl.pallas_call(
        kern,
        grid=(batch, C_HEADS, seq // bq),
        in_specs=[pl.BlockSpec((1, 1, LANES, bq), lambda b, h, i: (b, h, 0, i)),
                  pl.BlockSpec((1, seq, LANES), lambda b, h, i: (b, 0, h)),
                  pl.BlockSpec((1, 1, n, V_ROWS, bk), lambda b, h, i: (b, h, 0, 0, 0)),
                  pl.BlockSpec((4, C_HALF), lambda b, h, i: (0, 0)),
                  pl.BlockSpec((LANES, bq), lambda b, h, i: (0, 0))],
        out_specs=pl.BlockSpec((1, bq, LANES), lambda b, h, i: (b, i, h)),
        out_shape=jax.ShapeDtypeStruct((batch, seq, c_w), BF16),
        scratch_shapes=[pltpu.VMEM((2, 2, bk, bq), F32),
                        pltpu.VMEM((2, 2, 8, bq), F32),
                        pltpu.VMEM((2, 8, bq), F32),
                        pltpu.VMEM((2, V_ROWS, bq), F32)],
        compiler_params=_params(("parallel", "parallel", "arbitrary"), 48),
        name="diff_attn",
    )(qt, k.reshape(batch, seq, c_w), vt, lam, jnp.broadcast_to(subln_g[:, None], (LANES, bq)))
    return out.reshape(batch * seq, c_w)


def _fourier_wprep_kernel(c_ref, s_ref, w_ref, o_ref, *, scale):
    w = w_ref[0]
    mr = jnp.dot(c_ref[...], w, preferred_element_type=F32, precision=lax.Precision.HIGHEST)
    mi = jnp.dot(s_ref[...], w, preferred_element_type=F32, precision=lax.Precision.HIGHEST)
    o_ref[0, :, :LANES] = (mr * scale).astype(BF16)
    o_ref[0, :, LANES:] = (-mi * scale).astype(BF16)


def _fourier_weights(w_f, seq):
    groups, c, _ = w_f.shape
    idx = jnp.arange(c, dtype=jnp.int32)
    ang = (2.0 * math.pi / c) * ((idx[:, None] * idx[None, :]) % c).astype(F32)
    kern = functools.partial(_fourier_wprep_kernel, scale=1.0 / math.sqrt(seq * c))
    return pl.pallas_call(
        kern,
        grid=(groups,),
        in_specs=[pl.BlockSpec((c, c), lambda g: (0, 0)),
                  pl.BlockSpec((c, c), lambda g: (0, 0)),
                  pl.BlockSpec((1, c, c), lambda g: (g, 0, 0))],
        out_specs=pl.BlockSpec((1, c, 2 * c), lambda g: (g, 0, 0)),
        out_shape=jax.ShapeDtypeStruct((groups, c, 2 * c), BF16),
        compiler_params=_params(("parallel",), 32),
        name="fourier_wprep",
    )(jnp.cos(ang), jnp.sin(ang), w_f)


def _chan_mix_kernel(z_ref, m_ref, o_ref):
    o_ref[...] = jnp.dot(z_ref[...], m_ref[0], preferred_element_type=F32).astype(BF16)


def _fourier_channel_mix(proj, m_c, z_col, tm=2048):
    t = proj.shape[0]
    tm = min(tm, t)
    groups = m_c.shape[0]
    return pl.pallas_call(
        _chan_mix_kernel,
        grid=(t // tm, groups),
        in_specs=[pl.BlockSpec((tm, LANES), lambda i, g: (i, z_col + g)),
                  pl.BlockSpec((1, LANES, 2 * LANES), lambda i, g: (g, 0, 0))],
        out_specs=pl.BlockSpec((tm, 2 * LANES), lambda i, g: (i, g)),
        out_shape=jax.ShapeDtypeStruct((t, groups * 2 * LANES), BF16),
        compiler_params=_params(("parallel", "parallel"), 32),
        name="fourier_channel_mix",
    )(proj, m_c)


def _dft_stage1_kernel(u_ref, fr_ref, fi_ref, y_ref, *, pairs):
    u = u_ref[0]
    p = jnp.dot(fr_ref[...], u, preferred_element_type=F32)
    q = jnp.dot(fi_ref[...], u, preferred_element_type=F32)
    for j in range(pairs):
        re = slice(2 * j * LANES, (2 * j + 1) * LANES)
        im = slice((2 * j + 1) * LANES, (2 * j + 2) * LANES)
        y_ref[0, :, re] = (p[:, re] - q[:, im]).astype(BF16)
        y_ref[0, :, im] = (q[:, re] + p[:, im]).astype(BF16)


def _dft_stage2_kernel(y_ref, gc_ref, gs_ref, o_ref, *, groups):
    for t in range(y_ref.shape[1]):
        y = y_ref[0, t]
        a = jnp.dot(gc_ref[t], y, preferred_element_type=F32)
        b = jnp.dot(gs_ref[t], y, preferred_element_type=F32)
        for g in range(groups):
            re = slice(2 * g * LANES, (2 * g + 1) * LANES)
            im = slice((2 * g + 1) * LANES, (2 * g + 2) * LANES)
            out_cols = slice((t * groups + g) * LANES, (t * groups + g + 1) * LANES)
            o_ref[0, :, out_cols] = (a[:, re] + b[:, im]).astype(o_ref.dtype)


def _seq_dft_real(u, batch, seq, groups, tc=4096, kb=4):
    n1 = 128 if seq >= 16384 else 64
    n2 = seq // n1
    wc = groups * 2 * LANES
    i1 = jnp.arange(n1, dtype=jnp.int32)
    ang1 = (2.0 * math.pi / n1) * ((i1[:, None] * i1[None, :]) % n1).astype(F32)
    fr = jnp.cos(ang1).astype(BF16)
    fi = (-jnp.sin(ang1)).astype(BF16)
    cols = n2 * wc
    tc = min(tc, cols)
    y = pl.pallas_call(
        functools.partial(_dft_stage1_kernel, pairs=tc // (2 * LANES)),
        grid=(batch, cols // tc),
        in_specs=[pl.BlockSpec((1, n1, tc), lambda b, j: (b, 0, j)),
                  pl.BlockSpec((n1, n1), lambda b, j: (0, 0)),
                  pl.BlockSpec((n1, n1), lambda b, j: (0, 0))],
        out_specs=pl.BlockSpec((1, n1, tc), lambda b, j: (b, 0, j)),
        out_shape=jax.ShapeDtypeStruct((batch, n1, cols), BF16),
        compiler_params=_params(("parallel", "parallel"), 32),
        name="dft_stage1",
    )(u.reshape(batch, n1, cols), fr, fi)
    i2 = jnp.arange(n2, dtype=jnp.int32)
    tw = (i2[None, None, :] * (i1[:, None, None] + n1 * i2[None, :, None])) % seq
    ang2 = (2.0 * math.pi / seq) * tw.astype(F32)
    gc = jnp.cos(ang2).astype(BF16)
    gs = jnp.sin(ang2).astype(BF16)
    out = pl.pallas_call(
        functools.partial(_dft_stage2_kernel, groups=groups),
        grid=(batch, n1 // kb),
        in_specs=[pl.BlockSpec((1, kb, n2, wc), lambda b, k: (b, k, 0, 0)),
                  pl.BlockSpec((kb, n2, n2), lambda b, k: (k, 0, 0)),
                  pl.BlockSpec((kb, n2, n2), lambda b, k: (k, 0, 0))],
        out_specs=pl.BlockSpec((1, n2, kb * groups * LANES), lambda b, k: (b, 0, k)),
        out_shape=jax.ShapeDtypeStruct((batch, n2, n1 * groups * LANES), BF16),
        compiler_params=_params(("parallel", "parallel"), 32),
        name="dft_stage2",
    )(y.reshape(batch, n1, n2, wc), gc, gs)
    return out.reshape(batch * seq, groups * LANES)


def _mix_out_cross_attn_kernel(mix_ref, x_ref, wout_ref, g1_ref, g2_ref, wq_ref, kv_ref, wo_ref, g3_ref, o_ref,
                               *, heads, scale, halves):
    ca_w = heads * HEAD_DIM
    hr = mix_ref.shape[0] // halves
    rows = [slice(r * hr, (r + 1) * hr) for r in range(halves)]
    ys = [jnp.dot(mix_ref[r, :], wout_ref[...], preferred_element_type=F32) for r in rows]
    x1s = [x_ref[r, :] + _rms(y, g1_ref[...]) for r, y in zip(rows, ys)]
    qs = [jnp.dot(_rms(x1, g2_ref[...]).astype(BF16), wq_ref[...], preferred_element_type=F32).astype(BF16)
          for x1 in x1s]
    cas = []
    for q in qs:
        outs = []
        for h in range(heads):
            cols = slice(h * HEAD_DIM, (h + 1) * HEAD_DIM)
            s = _dot_nt(q[:, cols], kv_ref[0, :, cols]) * scale
            m = jnp.max(s, axis=-1, keepdims=True)
            p = jnp.exp(s - m)
            l = jnp.sum(p, axis=-1, keepdims=True)
            v = kv_ref[0, :, ca_w + h * HEAD_DIM:ca_w + (h + 1) * HEAD_DIM]
            outs.append((jnp.dot(p.astype(BF16), v, preferred_element_type=F32) / l).astype(BF16))
        cas.append(jnp.concatenate(outs, axis=1))
    zs = [jnp.dot(ca, wo_ref[...], preferred_element_type=F32) for ca in cas]
    for r, x1, z in zip(rows, x1s, zs):
        o_ref[r, :] = x1 + _rms(z, g3_ref[...])


def _mix_out_cross_attn(mix, x, w_out, w_q, w_o, layer, g1, g2, g3, kv, batch, seq, tm=512):
    t, d = x.shape
    ca_w = CA_HEADS * HEAD_DIM
    mem_len = kv.shape[0] // batch
    spb = seq // tm
    kern = functools.partial(_mix_out_cross_attn_kernel, heads=CA_HEADS, scale=HEAD_DIM ** -0.5, halves=2)
    once = pl.Buffered(1)
    vec = lambda: pl.BlockSpec((1, d), lambda i: (0, 0))
    return pl.pallas_call(
        kern,
        grid=(t // tm,),
        in_specs=[pl.BlockSpec((tm, d), lambda i: (i, 0)),
                  pl.BlockSpec((tm, d), lambda i: (i, 0)),
                  pl.BlockSpec((None, d, d), lambda i: (layer, 0, 0), pipeline_mode=once),
                  vec(), vec(),
                  pl.BlockSpec((None, d, ca_w), lambda i: (layer, 0, 0), pipeline_mode=once),
                  pl.BlockSpec((1, mem_len, 2 * ca_w), lambda i: (i // spb, 0, 0)),
                  pl.BlockSpec((None, ca_w, d), lambda i: (layer, 0, 0), pipeline_mode=once),
                  vec()],
        out_specs=pl.BlockSpec((tm, d), lambda i: (i, 0)),
        out_shape=jax.ShapeDtypeStruct((t, d), F32),
        compiler_params=_params(("parallel",), 56),
        name="mix_out_cross_attn",
    )(mix, x, w_out, g1.reshape(1, d), g2.reshape(1, d), w_q, kv.reshape(batch, mem_len, 2 * ca_w), w_o,
      g3.reshape(1, d))


HALO = 16


def _ffn_kernel(x_ref, xp_ref, xn_ref, g4_ref, wg_ref, wv_ref, cwg_ref, cwv_ref, cbg_ref, cbv_ref,
                wd_ref, g5_ref, o_ref, xe_ref, zg_ref, zv_ref, *, tm, tc, tiles_per_seq, nf):
    acc_ref = o_ref
    i = pl.program_id(0)
    f = pl.program_id(1)

    @pl.when(f == 0)
    def _():
        g4 = g4_ref[...]
        pos = i % tiles_per_seq
        prev = _rms(xp_ref[...], g4) * jnp.where(pos == 0, 0.0, 1.0)
        nxt = _rms(xn_ref[...], g4) * jnp.where(pos == tiles_per_seq - 1, 0.0, 1.0)
        xe_ref[0:HALO, :] = prev.astype(BF16)
        xe_ref[HALO:HALO + tm, :] = _rms(x_ref[...], g4).astype(BF16)
        xe_ref[HALO + tm:, :] = nxt.astype(BF16)
        acc_ref[...] = jnp.zeros(acc_ref.shape, F32)

    def conv(z_ref, c, cw, cb, lo, rows):
        lo = HALO + lo
        return (z_ref[c, lo - 1:lo - 1 + rows, :] * cw[0:1] + z_ref[c, lo:lo + rows, :] * cw[1:2]
                + z_ref[c, lo + 1:lo + 1 + rows, :] * cw[2:3] + cb)

    xe = xe_ref[...]
    n_sub = zg_ref.shape[0]
    for c in range(n_sub):
        cols = slice(c * tc, (c + 1) * tc)
        zg_ref[c] = jnp.dot(xe, wg_ref[:, cols], preferred_element_type=F32)
        zv_ref[c] = jnp.dot(xe, wv_ref[:, cols], preferred_element_type=F32)
    for c in range(n_sub):
        cols = slice(c * tc, (c + 1) * tc)
        pieces = 2 if c == n_sub - 1 else 1
        rows = tm // pieces
        for r in range(pieces):
            lo = r * rows
            h = (jax.nn.gelu(conv(zg_ref, c, cwg_ref[:, cols], cbg_ref[:, cols], lo, rows))
                 * conv(zv_ref, c, cwv_ref[:, cols], cbv_ref[:, cols], lo, rows))
            acc_ref[lo:lo + rows, :] += jnp.dot(h.astype(BF16), wd_ref[cols, :], preferred_element_type=F32)

    @pl.when(f == nf - 1)
    def _():
        o_ref[...] = x_ref[...] + _rms(acc_ref[...], g5_ref[...])


def _conv_ffn(x, seq, g4, w_up, conv_w, conv_b, w_down, layer, g5, tm=1024, tf=512, tc=256):
    t, d = x.shape
    d_ff = w_down.shape[1]
    nf = d_ff // tf
    hb = tm // HALO
    last_hb = t // HALO - 1
    kern = functools.partial(_ffn_kernel, tm=tm, tc=tc, tiles_per_seq=seq // tm, nf=nf)
    cb = conv_b.reshape(1, 2 * d_ff)
    return pl.pallas_call(
        kern,
        grid=(t // tm, nf),
        in_specs=[pl.BlockSpec((tm, d), lambda i, f: (i, 0), pipeline_mode=pl.Buffered(1)),
                  pl.BlockSpec((HALO, d), lambda i, f: (jnp.maximum(i * hb - 1, 0), 0)),
                  pl.BlockSpec((HALO, d), lambda i, f: (jnp.minimum((i + 1) * hb, last_hb), 0)),
                  pl.BlockSpec((1, d), lambda i, f: (0, 0)),
                  pl.BlockSpec((None, d, tf), lambda i, f: (layer, 0, f)),
                  pl.BlockSpec((None, d, tf), lambda i, f: (layer, 0, nf + f)),
                  pl.BlockSpec((CONV_W, tf), lambda i, f: (0, f)),
                  pl.BlockSpec((CONV_W, tf), lambda i, f: (0, nf + f)),
                  pl.BlockSpec((1, tf), lambda i, f: (0, f)),
                  pl.BlockSpec((1, tf), lambda i, f: (0, nf + f)),
                  pl.BlockSpec((None, tf, d), lambda i, f: (layer, f, 0)),
                  pl.BlockSpec((1, d), lambda i, f: (0, 0))],
        out_specs=pl.BlockSpec((tm, d), lambda i, f: (i, 0)),
        out_shape=jax.ShapeDtypeStruct((t, d), F32),
        scratch_shapes=[pltpu.VMEM((tm + 2 * HALO, d), BF16),
                        pltpu.VMEM((tf // tc, tm + 2 * HALO, tc), F32),
                        pltpu.VMEM((tf // tc, tm + 2 * HALO, tc), F32)],
        compiler_params=_params(("parallel", "arbitrary"), 58),
        name="conv_ffn",
    )(x, x, x, g4.reshape(1, d), w_up, w_up, conv_w, conv_w, cb, cb, w_down, g5.reshape(1, d))


def _trunk(x3, mem3, p):
    batch, seq, d = x3.shape
    x = x3.reshape(batch * seq, d)
    mem = mem3.reshape(-1, d)
    depth = p["w_in"].shape[0]
    a_w = A_HEADS * HEAD_DIM
    c_w = C_HEADS * 2 * C_HALF
    for layer in range(depth):
        g = p["norm_gains"][layer]
        proj = _norm_matmul(x, g[0], p["w_in"], layer, BF16)
        if layer % 2 == 0:
            e = layer // 2
            qkvs = _even_prep(proj, batch, seq, HEAD_DIM ** -0.5)
            parts = [_band_attention(qkv.reshape(batch * dil, seq // dil, 3 * a_w))
                     for (_, dil), qkv in zip(DILATED_CFGS, qkvs)]
            ya = _combine_configs(parts, batch, seq)
            yb = _spatial_gating(proj, p["b_w_spatial"][e], p["b_b_spatial"][e], p["b_ln_gain"][e],
                                 3 * A_HEADS, 3 * A_HEADS + B_GROUPS)
            mix = jnp.concatenate([ya, yb], axis=-1)
        else:
            o = layer // 2
            qt, kk, vt = _diff_prep(proj, batch, seq, C_HALF ** -0.5 * math.log2(math.e), DIFF_BK)
            yc = _diff_attention(qt, kk, vt, p["c_lambda"][o], p["c_subln_gain"][o], batch, seq, layer,
                                 bk=DIFF_BK)
            m_c = _fourier_weights(p["d_w_fourier"][o], seq)
            u = _fourier_channel_mix(proj, m_c, 3 * c_w // LANES)
            yd = _seq_dft_real(u, batch, seq, D_GROUPS)
            mix = jnp.concatenate([yc, yd], axis=-1)
        kv = _norm_matmul(mem, p["mem_norm_gain"][layer], p["ca_w_kv"], layer, BF16)
        x = _mix_out_cross_attn(mix, x, p["w_out"], p["ca_w_q"], p["ca_w_o"], layer, g[1], g[2], g[3], kv,
                                batch, seq)
        x = _conv_ffn(x, seq, g[4], p["ffn_w_up"], p["ffn_conv_w"][layer], p["ffn_conv_b"][layer],
                      p["ffn_w_down"], layer, g[5])
    return x.reshape(batch, seq, d)


def kernel(x_prompt, x_sample, mem_prompt, mem_sample, norm_gains, w_in, w_out, b_w_spatial, b_b_spatial,
           b_ln_gain, c_lambda, c_subln_gain, d_w_fourier, mem_norm_gain, ca_w_q, ca_w_kv, ca_w_o, ffn_w_up,
           ffn_conv_w, ffn_conv_b, ffn_w_down):
    p = dict(norm_gains=norm_gains, w_in=w_in.astype(BF16), w_out=w_out.astype(BF16), b_w_spatial=b_w_spatial,
             b_b_spatial=b_b_spatial, b_ln_gain=b_ln_gain, c_lambda=c_lambda, c_subln_gain=c_subln_gain,
             d_w_fourier=d_w_fourier, mem_norm_gain=mem_norm_gain, ca_w_q=ca_w_q.astype(BF16),
             ca_w_kv=ca_w_kv.astype(BF16), ca_w_o=ca_w_o.astype(BF16), ffn_w_up=ffn_w_up.astype(BF16),
             ffn_conv_w=ffn_conv_w, ffn_conv_b=ffn_conv_b, ffn_w_down=ffn_w_down.astype(BF16))
    return _trunk(x_prompt, mem_prompt, p), _trunk(x_sample, mem_sample, p)
```

```python
import functools
import math

import jax
import jax.numpy as jnp
from jax import lax
from jax.experimental import pallas as pl
from jax.experimental.pallas import tpu as pltpu

F32 = jnp.float32
BF16 = jnp.bfloat16

EPS = 1e-6
ROPE_THETA = 500000.0
ROPE_FRACTION = 4
LANES = 128
HEAD_DIM = 128
A_HEADS = 8
DILATED_CFGS = ((128, 1), (512, 4), (2048, 16))
BAND_RADIUS = 64
B_GROUPS = 8
CHUNK = 128
C_HEADS = 12
C_HALF = 64
D_GROUPS = 4
CA_HEADS = 4
CONV_W = 3
NEG_BIG = -1e30
MIB = 1024 * 1024


def _params(semantics, vmem_mib):
    return pltpu.CompilerParams(dimension_semantics=semantics, vmem_limit_bytes=vmem_mib * MIB)


def _rms(x, g):
    return x * lax.rsqrt(jnp.mean(x * x, axis=-1, keepdims=True) + EPS) * g


def _dot_nt(a, b):
    return lax.dot_general(a, b, (((1,), (1,)), ((), ())), preferred_element_type=F32)


def _norm_mm_kernel(x_ref, g_ref, w_ref, o_ref, xn_ref):
    @pl.when(pl.program_id(1) == 0)
    def _():
        xn_ref[...] = _rms(x_ref[...], g_ref[...]).astype(BF16)

    o_ref[...] = jnp.dot(xn_ref[...], w_ref[...], preferred_element_type=F32).astype(o_ref.dtype)


def _norm_matmul(x, g, w, layer, out_dtype, tm=1024, tn=1024):
    t, k = x.shape
    n = w.shape[2]
    tm = min(tm, t)
    return pl.pallas_call(
        _norm_mm_kernel,
        grid=(t // tm, n // tn),
        in_specs=[pl.BlockSpec((tm, k), lambda i, j: (i, 0)),
                  pl.BlockSpec((1, k), lambda i, j: (0, 0)),
                  pl.BlockSpec((None, k, tn), lambda i, j: (layer, 0, j))],
        out_specs=pl.BlockSpec((tm, tn), lambda i, j: (i, j)),
        out_shape=jax.ShapeDtypeStruct((t, n), out_dtype),
        scratch_shapes=[pltpu.VMEM((tm, k), BF16)],
        compiler_params=_params(("parallel", "arbitrary"), 40),
        name="norm_matmul",
    )(x, g.reshape(1, k), w)


def _mm_norm_res_kernel(a_ref, w_ref, g_ref, r_ref, o_ref):
    y = jnp.dot(a_ref[...], w_ref[...], preferred_element_type=F32)
    o_ref[...] = r_ref[...] + _rms(y, g_ref[...])


def _matmul_norm_res(a, w, layer, g, res, tm=512):
    t, k = a.shape
    n = w.shape[2]
    return pl.pallas_call(
        _mm_norm_res_kernel,
        grid=(t // tm,),
        in_specs=[pl.BlockSpec((tm, k), lambda i: (i, 0)),
                  pl.BlockSpec((None, k, n), lambda i: (layer, 0, 0)),
                  pl.BlockSpec((1, n), lambda i: (0, 0)),
                  pl.BlockSpec((tm, n), lambda i: (i, 0))],
        out_specs=pl.BlockSpec((tm, n), lambda i: (i, 0)),
        out_shape=jax.ShapeDtypeStruct((t, n), F32),
        compiler_params=_params(("parallel",), 48),
        name="matmul_norm_res",
    )(a, w, g.reshape(1, n), res)


def _rope_tables(seq, head_w):
    rd = head_w // ROPE_FRACTION
    half = rd // 2
    inv = ROPE_THETA ** (-jnp.arange(half, dtype=F32) / half)
    ang = jnp.arange(seq, dtype=F32)[:, None] * inv[None, :]
    cos = jnp.cos(ang)
    sin = jnp.sin(ang)
    rest = head_w - rd
    c_head = jnp.concatenate([cos, cos, jnp.ones((seq, rest), F32)], axis=-1)
    s_head = jnp.concatenate([-sin, sin, jnp.zeros((seq, rest), F32)], axis=-1)
    reps = LANES // head_w
    return jnp.tile(c_head, (1, reps)), jnp.tile(s_head, (1, reps))


def _rope_block(x, c, s, head_w):
    half = head_w // ROPE_FRACTION // 2
    first = (lax.broadcasted_iota(jnp.int32, x.shape, 1) & (head_w - 1)) < half
    partner = jnp.where(first, pltpu.roll(x, LANES - half, axis=1), pltpu.roll(x, half, axis=1))
    return x * c + partner * s


def _even_prep_kernel(x_ref, c_ref, s_ref, *refs, q_scale, n_blocks):
    out_refs, y_ref = refs[:-1], refs[-1]
    part = pl.program_id(1)
    tm = x_ref.shape[0]

    def rope(scale):
        c = c_ref[...]
        s = s_ref[...]
        for b in range(n_blocks):
            cols = slice(b * LANES, (b + 1) * LANES)
            y_ref[b] = _rope_block(x_ref[:, cols].astype(F32), c, s, HEAD_DIM) * scale

    @pl.when(part == 0)
    def _():
        rope(q_scale)

    @pl.when(part == 1)
    def _():
        rope(1.0)

    @pl.when(part == 2)
    def _():
        for b in range(n_blocks):
            y_ref[b] = x_ref[:, b * LANES:(b + 1) * LANES].astype(F32)

    for (_, dil), o_ref in zip(DILATED_CFGS, out_refs):
        for r in range(dil):
            for b in range(n_blocks):
                rows = y_ref[b, pl.ds(r, tm // dil, stride=dil), :]
                o_ref[0, r, :, b * LANES:(b + 1) * LANES] = rows.astype(BF16)


def _even_prep(proj, tables, batch, seq, q_scale, tm=512):
    a_w = A_HEADS * HEAD_DIM
    cos_t, sin_t = tables
    spb = seq // tm
    kern = functools.partial(_even_prep_kernel, q_scale=q_scale, n_blocks=a_w // LANES)
    return pl.pallas_call(
        kern,
        grid=(batch * spb, 3),
        in_specs=[pl.BlockSpec((tm, a_w), lambda i, p: (i, p)),
                  pl.BlockSpec((tm, LANES), lambda i, p: (i % spb, 0)),
                  pl.BlockSpec((tm, LANES), lambda i, p: (i % spb, 0))],
        out_specs=[pl.BlockSpec((1, dil, tm // dil, a_w), lambda i, p: (i // spb, 0, i % spb, p))
                   for _, dil in DILATED_CFGS],
        out_shape=[jax.ShapeDtypeStruct((batch, dil, seq // dil, 3 * a_w), BF16) for _, dil in DILATED_CFGS],
        scratch_shapes=[pltpu.VMEM((a_w // LANES, tm, LANES), F32)],
        compiler_params=_params(("parallel", "arbitrary"), 32),
        name="even_prep",
    )(proj, cos_t, sin_t)


def _band_attn_kernel(q_ref, kp_ref, kc_ref, kn_ref, vp_ref, vc_ref, vn_ref, o_ref, lse_ref, *, bq, sub_len):
    q0 = pl.program_id(1) * bq
    sq = 2 * BAND_RADIUS
    win = sq + 2 * BAND_RADIUS
    n_sub = bq // sq
    rel = (lax.broadcasted_iota(jnp.int32, (sq, win), 1) - BAND_RADIUS
           - lax.broadcasted_iota(jnp.int32, (sq, win), 0))
    band = (rel <= BAND_RADIUS) & (rel >= -BAND_RADIUS)
    biases = []
    for u in range(n_sub):
        kpos = q0 + u * sq - BAND_RADIUS + lax.broadcasted_iota(jnp.int32, (sq, win), 1)
        ok = band & (kpos >= 0) & (kpos < sub_len)
        biases.append(jnp.where(ok, 0.0, NEG_BIG))

    def window(p_ref, c_ref, n_ref, cols, u):
        parts = []
        if u == 0:
            parts.append(p_ref[0, :, cols])
        lo = max(u * sq - BAND_RADIUS, 0)
        hi = min(u * sq + sq + BAND_RADIUS, bq)
        parts.append(c_ref[0, lo:hi, cols])
        if u == n_sub - 1:
            parts.append(n_ref[0, :, cols])
        return jnp.concatenate(parts, axis=0)

    tiles = [(h, u) for h in range(A_HEADS) for u in range(n_sub)]

    def scores(h, u):
        cols = slice(h * HEAD_DIM, (h + 1) * HEAD_DIM)
        q = q_ref[0, u * sq:(u + 1) * sq, cols]
        return _dot_nt(q, window(kp_ref, kc_ref, kn_ref, cols, u)) + biases[u]

    s_next = scores(*tiles[0])
    for n, (h, u) in enumerate(tiles):
        cols = slice(h * HEAD_DIM, (h + 1) * HEAD_DIM)
        rows = slice(u * sq, (u + 1) * sq)
        s = s_next
        if n + 1 < len(tiles):
            s_next = scores(*tiles[n + 1])
        m = jnp.max(s, axis=-1, keepdims=True)
        p = jnp.exp(s - m)
        l = jnp.sum(p, axis=-1, keepdims=True)
        o = jnp.dot(p.astype(BF16), window(vp_ref, vc_ref, vn_ref, cols, u), preferred_element_type=F32)
        o_ref[0, rows, cols] = (o / l).astype(o_ref.dtype)
        lse_ref[0, rows, cols] = jnp.broadcast_to(m + jnp.log(l), (sq, HEAD_DIM))


def _band_attention(qkv, bq=256):
    n_seq, sub_len, _ = qkv.shape
    a_w = A_HEADS * HEAD_DIM
    hb = bq // BAND_RADIUS
    last = sub_len // BAND_RADIUS - 1
    kern = functools.partial(_band_attn_kernel, bq=bq, sub_len=sub_len)

    def halo_specs(col):
        return [pl.BlockSpec((1, BAND_RADIUS, a_w), lambda s, i: (s, jnp.maximum(i * hb - 1, 0), col)),
                pl.BlockSpec((1, bq, a_w), lambda s, i: (s, i, col)),
                pl.BlockSpec((1, BAND_RADIUS, a_w), lambda s, i: (s, jnp.minimum((i + 1) * hb, last), col))]

    out_spec = pl.BlockSpec((1, bq, a_w), lambda s, i: (s, i, 0))
    return pl.pallas_call(
        kern,
        grid=(n_seq, sub_len // bq),
        in_specs=[pl.BlockSpec((1, bq, a_w), lambda s, i: (s, i, 0))] + halo_specs(1) + halo_specs(2),
        out_specs=[out_spec, out_spec],
        out_shape=[jax.ShapeDtypeStruct((n_seq, sub_len, a_w), BF16),
                   jax.ShapeDtypeStruct((n_seq, sub_len, a_w), F32)],
        compiler_params=_params(("parallel", "parallel"), 40),
        name="band_attn",
    )(*([qkv] * 7))


def _combine_kernel(o1, l1, o4, l4, o16, l16, y_ref, so4, sl4, so16, sl16):
    tm = y_ref.shape[0]
    for dil, src, dst in ((4, o4, so4), (4, l4, sl4), (16, o16, so16), (16, l16, sl16)):
        for r in range(dil):
            for h in range(A_HEADS):
                dst[h, pl.ds(r, tm // dil, stride=dil), :] = (
                    src[0, r, :, h * HEAD_DIM:(h + 1) * HEAD_DIM].astype(F32))
    for h in range(A_HEADS):
        cols = slice(h * HEAD_DIM, (h + 1) * HEAD_DIM)
        a, b, c = l1[0, 0, :, cols], sl4[h], sl16[h]
        m = jnp.maximum(jnp.maximum(a, b), c)
        wa, wb, wc = jnp.exp(a - m), jnp.exp(b - m), jnp.exp(c - m)
        num = wa * o1[0, 0, :, cols].astype(F32) + wb * so4[h] + wc * so16[h]
        y_ref[:, cols] = (num / (wa + wb + wc)).astype(y_ref.dtype)


def _combine_configs(parts, batch, seq, tm=512):
    a_w = A_HEADS * HEAD_DIM
    spb = seq // tm
    flat, specs = [], []
    for (_, dil), pair in zip(DILATED_CFGS, parts):
        for arr in pair:
            flat.append(arr.reshape(batch, dil, seq // dil, a_w))
            specs.append(pl.BlockSpec((1, dil, tm // dil, a_w), lambda i: (i // spb, 0, i % spb, 0)))
    return pl.pallas_call(
        _combine_kernel,
        grid=(batch * spb,),
        in_specs=specs,
        out_specs=pl.BlockSpec((tm, a_w), lambda i: (i, 0)),
        out_shape=jax.ShapeDtypeStruct((batch * seq, a_w), BF16),
        scratch_shapes=[pltpu.VMEM((A_HEADS, tm, HEAD_DIM), F32)] * 4,
        compiler_params=_params(("parallel",), 40),
        name="dilated_combine",
    )(*flat)


def _gating_kernel(u_ref, v_ref, w_ref, b_ref, g_ref, o_ref, *, n_chunks):
    w = w_ref[0]
    bias = b_ref[0]
    gain = g_ref[0]
    for c in range(n_chunks):
        rows = pl.ds(c * CHUNK, CHUNK)
        v = jax.nn.gelu(v_ref[rows, :].astype(F32))
        mu = jnp.mean(v, axis=-1, keepdims=True)
        d = v - mu
        var = jnp.mean(d * d, axis=-1, keepdims=True)
        vn = d * lax.rsqrt(var + EPS) * gain
        sv = jnp.dot(w, vn.astype(BF16), preferred_element_type=F32) + bias
        o_ref[rows, :] = (jax.nn.gelu(u_ref[rows, :].astype(F32)) * sv).astype(o_ref.dtype)


def _spatial_gating(proj, w_s, b_s, g_ln, u_col, v_col, tm=2048):
    t = proj.shape[0]
    tm = min(tm, t)
    groups = w_s.shape[0]
    bias = jnp.broadcast_to(b_s[:, :, None], (groups, CHUNK, LANES))
    kern = functools.partial(_gating_kernel, n_chunks=tm // CHUNK)
    return pl.pallas_call(
        kern,
        grid=(t // tm, groups),
        in_specs=[pl.BlockSpec((tm, LANES), lambda i, g: (i, u_col + g)),
                  pl.BlockSpec((tm, LANES), lambda i, g: (i, v_col + g)),
                  pl.BlockSpec((1, CHUNK, CHUNK), lambda i, g: (g, 0, 0)),
                  pl.BlockSpec((1, CHUNK, LANES), lambda i, g: (g, 0, 0)),
                  pl.BlockSpec((1, 1, LANES), lambda i, g: (g, 0, 0))],
        out_specs=pl.BlockSpec((tm, LANES), lambda i, g: (i, g)),
        out_shape=jax.ShapeDtypeStruct((t, groups * LANES), BF16),
        compiler_params=_params(("parallel", "parallel"), 32),
        name="spatial_gating",
    )(proj, proj, w_s.astype(BF16), bias, g_ln.reshape(groups, 1, LANES))


V_ROWS = 2 * C_HALF + 16
DIFF_BK = 512


def _diff_prep_kernel(x_ref, c_ref, s_ref, qt_ref, k_ref, vt_ref, *, q_scale, n_blocks):
    part = pl.program_id(1)
    tm = x_ref.shape[0]

    @pl.when(part == 0)
    def _():
        c = c_ref[...]
        s = s_ref[...]
        for h in range(n_blocks):
            cols = slice(h * LANES, (h + 1) * LANES)
            qt_ref[0, h] = (_rope_block(x_ref[:, cols].astype(F32), c, s, C_HALF) * q_scale).T.astype(BF16)

    @pl.when(part == 1)
    def _():
        c = c_ref[...]
        s = s_ref[...]
        for h in range(n_blocks):
            cols = slice(h * LANES, (h + 1) * LANES)
            k_ref[:, cols] = _rope_block(x_ref[:, cols].astype(F32), c, s, C_HALF).astype(BF16)

    @pl.when(part == 2)
    def _():
        ones = jnp.ones((V_ROWS - LANES, tm), BF16)
        for h in range(n_blocks):
            vt_ref[0, h, 0, 0:LANES, :] = x_ref[:, h * LANES:(h + 1) * LANES].astype(F32).T.astype(BF16)
            vt_ref[0, h, 0, LANES:, :] = ones


def _diff_prep(proj, tables, batch, seq, q_scale, bk, tm=512):
    c_w = C_HEADS * 2 * C_HALF
    cos_t, sin_t = tables
    spb = seq // tm
    per_chunk = bk // tm
    kern = functools.partial(_diff_prep_kernel, q_scale=q_scale, n_blocks=C_HEADS)
    return pl.pallas_call(
        kern,
        grid=(batch * spb, 3),
        in_specs=[pl.BlockSpec((tm, c_w), lambda i, p: (i, p)),
                  pl.BlockSpec((tm, LANES), lambda i, p: (i % spb, 0)),
                  pl.BlockSpec((tm, LANES), lambda i, p: (i % spb, 0))],
        out_specs=[pl.BlockSpec((1, C_HEADS, LANES, tm), lambda i, p: (i // spb, 0, 0, i % spb)),
                   pl.BlockSpec((tm, c_w), lambda i, p: (i, 0)),
                   pl.BlockSpec((1, C_HEADS, 1, V_ROWS, tm),
                                lambda i, p: (i // spb, 0, (i % spb) // per_chunk, 0, (i % spb) % per_chunk))],
        out_shape=[jax.ShapeDtypeStruct((batch, C_HEADS, LANES, seq), BF16),
                   jax.ShapeDtypeStruct((batch * seq, c_w), BF16),
                   jax.ShapeDtypeStruct((batch, C_HEADS, seq // bk, V_ROWS, bk), BF16)],
        compiler_params=_params(("parallel", "arbitrary"), 40),
        name="diff_prep",
    )(proj, cos_t, sin_t)


def _diff_attn_kernel(qt_ref, k_ref, vt_ref, lam_ref, g_ref, o_ref, s_ref, mc_ref, m_ref, acc_ref,
                      *, bq, bk, seq, lam_init, unroll):
    n = seq // bk
    sub = 8
    qt = qt_ref[0, 0]
    row = lax.broadcasted_iota(jnp.int32, qt.shape, 0)
    zero = jnp.zeros_like(qt)
    qs = (jnp.where(row < C_HALF, qt, zero), jnp.where(row < C_HALF, zero, qt))

    def scores(j, slot):
        kc = k_ref[0, pl.ds(pl.multiple_of(j * bk, bk), bk), :]
        for br in range(2):
            s = jnp.dot(kc, qs[br], preferred_element_type=F32)
            s_ref[slot, br] = s
            mc = jnp.max(s.reshape(bk // sub, sub, bq), axis=0)
            mc_ref[slot, br] = jnp.broadcast_to(jnp.max(mc, axis=0, keepdims=True), (sub, bq))

    def accumulate(j, slot):
        vt = vt_ref[0, 0, j]
        for br in range(2):
            m_old = m_ref[br]
            m_new = jnp.maximum(m_old, mc_ref[slot, br])
            alpha = jnp.exp2(m_old - m_new)
            m_ref[br] = m_new
            s3 = s_ref[slot, br].reshape(bk // sub, sub, bq)
            p = jnp.exp2(s3 - m_new[None]).reshape(bk, bq).astype(BF16)
            pv = jnp.dot(vt, p, preferred_element_type=F32)
            acc3 = acc_ref[br].reshape(V_ROWS // sub, sub, bq)
            acc_ref[br] = (alpha[None] * acc3).reshape(V_ROWS, bq) + pv

    m_ref[...] = jnp.full(m_ref.shape, NEG_BIG, F32)
    acc_ref[...] = jnp.zeros(acc_ref.shape, F32)
    scores(0, 0)

    def trip(j, last):
        for u in range(unroll):
            if not (last and u == unroll - 1):
                scores(j + u + 1, (u + 1) % 2)
            accumulate(j + u, u % 2)

    def body(i, carry):
        trip(unroll * i, False)
        return carry

    lax.fori_loop(0, n // unroll - 1, body, 0)
    trip(n - unroll, True)

    lam = lam_ref[...]
    lam_full = (jnp.exp(jnp.sum(lam[0:1] * lam[1:2], keepdims=True))
                - jnp.exp(jnp.sum(lam[2:3] * lam[3:4], keepdims=True)) + lam_init)

    def normalised(a):
        num = a[0:LANES].reshape(LANES // sub, sub, bq)
        return (num / a[LANES:LANES + sub][None]).reshape(LANES, bq)

    o = normalised(acc_ref[0]) - lam_full * normalised(acc_ref[1])
    y = o * lax.rsqrt(jnp.mean(o * o, axis=0, keepdims=True) + EPS) * g_ref[...] * (1.0 - lam_init)
    o_ref[0] = y.T.astype(o_ref.dtype)


def _diff_attention(qt, k, vt, lam, subln_g, batch, seq, layer_idx, bq=512, bk=512, unroll=8):
    c_w = C_HEADS * 2 * C_HALF
    lam_init = 0.8 - 0.6 * math.exp(-0.3 * layer_idx)
    n = seq // bk
    unroll = min(unroll, n)
    assert unroll % 2 == 0 and n % unroll == 0 and vt.shape[2] == n
    kern = functools.partial(_diff_attn_kernel, bq=bq, bk=bk, seq=seq, lam_init=lam_init, unroll=unroll)
    out = pl.pallas_call(
        kern,
        grid=(batch, C_HEADS, seq // bq),
        in_specs=[pl.BlockSpec((1, 1, LANES, bq), lambda b, h, i: (b, h, 0, i)),
                  pl.BlockSpec((1, seq, LANES), lambda b, h, i: (b, 0, h)),
                  pl.BlockSpec((1, 1, n, V_ROWS, bk), lambda b, h, i: (b, h, 0, 0, 0)),
                  pl.BlockSpec((4, C_HALF), lambda b, h, i: (0, 0)),
                  pl.BlockSpec((LANES, bq), lambda b, h, i: (0, 0))],
        out_specs=pl.BlockSpec((1, bq, LANES), lambda b, h, i: (b, i, h)),
        out_shape=jax.ShapeDtypeStruct((batch, seq, c_w), BF16),
        scratch_shapes=[pltpu.VMEM((2, 2, bk, bq), F32),
                        pltpu.VMEM((2, 2, 8, bq), F32),
                        pltpu.VMEM((2, 8, bq), F32),
                        pltpu.VMEM((2, V_ROWS, bq), F32)],
        compiler_params=_params(("parallel", "parallel", "arbitrary"), 48),
        name="diff_attn",
    )(qt, k.reshape(batch, seq, c_w), vt, lam, jnp.broadcast_to(subln_g[:, None], (LANES, bq)))
    return out.reshape(batch * seq, c_w)


def _fourier_wprep_kernel(c_ref, s_ref, w_ref, o_ref, *, scale):
    w = w_ref[0]
    mr = jnp.dot(c_ref[...], w, preferred_element_type=F32, precision=lax.Precision.HIGHEST)
    mi = jnp.dot(s_ref[...], w, preferred_element_type=F32, precision=lax.Precision.HIGHEST)
    o_ref[0, :, :LANES] = (mr * scale).astype(BF16)
    o_ref[0, :, LANES:] = (-mi * scale).astype(BF16)


def _fourier_weights(w_f, seq):
    groups, c, _ = w_f.shape
    idx = jnp.arange(c, dtype=jnp.int32)
    ang = (2.0 * math.pi / c) * ((idx[:, None] * idx[None, :]) % c).astype(F32)
    kern = functools.partial(_fourier_wprep_kernel, scale=1.0 / math.sqrt(seq * c))
    return pl.pallas_call(
        kern,
        grid=(groups,),
        in_specs=[pl.BlockSpec((c, c), lambda g: (0, 0)),
                  pl.BlockSpec((c, c), lambda g: (0, 0)),
                  pl.BlockSpec((1, c, c), lambda g: (g, 0, 0))],
        out_specs=pl.BlockSpec((1, c, 2 * c), lambda g: (g, 0, 0)),
        out_shape=jax.ShapeDtypeStruct((groups, c, 2 * c), BF16),
        compiler_params=_params(("parallel",), 32),
        name="fourier_wprep",
    )(jnp.cos(ang), jnp.sin(ang), w_f)


def _chan_mix_kernel(z_ref, m_ref, o_ref):
    o_ref[...] = jnp.dot(z_ref[...], m_ref[0], preferred_element_type=F32).astype(BF16)


def _fourier_channel_mix(proj, m_c, z_col, tm=2048):
    t = proj.shape[0]
    tm = min(tm, t)
    groups = m_c.shape[0]
    return pl.pallas_call(
        _chan_mix_kernel,
        grid=(t // tm, groups),
        in_specs=[pl.BlockSpec((tm, LANES), lambda i, g: (i, z_col + g)),
                  pl.BlockSpec((1, LANES, 2 * LANES), lambda i, g: (g, 0, 0))],
        out_specs=pl.BlockSpec((tm, 2 * LANES), lambda i, g: (i, g)),
        out_shape=jax.ShapeDtypeStruct((t, groups * 2 * LANES), BF16),
        compiler_params=_params(("parallel", "parallel"), 32),
        name="fourier_channel_mix",
    )(proj, m_c)


def _dft_stage1_kernel(u_ref, fr_ref, fi_ref, y_ref, *, pairs):
    u = u_ref[0]
    p = jnp.dot(fr_ref[...], u, preferred_element_type=F32)
    q = jnp.dot(fi_ref[...], u, preferred_element_type=F32)
    for j in range(pairs):
        re = slice(2 * j * LANES, (2 * j + 1) * LANES)
        im = slice((2 * j + 1) * LANES, (2 * j + 2) * LANES)
        y_ref[0, :, re] = (p[:, re] - q[:, im]).astype(BF16)
        y_ref[0, :, im] = (q[:, re] + p[:, im]).astype(BF16)


def _dft_stage2_kernel(y_ref, gc_ref, gs_ref, o_ref, *, groups):
    for t in range(y_ref.shape[1]):
        y = y_ref[0, t]
        a = jnp.dot(gc_ref[t], y, preferred_element_type=F32)
        b = jnp.dot(gs_ref[t], y, preferred_element_type=F32)
        for g in range(groups):
            re = slice(2 * g * LANES, (2 * g + 1) * LANES)
            im = slice((2 * g + 1) * LANES, (2 * g + 2) * LANES)
            out_cols = slice((t * groups + g) * LANES, (t * groups + g + 1) * LANES)
            o_ref[0, :, out_cols] = (a[:, re] + b[:, im]).astype(o_ref.dtype)


def _seq_dft_real(u, batch, seq, groups, tc=4096, kb=4):
    n1 = 128 if seq >= 16384 else 64
    n2 = seq // n1
    wc = groups * 2 * LANES
    i1 = jnp.arange(n1, dtype=jnp.int32)
    ang1 = (2.0 * math.pi / n1) * ((i1[:, None] * i1[None, :]) % n1).astype(F32)
    fr = jnp.cos(ang1).astype(BF16)
    fi = (-jnp.sin(ang1)).astype(BF16)
    cols = n2 * wc
    tc = min(tc, cols)
    y = pl.pallas_call(
        functools.partial(_dft_stage1_kernel, pairs=tc // (2 * LANES)),
        grid=(batch, cols // tc),
        in_specs=[pl.BlockSpec((1, n1, tc), lambda b, j: (b, 0, j)),
                  pl.BlockSpec((n1, n1), lambda b, j: (0, 0)),
                  pl.BlockSpec((n1, n1), lambda b, j: (0, 0))],
        out_specs=pl.BlockSpec((1, n1, tc), lambda b, j: (b, 0, j)),
        out_shape=jax.ShapeDtypeStruct((batch, n1, cols), BF16),
        compiler_params=_params(("parallel", "parallel"), 32),
        name="dft_stage1",
    )(u.reshape(batch, n1, cols), fr, fi)
    i2 = jnp.arange(n2, dtype=jnp.int32)
    tw = (i2[None, None, :] * (i1[:, None, None] + n1 * i2[None, :, None])) % seq
    ang2 = (2.0 * math.pi / seq) * tw.astype(F32)
    gc = jnp.cos(ang2).astype(BF16)
    gs = jnp.sin(ang2).astype(BF16)
    out = pl.pallas_call(
        functools.partial(_dft_stage2_kernel, groups=groups),
        grid=(batch, n1 // kb),
        in_specs=[pl.BlockSpec((1, kb, n2, wc), lambda b, k: (b, k, 0, 0)),
                  pl.BlockSpec((kb, n2, n2), lambda b, k: (k, 0, 0)),
                  pl.BlockSpec((kb, n2, n2), lambda b, k: (k, 0, 0))],
        out_specs=pl.BlockSpec((1, n2, kb * groups * LANES), lambda b, k: (b, 0, k)),
        out_shape=jax.ShapeDtypeStruct((batch, n2, n1 * groups * LANES), BF16),
        compiler_params=_params(("parallel", "parallel"), 32),
        name="dft_stage2",
    )(y.reshape(batch, n1, n2, wc), gc, gs)
    return out.reshape(batch * seq, groups * LANES)


def _mix_out_cross_attn_kernel(mix_ref, x_ref, wout_ref, g1_ref, g2_ref, wq_ref, kv_ref, wo_ref, g3_ref, o_ref,
                               *, heads, scale, halves):
    ca_w = heads * HEAD_DIM
    hr = mix_ref.shape[0] // halves
    rows = [slice(r * hr, (r + 1) * hr) for r in range(halves)]
    ys = [jnp.dot(mix_ref[r, :], wout_ref[...], preferred_element_type=F32) for r in rows]
    x1s = [x_ref[r, :] + _rms(y, g1_ref[...]) for r, y in zip(rows, ys)]
    qs = [jnp.dot(_rms(x1, g2_ref[...]).astype(BF16), wq_ref[...], preferred_element_type=F32).astype(BF16)
          for x1 in x1s]
    cas = []
    for q in qs:
        outs = []
        for h in range(heads):
            cols = slice(h * HEAD_DIM, (h + 1) * HEAD_DIM)
            s = _dot_nt(q[:, cols], kv_ref[0, :, cols]) * scale
            m = jnp.max(s, axis=-1, keepdims=True)
            p = jnp.exp(s - m)
            l = jnp.sum(p, axis=-1, keepdims=True)
            v = kv_ref[0, :, ca_w + h * HEAD_DIM:ca_w + (h + 1) * HEAD_DIM]
            outs.append((jnp.dot(p.astype(BF16), v, preferred_element_type=F32) / l).astype(BF16))
        cas.append(jnp.concatenate(outs, axis=1))
    zs = [jnp.dot(ca, wo_ref[...], preferred_element_type=F32) for ca in cas]
    for r, x1, z in zip(rows, x1s, zs):
        o_ref[r, :] = x1 + _rms(z, g3_ref[...])


def _mix_out_cross_attn(mix, x, w_out, w_q, w_o, layer, g1, g2, g3, kv, batch, seq, tm=512):
    t, d = x.shape
    ca_w = CA_HEADS * HEAD_DIM
    mem_len = kv.shape[0] // batch
    spb = seq // tm
    kern = functools.partial(_mix_out_cross_attn_kernel, heads=CA_HEADS, scale=HEAD_DIM ** -0.5, halves=2)
    once = pl.Buffered(1)
    vec = lambda: pl.BlockSpec((1, d), lambda i: (0, 0))
    return pl.pallas_call(
        kern,
        grid=(t // tm,),
        in_specs=[pl.BlockSpec((tm, d), lambda i: (i, 0)),
                  pl.BlockSpec((tm, d), lambda i: (i, 0)),
                  pl.BlockSpec((None, d, d), lambda i: (layer, 0, 0), pipeline_mode=once),
                  vec(), vec(),
                  pl.BlockSpec((None, d, ca_w), lambda i: (layer, 0, 0), pipeline_mode=once),
                  pl.BlockSpec((1, mem_len, 2 * ca_w), lambda i: (i // spb, 0, 0)),
                  pl.BlockSpec((None, ca_w, d), lambda i: (layer, 0, 0), pipeline_mode=once),
                  vec()],
        out_specs=pl.BlockSpec((tm, d), lambda i: (i, 0)),
        out_shape=jax.ShapeDtypeStruct((t, d), F32),
        compiler_params=_params(("parallel",), 56),
        name="mix_out_cross_attn",
    )(mix, x, w_out, g1.reshape(1, d), g2.reshape(1, d), w_q, kv.reshape(batch, mem_len, 2 * ca_w), w_o,
      g3.reshape(1, d))


HALO = 16


def _ffn_kernel(x_ref, xp_ref, xn_ref, g4_ref, wg_ref, wv_ref, cwg_ref, cwv_ref, cbg_ref, cbv_ref,
                wd_ref, g5_ref, o_ref, xe_ref, zg_ref, zv_ref, *, tm, tc, tiles_per_seq, nf):
    acc_ref = o_ref
    i = pl.program_id(0)
    f = pl.program_id(1)

    @pl.when(f == 0)
    def _():
        g4 = g4_ref[...]
        pos = i % tiles_per_seq
        prev = _rms(xp_ref[...], g4) * jnp.where(pos == 0, 0.0, 1.0)
        nxt = _rms(xn_ref[...], g4) * jnp.where(pos == tiles_per_seq - 1, 0.0, 1.0)
        xe_ref[0:HALO, :] = prev.astype(BF16)
        xe_ref[HALO:HALO + tm, :] = _rms(x_ref[...], g4).astype(BF16)
        xe_ref[HALO + tm:, :] = nxt.astype(BF16)
        acc_ref[...] = jnp.zeros(acc_ref.shape, F32)

    def conv(z_ref, c, cw, cb, lo, rows):
        lo = HALO + lo
        return (z_ref[c, lo - 1:lo - 1 + rows, :] * cw[0:1] + z_ref[c, lo:lo + rows, :] * cw[1:2]
                + z_ref[c, lo + 1:lo + 1 + rows, :] * cw[2:3] + cb)

    xe = xe_ref[...]
    n_sub = zg_ref.shape[0]
    for c in range(n_sub):
        cols = slice(c * tc, (c + 1) * tc)
        zg_ref[c] = jnp.dot(xe, wg_ref[:, cols], preferred_element_type=F32)
        zv_ref[c] = jnp.dot(xe, wv_ref[:, cols], preferred_element_type=F32)
    for c in range(n_sub):
        cols = slice(c * tc, (c + 1) * tc)
        pieces = 2 if c == n_sub - 1 else 1
        rows = tm // pieces
        for r in range(pieces):
            lo = r * rows
            h = (jax.nn.gelu(conv(zg_ref, c, cwg_ref[:, cols], cbg_ref[:, cols], lo, rows))
                 * conv(zv_ref, c, cwv_ref[:, cols], cbv_ref[:, cols], lo, rows))
            acc_ref[lo:lo + rows, :] += jnp.dot(h.astype(BF16), wd_ref[cols, :], preferred_element_type=F32)

    @pl.when(f == nf - 1)
    def _():
        o_ref[...] = x_ref[...] + _rms(acc_ref[...], g5_ref[...])


def _conv_ffn(x, seq, g4, w_up, conv_w, conv_b, w_down, layer, g5, tm=1024, tf=512, tc=256):
    t, d = x.shape
    d_ff = w_down.shape[1]
    nf = d_ff // tf
    hb = tm // HALO
    last_hb = t // HALO - 1
    kern = functools.partial(_ffn_kernel, tm=tm, tc=tc, tiles_per_seq=seq // tm, nf=nf)
    cb = conv_b.reshape(1, 2 * d_ff)
    return pl.pallas_call(
        kern,
        grid=(t // tm, nf),
        in_specs=[pl.BlockSpec((tm, d), lambda i, f: (i, 0), pipeline_mode=pl.Buffered(1)),
                  pl.BlockSpec((HALO, d), lambda i, f: (jnp.maximum(i * hb - 1, 0), 0)),
                  pl.BlockSpec((HALO, d), lambda i, f: (jnp.minimum((i + 1) * hb, last_hb), 0)),
                  pl.BlockSpec((1, d), lambda i, f: (0, 0)),
                  pl.BlockSpec((None, d, tf), lambda i, f: (layer, 0, f)),
                  pl.BlockSpec((None, d, tf), lambda i, f: (layer, 0, nf + f)),
                  pl.BlockSpec((CONV_W, tf), lambda i, f: (0, f)),
                  pl.BlockSpec((CONV_W, tf), lambda i, f: (0, nf + f)),
                  pl.BlockSpec((1, tf), lambda i, f: (0, f)),
                  pl.BlockSpec((1, tf), lambda i, f: (0, nf + f)),
                  pl.BlockSpec((None, tf, d), lambda i, f: (layer, f, 0)),
                  pl.BlockSpec((1, d), lambda i, f: (0, 0))],
        out_specs=pl.BlockSpec((tm, d), lambda i, f: (i, 0)),
        out_shape=jax.ShapeDtypeStruct((t, d), F32),
        scratch_shapes=[pltpu.VMEM((tm + 2 * HALO, d), BF16),
                        pltpu.VMEM((tf // tc, tm + 2 * HALO, tc), F32),
                        pltpu.VMEM((tf // tc, tm + 2 * HALO, tc), F32)],
        compiler_params=_params(("parallel", "arbitrary"), 58),
        name="conv_ffn",
    )(x, x, x, g4.reshape(1, d), w_up, w_up, conv_w, conv_w, cb, cb, w_down, g5.reshape(1, d))


def _trunk(x3, mem3, p):
    batch, seq, d = x3.shape
    x = x3.reshape(batch * seq, d)
    mem = mem3.reshape(-1, d)
    depth = p["w_in"].shape[0]
    a_w = A_HEADS * HEAD_DIM
    c_w = C_HEADS * 2 * C_HALF
    for layer in range(depth):
        g = p["norm_gains"][layer]
        proj = _norm_matmul(x, g[0], p["w_in"], layer, BF16)
        if layer % 2 == 0:
            e = layer // 2
            qkvs = _even_prep(proj, p["rope"][HEAD_DIM], batch, seq, HEAD_DIM ** -0.5)
            parts = [_band_attention(qkv.reshape(batch * dil, seq // dil, 3 * a_w))
                     for (_, dil), qkv in zip(DILATED_CFGS, qkvs)]
            ya = _combine_configs(parts, batch, seq)
            yb = _spatial_gating(proj, p["b_w_spatial"][e], p["b_b_spatial"][e], p["b_ln_gain"][e],
                                 3 * A_HEADS, 3 * A_HEADS + B_GROUPS)
            mix = jnp.concatenate([ya, yb], axis=-1)
        else:
            o = layer // 2
            qt, kk, vt = _diff_prep(proj, p["rope"][C_HALF], batch, seq, C_HALF ** -0.5 * math.log2(math.e),
                                    DIFF_BK)
            yc = _diff_attention(qt, kk, vt, p["c_lambda"][o], p["c_subln_gain"][o], batch, seq, layer,
                                 bk=DIFF_BK)
            m_c = _fourier_weights(p["d_w_fourier"][o], seq)
            u = _fourier_channel_mix(proj, m_c, 3 * c_w // LANES)
            yd = _seq_dft_real(u, batch, seq, D_GROUPS)
            mix = jnp.concatenate([yc, yd], axis=-1)
        kv = _norm_matmul(mem, p["mem_norm_gain"][layer], p["ca_w_kv"], layer, BF16)
        x = _mix_out_cross_attn(mix, x, p["w_out"], p["ca_w_q"], p["ca_w_o"], layer, g[1], g[2], g[3], kv,
                                batch, seq)
        x = _conv_ffn(x, seq, g[4], p["ffn_w_up"], p["ffn_conv_w"][layer], p["ffn_conv_b"][layer],
                      p["ffn_w_down"], layer, g[5])
    return x.reshape(batch, seq, d)


def kernel(x_prompt, x_sample, mem_prompt, mem_sample, norm_gains, w_in, w_out, b_w_spatial, b_b_spatial,
           b_ln_gain, c_lambda, c_subln_gain, d_w_fourier, mem_norm_gain, ca_w_q, ca_w_kv, ca_w_o, ffn_w_up,
           ffn_conv_w, ffn_conv_b, ffn_w_down):
    p = dict(norm_gains=norm_gains, w_in=w_in.astype(BF16), w_out=w_out.astype(BF16), b_w_spatial=b_w_spatial,
             b_b_spatial=b_b_spatial, b_ln_gain=b_ln_gain, c_lambda=c_lambda, c_subln_gain=c_subln_gain,
             d_w_fourier=d_w_fourier, mem_norm_gain=mem_norm_gain, ca_w_q=ca_w_q.astype(BF16),
             ca_w_kv=ca_w_kv.astype(BF16), ca_w_o=ca_w_o.astype(BF16), ffn_w_up=ffn_w_up.astype(BF16),
             ffn_conv_w=ffn_conv_w, ffn_conv_b=ffn_conv_b, ffn_w_down=ffn_w_down.astype(BF16))
    seq_max = max(x_prompt.shape[1], x_sample.shape[1])
    p["rope"] = {w: _rope_tables(seq_max, w) for w in (HEAD_DIM, C_HALF)}
    return _trunk(x_prompt, mem_prompt, p), _trunk(x_sample, mem_sample, p)
```

```python
import functools
import math

import jax
import jax.numpy as jnp
from jax import lax
from jax.experimental import pallas as pl
from jax.experimental.pallas import tpu as pltpu

F32 = jnp.float32
BF16 = jnp.bfloat16

EPS = 1e-6
ROPE_THETA = 500000.0
ROPE_FRACTION = 4
LANES = 128
SUBLANES = 8
HEAD_DIM = 128
A_HEADS = 8
DILATED_CFGS = ((128, 1), (512, 4), (2048, 16))
BAND_RADIUS = 64
assert all(window // (2 * dil) == BAND_RADIUS for window, dil in DILATED_CFGS)
B_GROUPS = 8
CHUNK = 128
C_HEADS = 12
C_HALF = 64
D_GROUPS = 4
CA_HEADS = 4
CONV_W = 3
NEG_BIG = -1e30
MIB = 1024 * 1024


def _params(semantics, vmem_mib):
    return pltpu.CompilerParams(dimension_semantics=semantics, vmem_limit_bytes=vmem_mib * MIB)


def _rms(x, g):
    return x * lax.rsqrt(jnp.mean(x * x, axis=-1, keepdims=True) + EPS) * g


def _dot_nt(a, b):
    return lax.dot_general(a, b, (((1,), (1,)), ((), ())), preferred_element_type=F32)


def _norm_mm_kernel(x_ref, g_ref, w_ref, o_ref, xn_ref):
    j = pl.program_id(1)

    @pl.when(j == 0)
    def _():
        half = x_ref.shape[0] // 2
        for r in (slice(0, half), slice(half, 2 * half)):
            xn_ref[r, :] = _rms(x_ref[r, :], g_ref[...]).astype(BF16)
            o_ref[r, :] = jnp.dot(xn_ref[r, :], w_ref[...], preferred_element_type=F32).astype(o_ref.dtype)

    @pl.when(j > 0)
    def _():
        o_ref[...] = jnp.dot(xn_ref[...], w_ref[...], preferred_element_type=F32).astype(o_ref.dtype)


def _norm_matmul(x, g, w, layer, out_dtype, tm=1024, tn=1024):
    t, k = x.shape
    n = w.shape[2]
    tm = min(tm, t)
    return pl.pallas_call(
        _norm_mm_kernel,
        grid=(t // tm, n // tn),
        in_specs=[pl.BlockSpec((tm, k), lambda i, j: (i, 0)),
                  pl.BlockSpec((1, k), lambda i, j: (0, 0)),
                  pl.BlockSpec((None, k, tn), lambda i, j: (layer, 0, j))],
        out_specs=pl.BlockSpec((tm, tn), lambda i, j: (i, j)),
        out_shape=jax.ShapeDtypeStruct((t, n), out_dtype),
        scratch_shapes=[pltpu.VMEM((tm, k), BF16)],
        compiler_params=_params(("parallel", "arbitrary"), 40),
        name="norm_matmul",
    )(x, g.reshape(1, k), w)


def _rope_tables(seq, head_w):
    rd = head_w // ROPE_FRACTION
    half = rd // 2
    inv = ROPE_THETA ** (-jnp.arange(half, dtype=F32) / half)
    ang = jnp.arange(seq, dtype=F32)[:, None] * inv[None, :]
    cos = jnp.cos(ang)
    sin = jnp.sin(ang)
    rest = head_w - rd
    c_head = jnp.concatenate([cos, cos, jnp.ones((seq, rest), F32)], axis=-1)
    s_head = jnp.concatenate([-sin, sin, jnp.zeros((seq, rest), F32)], axis=-1)
    reps = LANES // head_w
    return jnp.tile(c_head, (1, reps)), jnp.tile(s_head, (1, reps))


def _rope_block(x, c, s, head_w):
    half = head_w // ROPE_FRACTION // 2
    first = (lax.broadcasted_iota(jnp.int32, x.shape, 1) & (head_w - 1)) < half
    partner = jnp.where(first, pltpu.roll(x, LANES - half, axis=1), pltpu.roll(x, half, axis=1))
    return x * c + partner * s


def _even_prep_kernel(x_ref, c_ref, s_ref, *refs, q_scale, n_blocks):
    out_refs, y_ref = refs[:-1], refs[-1]
    part = pl.program_id(1)
    tm = x_ref.shape[0]

    def rope(scale):
        c = c_ref[...]
        s = s_ref[...]
        for b in range(n_blocks):
            cols = slice(b * LANES, (b + 1) * LANES)
            y_ref[b] = _rope_block(x_ref[:, cols].astype(F32), c, s, HEAD_DIM) * scale

    @pl.when(part == 0)
    def _():
        rope(q_scale)

    @pl.when(part == 1)
    def _():
        rope(1.0)

    @pl.when(part == 2)
    def _():
        for b in range(n_blocks):
            y_ref[b] = x_ref[:, b * LANES:(b + 1) * LANES].astype(F32)

    for (_, dil), o_ref in zip(DILATED_CFGS, out_refs):
        for r in range(dil):
            for b in range(n_blocks):
                rows = y_ref[b, pl.ds(r, tm // dil, stride=dil), :]
                o_ref[0, r, :, b * LANES:(b + 1) * LANES] = rows.astype(BF16)


def _even_prep(proj, tables, batch, seq, q_scale, tm=1024):
    a_w = A_HEADS * HEAD_DIM
    cos_t, sin_t = tables
    spb = seq // tm
    kern = functools.partial(_even_prep_kernel, q_scale=q_scale, n_blocks=a_w // LANES)
    return pl.pallas_call(
        kern,
        grid=(batch * spb, 3),
        in_specs=[pl.BlockSpec((tm, a_w), lambda i, p: (i, p)),
                  pl.BlockSpec((tm, LANES), lambda i, p: (i % spb, 0)),
                  pl.BlockSpec((tm, LANES), lambda i, p: (i % spb, 0))],
        out_specs=[pl.BlockSpec((1, dil, tm // dil, a_w), lambda i, p: (i // spb, 0, i % spb, p))
                   for _, dil in DILATED_CFGS],
        out_shape=[jax.ShapeDtypeStruct((batch, dil, seq // dil, 3 * a_w), BF16) for _, dil in DILATED_CFGS],
        scratch_shapes=[pltpu.VMEM((a_w // LANES, tm, LANES), F32)],
        compiler_params=_params(("parallel", "arbitrary"), 32),
        name="even_prep",
    )(proj, cos_t, sin_t)


def _band_attn_kernel(q_ref, kp_ref, kc_ref, kn_ref, vp_ref, vc_ref, vn_ref, o_ref, lse_ref, *, bq, sub_len):
    q0 = pl.program_id(1) * bq
    sq = 2 * BAND_RADIUS
    win = sq + 2 * BAND_RADIUS
    n_sub = bq // sq
    rel = (lax.broadcasted_iota(jnp.int32, (sq, win), 1) - BAND_RADIUS
           - lax.broadcasted_iota(jnp.int32, (sq, win), 0))
    band = (rel <= BAND_RADIUS) & (rel >= -BAND_RADIUS)
    biases = []
    for u in range(n_sub):
        kpos = q0 + u * sq - BAND_RADIUS + lax.broadcasted_iota(jnp.int32, (sq, win), 1)
        ok = band & (kpos >= 0) & (kpos < sub_len)
        biases.append(jnp.where(ok, 0.0, NEG_BIG))

    def window(p_ref, c_ref, n_ref, cols, u):
        parts = []
        if u == 0:
            parts.append(p_ref[0, :, cols])
        lo = max(u * sq - BAND_RADIUS, 0)
        hi = min(u * sq + sq + BAND_RADIUS, bq)
        parts.append(c_ref[0, lo:hi, cols])
        if u == n_sub - 1:
            parts.append(n_ref[0, :, cols])
        return jnp.concatenate(parts, axis=0)

    tiles = [(h, u) for h in range(A_HEADS) for u in range(n_sub)]

    def scores(h, u):
        cols = slice(h * HEAD_DIM, (h + 1) * HEAD_DIM)
        q = q_ref[0, u * sq:(u + 1) * sq, cols]
        return _dot_nt(q, window(kp_ref, kc_ref, kn_ref, cols, u)) + biases[u]

    s_next = scores(*tiles[0])
    for n, (h, u) in enumerate(tiles):
        cols = slice(h * HEAD_DIM, (h + 1) * HEAD_DIM)
        rows = slice(u * sq, (u + 1) * sq)
        s = s_next
        if n + 1 < len(tiles):
            s_next = scores(*tiles[n + 1])
        m = jnp.max(s, axis=-1, keepdims=True)
        p = jnp.exp(s - m)
        l = jnp.sum(p, axis=-1, keepdims=True)
        o = jnp.dot(p.astype(BF16), window(vp_ref, vc_ref, vn_ref, cols, u), preferred_element_type=F32)
        o_ref[0, rows, cols] = (o / l).astype(o_ref.dtype)
        lse_ref[0, rows, cols] = jnp.broadcast_to(m + jnp.log(l), (sq, HEAD_DIM))


def _band_attention(qkv, bq=256):
    n_seq, sub_len, _ = qkv.shape
    a_w = A_HEADS * HEAD_DIM
    hb = bq // BAND_RADIUS
    last = sub_len // BAND_RADIUS - 1
    kern = functools.partial(_band_attn_kernel, bq=bq, sub_len=sub_len)

    def halo_specs(col):
        return [pl.BlockSpec((1, BAND_RADIUS, a_w), lambda s, i: (s, jnp.maximum(i * hb - 1, 0), col)),
                pl.BlockSpec((1, bq, a_w), lambda s, i: (s, i, col)),
                pl.BlockSpec((1, BAND_RADIUS, a_w), lambda s, i: (s, jnp.minimum((i + 1) * hb, last), col))]

    out_spec = pl.BlockSpec((1, bq, a_w), lambda s, i: (s, i, 0))
    return pl.pallas_call(
        kern,
        grid=(n_seq, sub_len // bq),
        in_specs=[pl.BlockSpec((1, bq, a_w), lambda s, i: (s, i, 0))] + halo_specs(1) + halo_specs(2),
        out_specs=[out_spec, out_spec],
        out_shape=[jax.ShapeDtypeStruct((n_seq, sub_len, a_w), BF16),
                   jax.ShapeDtypeStruct((n_seq, sub_len, a_w), F32)],
        compiler_params=_params(("parallel", "parallel"), 40),
        name="band_attn",
    )(*([qkv] * 7))


def _combine_kernel(o1, l1, o4, l4, o16, l16, y_ref, so4, sl4, so16, sl16):
    tm = y_ref.shape[0]
    for dil, src, dst in ((4, o4, so4), (4, l4, sl4), (16, o16, so16), (16, l16, sl16)):
        for r in range(dil):
            for h in range(A_HEADS):
                dst[h, pl.ds(r, tm // dil, stride=dil), :] = (
                    src[0, r, :, h * HEAD_DIM:(h + 1) * HEAD_DIM].astype(F32))
    for h in range(A_HEADS):
        cols = slice(h * HEAD_DIM, (h + 1) * HEAD_DIM)
        a, b, c = l1[0, 0, :, cols], sl4[h], sl16[h]
        m = jnp.maximum(jnp.maximum(a, b), c)
        wa, wb, wc = jnp.exp(a - m), jnp.exp(b - m), jnp.exp(c - m)
        num = wa * o1[0, 0, :, cols].astype(F32) + wb * so4[h] + wc * so16[h]
        y_ref[:, cols] = (num / (wa + wb + wc)).astype(y_ref.dtype)


def _combine_configs(parts, batch, seq, tm=512):
    a_w = A_HEADS * HEAD_DIM
    spb = seq // tm
    flat, specs = [], []
    for (_, dil), pair in zip(DILATED_CFGS, parts):
        for arr in pair:
            flat.append(arr.reshape(batch, dil, seq // dil, a_w))
            specs.append(pl.BlockSpec((1, dil, tm // dil, a_w), lambda i: (i // spb, 0, i % spb, 0)))
    return pl.pallas_call(
        _combine_kernel,
        grid=(batch * spb,),
        in_specs=specs,
        out_specs=pl.BlockSpec((tm, a_w), lambda i: (i, 0)),
        out_shape=jax.ShapeDtypeStruct((batch * seq, a_w), BF16),
        scratch_shapes=[pltpu.VMEM((A_HEADS, tm, HEAD_DIM), F32)] * 4,
        compiler_params=_params(("parallel",), 40),
        name="dilated_combine",
    )(*flat)


def _gating_kernel(u_ref, v_ref, w_ref, b_ref, g_ref, o_ref, *, n_chunks):
    w = w_ref[0]
    bias = b_ref[0]
    gain = g_ref[0]
    for c in range(n_chunks):
        rows = pl.ds(c * CHUNK, CHUNK)
        v = jax.nn.gelu(v_ref[rows, :].astype(F32))
        mu = jnp.mean(v, axis=-1, keepdims=True)
        d = v - mu
        var = jnp.mean(d * d, axis=-1, keepdims=True)
        vn = d * lax.rsqrt(var + EPS) * gain
        sv = jnp.dot(w, vn.astype(BF16), preferred_element_type=F32) + bias
        o_ref[rows, :] = (jax.nn.gelu(u_ref[rows, :].astype(F32)) * sv).astype(o_ref.dtype)


def _spatial_gating(proj, w_s, b_s, g_ln, u_col, v_col, tm=2048):
    t = proj.shape[0]
    tm = min(tm, t)
    groups = w_s.shape[0]
    bias = jnp.broadcast_to(b_s[:, :, None], (groups, CHUNK, LANES))
    kern = functools.partial(_gating_kernel, n_chunks=tm // CHUNK)
    return pl.pallas_call(
        kern,
        grid=(t // tm, groups),
        in_specs=[pl.BlockSpec((tm, LANES), lambda i, g: (i, u_col + g)),
                  pl.BlockSpec((tm, LANES), lambda i, g: (i, v_col + g)),
                  pl.BlockSpec((1, CHUNK, CHUNK), lambda i, g: (g, 0, 0)),
                  pl.BlockSpec((1, CHUNK, LANES), lambda i, g: (g, 0, 0)),
                  pl.BlockSpec((1, 1, LANES), lambda i, g: (g, 0, 0))],
        out_specs=pl.BlockSpec((tm, LANES), lambda i, g: (i, g)),
        out_shape=jax.ShapeDtypeStruct((t, groups * LANES), BF16),
        compiler_params=_params(("parallel", "parallel"), 32),
        name="spatial_gating",
    )(proj, proj, w_s.astype(BF16), bias, g_ln.reshape(groups, 1, LANES))


V_ROWS = 2 * C_HALF + 16
DIFF_BK = 512


def _diff_prep_kernel(x_ref, c_ref, s_ref, qt_ref, k_ref, vt_ref, *, q_scale, n_blocks):
    part = pl.program_id(1)
    tm = x_ref.shape[0]

    @pl.when(part == 0)
    def _():
        c = c_ref[...]
        s = s_ref[...]
        for h in range(n_blocks):
            cols = slice(h * LANES, (h + 1) * LANES)
            qt_ref[0, h] = (_rope_block(x_ref[:, cols].astype(F32), c, s, C_HALF) * q_scale).T.astype(BF16)

    @pl.when(part == 1)
    def _():
        c = c_ref[...]
        s = s_ref[...]
        for h in range(n_blocks):
            cols = slice(h * LANES, (h + 1) * LANES)
            k_ref[:, cols] = _rope_block(x_ref[:, cols].astype(F32), c, s, C_HALF).astype(BF16)

    @pl.when(part == 2)
    def _():
        ones = jnp.ones((V_ROWS - LANES, tm), BF16)
        for h in range(n_blocks):
            vt_ref[0, h, 0, 0:LANES, :] = x_ref[:, h * LANES:(h + 1) * LANES].astype(F32).T.astype(BF16)
            vt_ref[0, h, 0, LANES:, :] = ones


def _diff_prep(proj, tables, batch, seq, q_scale, bk, tm=512):
    c_w = C_HEADS * 2 * C_HALF
    cos_t, sin_t = tables
    spb = seq // tm
    per_chunk = bk // tm
    kern = functools.partial(_diff_prep_kernel, q_scale=q_scale, n_blocks=C_HEADS)
    return pl.pallas_call(
        kern,
        grid=(batch * spb, 3),
        in_specs=[pl.BlockSpec((tm, c_w), lambda i, p: (i, p)),
                  pl.BlockSpec((tm, LANES), lambda i, p: (i % spb, 0)),
                  pl.BlockSpec((tm, LANES), lambda i, p: (i % spb, 0))],
        out_specs=[pl.BlockSpec((1, C_HEADS, LANES, tm), lambda i, p: (i // spb, 0, 0, i % spb)),
                   pl.BlockSpec((tm, c_w), lambda i, p: (i, 0)),
                   pl.BlockSpec((1, C_HEADS, 1, V_ROWS, tm),
                                lambda i, p: (i // spb, 0, (i % spb) // per_chunk, 0, (i % spb) % per_chunk))],
        out_shape=[jax.ShapeDtypeStruct((batch, C_HEADS, LANES, seq), BF16),
                   jax.ShapeDtypeStruct((batch * seq, c_w), BF16),
                   jax.ShapeDtypeStruct((batch, C_HEADS, seq // bk, V_ROWS, bk), BF16)],
        compiler_params=_params(("parallel", "arbitrary"), 40),
        name="diff_prep",
    )(proj, cos_t, sin_t)


def _diff_attn_kernel(qt_ref, k_ref, vt_ref, lam_ref, g_ref, o_ref, s_ref, mc_ref, m_ref, acc_ref,
                      *, bq, bk, seq, lam_init, unroll):
    n = seq // bk
    sub = SUBLANES
    qt = qt_ref[0, 0]
    row = lax.broadcasted_iota(jnp.int32, qt.shape, 0)
    zero = jnp.zeros_like(qt)
    qs = (jnp.where(row < C_HALF, qt, zero), jnp.where(row < C_HALF, zero, qt))

    def scores(j, slot):
        kc = k_ref[0, pl.ds(pl.multiple_of(j * bk, bk), bk), :]
        for br in range(2):
            s = jnp.dot(kc, qs[br], preferred_element_type=F32)
            s_ref[slot, br] = s
            mc = jnp.max(s.reshape(bk // sub, sub, bq), axis=0)
            mc_ref[slot, br] = jnp.broadcast_to(jnp.max(mc, axis=0, keepdims=True), (sub, bq))

    def accumulate(j, slot):
        vt = vt_ref[0, 0, j]
        for br in range(2):
            m_old = m_ref[br]
            m_new = jnp.maximum(m_old, mc_ref[slot, br])
            alpha = jnp.exp2(m_old - m_new)
            m_ref[br] = m_new
            s3 = s_ref[slot, br].reshape(bk // sub, sub, bq)
            p = jnp.exp2(s3 - m_new[None]).reshape(bk, bq).astype(BF16)
            pv = jnp.dot(vt, p, preferred_element_type=F32)
            acc3 = acc_ref[br].reshape(V_ROWS // sub, sub, bq)
            acc_ref[br] = (alpha[None] * acc3).reshape(V_ROWS, bq) + pv

    m_ref[...] = jnp.full(m_ref.shape, NEG_BIG, F32)
    acc_ref[...] = jnp.zeros(acc_ref.shape, F32)
    scores(0, 0)

    def trip(j, last):
        for u in range(unroll):
            if not (last and u == unroll - 1):
                scores(j + u + 1, (u + 1) % 2)
            accumulate(j + u, u % 2)

    def body(i, carry):
        trip(unroll * i, False)
        return carry

    lax.fori_loop(0, n // unroll - 1, body, 0)
    trip(n - unroll, True)

    lam = lam_ref[...]
    lam_full = (jnp.exp(jnp.sum(lam[0:1] * lam[1:2], keepdims=True))
                - jnp.exp(jnp.sum(lam[2:3] * lam[3:4], keepdims=True)) + lam_init)

    def normalised(a):
        num = a[0:LANES].reshape(LANES // sub, sub, bq)
        return (num / a[LANES:LANES + sub][None]).reshape(LANES, bq)

    o = normalised(acc_ref[0]) - lam_full * normalised(acc_ref[1])
    y = o * lax.rsqrt(jnp.mean(o * o, axis=0, keepdims=True) + EPS) * g_ref[...] * (1.0 - lam_init)
    o_ref[0] = y.T.astype(o_ref.dtype)


def _diff_attention(qt, k, vt, lam, subln_g, batch, seq, layer_idx, bq=512, bk=512, unroll=8):
    c_w = C_HEADS * 2 * C_HALF
    lam_init = 0.8 - 0.6 * math.exp(-0.3 * layer_idx)
    n = seq // bk
    unroll = min(unroll, n)
    assert unroll % 2 == 0 and n % unroll == 0 and vt.shape[2] == n
    kern = functools.partial(_diff_attn_kernel, bq=bq, bk=bk, seq=seq, lam_init=lam_init, unroll=unroll)
    out = pl.pallas_call(
        kern,
        grid=(batch, C_HEADS, seq // bq),
        in_specs=[pl.BlockSpec((1, 1, LANES, bq), lambda b, h, i: (b, h, 0, i)),
                  pl.BlockSpec((1, seq, LANES), lambda b, h, i: (b, 0, h)),
                  pl.BlockSpec((1, 1, n, V_ROWS, bk), lambda b, h, i: (b, h, 0, 0, 0)),
                  pl.BlockSpec((4, C_HALF), lambda b, h, i: (0, 0)),
                  pl.BlockSpec((LANES, bq), lambda b, h, i: (0, 0))],
        out_specs=pl.BlockSpec((1, bq, LANES), lambda b, h, i: (b, i, h)),
        out_shape=jax.ShapeDtypeStruct((batch, seq, c_w), BF16),
        scratch_shapes=[pltpu.VMEM((2, 2, bk, bq), F32),
                        pltpu.VMEM((2, 2, SUBLANES, bq), F32),
                        pltpu.VMEM((2, SUBLANES, bq), F32),
                        pltpu.VMEM((2, V_ROWS, bq), F32)],
        compiler_params=_params(("parallel", "parallel", "arbitrary"), 48),
        name="diff_attn",
    )(qt, k.reshape(batch, seq, c_w), vt, lam, jnp.broadcast_to(subln_g[:, None], (LANES, bq)))
    return out.reshape(batch * seq, c_w)


def _fourier_wprep_kernel(c_ref, s_ref, w_ref, o_ref, *, scale):
    w = w_ref[0]
    mr = jnp.dot(c_ref[...], w, preferred_element_type=F32, precision=lax.Precision.HIGHEST)
    mi = jnp.dot(s_ref[...], w, preferred_element_type=F32, precision=lax.Precision.HIGHEST)
    o_ref[0, :, :LANES] = (mr * scale).astype(BF16)
    o_ref[0, :, LANES:] = (-mi * scale).astype(BF16)


def _fourier_weights(w_f, seq):
    groups, c, _ = w_f.shape
    idx = jnp.arange(c, dtype=jnp.int32)
    ang = (2.0 * math.pi / c) * ((idx[:, None] * idx[None, :]) % c).astype(F32)
    kern = functools.partial(_fourier_wprep_kernel, scale=1.0 / math.sqrt(seq * c))
    return pl.pallas_call(
        kern,
        grid=(groups,),
        in_specs=[pl.BlockSpec((c, c), lambda g: (0, 0)),
                  pl.BlockSpec((c, c), lambda g: (0, 0)),
                  pl.BlockSpec((1, c, c), lambda g: (g, 0, 0))],
        out_specs=pl.BlockSpec((1, c, 2 * c), lambda g: (g, 0, 0)),
        out_shape=jax.ShapeDtypeStruct((groups, c, 2 * c), BF16),
        compiler_params=_params(("parallel",), 32),
        name="fourier_wprep",
    )(jnp.cos(ang), jnp.sin(ang), w_f)


def _chan_mix_kernel(z_ref, m_ref, o_ref):
    o_ref[...] = jnp.dot(z_ref[...], m_ref[0], preferred_element_type=F32).astype(BF16)


def _fourier_channel_mix(proj, m_c, z_col, tm=2048):
    t = proj.shape[0]
    tm = min(tm, t)
    groups = m_c.shape[0]
    return pl.pallas_call(
        _chan_mix_kernel,
        grid=(t // tm, groups),
        in_specs=[pl.BlockSpec((tm, LANES), lambda i, g: (i, z_col + g)),
                  pl.BlockSpec((1, LANES, 2 * LANES), lambda i, g: (g, 0, 0))],
        out_specs=pl.BlockSpec((tm, 2 * LANES), lambda i, g: (i, g)),
        out_shape=jax.ShapeDtypeStruct((t, groups * 2 * LANES), BF16),
        compiler_params=_params(("parallel", "parallel"), 32),
        name="fourier_channel_mix",
    )(proj, m_c)


def _dft_stage1_kernel(u_ref, fr_ref, fi_ref, y_ref, *, pairs):
    u = u_ref[0]
    p = jnp.dot(fr_ref[...], u, preferred_element_type=F32)
    q = jnp.dot(fi_ref[...], u, preferred_element_type=F32)
    for j in range(pairs):
        re = slice(2 * j * LANES, (2 * j + 1) * LANES)
        im = slice((2 * j + 1) * LANES, (2 * j + 2) * LANES)
        y_ref[0, :, re] = (p[:, re] - q[:, im]).astype(BF16)
        y_ref[0, :, im] = (q[:, re] + p[:, im]).astype(BF16)


def _dft_stage2_kernel(y_ref, gc_ref, gs_ref, o_ref, *, groups):
    for t in range(y_ref.shape[1]):
        y = y_ref[0, t]
        a = jnp.dot(gc_ref[t], y, preferred_element_type=F32)
        b = jnp.dot(gs_ref[t], y, preferred_element_type=F32)
        for g in range(groups):
            re = slice(2 * g * LANES, (2 * g + 1) * LANES)
            im = slice((2 * g + 1) * LANES, (2 * g + 2) * LANES)
            out_cols = slice((t * groups + g) * LANES, (t * groups + g + 1) * LANES)
            o_ref[0, :, out_cols] = (a[:, re] + b[:, im]).astype(o_ref.dtype)


def _seq_dft_real(u, batch, seq, groups, tc=4096, kb=4):
    n1 = 128 if seq >= 16384 else 64
    n2 = seq // n1
    wc = groups * 2 * LANES
    i1 = jnp.arange(n1, dtype=jnp.int32)
    ang1 = (2.0 * math.pi / n1) * ((i1[:, None] * i1[None, :]) % n1).astype(F32)
    fr = jnp.cos(ang1).astype(BF16)
    fi = (-jnp.sin(ang1)).astype(BF16)
    cols = n2 * wc
    tc = min(tc, cols)
    y = pl.pallas_call(
        functools.partial(_dft_stage1_kernel, pairs=tc // (2 * LANES)),
        grid=(batch, cols // tc),
        in_specs=[pl.BlockSpec((1, n1, tc), lambda b, j: (b, 0, j)),
                  pl.BlockSpec((n1, n1), lambda b, j: (0, 0)),
                  pl.BlockSpec((n1, n1), lambda b, j: (0, 0))],
        out_specs=pl.BlockSpec((1, n1, tc), lambda b, j: (b, 0, j)),
        out_shape=jax.ShapeDtypeStruct((batch, n1, cols), BF16),
        compiler_params=_params(("parallel", "parallel"), 32),
        name="dft_stage1",
    )(u.reshape(batch, n1, cols), fr, fi)
    i2 = jnp.arange(n2, dtype=jnp.int32)
    tw = (i2[None, None, :] * (i1[:, None, None] + n1 * i2[None, :, None])) % seq
    ang2 = (2.0 * math.pi / seq) * tw.astype(F32)
    gc = jnp.cos(ang2).astype(BF16)
    gs = jnp.sin(ang2).astype(BF16)
    out = pl.pallas_call(
        functools.partial(_dft_stage2_kernel, groups=groups),
        grid=(batch, n1 // kb),
        in_specs=[pl.BlockSpec((1, kb, n2, wc), lambda b, k: (b, k, 0, 0)),
                  pl.BlockSpec((kb, n2, n2), lambda b, k: (k, 0, 0)),
                  pl.BlockSpec((kb, n2, n2), lambda b, k: (k, 0, 0))],
        out_specs=pl.BlockSpec((1, n2, kb * groups * LANES), lambda b, k: (b, 0, k)),
        out_shape=jax.ShapeDtypeStruct((batch, n2, n1 * groups * LANES), BF16),
        compiler_params=_params(("parallel", "parallel"), 32),
        name="dft_stage2",
    )(y.reshape(batch, n1, n2, wc), gc, gs)
    return out.reshape(batch * seq, groups * LANES)


def _mix_out_cross_attn_kernel(mix_ref, x_ref, wout_ref, g1_ref, g2_ref, wq_ref, kv_ref, wo_ref, g3_ref, o_ref,
                               *, heads, scale, halves):
    ca_w = heads * HEAD_DIM
    hr = mix_ref.shape[0] // halves
    rows = [slice(r * hr, (r + 1) * hr) for r in range(halves)]
    ys = [jnp.dot(mix_ref[r, :], wout_ref[...], preferred_element_type=F32) for r in rows]
    x1s = [x_ref[r, :] + _rms(y, g1_ref[...]) for r, y in zip(rows, ys)]
    qs = [jnp.dot(_rms(x1, g2_ref[...]).astype(BF16), wq_ref[...], preferred_element_type=F32).astype(BF16)
          for x1 in x1s]
    cas = []
    for q in qs:
        outs = []
        for h in range(heads):
            cols = slice(h * HEAD_DIM, (h + 1) * HEAD_DIM)
            s = _dot_nt(q[:, cols], kv_ref[0, :, cols]) * scale
            m = jnp.max(s, axis=-1, keepdims=True)
            p = jnp.exp(s - m)
            l = jnp.sum(p, axis=-1, keepdims=True)
            v = kv_ref[0, :, ca_w + h * HEAD_DIM:ca_w + (h + 1) * HEAD_DIM]
            outs.append((jnp.dot(p.astype(BF16), v, preferred_element_type=F32) / l).astype(BF16))
        cas.append(jnp.concatenate(outs, axis=1))
    zs = [jnp.dot(ca, wo_ref[...], preferred_element_type=F32) for ca in cas]
    for r, x1, z in zip(rows, x1s, zs):
        o_ref[r, :] = x1 + _rms(z, g3_ref[...])


def _mix_out_cross_attn(mix, x, w_out, w_q, w_o, layer, g1, g2, g3, kv, batch, seq, tm=512):
    t, d = x.shape
    ca_w = CA_HEADS * HEAD_DIM
    mem_len = kv.shape[0] // batch
    spb = seq // tm
    kern = functools.partial(_mix_out_cross_attn_kernel, heads=CA_HEADS, scale=HEAD_DIM ** -0.5, halves=2)
    once = pl.Buffered(1)
    vec = lambda: pl.BlockSpec((1, d), lambda i: (0, 0))
    return pl.pallas_call(
        kern,
        grid=(t // tm,),
        in_specs=[pl.BlockSpec((tm, d), lambda i: (i, 0)),
                  pl.BlockSpec((tm, d), lambda i: (i, 0)),
                  pl.BlockSpec((None, d, d), lambda i: (layer, 0, 0), pipeline_mode=once),
                  vec(), vec(),
                  pl.BlockSpec((None, d, ca_w), lambda i: (layer, 0, 0), pipeline_mode=once),
                  pl.BlockSpec((1, mem_len, 2 * ca_w), lambda i: (i // spb, 0, 0)),
                  pl.BlockSpec((None, ca_w, d), lambda i: (layer, 0, 0), pipeline_mode=once),
                  vec()],
        out_specs=pl.BlockSpec((tm, d), lambda i: (i, 0)),
        out_shape=jax.ShapeDtypeStruct((t, d), F32),
        compiler_params=_params(("parallel",), 56),
        name="mix_out_cross_attn",
    )(mix, x, w_out, g1.reshape(1, d), g2.reshape(1, d), w_q, kv.reshape(batch, mem_len, 2 * ca_w), w_o,
      g3.reshape(1, d))


HALO = 16


def _ffn_kernel(x_ref, xp_ref, xn_ref, g4_ref, wg_ref, wv_ref, cwg_ref, cwv_ref, cbg_ref, cbv_ref,
                wd_ref, g5_ref, o_ref, xe_ref, zg_ref, zv_ref, *, tm, tc, tiles_per_seq, nf):
    acc_ref = o_ref
    i = pl.program_id(0)
    f = pl.program_id(1)

    @pl.when(f == 0)
    def _():
        g4 = g4_ref[...]
        pos = i % tiles_per_seq
        prev = _rms(xp_ref[...], g4) * jnp.where(pos == 0, 0.0, 1.0)
        nxt = _rms(xn_ref[...], g4) * jnp.where(pos == tiles_per_seq - 1, 0.0, 1.0)
        xe_ref[0:HALO, :] = prev.astype(BF16)
        xe_ref[HALO:HALO + tm, :] = _rms(x_ref[...], g4).astype(BF16)
        xe_ref[HALO + tm:, :] = nxt.astype(BF16)
        acc_ref[...] = jnp.zeros(acc_ref.shape, F32)

    def conv(z_ref, c, cw, cb, lo, rows):
        lo = HALO + lo
        return (z_ref[c, lo - 1:lo - 1 + rows, :] * cw[0:1] + z_ref[c, lo:lo + rows, :] * cw[1:2]
                + z_ref[c, lo + 1:lo + 1 + rows, :] * cw[2:3] + cb)

    xe = xe_ref[...]
    n_sub = zg_ref.shape[0]
    for c in range(n_sub):
        cols = slice(c * tc, (c + 1) * tc)
        zg_ref[c] = jnp.dot(xe, wg_ref[:, cols], preferred_element_type=F32)
        zv_ref[c] = jnp.dot(xe, wv_ref[:, cols], preferred_element_type=F32)
    for c in range(n_sub):
        cols = slice(c * tc, (c + 1) * tc)
        pieces = 2 if c == n_sub - 1 else 1
        rows = tm // pieces
        for r in range(pieces):
            lo = r * rows
            h = (jax.nn.gelu(conv(zg_ref, c, cwg_ref[:, cols], cbg_ref[:, cols], lo, rows))
                 * conv(zv_ref, c, cwv_ref[:, cols], cbv_ref[:, cols], lo, rows))
            acc_ref[lo:lo + rows, :] += jnp.dot(h.astype(BF16), wd_ref[cols, :], preferred_element_type=F32)

    @pl.when(f == nf - 1)
    def _():
        o_ref[...] = x_ref[...] + _rms(acc_ref[...], g5_ref[...])


def _conv_ffn(x, seq, g4, w_up, conv_w, conv_b, w_down, layer, g5, tm=1024, tf=512, tc=256):
    t, d = x.shape
    d_ff = w_down.shape[1]
    nf = d_ff // tf
    hb = tm // HALO
    last_hb = t // HALO - 1
    kern = functools.partial(_ffn_kernel, tm=tm, tc=tc, tiles_per_seq=seq // tm, nf=nf)
    cb = conv_b.reshape(1, 2 * d_ff)
    return pl.pallas_call(
        kern,
        grid=(t // tm, nf),
        in_specs=[pl.BlockSpec((tm, d), lambda i, f: (i, 0), pipeline_mode=pl.Buffered(1)),
                  pl.BlockSpec((HALO, d), lambda i, f: (jnp.maximum(i * hb - 1, 0), 0)),
                  pl.BlockSpec((HALO, d), lambda i, f: (jnp.minimum((i + 1) * hb, last_hb), 0)),
                  pl.BlockSpec((1, d), lambda i, f: (0, 0)),
                  pl.BlockSpec((None, d, tf), lambda i, f: (layer, 0, f)),
                  pl.BlockSpec((None, d, tf), lambda i, f: (layer, 0, nf + f)),
                  pl.BlockSpec((CONV_W, tf), lambda i, f: (0, f)),
                  pl.BlockSpec((CONV_W, tf), lambda i, f: (0, nf + f)),
                  pl.BlockSpec((1, tf), lambda i, f: (0, f)),
                  pl.BlockSpec((1, tf), lambda i, f: (0, nf + f)),
                  pl.BlockSpec((None, tf, d), lambda i, f: (layer, f, 0)),
                  pl.BlockSpec((1, d), lambda i, f: (0, 0))],
        out_specs=pl.BlockSpec((tm, d), lambda i, f: (i, 0)),
        out_shape=jax.ShapeDtypeStruct((t, d), F32),
        scratch_shapes=[pltpu.VMEM((tm + 2 * HALO, d), BF16),
                        pltpu.VMEM((tf // tc, tm + 2 * HALO, tc), F32),
                        pltpu.VMEM((tf // tc, tm + 2 * HALO, tc), F32)],
        compiler_params=_params(("parallel", "arbitrary"), 58),
        name="conv_ffn",
    )(x, x, x, g4.reshape(1, d), w_up, w_up, conv_w, conv_w, cb, cb, w_down, g5.reshape(1, d))


def _trunk(x3, mem3, p):
    batch, seq, d = x3.shape
    x = x3.reshape(batch * seq, d)
    mem = mem3.reshape(-1, d)
    depth = p["w_in"].shape[0]
    a_w = A_HEADS * HEAD_DIM
    c_w = C_HEADS * 2 * C_HALF
    for layer in range(depth):
        g = p["norm_gains"][layer]
        proj = _norm_matmul(x, g[0], p["w_in"], layer, BF16)
        if layer % 2 == 0:
            e = layer // 2
            qkvs = _even_prep(proj, p["rope"][HEAD_DIM], batch, seq, HEAD_DIM ** -0.5)
            parts = [_band_attention(qkv.reshape(batch * dil, seq // dil, 3 * a_w))
                     for (_, dil), qkv in zip(DILATED_CFGS, qkvs)]
            ya = _combine_configs(parts, batch, seq)
            yb = _spatial_gating(proj, p["b_w_spatial"][e], p["b_b_spatial"][e], p["b_ln_gain"][e],
                                 3 * A_HEADS, 3 * A_HEADS + B_GROUPS)
            mix = jnp.concatenate([ya, yb], axis=-1)
        else:
            o = layer // 2
            qt, kk, vt = _diff_prep(proj, p["rope"][C_HALF], batch, seq, C_HALF ** -0.5 * math.log2(math.e),
                                    DIFF_BK)
            yc = _diff_attention(qt, kk, vt, p["c_lambda"][o], p["c_subln_gain"][o], batch, seq, layer,
                                 bk=DIFF_BK)
            m_c = _fourier_weights(p["d_w_fourier"][o], seq)
            u = _fourier_channel_mix(proj, m_c, 3 * c_w // LANES)
            yd = _seq_dft_real(u, batch, seq, D_GROUPS)
            mix = jnp.concatenate([yc, yd], axis=-1)
        kv = _norm_matmul(mem, p["mem_norm_gain"][layer], p["ca_w_kv"], layer, BF16)
        x = _mix_out_cross_attn(mix, x, p["w_out"], p["ca_w_q"], p["ca_w_o"], layer, g[1], g[2], g[3], kv,
                                batch, seq)
        x = _conv_ffn(x, seq, g[4], p["ffn_w_up"], p["ffn_conv_w"][layer], p["ffn_conv_b"][layer],
                      p["ffn_w_down"], layer, g[5])
    return x.reshape(batch, seq, d)


def kernel(x_prompt, x_sample, mem_prompt, mem_sample, norm_gains, w_in, w_out, b_w_spatial, b_b_spatial,
           b_ln_gain, c_lambda, c_subln_gain, d_w_fourier, mem_norm_gain, ca_w_q, ca_w_kv, ca_w_o, ffn_w_up,
           ffn_conv_w, ffn_conv_b, ffn_w_down):
    p = dict(norm_gains=norm_gains, w_in=w_in.astype(BF16), w_out=w_out.astype(BF16), b_w_spatial=b_w_spatial,
             b_b_spatial=b_b_spatial, b_ln_gain=b_ln_gain, c_lambda=c_lambda, c_subln_gain=c_subln_gain,
             d_w_fourier=d_w_fourier, mem_norm_gain=mem_norm_gain, ca_w_q=ca_w_q.astype(BF16),
             ca_w_kv=ca_w_kv.astype(BF16), ca_w_o=ca_w_o.astype(BF16), ffn_w_up=ffn_w_up.astype(BF16),
             ffn_conv_w=ffn_conv_w, ffn_conv_b=ffn_conv_b, ffn_w_down=ffn_w_down.astype(BF16))
    seq_max = max(x_prompt.shape[1], x_sample.shape[1])
    p["rope"] = {w: _rope_tables(seq_max, w) for w in (HEAD_DIM, C_HALF)}
    return _trunk(x_prompt, mem_prompt, p), _trunk(x_sample, mem_sample, p)
```

```python
import functools
import math

import jax
import jax.numpy as jnp
from jax import lax
from jax.experimental import pallas as pl
from jax.experimental.pallas import tpu as pltpu

F32 = jnp.float32
BF16 = jnp.bfloat16

EPS = 1e-6
ROPE_THETA = 500000.0
ROPE_FRACTION = 4
LANES = 128
SUBLANES = 8
HEAD_DIM = 128
A_HEADS = 8
DILATED_CFGS = ((128, 1), (512, 4), (2048, 16))
BAND_RADIUS = 64
assert all(window // (2 * dil) == BAND_RADIUS for window, dil in DILATED_CFGS)
B_GROUPS = 8
CHUNK = 128
C_HEADS = 12
C_HALF = 64
D_GROUPS = 4
CA_HEADS = 4
CONV_W = 3
NEG_BIG = -1e30
MIB = 1024 * 1024


def _params(semantics, vmem_mib):
    return pltpu.CompilerParams(dimension_semantics=semantics, vmem_limit_bytes=vmem_mib * MIB)


def _rms(x, g):
    return x * lax.rsqrt(jnp.mean(x * x, axis=-1, keepdims=True) + EPS) * g


def _dot_nt(a, b):
    return lax.dot_general(a, b, (((1,), (1,)), ((), ())), preferred_element_type=F32)


def _norm_mm_kernel(x_ref, g_ref, w_ref, o_ref, xn_ref):
    j = pl.program_id(1)

    @pl.when(j == 0)
    def _():
        half = x_ref.shape[0] // 2
        for r in (slice(0, half), slice(half, 2 * half)):
            xn_ref[r, :] = _rms(x_ref[r, :], g_ref[...]).astype(BF16)
            o_ref[r, :] = jnp.dot(xn_ref[r, :], w_ref[...], preferred_element_type=F32).astype(o_ref.dtype)

    @pl.when(j > 0)
    def _():
        o_ref[...] = jnp.dot(xn_ref[...], w_ref[...], preferred_element_type=F32).astype(o_ref.dtype)


def _norm_matmul(x, g, w, layer, out_dtype, tm=1024, tn=1024):
    t, k = x.shape
    n = w.shape[2]
    tm = min(tm, t)
    return pl.pallas_call(
        _norm_mm_kernel,
        grid=(t // tm, n // tn),
        in_specs=[pl.BlockSpec((tm, k), lambda i, j: (i, 0)),
                  pl.BlockSpec((1, k), lambda i, j: (0, 0)),
                  pl.BlockSpec((None, k, tn), lambda i, j: (layer, 0, j))],
        out_specs=pl.BlockSpec((tm, tn), lambda i, j: (i, j)),
        out_shape=jax.ShapeDtypeStruct((t, n), out_dtype),
        scratch_shapes=[pltpu.VMEM((tm, k), BF16)],
        compiler_params=_params(("parallel", "arbitrary"), 40),
        name="norm_matmul",
    )(x, g.reshape(1, k), w)


def _rope_tables(seq, head_w):
    rd = head_w // ROPE_FRACTION
    half = rd // 2
    inv = ROPE_THETA ** (-jnp.arange(half, dtype=F32) / half)
    ang = jnp.arange(seq, dtype=F32)[:, None] * inv[None, :]
    cos = jnp.cos(ang)
    sin = jnp.sin(ang)
    rest = head_w - rd
    c_head = jnp.concatenate([cos, cos, jnp.ones((seq, rest), F32)], axis=-1)
    s_head = jnp.concatenate([-sin, sin, jnp.zeros((seq, rest), F32)], axis=-1)
    reps = LANES // head_w
    return jnp.tile(c_head, (1, reps)), jnp.tile(s_head, (1, reps))


def _rope_block(x, c, s, head_w):
    half = head_w // ROPE_FRACTION // 2
    first = (lax.broadcasted_iota(jnp.int32, x.shape, 1) & (head_w - 1)) < half
    partner = jnp.where(first, pltpu.roll(x, LANES - half, axis=1), pltpu.roll(x, half, axis=1))
    return x * c + partner * s


def _even_prep_kernel(x_ref, c_ref, s_ref, *refs, q_scale, n_blocks):
    out_refs, y_ref = refs[:-1], refs[-1]
    part = pl.program_id(1)
    tm = x_ref.shape[0]

    def rope(scale):
        c = c_ref[...]
        s = s_ref[...]
        for b in range(n_blocks):
            cols = slice(b * LANES, (b + 1) * LANES)
            y_ref[b] = _rope_block(x_ref[:, cols].astype(F32), c, s, HEAD_DIM) * scale

    @pl.when(part == 0)
    def _():
        rope(q_scale)

    @pl.when(part == 1)
    def _():
        rope(1.0)

    @pl.when(part == 2)
    def _():
        for b in range(n_blocks):
            y_ref[b] = x_ref[:, b * LANES:(b + 1) * LANES].astype(F32)

    for (_, dil), o_ref in zip(DILATED_CFGS, out_refs):
        for r in range(dil):
            for b in range(n_blocks):
                rows = y_ref[b, pl.ds(r, tm // dil, stride=dil), :]
                o_ref[0, r, :, b * LANES:(b + 1) * LANES] = rows.astype(BF16)


def _even_prep(proj, tables, batch, seq, q_scale, tm=1024):
    a_w = A_HEADS * HEAD_DIM
    cos_t, sin_t = tables
    spb = seq // tm
    kern = functools.partial(_even_prep_kernel, q_scale=q_scale, n_blocks=a_w // LANES)
    return pl.pallas_call(
        kern,
        grid=(batch * spb, 3),
        in_specs=[pl.BlockSpec((tm, a_w), lambda i, p: (i, p)),
                  pl.BlockSpec((tm, LANES), lambda i, p: (i % spb, 0)),
                  pl.BlockSpec((tm, LANES), lambda i, p: (i % spb, 0))],
        out_specs=[pl.BlockSpec((1, dil, tm // dil, a_w), lambda i, p: (i // spb, 0, i % spb, p))
                   for _, dil in DILATED_CFGS],
        out_shape=[jax.ShapeDtypeStruct((batch, dil, seq // dil, 3 * a_w), BF16) for _, dil in DILATED_CFGS],
        scratch_shapes=[pltpu.VMEM((a_w // LANES, tm, LANES), F32)],
        compiler_params=_params(("parallel", "arbitrary"), 32),
        name="even_prep",
    )(proj, cos_t, sin_t)


def _band_attn_kernel(q_ref, kp_ref, kc_ref, kn_ref, vp_ref, vc_ref, vn_ref, *refs, bq, sub_len, merge):
    if merge:
        o4, l4, o16, l16, y_ref, so4, sl4, so16, sl16 = refs
        for dil, src, dst in ((4, o4, so4), (4, l4, sl4), (16, o16, so16), (16, l16, sl16)):
            for r in range(dil):
                for h in range(A_HEADS):
                    dst[h, pl.ds(r, bq // dil, stride=dil), :] = (
                        src[0, r, :, h * HEAD_DIM:(h + 1) * HEAD_DIM].astype(F32))
    else:
        o_ref, lse_ref = refs
    q0 = pl.program_id(1) * bq
    sq = 2 * BAND_RADIUS
    win = sq + 2 * BAND_RADIUS
    n_sub = bq // sq
    rel = (lax.broadcasted_iota(jnp.int32, (sq, win), 1) - BAND_RADIUS
           - lax.broadcasted_iota(jnp.int32, (sq, win), 0))
    band = (rel <= BAND_RADIUS) & (rel >= -BAND_RADIUS)
    biases = []
    for u in range(n_sub):
        kpos = q0 + u * sq - BAND_RADIUS + lax.broadcasted_iota(jnp.int32, (sq, win), 1)
        ok = band & (kpos >= 0) & (kpos < sub_len)
        biases.append(jnp.where(ok, 0.0, NEG_BIG))

    def window(p_ref, c_ref, n_ref, cols, u):
        parts = []
        if u == 0:
            parts.append(p_ref[0, :, cols])
        lo = max(u * sq - BAND_RADIUS, 0)
        hi = min(u * sq + sq + BAND_RADIUS, bq)
        parts.append(c_ref[0, lo:hi, cols])
        if u == n_sub - 1:
            parts.append(n_ref[0, :, cols])
        return jnp.concatenate(parts, axis=0)

    tiles = [(h, u) for h in range(A_HEADS) for u in range(n_sub)]

    def scores(h, u):
        cols = slice(h * HEAD_DIM, (h + 1) * HEAD_DIM)
        q = q_ref[0, u * sq:(u + 1) * sq, cols]
        return _dot_nt(q, window(kp_ref, kc_ref, kn_ref, cols, u)) + biases[u]

    s_next = scores(*tiles[0])
    for n, (h, u) in enumerate(tiles):
        cols = slice(h * HEAD_DIM, (h + 1) * HEAD_DIM)
        rows = slice(u * sq, (u + 1) * sq)
        s = s_next
        if n + 1 < len(tiles):
            s_next = scores(*tiles[n + 1])
        m = jnp.max(s, axis=-1, keepdims=True)
        p = jnp.exp(s - m)
        l = jnp.sum(p, axis=-1, keepdims=True)
        o = jnp.dot(p.astype(BF16), window(vp_ref, vc_ref, vn_ref, cols, u), preferred_element_type=F32) / l
        lse = jnp.broadcast_to(m + jnp.log(l), (sq, HEAD_DIM))
        if merge:
            b, c = sl4[h, rows, :], sl16[h, rows, :]
            top = jnp.maximum(jnp.maximum(lse, b), c)
            wa, wb, wc = jnp.exp(lse - top), jnp.exp(b - top), jnp.exp(c - top)
            num = wa * o + wb * so4[h, rows, :] + wc * so16[h, rows, :]
            y_ref[0, rows, cols] = (num / (wa + wb + wc)).astype(y_ref.dtype)
        else:
            o_ref[0, rows, cols] = o.astype(o_ref.dtype)
            lse_ref[0, rows, cols] = lse


def _band_attention(qkv, dilated=None, bq=256):
    n_seq, sub_len, _ = qkv.shape
    a_w = A_HEADS * HEAD_DIM
    hb = bq // BAND_RADIUS
    last = sub_len // BAND_RADIUS - 1
    merge = dilated is not None
    kern = functools.partial(_band_attn_kernel, bq=bq, sub_len=sub_len, merge=merge)

    def halo_specs(col):
        return [pl.BlockSpec((1, BAND_RADIUS, a_w), lambda s, i: (s, jnp.maximum(i * hb - 1, 0), col)),
                pl.BlockSpec((1, bq, a_w), lambda s, i: (s, i, col)),
                pl.BlockSpec((1, BAND_RADIUS, a_w), lambda s, i: (s, jnp.minimum((i + 1) * hb, last), col))]

    out_spec = pl.BlockSpec((1, bq, a_w), lambda s, i: (s, i, 0))
    in_specs = [pl.BlockSpec((1, bq, a_w), lambda s, i: (s, i, 0))] + halo_specs(1) + halo_specs(2)
    operands = [qkv] * 7
    if merge:
        for (_, dil), pair in zip(DILATED_CFGS[1:], dilated):
            for arr in pair:
                operands.append(arr.reshape(n_seq, dil, sub_len // dil, a_w))
                in_specs.append(pl.BlockSpec((1, dil, bq // dil, a_w), lambda s, i: (s, 0, i, 0)))
        out_specs, out_shape = out_spec, jax.ShapeDtypeStruct((n_seq, sub_len, a_w), BF16)
        scratch = [pltpu.VMEM((A_HEADS, bq, HEAD_DIM), F32)] * 4
    else:
        out_specs = [out_spec, out_spec]
        out_shape = [jax.ShapeDtypeStruct((n_seq, sub_len, a_w), BF16),
                     jax.ShapeDtypeStruct((n_seq, sub_len, a_w), F32)]
        scratch = []
    return pl.pallas_call(
        kern,
        grid=(n_seq, sub_len // bq),
        in_specs=in_specs,
        out_specs=out_specs,
        out_shape=out_shape,
        scratch_shapes=scratch,
        compiler_params=_params(("parallel", "parallel"), 40),
        name="band_attn_merge" if merge else "band_attn",
    )(*operands)


def _gating_kernel(u_ref, v_ref, w_ref, b_ref, g_ref, o_ref, *, n_chunks):
    w = w_ref[0]
    bias = b_ref[0]
    gain = g_ref[0]
    for c in range(n_chunks):
        rows = pl.ds(c * CHUNK, CHUNK)
        v = jax.nn.gelu(v_ref[rows, :].astype(F32))
        mu = jnp.mean(v, axis=-1, keepdims=True)
        d = v - mu
        var = jnp.mean(d * d, axis=-1, keepdims=True)
        vn = d * lax.rsqrt(var + EPS) * gain
        sv = jnp.dot(w, vn.astype(BF16), preferred_element_type=F32) + bias
        o_ref[rows, :] = (jax.nn.gelu(u_ref[rows, :].astype(F32)) * sv).astype(o_ref.dtype)


def _spatial_gating(proj, w_s, b_s, g_ln, u_col, v_col, tm=2048):
    t = proj.shape[0]
    tm = min(tm, t)
    groups = w_s.shape[0]
    bias = jnp.broadcast_to(b_s[:, :, None], (groups, CHUNK, LANES))
    kern = functools.partial(_gating_kernel, n_chunks=tm // CHUNK)
    return pl.pallas_call(
        kern,
        grid=(t // tm, groups),
        in_specs=[pl.BlockSpec((tm, LANES), lambda i, g: (i, u_col + g)),
                  pl.BlockSpec((tm, LANES), lambda i, g: (i, v_col + g)),
                  pl.BlockSpec((1, CHUNK, CHUNK), lambda i, g: (g, 0, 0)),
                  pl.BlockSpec((1, CHUNK, LANES), lambda i, g: (g, 0, 0)),
                  pl.BlockSpec((1, 1, LANES), lambda i, g: (g, 0, 0))],
        out_specs=pl.BlockSpec((tm, LANES), lambda i, g: (i, g)),
        out_shape=jax.ShapeDtypeStruct((t, groups * LANES), BF16),
        compiler_params=_params(("parallel", "parallel"), 32),
        name="spatial_gating",
    )(proj, proj, w_s.astype(BF16), bias, g_ln.reshape(groups, 1, LANES))


V_ROWS = 2 * C_HALF + 16
DIFF_BK = 512


def _diff_prep_kernel(x_ref, c_ref, s_ref, qt_ref, k_ref, vt_ref, *, q_scale, n_blocks):
    part = pl.program_id(1)
    tm = x_ref.shape[0]

    @pl.when(part == 0)
    def _():
        c = c_ref[...]
        s = s_ref[...]
        for h in range(n_blocks):
            cols = slice(h * LANES, (h + 1) * LANES)
            qt_ref[0, h] = (_rope_block(x_ref[:, cols].astype(F32), c, s, C_HALF) * q_scale).T.astype(BF16)

    @pl.when(part == 1)
    def _():
        c = c_ref[...]
        s = s_ref[...]
        for h in range(n_blocks):
            cols = slice(h * LANES, (h + 1) * LANES)
            k_ref[:, cols] = _rope_block(x_ref[:, cols].astype(F32), c, s, C_HALF).astype(BF16)

    @pl.when(part == 2)
    def _():
        ones = jnp.ones((V_ROWS - LANES, tm), BF16)
        for h in range(n_blocks):
            vt_ref[0, h, 0, 0:LANES, :] = x_ref[:, h * LANES:(h + 1) * LANES].astype(F32).T.astype(BF16)
            vt_ref[0, h, 0, LANES:, :] = ones


def _diff_prep(proj, tables, batch, seq, q_scale, bk, tm=512):
    c_w = C_HEADS * 2 * C_HALF
    cos_t, sin_t = tables
    spb = seq // tm
    per_chunk = bk // tm
    kern = functools.partial(_diff_prep_kernel, q_scale=q_scale, n_blocks=C_HEADS)
    return pl.pallas_call(
        kern,
        grid=(batch * spb, 3),
        in_specs=[pl.BlockSpec((tm, c_w), lambda i, p: (i, p)),
                  pl.BlockSpec((tm, LANES), lambda i, p: (i % spb, 0)),
                  pl.BlockSpec((tm, LANES), lambda i, p: (i % spb, 0))],
        out_specs=[pl.BlockSpec((1, C_HEADS, LANES, tm), lambda i, p: (i // spb, 0, 0, i % spb)),
                   pl.BlockSpec((tm, c_w), lambda i, p: (i, 0)),
                   pl.BlockSpec((1, C_HEADS, 1, V_ROWS, tm),
                                lambda i, p: (i // spb, 0, (i % spb) // per_chunk, 0, (i % spb) % per_chunk))],
        out_shape=[jax.ShapeDtypeStruct((batch, C_HEADS, LANES, seq), BF16),
                   jax.ShapeDtypeStruct((batch * seq, c_w), BF16),
                   jax.ShapeDtypeStruct((batch, C_HEADS, seq // bk, V_ROWS, bk), BF16)],
        compiler_params=_params(("parallel", "arbitrary"), 40),
        name="diff_prep",
    )(proj, cos_t, sin_t)


def _diff_attn_kernel(qt_ref, k_ref, vt_ref, lam_ref, g_ref, o_ref, s_ref, mc_ref, m_ref, acc_ref,
                      *, bq, bk, seq, lam_init, unroll):
    n = seq // bk
    sub = SUBLANES
    qt = qt_ref[0, 0]
    row = lax.broadcasted_iota(jnp.int32, qt.shape, 0)
    zero = jnp.zeros_like(qt)
    qs = (jnp.where(row < C_HALF, qt, zero), jnp.where(row < C_HALF, zero, qt))

    def scores(j, slot):
        kc = k_ref[0, pl.ds(pl.multiple_of(j * bk, bk), bk), :]
        for br in range(2):
            s = jnp.dot(kc, qs[br], preferred_element_type=F32)
            s_ref[slot, br] = s
            mc = jnp.max(s.reshape(bk // sub, sub, bq), axis=0)
            mc_ref[slot, br] = jnp.broadcast_to(jnp.max(mc, axis=0, keepdims=True), (sub, bq))

    def accumulate(j, slot):
        vt = vt_ref[0, 0, j]
        for br in range(2):
            m_old = m_ref[br]
            m_new = jnp.maximum(m_old, mc_ref[slot, br])
            alpha = jnp.exp2(m_old - m_new)
            m_ref[br] = m_new
            s3 = s_ref[slot, br].reshape(bk // sub, sub, bq)
            p = jnp.exp2(s3 - m_new[None]).reshape(bk, bq).astype(BF16)
            pv = jnp.dot(vt, p, preferred_element_type=F32)
            acc3 = acc_ref[br].reshape(V_ROWS // sub, sub, bq)
            acc_ref[br] = (alpha[None] * acc3).reshape(V_ROWS, bq) + pv

    m_ref[...] = jnp.full(m_ref.shape, NEG_BIG, F32)
    acc_ref[...] = jnp.zeros(acc_ref.shape, F32)
    scores(0, 0)

    def trip(j, last):
        for u in range(unroll):
            if not (last and u == unroll - 1):
                scores(j + u + 1, (u + 1) % 2)
            accumulate(j + u, u % 2)

    def body(i, carry):
        trip(unroll * i, False)
        return carry

    lax.fori_loop(0, n // unroll - 1, body, 0)
    trip(n - unroll, True)

    lam = lam_ref[...]
    lam_full = (jnp.exp(jnp.sum(lam[0:1] * lam[1:2], keepdims=True))
                - jnp.exp(jnp.sum(lam[2:3] * lam[3:4], keepdims=True)) + lam_init)

    def normalised(a):
        num = a[0:LANES].reshape(LANES // sub, sub, bq)
        return (num / a[LANES:LANES + sub][None]).reshape(LANES, bq)

    o = normalised(acc_ref[0]) - lam_full * normalised(acc_ref[1])
    y = o * lax.rsqrt(jnp.mean(o * o, axis=0, keepdims=True) + EPS) * g_ref[...] * (1.0 - lam_init)
    o_ref[0] = y.T.astype(o_ref.dtype)


def _diff_attention(qt, k, vt, lam, subln_g, batch, seq, layer_idx, bq=512, bk=512, unroll=8):
    c_w = C_HEADS * 2 * C_HALF
    lam_init = 0.8 - 0.6 * math.exp(-0.3 * layer_idx)
    n = seq // bk
    unroll = min(unroll, n)
    assert unroll % 2 == 0 and n % unroll == 0 and vt.shape[2] == n
    kern = functools.partial(_diff_attn_kernel, bq=bq, bk=bk, seq=seq, lam_init=lam_init, unroll=unroll)
    out = pl.pallas_call(
        kern,
        grid=(batch, C_HEADS, seq // bq),
        in_specs=[pl.BlockSpec((1, 1, LANES, bq), lambda b, h, i: (b, h, 0, i)),
                  pl.BlockSpec((1, seq, LANES), lambda b, h, i: (b, 0, h)),
                  pl.BlockSpec((1, 1, n, V_ROWS, bk), lambda b, h, i: (b, h, 0, 0, 0)),
                  pl.BlockSpec((4, C_HALF), lambda b, h, i: (0, 0)),
                  pl.BlockSpec((LANES, bq), lambda b, h, i: (0, 0))],
        out_specs=pl.BlockSpec((1, bq, LANES), lambda b, h, i: (b, i, h)),
        out_shape=jax.ShapeDtypeStruct((batch, seq, c_w), BF16),
        scratch_shapes=[pltpu.VMEM((2, 2, bk, bq), F32),
                        pltpu.VMEM((2, 2, SUBLANES, bq), F32),
                        pltpu.VMEM((2, SUBLANES, bq), F32),
                        pltpu.VMEM((2, V_ROWS, bq), F32)],
        compiler_params=_params(("parallel", "parallel", "arbitrary"), 48),
        name="diff_attn",
    )(qt, k.reshape(batch, seq, c_w), vt, lam, jnp.broadcast_to(subln_g[:, None], (LANES, bq)))
    return out.reshape(batch * seq, c_w)


def _fourier_wprep_kernel(c_ref, s_ref, w_ref, o_ref, *, scale):
    w = w_ref[0]
    mr = jnp.dot(c_ref[...], w, preferred_element_type=F32, precision=lax.Precision.HIGHEST)
    mi = jnp.dot(s_ref[...], w, preferred_element_type=F32, precision=lax.Precision.HIGHEST)
    o_ref[0, :, :LANES] = (mr * scale).astype(BF16)
    o_ref[0, :, LANES:] = (-mi * scale).astype(BF16)


def _fourier_weights(w_f, seq):
    groups, c, _ = w_f.shape
    idx = jnp.arange(c, dtype=jnp.int32)
    ang = (2.0 * math.pi / c) * ((idx[:, None] * idx[None, :]) % c).astype(F32)
    kern = functools.partial(_fourier_wprep_kernel, scale=1.0 / math.sqrt(seq * c))
    return pl.pallas_call(
        kern,
        grid=(groups,),
        in_specs=[pl.BlockSpec((c, c), lambda g: (0, 0)),
                  pl.BlockSpec((c, c), lambda g: (0, 0)),
                  pl.BlockSpec((1, c, c), lambda g: (g, 0, 0))],
        out_specs=pl.BlockSpec((1, c, 2 * c), lambda g: (g, 0, 0)),
        out_shape=jax.ShapeDtypeStruct((groups, c, 2 * c), BF16),
        compiler_params=_params(("parallel",), 32),
        name="fourier_wprep",
    )(jnp.cos(ang), jnp.sin(ang), w_f)


def _chan_mix_kernel(z_ref, m_ref, o_ref):
    o_ref[...] = jnp.dot(z_ref[...], m_ref[0], preferred_element_type=F32).astype(BF16)


def _fourier_channel_mix(proj, m_c, z_col, tm=2048):
    t = proj.shape[0]
    tm = min(tm, t)
    groups = m_c.shape[0]
    return pl.pallas_call(
        _chan_mix_kernel,
        grid=(t // tm, groups),
        in_specs=[pl.BlockSpec((tm, LANES), lambda i, g: (i, z_col + g)),
                  pl.BlockSpec((1, LANES, 2 * LANES), lambda i, g: (g, 0, 0))],
        out_specs=pl.BlockSpec((tm, 2 * LANES), lambda i, g: (i, g)),
        out_shape=jax.ShapeDtypeStruct((t, groups * 2 * LANES), BF16),
        compiler_params=_params(("parallel", "parallel"), 32),
        name="fourier_channel_mix",
    )(proj, m_c)


def _dft_stage1_kernel(u_ref, fr_ref, fi_ref, y_ref, *, pairs):
    u = u_ref[0]
    p = jnp.dot(fr_ref[...], u, preferred_element_type=F32)
    q = jnp.dot(fi_ref[...], u, preferred_element_type=F32)
    for j in range(pairs):
        re = slice(2 * j * LANES, (2 * j + 1) * LANES)
        im = slice((2 * j + 1) * LANES, (2 * j + 2) * LANES)
        y_ref[0, :, re] = (p[:, re] - q[:, im]).astype(BF16)
        y_ref[0, :, im] = (q[:, re] + p[:, im]).astype(BF16)


def _dft_stage2_kernel(y_ref, gc_ref, gs_ref, o_ref, *, groups):
    for t in range(y_ref.shape[1]):
        y = y_ref[0, t]
        a = jnp.dot(gc_ref[t], y, preferred_element_type=F32)
        b = jnp.dot(gs_ref[t], y, preferred_element_type=F32)
        for g in range(groups):
            re = slice(2 * g * LANES, (2 * g + 1) * LANES)
            im = slice((2 * g + 1) * LANES, (2 * g + 2) * LANES)
            out_cols = slice((t * groups + g) * LANES, (t * groups + g + 1) * LANES)
            o_ref[0, :, out_cols] = (a[:, re] + b[:, im]).astype(o_ref.dtype)


def _seq_dft_real(u, batch, seq, groups, tc=4096, kb=4):
    n1 = 128 if seq >= 16384 else 64
    n2 = seq // n1
    wc = groups * 2 * LANES
    i1 = jnp.arange(n1, dtype=jnp.int32)
    ang1 = (2.0 * math.pi / n1) * ((i1[:, None] * i1[None, :]) % n1).astype(F32)
    fr = jnp.cos(ang1).astype(BF16)
    fi = (-jnp.sin(ang1)).astype(BF16)
    cols = n2 * wc
    tc = min(tc, cols)
    y = pl.pallas_call(
        functools.partial(_dft_stage1_kernel, pairs=tc // (2 * LANES)),
        grid=(batch, cols // tc),
        in_specs=[pl.BlockSpec((1, n1, tc), lambda b, j: (b, 0, j)),
                  pl.BlockSpec((n1, n1), lambda b, j: (0, 0)),
                  pl.BlockSpec((n1, n1), lambda b, j: (0, 0))],
        out_specs=pl.BlockSpec((1, n1, tc), lambda b, j: (b, 0, j)),
        out_shape=jax.ShapeDtypeStruct((batch, n1, cols), BF16),
        compiler_params=_params(("parallel", "parallel"), 32),
        name="dft_stage1",
    )(u.reshape(batch, n1, cols), fr, fi)
    i2 = jnp.arange(n2, dtype=jnp.int32)
    tw = (i2[None, None, :] * (i1[:, None, None] + n1 * i2[None, :, None])) % seq
    ang2 = (2.0 * math.pi / seq) * tw.astype(F32)
    gc = jnp.cos(ang2).astype(BF16)
    gs = jnp.sin(ang2).astype(BF16)
    out = pl.pallas_call(
        functools.partial(_dft_stage2_kernel, groups=groups),
        grid=(batch, n1 // kb),
        in_specs=[pl.BlockSpec((1, kb, n2, wc), lambda b, k: (b, k, 0, 0)),
                  pl.BlockSpec((kb, n2, n2), lambda b, k: (k, 0, 0)),
                  pl.BlockSpec((kb, n2, n2), lambda b, k: (k, 0, 0))],
        out_specs=pl.BlockSpec((1, n2, kb * groups * LANES), lambda b, k: (b, 0, k)),
        out_shape=jax.ShapeDtypeStruct((batch, n2, n1 * groups * LANES), BF16),
        compiler_params=_params(("parallel", "parallel"), 32),
        name="dft_stage2",
    )(y.reshape(batch, n1, n2, wc), gc, gs)
    return out.reshape(batch * seq, groups * LANES)


def _mix_out_cross_attn_kernel(mix_ref, x_ref, wout_ref, g1_ref, g2_ref, wq_ref, kv_ref, wo_ref, g3_ref, o_ref,
                               *, heads, scale, halves):
    ca_w = heads * HEAD_DIM
    hr = mix_ref.shape[0] // halves
    rows = [slice(r * hr, (r + 1) * hr) for r in range(halves)]
    ys = [jnp.dot(mix_ref[r, :], wout_ref[...], preferred_element_type=F32) for r in rows]
    x1s = [x_ref[r, :] + _rms(y, g1_ref[...]) for r, y in zip(rows, ys)]
    qs = [jnp.dot(_rms(x1, g2_ref[...]).astype(BF16), wq_ref[...], preferred_element_type=F32).astype(BF16)
          for x1 in x1s]
    cas = []
    for q in qs:
        outs = []
        for h in range(heads):
            cols = slice(h * HEAD_DIM, (h + 1) * HEAD_DIM)
            s = _dot_nt(q[:, cols], kv_ref[0, :, cols]) * scale
            m = jnp.max(s, axis=-1, keepdims=True)
            p = jnp.exp(s - m)
            l = jnp.sum(p, axis=-1, keepdims=True)
            v = kv_ref[0, :, ca_w + h * HEAD_DIM:ca_w + (h + 1) * HEAD_DIM]
            outs.append((jnp.dot(p.astype(BF16), v, preferred_element_type=F32) / l).astype(BF16))
        cas.append(jnp.concatenate(outs, axis=1))
    zs = [jnp.dot(ca, wo_ref[...], preferred_element_type=F32) for ca in cas]
    for r, x1, z in zip(rows, x1s, zs):
        o_ref[r, :] = x1 + _rms(z, g3_ref[...])


def _mix_out_cross_attn(mix, x, w_out, w_q, w_o, layer, g1, g2, g3, kv, batch, seq, tm=512):
    t, d = x.shape
    ca_w = CA_HEADS * HEAD_DIM
    mem_len = kv.shape[0] // batch
    spb = seq // tm
    kern = functools.partial(_mix_out_cross_attn_kernel, heads=CA_HEADS, scale=HEAD_DIM ** -0.5, halves=2)
    once = pl.Buffered(1)
    vec = lambda: pl.BlockSpec((1, d), lambda i: (0, 0))
    return pl.pallas_call(
        kern,
        grid=(t // tm,),
        in_specs=[pl.BlockSpec((tm, d), lambda i: (i, 0)),
                  pl.BlockSpec((tm, d), lambda i: (i, 0)),
                  pl.BlockSpec((None, d, d), lambda i: (layer, 0, 0), pipeline_mode=once),
                  vec(), vec(),
                  pl.BlockSpec((None, d, ca_w), lambda i: (layer, 0, 0), pipeline_mode=once),
                  pl.BlockSpec((1, mem_len, 2 * ca_w), lambda i: (i // spb, 0, 0)),
                  pl.BlockSpec((None, ca_w, d), lambda i: (layer, 0, 0), pipeline_mode=once),
                  vec()],
        out_specs=pl.BlockSpec((tm, d), lambda i: (i, 0)),
        out_shape=jax.ShapeDtypeStruct((t, d), F32),
        compiler_params=_params(("parallel",), 56),
        name="mix_out_cross_attn",
    )(mix, x, w_out, g1.reshape(1, d), g2.reshape(1, d), w_q, kv.reshape(batch, mem_len, 2 * ca_w), w_o,
      g3.reshape(1, d))


HALO = 16


def _ffn_kernel(x_ref, xp_ref, xn_ref, g4_ref, wg_ref, wv_ref, cwg_ref, cwv_ref, cbg_ref, cbv_ref,
                wd_ref, g5_ref, o_ref, xe_ref, zg_ref, zv_ref, *, tm, tc, tiles_per_seq, nf):
    acc_ref = o_ref
    i = pl.program_id(0)
    f = pl.program_id(1)

    @pl.when(f == 0)
    def _():
        g4 = g4_ref[...]
        pos = i % tiles_per_seq
        prev = _rms(xp_ref[...], g4) * jnp.where(pos == 0, 0.0, 1.0)
        nxt = _rms(xn_ref[...], g4) * jnp.where(pos == tiles_per_seq - 1, 0.0, 1.0)
        xe_ref[0:HALO, :] = prev.astype(BF16)
        xe_ref[HALO:HALO + tm, :] = _rms(x_ref[...], g4).astype(BF16)
        xe_ref[HALO + tm:, :] = nxt.astype(BF16)
        acc_ref[...] = jnp.zeros(acc_ref.shape, F32)

    def conv(z_ref, c, cw, cb, lo, rows):
        lo = HALO + lo
        return (z_ref[c, lo - 1:lo - 1 + rows, :] * cw[0:1] + z_ref[c, lo:lo + rows, :] * cw[1:2]
                + z_ref[c, lo + 1:lo + 1 + rows, :] * cw[2:3] + cb)

    xe = xe_ref[...]
    n_sub = zg_ref.shape[0]
    for c in range(n_sub):
        cols = slice(c * tc, (c + 1) * tc)
        zg_ref[c] = jnp.dot(xe, wg_ref[:, cols], preferred_element_type=F32)
        zv_ref[c] = jnp.dot(xe, wv_ref[:, cols], preferred_element_type=F32)
    for c in range(n_sub):
        cols = slice(c * tc, (c + 1) * tc)
        pieces = 2 if c == n_sub - 1 else 1
        rows = tm // pieces
        for r in range(pieces):
            lo = r * rows
            h = (jax.nn.gelu(conv(zg_ref, c, cwg_ref[:, cols], cbg_ref[:, cols], lo, rows))
                 * conv(zv_ref, c, cwv_ref[:, cols], cbv_ref[:, cols], lo, rows))
            acc_ref[lo:lo + rows, :] += jnp.dot(h.astype(BF16), wd_ref[cols, :], preferred_element_type=F32)

    @pl.when(f == nf - 1)
    def _():
        o_ref[...] = x_ref[...] + _rms(acc_ref[...], g5_ref[...])


def _conv_ffn(x, seq, g4, w_up, conv_w, conv_b, w_down, layer, g5, tm=1024, tf=512, tc=256):
    t, d = x.shape
    d_ff = w_down.shape[1]
    nf = d_ff // tf
    hb = tm // HALO
    last_hb = t // HALO - 1
    kern = functools.partial(_ffn_kernel, tm=tm, tc=tc, tiles_per_seq=seq // tm, nf=nf)
    cb = conv_b.reshape(1, 2 * d_ff)
    return pl.pallas_call(
        kern,
        grid=(t // tm, nf),
        in_specs=[pl.BlockSpec((tm, d), lambda i, f: (i, 0), pipeline_mode=pl.Buffered(1)),
                  pl.BlockSpec((HALO, d), lambda i, f: (jnp.maximum(i * hb - 1, 0), 0)),
                  pl.BlockSpec((HALO, d), lambda i, f: (jnp.minimum((i + 1) * hb, last_hb), 0)),
                  pl.BlockSpec((1, d), lambda i, f: (0, 0)),
                  pl.BlockSpec((None, d, tf), lambda i, f: (layer, 0, f)),
                  pl.BlockSpec((None, d, tf), lambda i, f: (layer, 0, nf + f)),
                  pl.BlockSpec((CONV_W, tf), lambda i, f: (0, f)),
                  pl.BlockSpec((CONV_W, tf), lambda i, f: (0, nf + f)),
                  pl.BlockSpec((1, tf), lambda i, f: (0, f)),
                  pl.BlockSpec((1, tf), lambda i, f: (0, nf + f)),
                  pl.BlockSpec((None, tf, d), lambda i, f: (layer, f, 0)),
                  pl.BlockSpec((1, d), lambda i, f: (0, 0))],
        out_specs=pl.BlockSpec((tm, d), lambda i, f: (i, 0)),
        out_shape=jax.ShapeDtypeStruct((t, d), F32),
        scratch_shapes=[pltpu.VMEM((tm + 2 * HALO, d), BF16),
                        pltpu.VMEM((tf // tc, tm + 2 * HALO, tc), F32),
                        pltpu.VMEM((tf // tc, tm + 2 * HALO, tc), F32)],
        compiler_params=_params(("parallel", "arbitrary"), 58),
        name="conv_ffn",
    )(x, x, x, g4.reshape(1, d), w_up, w_up, conv_w, conv_w, cb, cb, w_down, g5.reshape(1, d))


def _trunk(x3, mem3, p):
    batch, seq, d = x3.shape
    x = x3.reshape(batch * seq, d)
    mem = mem3.reshape(-1, d)
    depth = p["w_in"].shape[0]
    a_w = A_HEADS * HEAD_DIM
    c_w = C_HEADS * 2 * C_HALF
    for layer in range(depth):
        g = p["norm_gains"][layer]
        proj = _norm_matmul(x, g[0], p["w_in"], layer, BF16)
        if layer % 2 == 0:
            e = layer // 2
            qkvs = _even_prep(proj, p["rope"][HEAD_DIM], batch, seq, HEAD_DIM ** -0.5)
            seqs = [qkv.reshape(batch * dil, seq // dil, 3 * a_w) for (_, dil), qkv in zip(DILATED_CFGS, qkvs)]
            dilated = [_band_attention(s) for s in seqs[1:]]
            ya = _band_attention(seqs[0], dilated).reshape(batch * seq, a_w)
            yb = _spatial_gating(proj, p["b_w_spatial"][e], p["b_b_spatial"][e], p["b_ln_gain"][e],
                                 3 * A_HEADS, 3 * A_HEADS + B_GROUPS)
            mix = jnp.concatenate([ya, yb], axis=-1)
        else:
            o = layer // 2
            qt, kk, vt = _diff_prep(proj, p["rope"][C_HALF], batch, seq, C_HALF ** -0.5 * math.log2(math.e),
                                    DIFF_BK)
            yc = _diff_attention(qt, kk, vt, p["c_lambda"][o], p["c_subln_gain"][o], batch, seq, layer,
                                 bk=DIFF_BK)
            m_c = _fourier_weights(p["d_w_fourier"][o], seq)
            u = _fourier_channel_mix(proj, m_c, 3 * c_w // LANES)
            yd = _seq_dft_real(u, batch, seq, D_GROUPS)
            mix = jnp.concatenate([yc, yd], axis=-1)
        kv = _norm_matmul(mem, p["mem_norm_gain"][layer], p["ca_w_kv"], layer, BF16)
        x = _mix_out_cross_attn(mix, x, p["w_out"], p["ca_w_q"], p["ca_w_o"], layer, g[1], g[2], g[3], kv,
                                batch, seq)
        x = _conv_ffn(x, seq, g[4], p["ffn_w_up"], p["ffn_conv_w"][layer], p["ffn_conv_b"][layer],
                      p["ffn_w_down"], layer, g[5])
    return x.reshape(batch, seq, d)


def kernel(x_prompt, x_sample, mem_prompt, mem_sample, norm_gains, w_in, w_out, b_w_spatial, b_b_spatial,
           b_ln_gain, c_lambda, c_subln_gain, d_w_fourier, mem_norm_gain, ca_w_q, ca_w_kv, ca_w_o, ffn_w_up,
           ffn_conv_w, ffn_conv_b, ffn_w_down):
    p = dict(norm_gains=norm_gains, w_in=w_in.astype(BF16), w_out=w_out.astype(BF16), b_w_spatial=b_w_spatial,
             b_b_spatial=b_b_spatial, b_ln_gain=b_ln_gain, c_lambda=c_lambda, c_subln_gain=c_subln_gain,
             d_w_fourier=d_w_fourier, mem_norm_gain=mem_norm_gain, ca_w_q=ca_w_q.astype(BF16),
             ca_w_kv=ca_w_kv.astype(BF16), ca_w_o=ca_w_o.astype(BF16), ffn_w_up=ffn_w_up.astype(BF16),
             ffn_conv_w=ffn_conv_w, ffn_conv_b=ffn_conv_b, ffn_w_down=ffn_w_down.astype(BF16))
    seq_max = max(x_prompt.shape[1], x_sample.shape[1])
    p["rope"] = {w: _rope_tables(seq_max, w) for w in (HEAD_DIM, C_HALF)}
    return _trunk(x_prompt, mem_prompt, p), _trunk(x_sample, mem_sample, p)
```

```python
import functools
import math

import jax
import jax.numpy as jnp
from jax import lax
from jax.experimental import pallas as pl
from jax.experimental.pallas import tpu as pltpu

F32 = jnp.float32
BF16 = jnp.bfloat16

EPS = 1e-6
ROPE_THETA = 500000.0
ROPE_FRACTION = 4
LANES = 128
SUBLANES = 8
HEAD_DIM = 128
A_HEADS = 8
DILATED_CFGS = ((128, 1), (512, 4), (2048, 16))
BAND_RADIUS = 64
assert all(window // (2 * dil) == BAND_RADIUS for window, dil in DILATED_CFGS)
B_GROUPS = 8
CHUNK = 128
C_HEADS = 12
C_HALF = 64
D_GROUPS = 4
CA_HEADS = 4
CONV_W = 3
NEG_BIG = -1e30
MIB = 1024 * 1024


def _params(semantics, vmem_mib):
    return pltpu.CompilerParams(dimension_semantics=semantics, vmem_limit_bytes=vmem_mib * MIB)


def _rms(x, g):
    return x * lax.rsqrt(jnp.mean(x * x, axis=-1, keepdims=True) + EPS) * g


def _dot_nt(a, b):
    return lax.dot_general(a, b, (((1,), (1,)), ((), ())), preferred_element_type=F32)


def _norm_mm_kernel(x_ref, g_ref, w_ref, o_ref, xn_ref):
    j = pl.program_id(1)

    @pl.when(j == 0)
    def _():
        half = x_ref.shape[0] // 2
        for r in (slice(0, half), slice(half, 2 * half)):
            xn_ref[r, :] = _rms(x_ref[r, :], g_ref[...]).astype(BF16)
            o_ref[r, :] = jnp.dot(xn_ref[r, :], w_ref[...], preferred_element_type=F32).astype(o_ref.dtype)

    @pl.when(j > 0)
    def _():
        o_ref[...] = jnp.dot(xn_ref[...], w_ref[...], preferred_element_type=F32).astype(o_ref.dtype)


def _norm_matmul(x, g, w, layer, out_dtype, tm=1024, tn=1024):
    t, k = x.shape
    n = w.shape[2]
    tm = min(tm, t)
    return pl.pallas_call(
        _norm_mm_kernel,
        grid=(t // tm, n // tn),
        in_specs=[pl.BlockSpec((tm, k), lambda i, j: (i, 0)),
                  pl.BlockSpec((1, k), lambda i, j: (0, 0)),
                  pl.BlockSpec((None, k, tn), lambda i, j: (layer, 0, j))],
        out_specs=pl.BlockSpec((tm, tn), lambda i, j: (i, j)),
        out_shape=jax.ShapeDtypeStruct((t, n), out_dtype),
        scratch_shapes=[pltpu.VMEM((tm, k), BF16)],
        compiler_params=_params(("parallel", "arbitrary"), 40),
        name="norm_matmul",
    )(x, g.reshape(1, k), w)


def _rope_tables(seq, head_w):
    rd = head_w // ROPE_FRACTION
    half = rd // 2
    inv = ROPE_THETA ** (-jnp.arange(half, dtype=F32) / half)
    ang = jnp.arange(seq, dtype=F32)[:, None] * inv[None, :]
    reps = LANES // half
    cos = jnp.broadcast_to(jnp.cos(ang)[:, None, :], (seq, reps, half)).reshape(seq, LANES)
    sin = jnp.broadcast_to(jnp.sin(ang)[:, None, :], (seq, reps, half)).reshape(seq, LANES)
    lane = (jnp.arange(LANES, dtype=jnp.int32) & (head_w - 1))[None, :]
    return jnp.where(lane < rd, cos, 1.0), jnp.where(lane < rd, jnp.where(lane < half, -sin, sin), 0.0)


def _rope_block(x, c, s, head_w):
    half = head_w // ROPE_FRACTION // 2
    first = (lax.broadcasted_iota(jnp.int32, x.shape, 1) & (head_w - 1)) < half
    partner = jnp.where(first, pltpu.roll(x, LANES - half, axis=1), pltpu.roll(x, half, axis=1))
    return x * c + partner * s


def _even_prep_kernel(x_ref, c_ref, s_ref, *refs, q_scale, n_blocks):
    out_refs, y_ref = refs[:-1], refs[-1]
    part = pl.program_id(1)
    tm = x_ref.shape[0]

    def rope(scale):
        c = c_ref[...]
        s = s_ref[...]
        for b in range(n_blocks):
            cols = slice(b * LANES, (b + 1) * LANES)
            y_ref[b] = _rope_block(x_ref[:, cols].astype(F32), c, s, HEAD_DIM) * scale

    @pl.when(part == 0)
    def _():
        rope(q_scale)

    @pl.when(part == 1)
    def _():
        rope(1.0)

    @pl.when(part == 2)
    def _():
        for b in range(n_blocks):
            y_ref[b] = x_ref[:, b * LANES:(b + 1) * LANES].astype(F32)

    for (_, dil), o_ref in zip(DILATED_CFGS, out_refs):
        for r in range(dil):
            for b in range(n_blocks):
                rows = y_ref[b, pl.ds(r, tm // dil, stride=dil), :]
                o_ref[0, r, :, b * LANES:(b + 1) * LANES] = rows.astype(BF16)


def _even_prep(proj, tables, batch, seq, q_scale, tm=1024):
    a_w = A_HEADS * HEAD_DIM
    cos_t, sin_t = tables
    spb = seq // tm
    kern = functools.partial(_even_prep_kernel, q_scale=q_scale, n_blocks=a_w // LANES)
    return pl.pallas_call(
        kern,
        grid=(batch * spb, 3),
        in_specs=[pl.BlockSpec((tm, a_w), lambda i, p: (i, p)),
                  pl.BlockSpec((tm, LANES), lambda i, p: (i % spb, 0)),
                  pl.BlockSpec((tm, LANES), lambda i, p: (i % spb, 0))],
        out_specs=[pl.BlockSpec((1, dil, tm // dil, a_w), lambda i, p: (i // spb, 0, i % spb, p))
                   for _, dil in DILATED_CFGS],
        out_shape=[jax.ShapeDtypeStruct((batch, dil, seq // dil, 3 * a_w), BF16) for _, dil in DILATED_CFGS],
        scratch_shapes=[pltpu.VMEM((a_w // LANES, tm, LANES), F32)],
        compiler_params=_params(("parallel", "arbitrary"), 32),
        name="even_prep",
    )(proj, cos_t, sin_t)


def _band_attn_kernel(q_ref, kp_ref, kc_ref, kn_ref, vp_ref, vc_ref, vn_ref, *refs, bq, sub_len, merge):
    if merge:
        o4, l4, o16, l16, y_ref, so4, sl4, so16, sl16 = refs
        for dil, src, dst in ((4, o4, so4), (4, l4, sl4), (16, o16, so16), (16, l16, sl16)):
            for r in range(dil):
                for h in range(A_HEADS):
                    dst[h, pl.ds(r, bq // dil, stride=dil), :] = (
                        src[0, r, :, h * HEAD_DIM:(h + 1) * HEAD_DIM].astype(F32))
    else:
        o_ref, lse_ref = refs
    q0 = pl.program_id(1) * bq
    sq = 2 * BAND_RADIUS
    win = sq + 2 * BAND_RADIUS
    n_sub = bq // sq
    rel = (lax.broadcasted_iota(jnp.int32, (sq, win), 1) - BAND_RADIUS
           - lax.broadcasted_iota(jnp.int32, (sq, win), 0))
    band = (rel <= BAND_RADIUS) & (rel >= -BAND_RADIUS)
    biases = []
    for u in range(n_sub):
        kpos = q0 + u * sq - BAND_RADIUS + lax.broadcasted_iota(jnp.int32, (sq, win), 1)
        ok = band & (kpos >= 0) & (kpos < sub_len)
        biases.append(jnp.where(ok, 0.0, NEG_BIG))

    def window(p_ref, c_ref, n_ref, cols, u):
        parts = []
        if u == 0:
            parts.append(p_ref[0, :, cols])
        lo = max(u * sq - BAND_RADIUS, 0)
        hi = min(u * sq + sq + BAND_RADIUS, bq)
        parts.append(c_ref[0, lo:hi, cols])
        if u == n_sub - 1:
            parts.append(n_ref[0, :, cols])
        return jnp.concatenate(parts, axis=0)

    tiles = [(h, u) for h in range(A_HEADS) for u in range(n_sub)]

    def scores(h, u):
        cols = slice(h * HEAD_DIM, (h + 1) * HEAD_DIM)
        q = q_ref[0, u * sq:(u + 1) * sq, cols]
        return _dot_nt(q, window(kp_ref, kc_ref, kn_ref, cols, u)) + biases[u]

    s_next = scores(*tiles[0])
    for n, (h, u) in enumerate(tiles):
        cols = slice(h * HEAD_DIM, (h + 1) * HEAD_DIM)
        rows = slice(u * sq, (u + 1) * sq)
        s = s_next
        if n + 1 < len(tiles):
            s_next = scores(*tiles[n + 1])
        m = jnp.max(s, axis=-1, keepdims=True)
        p = jnp.exp(s - m)
        l = jnp.sum(p, axis=-1, keepdims=True)
        o = jnp.dot(p.astype(BF16), window(vp_ref, vc_ref, vn_ref, cols, u), preferred_element_type=F32) / l
        lse = jnp.broadcast_to(m + jnp.log(l), (sq, HEAD_DIM))
        if merge:
            b, c = sl4[h, rows, :], sl16[h, rows, :]
            top = jnp.maximum(jnp.maximum(lse, b), c)
            wa, wb, wc = jnp.exp(lse - top), jnp.exp(b - top), jnp.exp(c - top)
            num = wa * o + wb * so4[h, rows, :] + wc * so16[h, rows, :]
            y_ref[0, rows, cols] = (num / (wa + wb + wc)).astype(y_ref.dtype)
        else:
            o_ref[0, rows, cols] = o.astype(o_ref.dtype)
            lse_ref[0, rows, cols] = lse


def _band_attention(qkv, dilated=None, bq=256):
    n_seq, sub_len, _ = qkv.shape
    a_w = A_HEADS * HEAD_DIM
    hb = bq // BAND_RADIUS
    last = sub_len // BAND_RADIUS - 1
    merge = dilated is not None
    kern = functools.partial(_band_attn_kernel, bq=bq, sub_len=sub_len, merge=merge)

    def halo_specs(col):
        return [pl.BlockSpec((1, BAND_RADIUS, a_w), lambda s, i: (s, jnp.maximum(i * hb - 1, 0), col)),
                pl.BlockSpec((1, bq, a_w), lambda s, i: (s, i, col)),
                pl.BlockSpec((1, BAND_RADIUS, a_w), lambda s, i: (s, jnp.minimum((i + 1) * hb, last), col))]

    out_spec = pl.BlockSpec((1, bq, a_w), lambda s, i: (s, i, 0))
    in_specs = [pl.BlockSpec((1, bq, a_w), lambda s, i: (s, i, 0))] + halo_specs(1) + halo_specs(2)
    operands = [qkv] * 7
    if merge:
        for (_, dil), pair in zip(DILATED_CFGS[1:], dilated):
            for arr in pair:
                operands.append(arr.reshape(n_seq, dil, sub_len // dil, a_w))
                in_specs.append(pl.BlockSpec((1, dil, bq // dil, a_w), lambda s, i: (s, 0, i, 0)))
        out_specs, out_shape = out_spec, jax.ShapeDtypeStruct((n_seq, sub_len, a_w), BF16)
        scratch = [pltpu.VMEM((A_HEADS, bq, HEAD_DIM), F32)] * 4
    else:
        out_specs = [out_spec, out_spec]
        out_shape = [jax.ShapeDtypeStruct((n_seq, sub_len, a_w), BF16),
                     jax.ShapeDtypeStruct((n_seq, sub_len, a_w), F32)]
        scratch = []
    return pl.pallas_call(
        kern,
        grid=(n_seq, sub_len // bq),
        in_specs=in_specs,
        out_specs=out_specs,
        out_shape=out_shape,
        scratch_shapes=scratch,
        compiler_params=_params(("parallel", "parallel"), 40),
        name="band_attn_merge" if merge else "band_attn",
    )(*operands)


def _gating_kernel(u_ref, v_ref, w_ref, b_ref, g_ref, o_ref, *, n_chunks):
    w = w_ref[0]
    bias = b_ref[0]
    gain = g_ref[0]
    for c in range(n_chunks):
        rows = pl.ds(c * CHUNK, CHUNK)
        v = jax.nn.gelu(v_ref[rows, :].astype(F32))
        mu = jnp.mean(v, axis=-1, keepdims=True)
        d = v - mu
        var = jnp.mean(d * d, axis=-1, keepdims=True)
        vn = d * lax.rsqrt(var + EPS) * gain
        sv = jnp.dot(w, vn.astype(BF16), preferred_element_type=F32) + bias
        o_ref[rows, :] = (jax.nn.gelu(u_ref[rows, :].astype(F32)) * sv).astype(o_ref.dtype)


def _spatial_gating(proj, w_s, b_s, g_ln, u_col, v_col, tm=2048):
    t = proj.shape[0]
    tm = min(tm, t)
    groups = w_s.shape[0]
    bias = jnp.broadcast_to(b_s[:, :, None], (groups, CHUNK, LANES))
    kern = functools.partial(_gating_kernel, n_chunks=tm // CHUNK)
    return pl.pallas_call(
        kern,
        grid=(t // tm, groups),
        in_specs=[pl.BlockSpec((tm, LANES), lambda i, g: (i, u_col + g)),
                  pl.BlockSpec((tm, LANES), lambda i, g: (i, v_col + g)),
                  pl.BlockSpec((1, CHUNK, CHUNK), lambda i, g: (g, 0, 0)),
                  pl.BlockSpec((1, CHUNK, LANES), lambda i, g: (g, 0, 0)),
                  pl.BlockSpec((1, 1, LANES), lambda i, g: (g, 0, 0))],
        out_specs=pl.BlockSpec((tm, LANES), lambda i, g: (i, g)),
        out_shape=jax.ShapeDtypeStruct((t, groups * LANES), BF16),
        compiler_params=_params(("parallel", "parallel"), 32),
        name="spatial_gating",
    )(proj, proj, w_s.astype(BF16), bias, g_ln.reshape(groups, 1, LANES))


V_ROWS = 2 * C_HALF + 16
DIFF_BK = 512


def _diff_prep_kernel(x_ref, c_ref, s_ref, q_ref, k_ref, vt_ref, *, q_scale, n_blocks):
    part = pl.program_id(1)
    tm = x_ref.shape[0]

    def rope(o_ref, scale):
        c = c_ref[...]
        s = s_ref[...]
        for h in range(n_blocks):
            cols = slice(h * LANES, (h + 1) * LANES)
            o_ref[:, cols] = (_rope_block(x_ref[:, cols].astype(F32), c, s, C_HALF) * scale).astype(BF16)

    @pl.when(part == 0)
    def _():
        rope(q_ref, q_scale)

    @pl.when(part == 1)
    def _():
        rope(k_ref, 1.0)

    @pl.when(part == 2)
    def _():
        ones = jnp.ones((V_ROWS - LANES, tm), BF16)
        for h in range(n_blocks):
            vt_ref[0, h, 0, 0:LANES, :] = x_ref[:, h * LANES:(h + 1) * LANES].astype(F32).T.astype(BF16)
            vt_ref[0, h, 0, LANES:, :] = ones


def _diff_prep(proj, tables, batch, seq, q_scale, bk, tm=512):
    c_w = C_HEADS * 2 * C_HALF
    cos_t, sin_t = tables
    spb = seq // tm
    per_chunk = bk // tm
    kern = functools.partial(_diff_prep_kernel, q_scale=q_scale, n_blocks=C_HEADS)
    return pl.pallas_call(
        kern,
        grid=(batch * spb, 3),
        in_specs=[pl.BlockSpec((tm, c_w), lambda i, p: (i, p)),
                  pl.BlockSpec((tm, LANES), lambda i, p: (i % spb, 0)),
                  pl.BlockSpec((tm, LANES), lambda i, p: (i % spb, 0))],
        out_specs=[pl.BlockSpec((tm, c_w), lambda i, p: (i, 0)),
                   pl.BlockSpec((tm, c_w), lambda i, p: (i, 0)),
                   pl.BlockSpec((1, C_HEADS, 1, V_ROWS, tm),
                                lambda i, p: (i // spb, 0, (i % spb) // per_chunk, 0, (i % spb) % per_chunk))],
        out_shape=[jax.ShapeDtypeStruct((batch * seq, c_w), BF16),
                   jax.ShapeDtypeStruct((batch * seq, c_w), BF16),
                   jax.ShapeDtypeStruct((batch, C_HEADS, seq // bk, V_ROWS, bk), BF16)],
        compiler_params=_params(("parallel", "arbitrary"), 40),
        name="diff_prep",
    )(proj, cos_t, sin_t)


def _diff_attn_kernel(q_ref, k_ref, vt_ref, lam_ref, g_ref, o_ref, s_ref, mc_ref, m_ref, acc_ref,
                      *, bq, bk, seq, lam_init, unroll):
    n = seq // bk
    sub = SUBLANES
    q = q_ref[0]
    lane = lax.broadcasted_iota(jnp.int32, q.shape, 1)
    zero = jnp.zeros_like(q)
    qs = (jnp.where(lane < C_HALF, q, zero), jnp.where(lane < C_HALF, zero, q))

    def scores(j, slot):
        kc = k_ref[0, pl.ds(pl.multiple_of(j * bk, bk), bk), :]
        for br in range(2):
            s = _dot_nt(kc, qs[br])
            s_ref[slot, br] = s
            mc = jnp.max(s.reshape(bk // sub, sub, bq), axis=0)
            mc_ref[slot, br] = jnp.broadcast_to(jnp.max(mc, axis=0, keepdims=True), (sub, bq))

    def accumulate(j, slot):
        vt = vt_ref[0, 0, j]
        for br in range(2):
            m_old = m_ref[br]
            m_new = jnp.maximum(m_old, mc_ref[slot, br])
            alpha = jnp.exp2(m_old - m_new)
            m_ref[br] = m_new
            s3 = s_ref[slot, br].reshape(bk // sub, sub, bq)
            p = jnp.exp2(s3 - m_new[None]).reshape(bk, bq).astype(BF16)
            pv = jnp.dot(vt, p, preferred_element_type=F32)
            acc3 = acc_ref[br].reshape(V_ROWS // sub, sub, bq)
            acc_ref[br] = (alpha[None] * acc3).reshape(V_ROWS, bq) + pv

    m_ref[...] = jnp.full(m_ref.shape, NEG_BIG, F32)
    acc_ref[...] = jnp.zeros(acc_ref.shape, F32)
    scores(0, 0)

    def trip(j, last):
        for u in range(unroll):
            if not (last and u == unroll - 1):
                scores(j + u + 1, (u + 1) % 2)
            accumulate(j + u, u % 2)

    def body(i, carry):
        trip(unroll * i, False)
        return carry

    lax.fori_loop(0, n // unroll - 1, body, 0)
    trip(n - unroll, True)

    lam = lam_ref[...]
    lam_full = (jnp.exp(jnp.sum(lam[0:1] * lam[1:2], keepdims=True))
                - jnp.exp(jnp.sum(lam[2:3] * lam[3:4], keepdims=True)) + lam_init)

    def normalised(a):
        num = a[0:LANES].reshape(LANES // sub, sub, bq)
        return (num / a[LANES:LANES + sub][None]).reshape(LANES, bq)

    o = normalised(acc_ref[0]) - lam_full * normalised(acc_ref[1])
    y = o * lax.rsqrt(jnp.mean(o * o, axis=0, keepdims=True) + EPS) * g_ref[...] * (1.0 - lam_init)
    o_ref[0] = y.T.astype(o_ref.dtype)


def _diff_attention(q, k, vt, lam, subln_g, batch, seq, layer_idx, bq=512, bk=512, unroll=8):
    c_w = C_HEADS * 2 * C_HALF
    lam_init = 0.8 - 0.6 * math.exp(-0.3 * layer_idx)
    n = seq // bk
    unroll = min(unroll, n)
    assert unroll % 2 == 0 and n % unroll == 0 and vt.shape[2] == n
    kern = functools.partial(_diff_attn_kernel, bq=bq, bk=bk, seq=seq, lam_init=lam_init, unroll=unroll)
    out = pl.pallas_call(
        kern,
        grid=(batch, C_HEADS, seq // bq),
        in_specs=[pl.BlockSpec((1, bq, LANES), lambda b, h, i: (b, i, h)),
                  pl.BlockSpec((1, seq, LANES), lambda b, h, i: (b, 0, h)),
                  pl.BlockSpec((1, 1, n, V_ROWS, bk), lambda b, h, i: (b, h, 0, 0, 0)),
                  pl.BlockSpec((4, C_HALF), lambda b, h, i: (0, 0)),
                  pl.BlockSpec((LANES, bq), lambda b, h, i: (0, 0))],
        out_specs=pl.BlockSpec((1, bq, LANES), lambda b, h, i: (b, i, h)),
        out_shape=jax.ShapeDtypeStruct((batch, seq, c_w), BF16),
        scratch_shapes=[pltpu.VMEM((2, 2, bk, bq), F32),
                        pltpu.VMEM((2, 2, SUBLANES, bq), F32),
                        pltpu.VMEM((2, SUBLANES, bq), F32),
                        pltpu.VMEM((2, V_ROWS, bq), F32)],
        compiler_params=_params(("parallel", "parallel", "arbitrary"), 48),
        name="diff_attn",
    )(q.reshape(batch, seq, c_w), k.reshape(batch, seq, c_w), vt, lam,
      jnp.broadcast_to(subln_g[:, None], (LANES, bq)))
    return out.reshape(batch * seq, c_w)


def _fourier_wprep_kernel(c_ref, s_ref, w_ref, o_ref, *, scale):
    w = w_ref[0]
    mr = jnp.dot(c_ref[...], w, preferred_element_type=F32, precision=lax.Precision.HIGHEST)
    mi = jnp.dot(s_ref[...], w, preferred_element_type=F32, precision=lax.Precision.HIGHEST)
    o_ref[0, :, :LANES] = (mr * scale).astype(BF16)
    o_ref[0, :, LANES:] = (-mi * scale).astype(BF16)


def _fourier_weights(w_f, seq):
    groups, c, _ = w_f.shape
    idx = jnp.arange(c, dtype=jnp.int32)
    ang = (2.0 * math.pi / c) * ((idx[:, None] * idx[None, :]) % c).astype(F32)
    kern = functools.partial(_fourier_wprep_kernel, scale=1.0 / math.sqrt(seq * c))
    return pl.pallas_call(
        kern,
        grid=(groups,),
        in_specs=[pl.BlockSpec((c, c), lambda g: (0, 0)),
                  pl.BlockSpec((c, c), lambda g: (0, 0)),
                  pl.BlockSpec((1, c, c), lambda g: (g, 0, 0))],
        out_specs=pl.BlockSpec((1, c, 2 * c), lambda g: (g, 0, 0)),
        out_shape=jax.ShapeDtypeStruct((groups, c, 2 * c), BF16),
        compiler_params=_params(("parallel",), 32),
        name="fourier_wprep",
    )(jnp.cos(ang), jnp.sin(ang), w_f)


def _chan_mix_kernel(z_ref, m_ref, o_ref):
    o_ref[...] = jnp.dot(z_ref[...], m_ref[0], preferred_element_type=F32).astype(BF16)


def _fourier_channel_mix(proj, m_c, z_col, tm=2048):
    t = proj.shape[0]
    tm = min(tm, t)
    groups = m_c.shape[0]
    return pl.pallas_call(
        _chan_mix_kernel,
        grid=(t // tm, groups),
        in_specs=[pl.BlockSpec((tm, LANES), lambda i, g: (i, z_col + g)),
                  pl.BlockSpec((1, LANES, 2 * LANES), lambda i, g: (g, 0, 0))],
        out_specs=pl.BlockSpec((tm, 2 * LANES), lambda i, g: (i, g)),
        out_shape=jax.ShapeDtypeStruct((t, groups * 2 * LANES), BF16),
        compiler_params=_params(("parallel", "parallel"), 32),
        name="fourier_channel_mix",
    )(proj, m_c)


def _dft_stage1_kernel(u_ref, fr_ref, fi_ref, y_ref, *, pairs):
    u = u_ref[0]
    p = jnp.dot(fr_ref[...], u, preferred_element_type=F32)
    q = jnp.dot(fi_ref[...], u, preferred_element_type=F32)
    for j in range(pairs):
        re = slice(2 * j * LANES, (2 * j + 1) * LANES)
        im = slice((2 * j + 1) * LANES, (2 * j + 2) * LANES)
        y_ref[0, :, re] = (p[:, re] - q[:, im]).astype(BF16)
        y_ref[0, :, im] = (q[:, re] + p[:, im]).astype(BF16)


def _dft_stage2_kernel(y_ref, gc_ref, gs_ref, o_ref, *, groups):
    for t in range(y_ref.shape[1]):
        y = y_ref[0, t]
        a = jnp.dot(gc_ref[t], y, preferred_element_type=F32)
        b = jnp.dot(gs_ref[t], y, preferred_element_type=F32)
        for g in range(groups):
            re = slice(2 * g * LANES, (2 * g + 1) * LANES)
            im = slice((2 * g + 1) * LANES, (2 * g + 2) * LANES)
            out_cols = slice((t * groups + g) * LANES, (t * groups + g + 1) * LANES)
            o_ref[0, :, out_cols] = (a[:, re] + b[:, im]).astype(o_ref.dtype)


def _seq_dft_real(u, batch, seq, groups, tc=4096, kb=4):
    n1 = 128 if seq >= 16384 else 64
    n2 = seq // n1
    wc = groups * 2 * LANES
    i1 = jnp.arange(n1, dtype=jnp.int32)
    ang1 = (2.0 * math.pi / n1) * ((i1[:, None] * i1[None, :]) % n1).astype(F32)
    fr = jnp.cos(ang1).astype(BF16)
    fi = (-jnp.sin(ang1)).astype(BF16)
    cols = n2 * wc
    tc = min(tc, cols)
    y = pl.pallas_call(
        functools.partial(_dft_stage1_kernel, pairs=tc // (2 * LANES)),
        grid=(batch, cols // tc),
        in_specs=[pl.BlockSpec((1, n1, tc), lambda b, j: (b, 0, j)),
                  pl.BlockSpec((n1, n1), lambda b, j: (0, 0)),
                  pl.BlockSpec((n1, n1), lambda b, j: (0, 0))],
        out_specs=pl.BlockSpec((1, n1, tc), lambda b, j: (b, 0, j)),
        out_shape=jax.ShapeDtypeStruct((batch, n1, cols), BF16),
        compiler_params=_params(("parallel", "parallel"), 32),
        name="dft_stage1",
    )(u.reshape(batch, n1, cols), fr, fi)
    i2 = jnp.arange(n2, dtype=jnp.int32)
    tw = (i2[None, None, :] * (i1[:, None, None] + n1 * i2[None, :, None])) % seq
    ang2 = (2.0 * math.pi / seq) * tw.astype(F32)
    gc = jnp.cos(ang2).astype(BF16)
    gs = jnp.sin(ang2).astype(BF16)
    out = pl.pallas_call(
        functools.partial(_dft_stage2_kernel, groups=groups),
        grid=(batch, n1 // kb),
        in_specs=[pl.BlockSpec((1, kb, n2, wc), lambda b, k: (b, k, 0, 0)),
                  pl.BlockSpec((kb, n2, n2), lambda b, k: (k, 0, 0)),
                  pl.BlockSpec((kb, n2, n2), lambda b, k: (k, 0, 0))],
        out_specs=pl.BlockSpec((1, n2, kb * groups * LANES), lambda b, k: (b, 0, k)),
        out_shape=jax.ShapeDtypeStruct((batch, n2, n1 * groups * LANES), BF16),
        compiler_params=_params(("parallel", "parallel"), 32),
        name="dft_stage2",
    )(y.reshape(batch, n1, n2, wc), gc, gs)
    return out.reshape(batch * seq, groups * LANES)


def _mix_out_cross_attn_kernel(mix_ref, x_ref, wout_ref, g1_ref, g2_ref, wq_ref, kv_ref, wo_ref, g3_ref, o_ref,
                               *, heads, scale, halves):
    ca_w = heads * HEAD_DIM
    hr = mix_ref.shape[0] // halves
    rows = [slice(r * hr, (r + 1) * hr) for r in range(halves)]
    ys = [jnp.dot(mix_ref[r, :], wout_ref[...], preferred_element_type=F32) for r in rows]
    x1s = [x_ref[r, :] + _rms(y, g1_ref[...]) for r, y in zip(rows, ys)]
    qs = [jnp.dot(_rms(x1, g2_ref[...]).astype(BF16), wq_ref[...], preferred_element_type=F32).astype(BF16)
          for x1 in x1s]
    cas = []
    for q in qs:
        outs = []
        for h in range(heads):
            cols = slice(h * HEAD_DIM, (h + 1) * HEAD_DIM)
            s = _dot_nt(q[:, cols], kv_ref[0, :, cols]) * scale
            m = jnp.max(s, axis=-1, keepdims=True)
            p = jnp.exp(s - m)
            l = jnp.sum(p, axis=-1, keepdims=True)
            v = kv_ref[0, :, ca_w + h * HEAD_DIM:ca_w + (h + 1) * HEAD_DIM]
            outs.append((jnp.dot(p.astype(BF16), v, preferred_element_type=F32) / l).astype(BF16))
        cas.append(jnp.concatenate(outs, axis=1))
    zs = [jnp.dot(ca, wo_ref[...], preferred_element_type=F32) for ca in cas]
    for r, x1, z in zip(rows, x1s, zs):
        o_ref[r, :] = x1 + _rms(z, g3_ref[...])


def _mix_out_cross_attn(mix, x, w_out, w_q, w_o, layer, g1, g2, g3, kv, batch, seq, tm=512):
    t, d = x.shape
    ca_w = CA_HEADS * HEAD_DIM
    mem_len = kv.shape[0] // batch
    spb = seq // tm
    kern = functools.partial(_mix_out_cross_attn_kernel, heads=CA_HEADS, scale=HEAD_DIM ** -0.5, halves=2)
    once = pl.Buffered(1)
    vec = lambda: pl.BlockSpec((1, d), lambda i: (0, 0))
    return pl.pallas_call(
        kern,
        grid=(t // tm,),
        in_specs=[pl.BlockSpec((tm, d), lambda i: (i, 0)),
                  pl.BlockSpec((tm, d), lambda i: (i, 0)),
                  pl.BlockSpec((None, d, d), lambda i: (layer, 0, 0), pipeline_mode=once),
                  vec(), vec(),
                  pl.BlockSpec((None, d, ca_w), lambda i: (layer, 0, 0), pipeline_mode=once),
                  pl.BlockSpec((1, mem_len, 2 * ca_w), lambda i: (i // spb, 0, 0)),
                  pl.BlockSpec((None, ca_w, d), lambda i: (layer, 0, 0), pipeline_mode=once),
                  vec()],
        out_specs=pl.BlockSpec((tm, d), lambda i: (i, 0)),
        out_shape=jax.ShapeDtypeStruct((t, d), F32),
        compiler_params=_params(("parallel",), 56),
        name="mix_out_cross_attn",
    )(mix, x, w_out, g1.reshape(1, d), g2.reshape(1, d), w_q, kv.reshape(batch, mem_len, 2 * ca_w), w_o,
      g3.reshape(1, d))


HALO = 16


def _ffn_kernel(x_ref, xp_ref, xn_ref, g4_ref, wg_ref, wv_ref, cwg_ref, cwv_ref, cbg_ref, cbv_ref,
                wd_ref, g5_ref, o_ref, xe_ref, zg_ref, zv_ref, *, tm, tc, tiles_per_seq, nf):
    acc_ref = o_ref
    i = pl.program_id(0)
    f = pl.program_id(1)

    @pl.when(f == 0)
    def _():
        g4 = g4_ref[...]
        pos = i % tiles_per_seq
        prev = _rms(xp_ref[...], g4) * jnp.where(pos == 0, 0.0, 1.0)
        nxt = _rms(xn_ref[...], g4) * jnp.where(pos == tiles_per_seq - 1, 0.0, 1.0)
        xe_ref[0:HALO, :] = prev.astype(BF16)
        xe_ref[HALO:HALO + tm, :] = _rms(x_ref[...], g4).astype(BF16)
        xe_ref[HALO + tm:, :] = nxt.astype(BF16)
        acc_ref[...] = jnp.zeros(acc_ref.shape, F32)

    def conv(z_ref, c, cw, cb, lo, rows):
        lo = HALO + lo
        return (z_ref[c, lo - 1:lo - 1 + rows, :] * cw[0:1] + z_ref[c, lo:lo + rows, :] * cw[1:2]
                + z_ref[c, lo + 1:lo + 1 + rows, :] * cw[2:3] + cb)

    xe = xe_ref[...]
    n_sub = zg_ref.shape[0]
    for c in range(n_sub):
        cols = slice(c * tc, (c + 1) * tc)
        zg_ref[c] = jnp.dot(xe, wg_ref[:, cols], preferred_element_type=F32)
        zv_ref[c] = jnp.dot(xe, wv_ref[:, cols], preferred_element_type=F32)
    for c in range(n_sub):
        cols = slice(c * tc, (c + 1) * tc)
        pieces = 2 if c == n_sub - 1 else 1
        rows = tm // pieces
        for r in range(pieces):
            lo = r * rows
            h = (jax.nn.gelu(conv(zg_ref, c, cwg_ref[:, cols], cbg_ref[:, cols], lo, rows))
                 * conv(zv_ref, c, cwv_ref[:, cols], cbv_ref[:, cols], lo, rows))
            acc_ref[lo:lo + rows, :] += jnp.dot(h.astype(BF16), wd_ref[cols, :], preferred_element_type=F32)

    @pl.when(f == nf - 1)
    def _():
        o_ref[...] = x_ref[...] + _rms(acc_ref[...], g5_ref[...])


def _conv_ffn(x, seq, g4, w_up, conv_w, conv_b, w_down, layer, g5, tm=1024, tf=512, tc=256):
    t, d = x.shape
    d_ff = w_down.shape[1]
    nf = d_ff // tf
    hb = tm // HALO
    last_hb = t // HALO - 1
    kern = functools.partial(_ffn_kernel, tm=tm, tc=tc, tiles_per_seq=seq // tm, nf=nf)
    cb = conv_b.reshape(1, 2 * d_ff)
    return pl.pallas_call(
        kern,
        grid=(t // tm, nf),
        in_specs=[pl.BlockSpec((tm, d), lambda i, f: (i, 0), pipeline_mode=pl.Buffered(1)),
                  pl.BlockSpec((HALO, d), lambda i, f: (jnp.maximum(i * hb - 1, 0), 0)),
                  pl.BlockSpec((HALO, d), lambda i, f: (jnp.minimum((i + 1) * hb, last_hb), 0)),
                  pl.BlockSpec((1, d), lambda i, f: (0, 0)),
                  pl.BlockSpec((None, d, tf), lambda i, f: (layer, 0, f)),
                  pl.BlockSpec((None, d, tf), lambda i, f: (layer, 0, nf + f)),
                  pl.BlockSpec((CONV_W, tf), lambda i, f: (0, f)),
                  pl.BlockSpec((CONV_W, tf), lambda i, f: (0, nf + f)),
                  pl.BlockSpec((1, tf), lambda i, f: (0, f)),
                  pl.BlockSpec((1, tf), lambda i, f: (0, nf + f)),
                  pl.BlockSpec((None, tf, d), lambda i, f: (layer, f, 0)),
                  pl.BlockSpec((1, d), lambda i, f: (0, 0))],
        out_specs=pl.BlockSpec((tm, d), lambda i, f: (i, 0)),
        out_shape=jax.ShapeDtypeStruct((t, d), F32),
        scratch_shapes=[pltpu.VMEM((tm + 2 * HALO, d), BF16),
                        pltpu.VMEM((tf // tc, tm + 2 * HALO, tc), F32),
                        pltpu.VMEM((tf // tc, tm + 2 * HALO, tc), F32)],
        compiler_params=_params(("parallel", "arbitrary"), 58),
        name="conv_ffn",
    )(x, x, x, g4.reshape(1, d), w_up, w_up, conv_w, conv_w, cb, cb, w_down, g5.reshape(1, d))


def _trunk(x3, mem3, p):
    batch, seq, d = x3.shape
    x = x3.reshape(batch * seq, d)
    mem = mem3.reshape(-1, d)
    depth = p["w_in"].shape[0]
    a_w = A_HEADS * HEAD_DIM
    c_w = C_HEADS * 2 * C_HALF
    for layer in range(depth):
        g = p["norm_gains"][layer]
        proj = _norm_matmul(x, g[0], p["w_in"], layer, BF16)
        if layer % 2 == 0:
            e = layer // 2
            qkvs = _even_prep(proj, p["rope"][HEAD_DIM], batch, seq, HEAD_DIM ** -0.5)
            seqs = [qkv.reshape(batch * dil, seq // dil, 3 * a_w) for (_, dil), qkv in zip(DILATED_CFGS, qkvs)]
            dilated = [_band_attention(s) for s in seqs[1:]]
            ya = _band_attention(seqs[0], dilated).reshape(batch * seq, a_w)
            yb = _spatial_gating(proj, p["b_w_spatial"][e], p["b_b_spatial"][e], p["b_ln_gain"][e],
                                 3 * A_HEADS, 3 * A_HEADS + B_GROUPS)
            mix = jnp.concatenate([ya, yb], axis=-1)
        else:
            o = layer // 2
            qq, kk, vt = _diff_prep(proj, p["rope"][C_HALF], batch, seq, C_HALF ** -0.5 * math.log2(math.e),
                                    DIFF_BK)
            yc = _diff_attention(qq, kk, vt, p["c_lambda"][o], p["c_subln_gain"][o], batch, seq, layer,
                                 bk=DIFF_BK)
            m_c = _fourier_weights(p["d_w_fourier"][o], seq)
            u = _fourier_channel_mix(proj, m_c, 3 * c_w // LANES)
            yd = _seq_dft_real(u, batch, seq, D_GROUPS)
            mix = jnp.concatenate([yc, yd], axis=-1)
        kv = _norm_matmul(mem, p["mem_norm_gain"][layer], p["ca_w_kv"], layer, BF16)
        x = _mix_out_cross_attn(mix, x, p["w_out"], p["ca_w_q"], p["ca_w_o"], layer, g[1], g[2], g[3], kv,
                                batch, seq)
        x = _conv_ffn(x, seq, g[4], p["ffn_w_up"], p["ffn_conv_w"][layer], p["ffn_conv_b"][layer],
                      p["ffn_w_down"], layer, g[5])
    return x.reshape(batch, seq, d)


def kernel(x_prompt, x_sample, mem_prompt, mem_sample, norm_gains, w_in, w_out, b_w_spatial, b_b_spatial,
           b_ln_gain, c_lambda, c_subln_gain, d_w_fourier, mem_norm_gain, ca_w_q, ca_w_kv, ca_w_o, ffn_w_up,
           ffn_conv_w, ffn_conv_b, ffn_w_down):
    p = dict(norm_gains=norm_gains, w_in=w_in.astype(BF16), w_out=w_out.astype(BF16), b_w_spatial=b_w_spatial,
             b_b_spatial=b_b_spatial, b_ln_gain=b_ln_gain, c_lambda=c_lambda, c_subln_gain=c_subln_gain,
             d_w_fourier=d_w_fourier, mem_norm_gain=mem_norm_gain, ca_w_q=ca_w_q.astype(BF16),
             ca_w_kv=ca_w_kv.astype(BF16), ca_w_o=ca_w_o.astype(BF16), ffn_w_up=ffn_w_up.astype(BF16),
             ffn_conv_w=ffn_conv_w, ffn_conv_b=ffn_conv_b, ffn_w_down=ffn_w_down.astype(BF16))
    seq_max = max(x_prompt.shape[1], x_sample.shape[1])
    p["rope"] = {w: _rope_tables(seq_max, w) for w in (HEAD_DIM, C_HALF)}
    return _trunk(x_prompt, mem_prompt, p), _trunk(x_sample, mem_sample, p)
```

```python
import functools
import math

import jax
import jax.numpy as jnp
from jax import lax
from jax.experimental import pallas as pl
from jax.experimental.pallas import tpu as pltpu

F32 = jnp.float32
BF16 = jnp.bfloat16

EPS = 1e-6
ROPE_THETA = 500000.0
ROPE_FRACTION = 4
LANES = 128
SUBLANES = 8
HEAD_DIM = 128
A_HEADS = 8
DILATED_CFGS = ((128, 1), (512, 4), (2048, 16))
BAND_RADIUS = 64
assert all(window // (2 * dil) == BAND_RADIUS for window, dil in DILATED_CFGS)
B_GROUPS = 8
CHUNK = 128
C_HEADS = 12
C_HALF = 64
D_GROUPS = 4
CA_HEADS = 4
CONV_W = 3
NEG_BIG = -1e30
MIB = 1024 * 1024


def _params(semantics, vmem_mib):
    return pltpu.CompilerParams(dimension_semantics=semantics, vmem_limit_bytes=vmem_mib * MIB)


def _rms(x, g):
    return x * lax.rsqrt(jnp.mean(x * x, axis=-1, keepdims=True) + EPS) * g


def _dot_nt(a, b):
    return lax.dot_general(a, b, (((1,), (1,)), ((), ())), preferred_element_type=F32)


def _norm_mm_kernel(x_ref, g_ref, w_ref, o_ref, xn_ref):
    j = pl.program_id(1)

    @pl.when(j == 0)
    def _():
        half = x_ref.shape[0] // 2
        for r in (slice(0, half), slice(half, 2 * half)):
            xn_ref[r, :] = _rms(x_ref[r, :], g_ref[...]).astype(BF16)
            o_ref[r, :] = jnp.dot(xn_ref[r, :], w_ref[...], preferred_element_type=F32).astype(o_ref.dtype)

    @pl.when(j > 0)
    def _():
        o_ref[...] = jnp.dot(xn_ref[...], w_ref[...], preferred_element_type=F32).astype(o_ref.dtype)


def _norm_matmul(x, g, w, layer, out_dtype, tm=1024, tn=1024):
    t, k = x.shape
    n = w.shape[2]
    tm = min(tm, t)
    return pl.pallas_call(
        _norm_mm_kernel,
        grid=(t // tm, n // tn),
        in_specs=[pl.BlockSpec((tm, k), lambda i, j: (i, 0)),
                  pl.BlockSpec((1, k), lambda i, j: (0, 0)),
                  pl.BlockSpec((None, k, tn), lambda i, j: (layer, 0, j))],
        out_specs=pl.BlockSpec((tm, tn), lambda i, j: (i, j)),
        out_shape=jax.ShapeDtypeStruct((t, n), out_dtype),
        scratch_shapes=[pltpu.VMEM((tm, k), BF16)],
        compiler_params=_params(("parallel", "arbitrary"), 40),
        name="norm_matmul",
    )(x, g.reshape(1, k), w)


def _rope_tables(seq, head_w):
    rd = head_w // ROPE_FRACTION
    half = rd // 2
    inv = ROPE_THETA ** (-jnp.arange(half, dtype=F32) / half)
    ang = jnp.arange(seq, dtype=F32)[:, None] * inv[None, :]
    reps = LANES // half
    cos = jnp.broadcast_to(jnp.cos(ang)[:, None, :], (seq, reps, half)).reshape(seq, LANES)
    sin = jnp.broadcast_to(jnp.sin(ang)[:, None, :], (seq, reps, half)).reshape(seq, LANES)
    lane = (jnp.arange(LANES, dtype=jnp.int32) & (head_w - 1))[None, :]
    return jnp.where(lane < rd, cos, 1.0), jnp.where(lane < rd, jnp.where(lane < half, -sin, sin), 0.0)


def _rope_block(x, c, s, head_w):
    half = head_w // ROPE_FRACTION // 2
    first = (lax.broadcasted_iota(jnp.int32, x.shape, 1) & (head_w - 1)) < half
    partner = jnp.where(first, pltpu.roll(x, LANES - half, axis=1), pltpu.roll(x, half, axis=1))
    return x * c + partner * s


def _even_prep_kernel(x_ref, c_ref, s_ref, *refs, q_scale, n_blocks):
    out_refs, y_ref = refs[:-1], refs[-1]
    part = pl.program_id(1)
    tm = x_ref.shape[0]

    def rope(scale):
        c = c_ref[...]
        s = s_ref[...]
        for b in range(n_blocks):
            cols = slice(b * LANES, (b + 1) * LANES)
            y_ref[b] = _rope_block(x_ref[:, cols].astype(F32), c, s, HEAD_DIM) * scale

    @pl.when(part == 0)
    def _():
        rope(q_scale)

    @pl.when(part == 1)
    def _():
        rope(1.0)

    @pl.when(part == 2)
    def _():
        for b in range(n_blocks):
            y_ref[b] = x_ref[:, b * LANES:(b + 1) * LANES].astype(F32)

    for (_, dil), o_ref in zip(DILATED_CFGS, out_refs):
        for r in range(dil):
            for b in range(n_blocks):
                rows = y_ref[b, pl.ds(r, tm // dil, stride=dil), :]
                o_ref[0, r, :, b * LANES:(b + 1) * LANES] = rows.astype(BF16)


def _even_prep(proj, tables, batch, seq, q_scale, tm=1024):
    a_w = A_HEADS * HEAD_DIM
    cos_t, sin_t = tables
    spb = seq // tm
    kern = functools.partial(_even_prep_kernel, q_scale=q_scale, n_blocks=a_w // LANES)
    return pl.pallas_call(
        kern,
        grid=(batch * spb, 3),
        in_specs=[pl.BlockSpec((tm, a_w), lambda i, p: (i, p)),
                  pl.BlockSpec((tm, LANES), lambda i, p: (i % spb, 0)),
                  pl.BlockSpec((tm, LANES), lambda i, p: (i % spb, 0))],
        out_specs=[pl.BlockSpec((1, dil, tm // dil, a_w), lambda i, p: (i // spb, 0, i % spb, p))
                   for _, dil in DILATED_CFGS],
        out_shape=[jax.ShapeDtypeStruct((batch, dil, seq // dil, 3 * a_w), BF16) for _, dil in DILATED_CFGS],
        scratch_shapes=[pltpu.VMEM((a_w // LANES, tm, LANES), F32)],
        compiler_params=_params(("parallel", "arbitrary"), 32),
        name="even_prep",
    )(proj, cos_t, sin_t)


def _band_attn_kernel(q_ref, kp_ref, kc_ref, kn_ref, vp_ref, vc_ref, vn_ref, *refs, bq, sub_len, merge):
    if merge:
        o4, l4, o16, l16, y_ref, so4, sl4, so16, sl16 = refs
        for dil, src, dst in ((4, o4, so4), (4, l4, sl4), (16, o16, so16), (16, l16, sl16)):
            for r in range(dil):
                for h in range(A_HEADS):
                    dst[h, pl.ds(r, bq // dil, stride=dil), :] = (
                        src[0, r, :, h * HEAD_DIM:(h + 1) * HEAD_DIM].astype(F32))
    else:
        o_ref, lse_ref = refs
    q0 = pl.program_id(1) * bq
    sq = 2 * BAND_RADIUS
    win = sq + 2 * BAND_RADIUS
    n_sub = bq // sq
    rel = (lax.broadcasted_iota(jnp.int32, (sq, win), 1) - BAND_RADIUS
           - lax.broadcasted_iota(jnp.int32, (sq, win), 0))
    band = (rel <= BAND_RADIUS) & (rel >= -BAND_RADIUS)
    biases = []
    for u in range(n_sub):
        kpos = q0 + u * sq - BAND_RADIUS + lax.broadcasted_iota(jnp.int32, (sq, win), 1)
        ok = band & (kpos >= 0) & (kpos < sub_len)
        biases.append(jnp.where(ok, 0.0, NEG_BIG))

    def window(p_ref, c_ref, n_ref, cols, u):
        parts = []
        if u == 0:
            parts.append(p_ref[0, :, cols])
        lo = max(u * sq - BAND_RADIUS, 0)
        hi = min(u * sq + sq + BAND_RADIUS, bq)
        parts.append(c_ref[0, lo:hi, cols])
        if u == n_sub - 1:
            parts.append(n_ref[0, :, cols])
        return jnp.concatenate(parts, axis=0)

    tiles = [(h, u) for h in range(A_HEADS) for u in range(n_sub)]

    def scores(h, u):
        cols = slice(h * HEAD_DIM, (h + 1) * HEAD_DIM)
        q = q_ref[0, u * sq:(u + 1) * sq, cols]
        return _dot_nt(q, window(kp_ref, kc_ref, kn_ref, cols, u)) + biases[u]

    s_next = scores(*tiles[0])
    for n, (h, u) in enumerate(tiles):
        cols = slice(h * HEAD_DIM, (h + 1) * HEAD_DIM)
        rows = slice(u * sq, (u + 1) * sq)
        s = s_next
        if n + 1 < len(tiles):
            s_next = scores(*tiles[n + 1])
        m = jnp.max(s, axis=-1, keepdims=True)
        p = jnp.exp(s - m)
        l = jnp.sum(p, axis=-1, keepdims=True)
        o = jnp.dot(p.astype(BF16), window(vp_ref, vc_ref, vn_ref, cols, u), preferred_element_type=F32) / l
        lse = jnp.broadcast_to(m + jnp.log(l), (sq, HEAD_DIM))
        if merge:
            b, c = sl4[h, rows, :], sl16[h, rows, :]
            top = jnp.maximum(jnp.maximum(lse, b), c)
            wa, wb, wc = jnp.exp(lse - top), jnp.exp(b - top), jnp.exp(c - top)
            num = wa * o + wb * so4[h, rows, :] + wc * so16[h, rows, :]
            y_ref[0, rows, cols] = (num / (wa + wb + wc)).astype(y_ref.dtype)
        else:
            o_ref[0, rows, cols] = o.astype(o_ref.dtype)
            lse_ref[0, rows, cols] = lse


def _band_attention(qkv, dilated=None, bq=256):
    n_seq, sub_len, _ = qkv.shape
    a_w = A_HEADS * HEAD_DIM
    hb = bq // BAND_RADIUS
    last = sub_len // BAND_RADIUS - 1
    merge = dilated is not None
    kern = functools.partial(_band_attn_kernel, bq=bq, sub_len=sub_len, merge=merge)

    def halo_specs(col):
        return [pl.BlockSpec((1, BAND_RADIUS, a_w), lambda s, i: (s, jnp.maximum(i * hb - 1, 0), col)),
                pl.BlockSpec((1, bq, a_w), lambda s, i: (s, i, col)),
                pl.BlockSpec((1, BAND_RADIUS, a_w), lambda s, i: (s, jnp.minimum((i + 1) * hb, last), col))]

    out_spec = pl.BlockSpec((1, bq, a_w), lambda s, i: (s, i, 0))
    in_specs = [pl.BlockSpec((1, bq, a_w), lambda s, i: (s, i, 0))] + halo_specs(1) + halo_specs(2)
    operands = [qkv] * 7
    if merge:
        for (_, dil), pair in zip(DILATED_CFGS[1:], dilated):
            for arr in pair:
                operands.append(arr.reshape(n_seq, dil, sub_len // dil, a_w))
                in_specs.append(pl.BlockSpec((1, dil, bq // dil, a_w), lambda s, i: (s, 0, i, 0)))
        out_specs, out_shape = out_spec, jax.ShapeDtypeStruct((n_seq, sub_len, a_w), BF16)
        scratch = [pltpu.VMEM((A_HEADS, bq, HEAD_DIM), F32)] * 4
    else:
        out_specs = [out_spec, out_spec]
        out_shape = [jax.ShapeDtypeStruct((n_seq, sub_len, a_w), BF16),
                     jax.ShapeDtypeStruct((n_seq, sub_len, a_w), F32)]
        scratch = []
    return pl.pallas_call(
        kern,
        grid=(n_seq, sub_len // bq),
        in_specs=in_specs,
        out_specs=out_specs,
        out_shape=out_shape,
        scratch_shapes=scratch,
        compiler_params=_params(("parallel", "parallel"), 40),
        name="band_attn_merge" if merge else "band_attn",
    )(*operands)


def _gating_kernel(u_ref, v_ref, w_ref, b_ref, g_ref, o_ref, *, n_chunks):
    w = w_ref[0]
    bias = b_ref[0]
    gain = g_ref[0]
    for c in range(n_chunks):
        rows = pl.ds(c * CHUNK, CHUNK)
        v = jax.nn.gelu(v_ref[rows, :].astype(F32))
        mu = jnp.mean(v, axis=-1, keepdims=True)
        d = v - mu
        var = jnp.mean(d * d, axis=-1, keepdims=True)
        vn = d * lax.rsqrt(var + EPS) * gain
        sv = jnp.dot(w, vn.astype(BF16), preferred_element_type=F32) + bias
        o_ref[rows, :] = (jax.nn.gelu(u_ref[rows, :].astype(F32)) * sv).astype(o_ref.dtype)


def _spatial_gating(proj, w_s, b_s, g_ln, u_col, v_col, tm=2048):
    t = proj.shape[0]
    tm = min(tm, t)
    groups = w_s.shape[0]
    bias = jnp.broadcast_to(b_s[:, :, None], (groups, CHUNK, LANES))
    kern = functools.partial(_gating_kernel, n_chunks=tm // CHUNK)
    return pl.pallas_call(
        kern,
        grid=(t // tm, groups),
        in_specs=[pl.BlockSpec((tm, LANES), lambda i, g: (i, u_col + g)),
                  pl.BlockSpec((tm, LANES), lambda i, g: (i, v_col + g)),
                  pl.BlockSpec((1, CHUNK, CHUNK), lambda i, g: (g, 0, 0)),
                  pl.BlockSpec((1, CHUNK, LANES), lambda i, g: (g, 0, 0)),
                  pl.BlockSpec((1, 1, LANES), lambda i, g: (g, 0, 0))],
        out_specs=pl.BlockSpec((tm, LANES), lambda i, g: (i, g)),
        out_shape=jax.ShapeDtypeStruct((t, groups * LANES), BF16),
        compiler_params=_params(("parallel", "parallel"), 32),
        name="spatial_gating",
    )(proj, proj, w_s.astype(BF16), bias, g_ln.reshape(groups, 1, LANES))


V_ROWS = 2 * C_HALF + 16
DIFF_BK = 512


def _diff_prep_kernel(x_ref, c_ref, s_ref, q_ref, k_ref, vt_ref, *, q_scale, n_blocks):
    part = pl.program_id(1)
    tm = x_ref.shape[0]

    def rope(o_ref, scale):
        c = c_ref[...]
        s = s_ref[...]
        for h in range(n_blocks):
            cols = slice(h * LANES, (h + 1) * LANES)
            o_ref[:, cols] = (_rope_block(x_ref[:, cols].astype(F32), c, s, C_HALF) * scale).astype(BF16)

    @pl.when(part == 0)
    def _():
        rope(q_ref, q_scale)

    @pl.when(part == 1)
    def _():
        rope(k_ref, 1.0)

    @pl.when(part == 2)
    def _():
        ones = jnp.ones((V_ROWS - LANES, tm), BF16)
        for h in range(n_blocks):
            vt_ref[0, h, 0, 0:LANES, :] = x_ref[:, h * LANES:(h + 1) * LANES].astype(F32).T.astype(BF16)
            vt_ref[0, h, 0, LANES:, :] = ones


def _diff_prep(proj, tables, batch, seq, q_scale, bk, tm=512):
    c_w = C_HEADS * 2 * C_HALF
    cos_t, sin_t = tables
    spb = seq // tm
    per_chunk = bk // tm
    kern = functools.partial(_diff_prep_kernel, q_scale=q_scale, n_blocks=C_HEADS)
    return pl.pallas_call(
        kern,
        grid=(batch * spb, 3),
        in_specs=[pl.BlockSpec((tm, c_w), lambda i, p: (i, p)),
                  pl.BlockSpec((tm, LANES), lambda i, p: (i % spb, 0)),
                  pl.BlockSpec((tm, LANES), lambda i, p: (i % spb, 0))],
        out_specs=[pl.BlockSpec((tm, c_w), lambda i, p: (i, 0)),
                   pl.BlockSpec((tm, c_w), lambda i, p: (i, 0)),
                   pl.BlockSpec((1, C_HEADS, 1, V_ROWS, tm),
                                lambda i, p: (i // spb, 0, (i % spb) // per_chunk, 0, (i % spb) % per_chunk))],
        out_shape=[jax.ShapeDtypeStruct((batch * seq, c_w), BF16),
                   jax.ShapeDtypeStruct((batch * seq, c_w), BF16),
                   jax.ShapeDtypeStruct((batch, C_HEADS, seq // bk, V_ROWS, bk), BF16)],
        compiler_params=_params(("parallel", "arbitrary"), 40),
        name="diff_prep",
    )(proj, cos_t, sin_t)


def _diff_attn_kernel(q_ref, k_ref, vt_ref, lam_ref, g_ref, o_ref, s_ref, mc_ref, m_ref, acc_ref,
                      *, bq, bk, seq, lam_init, unroll):
    n = seq // bk
    sub = SUBLANES
    q = q_ref[0]
    lane = lax.broadcasted_iota(jnp.int32, q.shape, 1)
    zero = jnp.zeros_like(q)
    qs = (jnp.where(lane < C_HALF, q, zero), jnp.where(lane < C_HALF, zero, q))

    def scores(j, slot):
        kc = k_ref[0, pl.ds(pl.multiple_of(j * bk, bk), bk), :]
        for br in range(2):
            s = _dot_nt(kc, qs[br])
            s_ref[slot, br] = s
            mc = jnp.max(s.reshape(bk // sub, sub, bq), axis=0)
            mc_ref[slot, br] = jnp.broadcast_to(jnp.max(mc, axis=0, keepdims=True), (sub, bq))

    def accumulate(j, slot):
        vt = vt_ref[0, 0, j]
        for br in range(2):
            m_old = m_ref[br]
            m_new = jnp.maximum(m_old, mc_ref[slot, br])
            alpha = jnp.exp2(m_old - m_new)
            m_ref[br] = m_new
            s3 = s_ref[slot, br].reshape(bk // sub, sub, bq)
            p = jnp.exp2(s3 - m_new[None]).reshape(bk, bq).astype(BF16)
            pv = jnp.dot(vt, p, preferred_element_type=F32)
            acc3 = acc_ref[br].reshape(V_ROWS // sub, sub, bq)
            acc_ref[br] = (alpha[None] * acc3).reshape(V_ROWS, bq) + pv

    m_ref[...] = jnp.full(m_ref.shape, NEG_BIG, F32)
    acc_ref[...] = jnp.zeros(acc_ref.shape, F32)
    scores(0, 0)

    def trip(j, last):
        for u in range(unroll):
            if not (last and u == unroll - 1):
                scores(j + u + 1, (u + 1) % 2)
            accumulate(j + u, u % 2)

    def body(i, carry):
        trip(unroll * i, False)
        return carry

    lax.fori_loop(0, n // unroll - 1, body, 0)
    trip(n - unroll, True)

    lam = lam_ref[...]
    lam_full = (jnp.exp(jnp.sum(lam[0:1] * lam[1:2], keepdims=True))
                - jnp.exp(jnp.sum(lam[2:3] * lam[3:4], keepdims=True)) + lam_init)

    def normalised(a):
        num = a[0:LANES].reshape(LANES // sub, sub, bq)
        return (num / a[LANES:LANES + sub][None]).reshape(LANES, bq)

    o = normalised(acc_ref[0]) - lam_full * normalised(acc_ref[1])
    y = o * lax.rsqrt(jnp.mean(o * o, axis=0, keepdims=True) + EPS) * g_ref[...] * (1.0 - lam_init)
    o_ref[0] = y.T.astype(o_ref.dtype)


def _diff_attention(q, k, vt, lam, subln_g, batch, seq, layer_idx, bq=512, bk=512, unroll=8):
    c_w = C_HEADS * 2 * C_HALF
    lam_init = 0.8 - 0.6 * math.exp(-0.3 * layer_idx)
    n = seq // bk
    unroll = min(unroll, n)
    assert unroll % 2 == 0 and n % unroll == 0 and vt.shape[2] == n
    kern = functools.partial(_diff_attn_kernel, bq=bq, bk=bk, seq=seq, lam_init=lam_init, unroll=unroll)
    out = pl.pallas_call(
        kern,
        grid=(batch, C_HEADS, seq // bq),
        in_specs=[pl.BlockSpec((1, bq, LANES), lambda b, h, i: (b, i, h)),
                  pl.BlockSpec((1, seq, LANES), lambda b, h, i: (b, 0, h)),
                  pl.BlockSpec((1, 1, n, V_ROWS, bk), lambda b, h, i: (b, h, 0, 0, 0)),
                  pl.BlockSpec((4, C_HALF), lambda b, h, i: (0, 0)),
                  pl.BlockSpec((LANES, bq), lambda b, h, i: (0, 0))],
        out_specs=pl.BlockSpec((1, bq, LANES), lambda b, h, i: (b, i, h)),
        out_shape=jax.ShapeDtypeStruct((batch, seq, c_w), BF16),
        scratch_shapes=[pltpu.VMEM((2, 2, bk, bq), F32),
                        pltpu.VMEM((2, 2, SUBLANES, bq), F32),
                        pltpu.VMEM((2, SUBLANES, bq), F32),
                        pltpu.VMEM((2, V_ROWS, bq), F32)],
        compiler_params=_params(("parallel", "parallel", "arbitrary"), 48),
        name="diff_attn",
    )(q.reshape(batch, seq, c_w), k.reshape(batch, seq, c_w), vt, lam,
      jnp.broadcast_to(subln_g[:, None], (LANES, bq)))
    return out.reshape(batch * seq, c_w)


def _fourier_wprep_kernel(c_ref, s_ref, w_ref, o_ref, *, scale):
    w = w_ref[0]
    mr = jnp.dot(c_ref[...], w, preferred_element_type=F32, precision=lax.Precision.HIGHEST)
    mi = jnp.dot(s_ref[...], w, preferred_element_type=F32, precision=lax.Precision.HIGHEST)
    o_ref[0, :, :LANES] = (mr * scale).astype(BF16)
    o_ref[0, :, LANES:] = (-mi * scale).astype(BF16)


def _fourier_weights(w_f, seq):
    groups, c, _ = w_f.shape
    idx = jnp.arange(c, dtype=jnp.int32)
    ang = (2.0 * math.pi / c) * ((idx[:, None] * idx[None, :]) % c).astype(F32)
    kern = functools.partial(_fourier_wprep_kernel, scale=1.0 / math.sqrt(seq * c))
    return pl.pallas_call(
        kern,
        grid=(groups,),
        in_specs=[pl.BlockSpec((c, c), lambda g: (0, 0)),
                  pl.BlockSpec((c, c), lambda g: (0, 0)),
                  pl.BlockSpec((1, c, c), lambda g: (g, 0, 0))],
        out_specs=pl.BlockSpec((1, c, 2 * c), lambda g: (g, 0, 0)),
        out_shape=jax.ShapeDtypeStruct((groups, c, 2 * c), BF16),
        compiler_params=_params(("parallel",), 32),
        name="fourier_wprep",
    )(jnp.cos(ang), jnp.sin(ang), w_f)


def _chan_mix_kernel(z_ref, m_ref, o_ref):
    o_ref[...] = jnp.dot(z_ref[...], m_ref[0], preferred_element_type=F32).astype(BF16)


def _fourier_channel_mix(proj, m_c, z_col, tm=2048):
    t = proj.shape[0]
    tm = min(tm, t)
    groups = m_c.shape[0]
    return pl.pallas_call(
        _chan_mix_kernel,
        grid=(t // tm, groups),
        in_specs=[pl.BlockSpec((tm, LANES), lambda i, g: (i, z_col + g)),
                  pl.BlockSpec((1, LANES, 2 * LANES), lambda i, g: (g, 0, 0))],
        out_specs=pl.BlockSpec((tm, 2 * LANES), lambda i, g: (i, g)),
        out_shape=jax.ShapeDtypeStruct((t, groups * 2 * LANES), BF16),
        compiler_params=_params(("parallel", "parallel"), 32),
        name="fourier_channel_mix",
    )(proj, m_c)


def _dft_stage1_kernel(u_ref, fr_ref, fi_ref, y_ref, *, pairs):
    u = u_ref[0]
    p = jnp.dot(fr_ref[...], u, preferred_element_type=F32)
    q = jnp.dot(fi_ref[...], u, preferred_element_type=F32)
    for j in range(pairs):
        re = slice(2 * j * LANES, (2 * j + 1) * LANES)
        im = slice((2 * j + 1) * LANES, (2 * j + 2) * LANES)
        y_ref[0, :, re] = (p[:, re] - q[:, im]).astype(BF16)
        y_ref[0, :, im] = (q[:, re] + p[:, im]).astype(BF16)


def _dft_stage2_kernel(y_ref, gc_ref, gs_ref, o_ref, *, groups):
    for t in range(y_ref.shape[1]):
        y = y_ref[0, t]
        a = jnp.dot(gc_ref[t], y, preferred_element_type=F32)
        b = jnp.dot(gs_ref[t], y, preferred_element_type=F32)
        for g in range(groups):
            re = slice(2 * g * LANES, (2 * g + 1) * LANES)
            im = slice((2 * g + 1) * LANES, (2 * g + 2) * LANES)
            out_cols = slice((t * groups + g) * LANES, (t * groups + g + 1) * LANES)
            o_ref[0, :, out_cols] = (a[:, re] + b[:, im]).astype(o_ref.dtype)


def _seq_dft_real(u, batch, seq, groups, tc=4096, kb=4):
    n1 = 128 if seq >= 16384 else 64
    n2 = seq // n1
    wc = groups * 2 * LANES
    i1 = jnp.arange(n1, dtype=jnp.int32)
    ang1 = (2.0 * math.pi / n1) * ((i1[:, None] * i1[None, :]) % n1).astype(F32)
    fr = jnp.cos(ang1).astype(BF16)
    fi = (-jnp.sin(ang1)).astype(BF16)
    cols = n2 * wc
    tc = min(tc, cols)
    y = pl.pallas_call(
        functools.partial(_dft_stage1_kernel, pairs=tc // (2 * LANES)),
        grid=(batch, cols // tc),
        in_specs=[pl.BlockSpec((1, n1, tc), lambda b, j: (b, 0, j)),
                  pl.BlockSpec((n1, n1), lambda b, j: (0, 0)),
                  pl.BlockSpec((n1, n1), lambda b, j: (0, 0))],
        out_specs=pl.BlockSpec((1, n1, tc), lambda b, j: (b, 0, j)),
        out_shape=jax.ShapeDtypeStruct((batch, n1, cols), BF16),
        compiler_params=_params(("parallel", "parallel"), 32),
        name="dft_stage1",
    )(u.reshape(batch, n1, cols), fr, fi)
    i2 = jnp.arange(n2, dtype=jnp.int32)
    tw = (i2[None, None, :] * (i1[:, None, None] + n1 * i2[None, :, None])) % seq
    ang2 = (2.0 * math.pi / seq) * tw.astype(F32)
    gc = jnp.cos(ang2).astype(BF16)
    gs = jnp.sin(ang2).astype(BF16)
    out = pl.pallas_call(
        functools.partial(_dft_stage2_kernel, groups=groups),
        grid=(batch, n1 // kb),
        in_specs=[pl.BlockSpec((1, kb, n2, wc), lambda b, k: (b, k, 0, 0)),
                  pl.BlockSpec((kb, n2, n2), lambda b, k: (k, 0, 0)),
                  pl.BlockSpec((kb, n2, n2), lambda b, k: (k, 0, 0))],
        out_specs=pl.BlockSpec((1, n2, kb * groups * LANES), lambda b, k: (b, 0, k)),
        out_shape=jax.ShapeDtypeStruct((batch, n2, n1 * groups * LANES), BF16),
        compiler_params=_params(("parallel", "parallel"), 32),
        name="dft_stage2",
    )(y.reshape(batch, n1, n2, wc), gc, gs)
    return out.reshape(batch * seq, groups * LANES)


def _mix_out_cross_attn_kernel(mixa_ref, mixb_ref, x_ref, wout_ref, g1_ref, g2_ref, wq_ref, kv_ref, wo_ref, g3_ref,
                               o_ref, *, heads, scale, halves):
    ca_w = heads * HEAD_DIM
    hr = x_ref.shape[0] // halves
    ka = mixa_ref.shape[1]
    rows = [slice(r * hr, (r + 1) * hr) for r in range(halves)]
    ys = [jnp.dot(mixa_ref[r, :], wout_ref[0:ka, :], preferred_element_type=F32)
          + jnp.dot(mixb_ref[r, :], wout_ref[ka:, :], preferred_element_type=F32) for r in rows]
    x1s = [x_ref[r, :] + _rms(y, g1_ref[...]) for r, y in zip(rows, ys)]
    qs = [jnp.dot(_rms(x1, g2_ref[...]).astype(BF16), wq_ref[...], preferred_element_type=F32).astype(BF16)
          for x1 in x1s]
    cas = []
    for q in qs:
        outs = []
        for h in range(heads):
            cols = slice(h * HEAD_DIM, (h + 1) * HEAD_DIM)
            s = _dot_nt(q[:, cols], kv_ref[0, :, cols]) * scale
            m = jnp.max(s, axis=-1, keepdims=True)
            p = jnp.exp(s - m)
            l = jnp.sum(p, axis=-1, keepdims=True)
            v = kv_ref[0, :, ca_w + h * HEAD_DIM:ca_w + (h + 1) * HEAD_DIM]
            outs.append((jnp.dot(p.astype(BF16), v, preferred_element_type=F32) / l).astype(BF16))
        cas.append(jnp.concatenate(outs, axis=1))
    zs = [jnp.dot(ca, wo_ref[...], preferred_element_type=F32) for ca in cas]
    for r, x1, z in zip(rows, x1s, zs):
        o_ref[r, :] = x1 + _rms(z, g3_ref[...])


def _mix_out_cross_attn(mix_a, mix_b, x, w_out, w_q, w_o, layer, g1, g2, g3, kv, batch, seq, tm=512):
    t, d = x.shape
    ka, kb = mix_a.shape[1], mix_b.shape[1]
    assert ka + kb == w_out.shape[1]
    ca_w = CA_HEADS * HEAD_DIM
    mem_len = kv.shape[0] // batch
    spb = seq // tm
    kern = functools.partial(_mix_out_cross_attn_kernel, heads=CA_HEADS, scale=HEAD_DIM ** -0.5, halves=2)
    once = pl.Buffered(1)
    vec = lambda: pl.BlockSpec((1, d), lambda i: (0, 0))
    return pl.pallas_call(
        kern,
        grid=(t // tm,),
        in_specs=[pl.BlockSpec((tm, ka), lambda i: (i, 0)),
                  pl.BlockSpec((tm, kb), lambda i: (i, 0)),
                  pl.BlockSpec((tm, d), lambda i: (i, 0)),
                  pl.BlockSpec((None, d, d), lambda i: (layer, 0, 0), pipeline_mode=once),
                  vec(), vec(),
                  pl.BlockSpec((None, d, ca_w), lambda i: (layer, 0, 0), pipeline_mode=once),
                  pl.BlockSpec((1, mem_len, 2 * ca_w), lambda i: (i // spb, 0, 0)),
                  pl.BlockSpec((None, ca_w, d), lambda i: (layer, 0, 0), pipeline_mode=once),
                  vec()],
        out_specs=pl.BlockSpec((tm, d), lambda i: (i, 0)),
        out_shape=jax.ShapeDtypeStruct((t, d), F32),
        compiler_params=_params(("parallel",), 56),
        name="mix_out_cross_attn",
    )(mix_a, mix_b, x, w_out, g1.reshape(1, d), g2.reshape(1, d), w_q, kv.reshape(batch, mem_len, 2 * ca_w), w_o,
      g3.reshape(1, d))


HALO = 16


def _ffn_kernel(x_ref, xp_ref, xn_ref, g4_ref, wg_ref, wv_ref, cwg_ref, cwv_ref, cbg_ref, cbv_ref,
                wd_ref, g5_ref, o_ref, xe_ref, zg_ref, zv_ref, *, tm, tc, tiles_per_seq, nf):
    acc_ref = o_ref
    i = pl.program_id(0)
    f = pl.program_id(1)

    @pl.when(f == 0)
    def _():
        g4 = g4_ref[...]
        pos = i % tiles_per_seq
        prev = _rms(xp_ref[...], g4) * jnp.where(pos == 0, 0.0, 1.0)
        nxt = _rms(xn_ref[...], g4) * jnp.where(pos == tiles_per_seq - 1, 0.0, 1.0)
        xe_ref[0:HALO, :] = prev.astype(BF16)
        xe_ref[HALO:HALO + tm, :] = _rms(x_ref[...], g4).astype(BF16)
        xe_ref[HALO + tm:, :] = nxt.astype(BF16)
        acc_ref[...] = jnp.zeros(acc_ref.shape, F32)

    def conv(z_ref, c, cw, cb, lo, rows):
        lo = HALO + lo
        return (z_ref[c, lo - 1:lo - 1 + rows, :] * cw[0:1] + z_ref[c, lo:lo + rows, :] * cw[1:2]
                + z_ref[c, lo + 1:lo + 1 + rows, :] * cw[2:3] + cb)

    xe = xe_ref[...]
    n_sub = zg_ref.shape[0]
    for c in range(n_sub):
        cols = slice(c * tc, (c + 1) * tc)
        zg_ref[c] = jnp.dot(xe, wg_ref[:, cols], preferred_element_type=F32)
        zv_ref[c] = jnp.dot(xe, wv_ref[:, cols], preferred_element_type=F32)
    for c in range(n_sub):
        cols = slice(c * tc, (c + 1) * tc)
        pieces = 2 if c == n_sub - 1 else 1
        rows = tm // pieces
        for r in range(pieces):
            lo = r * rows
            h = (jax.nn.gelu(conv(zg_ref, c, cwg_ref[:, cols], cbg_ref[:, cols], lo, rows))
                 * conv(zv_ref, c, cwv_ref[:, cols], cbv_ref[:, cols], lo, rows))
            acc_ref[lo:lo + rows, :] += jnp.dot(h.astype(BF16), wd_ref[cols, :], preferred_element_type=F32)

    @pl.when(f == nf - 1)
    def _():
        o_ref[...] = x_ref[...] + _rms(acc_ref[...], g5_ref[...])


def _conv_ffn(x, seq, g4, w_up, conv_w, conv_b, w_down, layer, g5, tm=1024, tf=512, tc=256):
    t, d = x.shape
    d_ff = w_down.shape[1]
    nf = d_ff // tf
    hb = tm // HALO
    last_hb = t // HALO - 1
    kern = functools.partial(_ffn_kernel, tm=tm, tc=tc, tiles_per_seq=seq // tm, nf=nf)
    cb = conv_b.reshape(1, 2 * d_ff)
    return pl.pallas_call(
        kern,
        grid=(t // tm, nf),
        in_specs=[pl.BlockSpec((tm, d), lambda i, f: (i, 0), pipeline_mode=pl.Buffered(1)),
                  pl.BlockSpec((HALO, d), lambda i, f: (jnp.maximum(i * hb - 1, 0), 0)),
                  pl.BlockSpec((HALO, d), lambda i, f: (jnp.minimum((i + 1) * hb, last_hb), 0)),
                  pl.BlockSpec((1, d), lambda i, f: (0, 0)),
                  pl.BlockSpec((None, d, tf), lambda i, f: (layer, 0, f)),
                  pl.BlockSpec((None, d, tf), lambda i, f: (layer, 0, nf + f)),
                  pl.BlockSpec((CONV_W, tf), lambda i, f: (0, f)),
                  pl.BlockSpec((CONV_W, tf), lambda i, f: (0, nf + f)),
                  pl.BlockSpec((1, tf), lambda i, f: (0, f)),
                  pl.BlockSpec((1, tf), lambda i, f: (0, nf + f)),
                  pl.BlockSpec((None, tf, d), lambda i, f: (layer, f, 0)),
                  pl.BlockSpec((1, d), lambda i, f: (0, 0))],
        out_specs=pl.BlockSpec((tm, d), lambda i, f: (i, 0)),
        out_shape=jax.ShapeDtypeStruct((t, d), F32),
        scratch_shapes=[pltpu.VMEM((tm + 2 * HALO, d), BF16),
                        pltpu.VMEM((tf // tc, tm + 2 * HALO, tc), F32),
                        pltpu.VMEM((tf // tc, tm + 2 * HALO, tc), F32)],
        compiler_params=_params(("parallel", "arbitrary"), 58),
        name="conv_ffn",
    )(x, x, x, g4.reshape(1, d), w_up, w_up, conv_w, conv_w, cb, cb, w_down, g5.reshape(1, d))


def _trunk(x3, mem3, p):
    batch, seq, d = x3.shape
    x = x3.reshape(batch * seq, d)
    mem = mem3.reshape(-1, d)
    depth = p["w_in"].shape[0]
    a_w = A_HEADS * HEAD_DIM
    c_w = C_HEADS * 2 * C_HALF
    for layer in range(depth):
        g = p["norm_gains"][layer]
        proj = _norm_matmul(x, g[0], p["w_in"], layer, BF16)
        if layer % 2 == 0:
            e = layer // 2
            qkvs = _even_prep(proj, p["rope"][HEAD_DIM], batch, seq, HEAD_DIM ** -0.5)
            seqs = [qkv.reshape(batch * dil, seq // dil, 3 * a_w) for (_, dil), qkv in zip(DILATED_CFGS, qkvs)]
            dilated = [_band_attention(s) for s in seqs[1:]]
            ya = _band_attention(seqs[0], dilated).reshape(batch * seq, a_w)
            yb = _spatial_gating(proj, p["b_w_spatial"][e], p["b_b_spatial"][e], p["b_ln_gain"][e],
                                 3 * A_HEADS, 3 * A_HEADS + B_GROUPS)
            mix = (ya, yb)
        else:
            o = layer // 2
            qq, kk, vt = _diff_prep(proj, p["rope"][C_HALF], batch, seq, C_HALF ** -0.5 * math.log2(math.e),
                                    DIFF_BK)
            yc = _diff_attention(qq, kk, vt, p["c_lambda"][o], p["c_subln_gain"][o], batch, seq, layer,
                                 bk=DIFF_BK)
            m_c = _fourier_weights(p["d_w_fourier"][o], seq)
            u = _fourier_channel_mix(proj, m_c, 3 * c_w // LANES)
            yd = _seq_dft_real(u, batch, seq, D_GROUPS)
            mix = (yc, yd)
        kv = _norm_matmul(mem, p["mem_norm_gain"][layer], p["ca_w_kv"], layer, BF16)
        x = _mix_out_cross_attn(*mix, x, p["w_out"], p["ca_w_q"], p["ca_w_o"], layer, g[1], g[2], g[3], kv,
                                batch, seq)
        x = _conv_ffn(x, seq, g[4], p["ffn_w_up"], p["ffn_conv_w"][layer], p["ffn_conv_b"][layer],
                      p["ffn_w_down"], layer, g[5])
    return x.reshape(batch, seq, d)


def kernel(x_prompt, x_sample, mem_prompt, mem_sample, norm_gains, w_in, w_out, b_w_spatial, b_b_spatial,
           b_ln_gain, c_lambda, c_subln_gain, d_w_fourier, mem_norm_gain, ca_w_q, ca_w_kv, ca_w_o, ffn_w_up,
           ffn_conv_w, ffn_conv_b, ffn_w_down):
    p = dict(norm_gains=norm_gains, w_in=w_in.astype(BF16), w_out=w_out.astype(BF16), b_w_spatial=b_w_spatial,
             b_b_spatial=b_b_spatial, b_ln_gain=b_ln_gain, c_lambda=c_lambda, c_subln_gain=c_subln_gain,
             d_w_fourier=d_w_fourier, mem_norm_gain=mem_norm_gain, ca_w_q=ca_w_q.astype(BF16),
             ca_w_kv=ca_w_kv.astype(BF16), ca_w_o=ca_w_o.astype(BF16), ffn_w_up=ffn_w_up.astype(BF16),
             ffn_conv_w=ffn_conv_w, ffn_conv_b=ffn_conv_b, ffn_w_down=ffn_w_down.astype(BF16))
    seq_max = max(x_prompt.shape[1], x_sample.shape[1])
    p["rope"] = {w: _rope_tables(seq_max, w) for w in (HEAD_DIM, C_HALF)}
    return _trunk(x_prompt, mem_prompt, p), _trunk(x_sample, mem_sample, p)
```

```python
import functools
import math

import jax
import jax.numpy as jnp
from jax import lax
from jax.experimental import pallas as pl
from jax.experimental.pallas import tpu as pltpu

F32 = jnp.float32
BF16 = jnp.bfloat16

EPS = 1e-6
ROPE_THETA = 500000.0
ROPE_FRACTION = 4
LANES = 128
SUBLANES = 8
HEAD_DIM = 128
A_HEADS = 8
DILATED_CFGS = ((128, 1), (512, 4), (2048, 16))
BAND_RADIUS = 64
assert all(window // (2 * dil) == BAND_RADIUS for window, dil in DILATED_CFGS)
B_GROUPS = 8
CHUNK = 128
C_HEADS = 12
C_HALF = 64
D_GROUPS = 4
CA_HEADS = 4
CONV_W = 3
NEG_BIG = -1e30
MIB = 1024 * 1024


def _params(semantics, vmem_mib):
    return pltpu.CompilerParams(dimension_semantics=semantics, vmem_limit_bytes=vmem_mib * MIB)


def _rms(x, g):
    return x * lax.rsqrt(jnp.mean(x * x, axis=-1, keepdims=True) + EPS) * g


def _dot_nt(a, b):
    return lax.dot_general(a, b, (((1,), (1,)), ((), ())), preferred_element_type=F32)


def _norm_mm_kernel(x_ref, g_ref, w_ref, o_ref, xn_ref):
    j = pl.program_id(1)

    @pl.when(j == 0)
    def _():
        half = x_ref.shape[0] // 2
        for r in (slice(0, half), slice(half, 2 * half)):
            xn_ref[r, :] = _rms(x_ref[r, :], g_ref[...]).astype(BF16)
            o_ref[r, :] = jnp.dot(xn_ref[r, :], w_ref[...], preferred_element_type=F32).astype(o_ref.dtype)

    @pl.when(j > 0)
    def _():
        o_ref[...] = jnp.dot(xn_ref[...], w_ref[...], preferred_element_type=F32).astype(o_ref.dtype)


def _norm_matmul(x, g, w, layer, out_dtype, tm=1024, tn=1024):
    t, k = x.shape
    n = w.shape[2]
    tm = min(tm, t)
    return pl.pallas_call(
        _norm_mm_kernel,
        grid=(t // tm, n // tn),
        in_specs=[pl.BlockSpec((tm, k), lambda i, j: (i, 0)),
                  pl.BlockSpec((1, k), lambda i, j: (0, 0)),
                  pl.BlockSpec((None, k, tn), lambda i, j: (layer, 0, j))],
        out_specs=pl.BlockSpec((tm, tn), lambda i, j: (i, j)),
        out_shape=jax.ShapeDtypeStruct((t, n), out_dtype),
        scratch_shapes=[pltpu.VMEM((tm, k), BF16)],
        compiler_params=_params(("parallel", "arbitrary"), 40),
        name="norm_matmul",
    )(x, g.reshape(1, k), w)


def _rope_tables(seq, head_w):
    rd = head_w // ROPE_FRACTION
    half = rd // 2
    inv = ROPE_THETA ** (-jnp.arange(half, dtype=F32) / half)
    ang = jnp.arange(seq, dtype=F32)[:, None] * inv[None, :]
    reps = LANES // half
    cos = jnp.broadcast_to(jnp.cos(ang)[:, None, :], (seq, reps, half)).reshape(seq, LANES)
    sin = jnp.broadcast_to(jnp.sin(ang)[:, None, :], (seq, reps, half)).reshape(seq, LANES)
    lane = (jnp.arange(LANES, dtype=jnp.int32) & (head_w - 1))[None, :]
    return jnp.where(lane < rd, cos, 1.0), jnp.where(lane < rd, jnp.where(lane < half, -sin, sin), 0.0)


def _rope_block(x, c, s, head_w):
    half = head_w // ROPE_FRACTION // 2
    first = (lax.broadcasted_iota(jnp.int32, x.shape, 1) & (head_w - 1)) < half
    partner = jnp.where(first, pltpu.roll(x, LANES - half, axis=1), pltpu.roll(x, half, axis=1))
    return x * c + partner * s


def _even_prep_kernel(x_ref, c_ref, s_ref, *refs, q_scale, n_blocks):
    out_refs, y_ref = refs[:-1], refs[-1]
    part = pl.program_id(1)
    tm = x_ref.shape[0]

    def rope(scale):
        c = c_ref[...]
        s = s_ref[...]
        for b in range(n_blocks):
            cols = slice(b * LANES, (b + 1) * LANES)
            y_ref[b] = _rope_block(x_ref[:, cols].astype(F32), c, s, HEAD_DIM) * scale

    @pl.when(part == 0)
    def _():
        rope(q_scale)

    @pl.when(part == 1)
    def _():
        rope(1.0)

    @pl.when(part == 2)
    def _():
        for b in range(n_blocks):
            y_ref[b] = x_ref[:, b * LANES:(b + 1) * LANES].astype(F32)

    for (_, dil), o_ref in zip(DILATED_CFGS, out_refs):
        for r in range(dil):
            for b in range(n_blocks):
                rows = y_ref[b, pl.ds(r, tm // dil, stride=dil), :]
                o_ref[0, r, :, b * LANES:(b + 1) * LANES] = rows.astype(BF16)


def _even_prep(proj, tables, batch, seq, q_scale, tm=1024):
    a_w = A_HEADS * HEAD_DIM
    cos_t, sin_t = tables
    spb = seq // tm
    kern = functools.partial(_even_prep_kernel, q_scale=q_scale, n_blocks=a_w // LANES)
    return pl.pallas_call(
        kern,
        grid=(batch * spb, 3),
        in_specs=[pl.BlockSpec((tm, a_w), lambda i, p: (i, p)),
                  pl.BlockSpec((tm, LANES), lambda i, p: (i % spb, 0)),
                  pl.BlockSpec((tm, LANES), lambda i, p: (i % spb, 0))],
        out_specs=[pl.BlockSpec((1, dil, tm // dil, a_w), lambda i, p: (i // spb, 0, i % spb, p))
                   for _, dil in DILATED_CFGS],
        out_shape=[jax.ShapeDtypeStruct((batch, dil, seq // dil, 3 * a_w), BF16) for _, dil in DILATED_CFGS],
        scratch_shapes=[pltpu.VMEM((a_w // LANES, tm, LANES), F32)],
        compiler_params=_params(("parallel", "arbitrary"), 32),
        name="even_prep",
    )(proj, cos_t, sin_t)


def _band_attn_kernel(q_ref, kp_ref, kc_ref, kn_ref, vp_ref, vc_ref, vn_ref, *refs, bq, sub_len, merge):
    if merge:
        o4, l4, o16, l16, y_ref, so4, sl4, so16, sl16 = refs
        for dil, src, dst in ((4, o4, so4), (4, l4, sl4), (16, o16, so16), (16, l16, sl16)):
            for r in range(dil):
                for h in range(A_HEADS):
                    dst[h, pl.ds(r, bq // dil, stride=dil), :] = (
                        src[0, r, :, h * HEAD_DIM:(h + 1) * HEAD_DIM].astype(F32))
    else:
        o_ref, lse_ref = refs
    q0 = pl.program_id(1) * bq
    sq = 2 * BAND_RADIUS
    win = sq + 2 * BAND_RADIUS
    n_sub = bq // sq
    rel = (lax.broadcasted_iota(jnp.int32, (sq, win), 1) - BAND_RADIUS
           - lax.broadcasted_iota(jnp.int32, (sq, win), 0))
    band = (rel <= BAND_RADIUS) & (rel >= -BAND_RADIUS)
    biases = []
    for u in range(n_sub):
        kpos = q0 + u * sq - BAND_RADIUS + lax.broadcasted_iota(jnp.int32, (sq, win), 1)
        ok = band & (kpos >= 0) & (kpos < sub_len)
        biases.append(jnp.where(ok, 0.0, NEG_BIG))

    def window(p_ref, c_ref, n_ref, cols, u):
        parts = []
        if u == 0:
            parts.append(p_ref[0, :, cols])
        lo = max(u * sq - BAND_RADIUS, 0)
        hi = min(u * sq + sq + BAND_RADIUS, bq)
        parts.append(c_ref[0, lo:hi, cols])
        if u == n_sub - 1:
            parts.append(n_ref[0, :, cols])
        return jnp.concatenate(parts, axis=0)

    tiles = [(h, u) for h in range(A_HEADS) for u in range(n_sub)]

    def scores(h, u):
        cols = slice(h * HEAD_DIM, (h + 1) * HEAD_DIM)
        q = q_ref[0, u * sq:(u + 1) * sq, cols]
        return _dot_nt(q, window(kp_ref, kc_ref, kn_ref, cols, u)) + biases[u]

    s_next = scores(*tiles[0])
    for n, (h, u) in enumerate(tiles):
        cols = slice(h * HEAD_DIM, (h + 1) * HEAD_DIM)
        rows = slice(u * sq, (u + 1) * sq)
        s = s_next
        if n + 1 < len(tiles):
            s_next = scores(*tiles[n + 1])
        m = jnp.max(s, axis=-1, keepdims=True)
        p = jnp.exp(s - m)
        l = jnp.sum(p, axis=-1, keepdims=True)
        o = jnp.dot(p.astype(BF16), window(vp_ref, vc_ref, vn_ref, cols, u), preferred_element_type=F32) / l
        lse = jnp.broadcast_to(m + jnp.log(l), (sq, HEAD_DIM))
        if merge:
            b, c = sl4[h, rows, :], sl16[h, rows, :]
            top = jnp.maximum(jnp.maximum(lse, b), c)
            wa, wb, wc = jnp.exp(lse - top), jnp.exp(b - top), jnp.exp(c - top)
            num = wa * o + wb * so4[h, rows, :] + wc * so16[h, rows, :]
            y_ref[0, rows, cols] = (num / (wa + wb + wc)).astype(y_ref.dtype)
        else:
            o_ref[0, rows, cols] = o.astype(o_ref.dtype)
            lse_ref[0, rows, cols] = lse


def _band_attention(qkv, dilated=None, bq=256):
    n_seq, sub_len, _ = qkv.shape
    a_w = A_HEADS * HEAD_DIM
    hb = bq // BAND_RADIUS
    last = sub_len // BAND_RADIUS - 1
    merge = dilated is not None
    kern = functools.partial(_band_attn_kernel, bq=bq, sub_len=sub_len, merge=merge)

    def halo_specs(col):
        return [pl.BlockSpec((1, BAND_RADIUS, a_w), lambda s, i: (s, jnp.maximum(i * hb - 1, 0), col)),
                pl.BlockSpec((1, bq, a_w), lambda s, i: (s, i, col)),
                pl.BlockSpec((1, BAND_RADIUS, a_w), lambda s, i: (s, jnp.minimum((i + 1) * hb, last), col))]

    out_spec = pl.BlockSpec((1, bq, a_w), lambda s, i: (s, i, 0))
    in_specs = [pl.BlockSpec((1, bq, a_w), lambda s, i: (s, i, 0))] + halo_specs(1) + halo_specs(2)
    operands = [qkv] * 7
    if merge:
        for (_, dil), pair in zip(DILATED_CFGS[1:], dilated):
            for arr in pair:
                operands.append(arr.reshape(n_seq, dil, sub_len // dil, a_w))
                in_specs.append(pl.BlockSpec((1, dil, bq // dil, a_w), lambda s, i: (s, 0, i, 0)))
        out_specs, out_shape = out_spec, jax.ShapeDtypeStruct((n_seq, sub_len, a_w), BF16)
        scratch = [pltpu.VMEM((A_HEADS, bq, HEAD_DIM), F32)] * 4
    else:
        out_specs = [out_spec, out_spec]
        out_shape = [jax.ShapeDtypeStruct((n_seq, sub_len, a_w), BF16),
                     jax.ShapeDtypeStruct((n_seq, sub_len, a_w), F32)]
        scratch = []
    return pl.pallas_call(
        kern,
        grid=(n_seq, sub_len // bq),
        in_specs=in_specs,
        out_specs=out_specs,
        out_shape=out_shape,
        scratch_shapes=scratch,
        compiler_params=_params(("parallel", "parallel"), 40),
        name="band_attn_merge" if merge else "band_attn",
    )(*operands)


def _gating_kernel(u_ref, v_ref, w_ref, b_ref, g_ref, o_ref, *, n_chunks):
    w = w_ref[0]
    bias = b_ref[0]
    gain = g_ref[0]
    for c in range(n_chunks):
        rows = pl.ds(c * CHUNK, CHUNK)
        v = jax.nn.gelu(v_ref[rows, :].astype(F32))
        mu = jnp.mean(v, axis=-1, keepdims=True)
        d = v - mu
        var = jnp.mean(d * d, axis=-1, keepdims=True)
        vn = d * lax.rsqrt(var + EPS) * gain
        sv = jnp.dot(w, vn.astype(BF16), preferred_element_type=F32) + bias
        o_ref[rows, :] = (jax.nn.gelu(u_ref[rows, :].astype(F32)) * sv).astype(o_ref.dtype)


def _spatial_gating(proj, w_s, b_s, g_ln, u_col, v_col, tm=2048):
    t = proj.shape[0]
    tm = min(tm, t)
    groups = w_s.shape[0]
    bias = jnp.broadcast_to(b_s[:, :, None], (groups, CHUNK, LANES))
    kern = functools.partial(_gating_kernel, n_chunks=tm // CHUNK)
    return pl.pallas_call(
        kern,
        grid=(t // tm, groups),
        in_specs=[pl.BlockSpec((tm, LANES), lambda i, g: (i, u_col + g)),
                  pl.BlockSpec((tm, LANES), lambda i, g: (i, v_col + g)),
                  pl.BlockSpec((1, CHUNK, CHUNK), lambda i, g: (g, 0, 0)),
                  pl.BlockSpec((1, CHUNK, LANES), lambda i, g: (g, 0, 0)),
                  pl.BlockSpec((1, 1, LANES), lambda i, g: (g, 0, 0))],
        out_specs=pl.BlockSpec((tm, LANES), lambda i, g: (i, g)),
        out_shape=jax.ShapeDtypeStruct((t, groups * LANES), BF16),
        compiler_params=_params(("parallel", "parallel"), 32),
        name="spatial_gating",
    )(proj, proj, w_s.astype(BF16), bias, g_ln.reshape(groups, 1, LANES))


V_ROWS = 2 * C_HALF + 16
DIFF_BK = 512


def _diff_prep_kernel(x_ref, c_ref, s_ref, qt_ref, k_ref, vt_ref, *, q_scale, n_blocks):
    part = pl.program_id(1)
    tm = x_ref.shape[0]

    @pl.when(part == 0)
    def _():
        c = c_ref[...]
        s = s_ref[...]
        for h in range(n_blocks):
            cols = slice(h * LANES, (h + 1) * LANES)
            qt_ref[0, h] = (_rope_block(x_ref[:, cols].astype(F32), c, s, C_HALF) * q_scale).T.astype(BF16)

    @pl.when(part == 1)
    def _():
        c = c_ref[...]
        s = s_ref[...]
        for h in range(n_blocks):
            cols = slice(h * LANES, (h + 1) * LANES)
            k_ref[:, cols] = _rope_block(x_ref[:, cols].astype(F32), c, s, C_HALF).astype(BF16)

    @pl.when(part == 2)
    def _():
        ones = jnp.ones((V_ROWS - LANES, tm), BF16)
        for h in range(n_blocks):
            vt_ref[0, h, 0, 0:LANES, :] = x_ref[:, h * LANES:(h + 1) * LANES].astype(F32).T.astype(BF16)
            vt_ref[0, h, 0, LANES:, :] = ones


def _diff_prep(proj, tables, batch, seq, q_scale, bk, tm=512):
    c_w = C_HEADS * 2 * C_HALF
    cos_t, sin_t = tables
    spb = seq // tm
    per_chunk = bk // tm
    kern = functools.partial(_diff_prep_kernel, q_scale=q_scale, n_blocks=C_HEADS)
    return pl.pallas_call(
        kern,
        grid=(batch * spb, 3),
        in_specs=[pl.BlockSpec((tm, c_w), lambda i, p: (i, p)),
                  pl.BlockSpec((tm, LANES), lambda i, p: (i % spb, 0)),
                  pl.BlockSpec((tm, LANES), lambda i, p: (i % spb, 0))],
        out_specs=[pl.BlockSpec((1, C_HEADS, LANES, tm), lambda i, p: (i // spb, 0, 0, i % spb)),
                   pl.BlockSpec((tm, c_w), lambda i, p: (i, 0)),
                   pl.BlockSpec((1, C_HEADS, 1, V_ROWS, tm),
                                lambda i, p: (i // spb, 0, (i % spb) // per_chunk, 0, (i % spb) % per_chunk))],
        out_shape=[jax.ShapeDtypeStruct((batch, C_HEADS, LANES, seq), BF16),
                   jax.ShapeDtypeStruct((batch * seq, c_w), BF16),
                   jax.ShapeDtypeStruct((batch, C_HEADS, seq // bk, V_ROWS, bk), BF16)],
        compiler_params=_params(("parallel", "arbitrary"), 40),
        name="diff_prep",
    )(proj, cos_t, sin_t)


def _diff_attn_kernel(qt_ref, k_ref, vt_ref, lam_ref, g_ref, o_ref, s_ref, mc_ref, m_ref, acc_ref,
                      *, bq, bk, seq, lam_init, unroll):
    n = seq // bk
    sub = SUBLANES
    qt = qt_ref[0, 0]
    row = lax.broadcasted_iota(jnp.int32, qt.shape, 0)
    zero = jnp.zeros_like(qt)
    qs = (jnp.where(row < C_HALF, qt, zero), jnp.where(row < C_HALF, zero, qt))

    def scores(j, slot):
        kc = k_ref[0, pl.ds(pl.multiple_of(j * bk, bk), bk), :]
        for br in range(2):
            s = jnp.dot(kc, qs[br], preferred_element_type=F32)
            s_ref[slot, br] = s
            mc = jnp.max(s.reshape(bk // sub, sub, bq), axis=0)
            mc_ref[slot, br] = jnp.broadcast_to(jnp.max(mc, axis=0, keepdims=True), (sub, bq))

    def accumulate(j, slot):
        vt = vt_ref[0, 0, j]
        for br in range(2):
            m_old = m_ref[br]
            m_new = jnp.maximum(m_old, mc_ref[slot, br])
            alpha = jnp.exp2(m_old - m_new)
            m_ref[br] = m_new
            s3 = s_ref[slot, br].reshape(bk // sub, sub, bq)
            p = jnp.exp2(s3 - m_new[None]).reshape(bk, bq).astype(BF16)
            pv = jnp.dot(vt, p, preferred_element_type=F32)
            acc3 = acc_ref[br].reshape(V_ROWS // sub, sub, bq)
            acc_ref[br] = (alpha[None] * acc3).reshape(V_ROWS, bq) + pv

    m_ref[...] = jnp.full(m_ref.shape, NEG_BIG, F32)
    acc_ref[...] = jnp.zeros(acc_ref.shape, F32)
    scores(0, 0)

    def trip(j, last):
        for u in range(unroll):
            if not (last and u == unroll - 1):
                scores(j + u + 1, (u + 1) % 2)
            accumulate(j + u, u % 2)

    def body(i, carry):
        trip(unroll * i, False)
        return carry

    lax.fori_loop(0, n // unroll - 1, body, 0)
    trip(n - unroll, True)

    lam = lam_ref[...]
    lam_full = (jnp.exp(jnp.sum(lam[0:1] * lam[1:2], keepdims=True))
                - jnp.exp(jnp.sum(lam[2:3] * lam[3:4], keepdims=True)) + lam_init)

    def normalised(a):
        num = a[0:LANES].reshape(LANES // sub, sub, bq)
        return (num / a[LANES:LANES + sub][None]).reshape(LANES, bq)

    o = normalised(acc_ref[0]) - lam_full * normalised(acc_ref[1])
    y = o * lax.rsqrt(jnp.mean(o * o, axis=0, keepdims=True) + EPS) * g_ref[...] * (1.0 - lam_init)
    o_ref[0] = y.T.astype(o_ref.dtype)


def _diff_attention(qt, k, vt, lam, subln_g, batch, seq, layer_idx, bq=512, bk=512, unroll=8):
    c_w = C_HEADS * 2 * C_HALF
    lam_init = 0.8 - 0.6 * math.exp(-0.3 * layer_idx)
    n = seq // bk
    unroll = min(unroll, n)
    assert unroll % 2 == 0 and n % unroll == 0 and vt.shape[2] == n
    kern = functools.partial(_diff_attn_kernel, bq=bq, bk=bk, seq=seq, lam_init=lam_init, unroll=unroll)
    out = pl.pallas_call(
        kern,
        grid=(batch, C_HEADS, seq // bq),
        in_specs=[pl.BlockSpec((1, 1, LANES, bq), lambda b, h, i: (b, h, 0, i)),
                  pl.BlockSpec((1, seq, LANES), lambda b, h, i: (b, 0, h)),
                  pl.BlockSpec((1, 1, n, V_ROWS, bk), lambda b, h, i: (b, h, 0, 0, 0)),
                  pl.BlockSpec((4, C_HALF), lambda b, h, i: (0, 0)),
                  pl.BlockSpec((LANES, bq), lambda b, h, i: (0, 0))],
        out_specs=pl.BlockSpec((1, bq, LANES), lambda b, h, i: (b, i, h)),
        out_shape=jax.ShapeDtypeStruct((batch, seq, c_w), BF16),
        scratch_shapes=[pltpu.VMEM((2, 2, bk, bq), F32),
                        pltpu.VMEM((2, 2, SUBLANES, bq), F32),
                        pltpu.VMEM((2, SUBLANES, bq), F32),
                        pltpu.VMEM((2, V_ROWS, bq), F32)],
        compiler_params=_params(("parallel", "parallel", "arbitrary"), 48),
        name="diff_attn",
    )(qt, k.reshape(batch, seq, c_w), vt, lam, jnp.broadcast_to(subln_g[:, None], (LANES, bq)))
    return out.reshape(batch * seq, c_w)


def _fourier_wprep_kernel(c_ref, s_ref, w_ref, o_ref, *, scale):
    w = w_ref[0]
    mr = jnp.dot(c_ref[...], w, preferred_element_type=F32, precision=lax.Precision.HIGHEST)
    mi = jnp.dot(s_ref[...], w, preferred_element_type=F32, precision=lax.Precision.HIGHEST)
    o_ref[0, :, :LANES] = (mr * scale).astype(BF16)
    o_ref[0, :, LANES:] = (-mi * scale).astype(BF16)


def _fourier_weights(w_f, seq):
    groups, c, _ = w_f.shape
    idx = jnp.arange(c, dtype=jnp.int32)
    ang = (2.0 * math.pi / c) * ((idx[:, None] * idx[None, :]) % c).astype(F32)
    kern = functools.partial(_fourier_wprep_kernel, scale=1.0 / math.sqrt(seq * c))
    return pl.pallas_call(
        kern,
        grid=(groups,),
        in_specs=[pl.BlockSpec((c, c), lambda g: (0, 0)),
                  pl.BlockSpec((c, c), lambda g: (0, 0)),
                  pl.BlockSpec((1, c, c), lambda g: (g, 0, 0))],
        out_specs=pl.BlockSpec((1, c, 2 * c), lambda g: (g, 0, 0)),
        out_shape=jax.ShapeDtypeStruct((groups, c, 2 * c), BF16),
        compiler_params=_params(("parallel",), 32),
        name="fourier_wprep",
    )(jnp.cos(ang), jnp.sin(ang), w_f)


def _chan_mix_kernel(z_ref, m_ref, o_ref):
    o_ref[...] = jnp.dot(z_ref[...], m_ref[0], preferred_element_type=F32).astype(BF16)


def _fourier_channel_mix(proj, m_c, z_col, tm=2048):
    t = proj.shape[0]
    tm = min(tm, t)
    groups = m_c.shape[0]
    return pl.pallas_call(
        _chan_mix_kernel,
        grid=(t // tm, groups),
        in_specs=[pl.BlockSpec((tm, LANES), lambda i, g: (i, z_col + g)),
                  pl.BlockSpec((1, LANES, 2 * LANES), lambda i, g: (g, 0, 0))],
        out_specs=pl.BlockSpec((tm, 2 * LANES), lambda i, g: (i, g)),
        out_shape=jax.ShapeDtypeStruct((t, groups * 2 * LANES), BF16),
        compiler_params=_params(("parallel", "parallel"), 32),
        name="fourier_channel_mix",
    )(proj, m_c)


def _dft_stage1_kernel(u_ref, fr_ref, fi_ref, y_ref, *, pairs):
    u = u_ref[0]
    p = jnp.dot(fr_ref[...], u, preferred_element_type=F32)
    q = jnp.dot(fi_ref[...], u, preferred_element_type=F32)
    for j in range(pairs):
        re = slice(2 * j * LANES, (2 * j + 1) * LANES)
        im = slice((2 * j + 1) * LANES, (2 * j + 2) * LANES)
        y_ref[0, :, re] = (p[:, re] - q[:, im]).astype(BF16)
        y_ref[0, :, im] = (q[:, re] + p[:, im]).astype(BF16)


def _dft_stage2_kernel(y_ref, gc_ref, gs_ref, o_ref, *, groups):
    for t in range(y_ref.shape[1]):
        y = y_ref[0, t]
        a = jnp.dot(gc_ref[t], y, preferred_element_type=F32)
        b = jnp.dot(gs_ref[t], y, preferred_element_type=F32)
        for g in range(groups):
            re = slice(2 * g * LANES, (2 * g + 1) * LANES)
            im = slice((2 * g + 1) * LANES, (2 * g + 2) * LANES)
            out_cols = slice((t * groups + g) * LANES, (t * groups + g + 1) * LANES)
            o_ref[0, :, out_cols] = (a[:, re] + b[:, im]).astype(o_ref.dtype)


def _seq_dft_real(u, batch, seq, groups, tc=4096, kb=4):
    n1 = 128 if seq >= 16384 else 64
    n2 = seq // n1
    wc = groups * 2 * LANES
    i1 = jnp.arange(n1, dtype=jnp.int32)
    ang1 = (2.0 * math.pi / n1) * ((i1[:, None] * i1[None, :]) % n1).astype(F32)
    fr = jnp.cos(ang1).astype(BF16)
    fi = (-jnp.sin(ang1)).astype(BF16)
    cols = n2 * wc
    tc = min(tc, cols)
    y = pl.pallas_call(
        functools.partial(_dft_stage1_kernel, pairs=tc // (2 * LANES)),
        grid=(batch, cols // tc),
        in_specs=[pl.BlockSpec((1, n1, tc), lambda b, j: (b, 0, j)),
                  pl.BlockSpec((n1, n1), lambda b, j: (0, 0)),
                  pl.BlockSpec((n1, n1), lambda b, j: (0, 0))],
        out_specs=pl.BlockSpec((1, n1, tc), lambda b, j: (b, 0, j)),
        out_shape=jax.ShapeDtypeStruct((batch, n1, cols), BF16),
        compiler_params=_params(("parallel", "parallel"), 32),
        name="dft_stage1",
    )(u.reshape(batch, n1, cols), fr, fi)
    i2 = jnp.arange(n2, dtype=jnp.int32)
    tw = (i2[None, None, :] * (i1[:, None, None] + n1 * i2[None, :, None])) % seq
    ang2 = (2.0 * math.pi / seq) * tw.astype(F32)
    gc = jnp.cos(ang2).astype(BF16)
    gs = jnp.sin(ang2).astype(BF16)
    out = pl.pallas_call(
        functools.partial(_dft_stage2_kernel, groups=groups),
        grid=(batch, n1 // kb),
        in_specs=[pl.BlockSpec((1, kb, n2, wc), lambda b, k: (b, k, 0, 0)),
                  pl.BlockSpec((kb, n2, n2), lambda b, k: (k, 0, 0)),
                  pl.BlockSpec((kb, n2, n2), lambda b, k: (k, 0, 0))],
        out_specs=pl.BlockSpec((1, n2, kb * groups * LANES), lambda b, k: (b, 0, k)),
        out_shape=jax.ShapeDtypeStruct((batch, n2, n1 * groups * LANES), BF16),
        compiler_params=_params(("parallel", "parallel"), 32),
        name="dft_stage2",
    )(y.reshape(batch, n1, n2, wc), gc, gs)
    return out.reshape(batch * seq, groups * LANES)


def _mix_out_cross_attn_kernel(mixa_ref, mixb_ref, x_ref, wout_ref, g1_ref, g2_ref, wq_ref, kv_ref, wo_ref, g3_ref,
                               o_ref, *, heads, scale, halves):
    ca_w = heads * HEAD_DIM
    hr = x_ref.shape[0] // halves
    ka = mixa_ref.shape[1]
    rows = [slice(r * hr, (r + 1) * hr) for r in range(halves)]
    ys = [jnp.dot(mixa_ref[r, :], wout_ref[0:ka, :], preferred_element_type=F32)
          + jnp.dot(mixb_ref[r, :], wout_ref[ka:, :], preferred_element_type=F32) for r in rows]
    x1s = [x_ref[r, :] + _rms(y, g1_ref[...]) for r, y in zip(rows, ys)]
    qs = [jnp.dot(_rms(x1, g2_ref[...]).astype(BF16), wq_ref[...], preferred_element_type=F32).astype(BF16)
          for x1 in x1s]
    cas = []
    for q in qs:
        outs = []
        for h in range(heads):
            cols = slice(h * HEAD_DIM, (h + 1) * HEAD_DIM)
            s = _dot_nt(q[:, cols], kv_ref[0, :, cols]) * scale
            m = jnp.max(s, axis=-1, keepdims=True)
            p = jnp.exp(s - m)
            l = jnp.sum(p, axis=-1, keepdims=True)
            v = kv_ref[0, :, ca_w + h * HEAD_DIM:ca_w + (h + 1) * HEAD_DIM]
            outs.append((jnp.dot(p.astype(BF16), v, preferred_element_type=F32) / l).astype(BF16))
        cas.append(jnp.concatenate(outs, axis=1))
    zs = [jnp.dot(ca, wo_ref[...], preferred_element_type=F32) for ca in cas]
    for r, x1, z in zip(rows, x1s, zs):
        o_ref[r, :] = x1 + _rms(z, g3_ref[...])


def _mix_out_cross_attn(mix_a, mix_b, x, w_out, w_q, w_o, layer, g1, g2, g3, kv, batch, seq, tm=512):
    t, d = x.shape
    ka, kb = mix_a.shape[1], mix_b.shape[1]
    assert ka + kb == w_out.shape[1]
    ca_w = CA_HEADS * HEAD_DIM
    mem_len = kv.shape[0] // batch
    spb = seq // tm
    kern = functools.partial(_mix_out_cross_attn_kernel, heads=CA_HEADS, scale=HEAD_DIM ** -0.5, halves=2)
    once = pl.Buffered(1)
    vec = lambda: pl.BlockSpec((1, d), lambda i: (0, 0))
    return pl.pallas_call(
        kern,
        grid=(t // tm,),
        in_specs=[pl.BlockSpec((tm, ka), lambda i: (i, 0)),
                  pl.BlockSpec((tm, kb), lambda i: (i, 0)),
                  pl.BlockSpec((tm, d), lambda i: (i, 0)),
                  pl.BlockSpec((None, d, d), lambda i: (layer, 0, 0), pipeline_mode=once),
                  vec(), vec(),
                  pl.BlockSpec((None, d, ca_w), lambda i: (layer, 0, 0), pipeline_mode=once),
                  pl.BlockSpec((1, mem_len, 2 * ca_w), lambda i: (i // spb, 0, 0)),
                  pl.BlockSpec((None, ca_w, d), lambda i: (layer, 0, 0), pipeline_mode=once),
                  vec()],
        out_specs=pl.BlockSpec((tm, d), lambda i: (i, 0)),
        out_shape=jax.ShapeDtypeStruct((t, d), F32),
        compiler_params=_params(("parallel",), 56),
        name="mix_out_cross_attn",
    )(mix_a, mix_b, x, w_out, g1.reshape(1, d), g2.reshape(1, d), w_q, kv.reshape(batch, mem_len, 2 * ca_w), w_o,
      g3.reshape(1, d))


HALO = 16


def _ffn_kernel(x_ref, xp_ref, xn_ref, g4_ref, wg_ref, wv_ref, cwg_ref, cwv_ref, cbg_ref, cbv_ref,
                wd_ref, g5_ref, o_ref, xe_ref, zg_ref, zv_ref, *, tm, tc, tiles_per_seq, nf):
    acc_ref = o_ref
    i = pl.program_id(0)
    f = pl.program_id(1)

    @pl.when(f == 0)
    def _():
        g4 = g4_ref[...]
        pos = i % tiles_per_seq
        prev = _rms(xp_ref[...], g4) * jnp.where(pos == 0, 0.0, 1.0)
        nxt = _rms(xn_ref[...], g4) * jnp.where(pos == tiles_per_seq - 1, 0.0, 1.0)
        xe_ref[0:HALO, :] = prev.astype(BF16)
        xe_ref[HALO:HALO + tm, :] = _rms(x_ref[...], g4).astype(BF16)
        xe_ref[HALO + tm:, :] = nxt.astype(BF16)
        acc_ref[...] = jnp.zeros(acc_ref.shape, F32)

    def conv(z_ref, c, cw, cb, lo, rows):
        lo = HALO + lo
        return (z_ref[c, lo - 1:lo - 1 + rows, :] * cw[0:1] + z_ref[c, lo:lo + rows, :] * cw[1:2]
                + z_ref[c, lo + 1:lo + 1 + rows, :] * cw[2:3] + cb)

    xe = xe_ref[...]
    n_sub = zg_ref.shape[0]
    for c in range(n_sub):
        cols = slice(c * tc, (c + 1) * tc)
        zg_ref[c] = jnp.dot(xe, wg_ref[:, cols], preferred_element_type=F32)
        zv_ref[c] = jnp.dot(xe, wv_ref[:, cols], preferred_element_type=F32)
    for c in range(n_sub):
        cols = slice(c * tc, (c + 1) * tc)
        pieces = 2 if c == n_sub - 1 else 1
        rows = tm // pieces
        for r in range(pieces):
            lo = r * rows
            h = (jax.nn.gelu(conv(zg_ref, c, cwg_ref[:, cols], cbg_ref[:, cols], lo, rows))
                 * conv(zv_ref, c, cwv_ref[:, cols], cbv_ref[:, cols], lo, rows))
            acc_ref[lo:lo + rows, :] += jnp.dot(h.astype(BF16), wd_ref[cols, :], preferred_element_type=F32)

    @pl.when(f == nf - 1)
    def _():
        o_ref[...] = x_ref[...] + _rms(acc_ref[...], g5_ref[...])


def _conv_ffn(x, seq, g4, w_up, conv_w, conv_b, w_down, layer, g5, tm=1024, tf=512, tc=256):
    t, d = x.shape
    d_ff = w_down.shape[1]
    nf = d_ff // tf
    hb = tm // HALO
    last_hb = t // HALO - 1
    kern = functools.partial(_ffn_kernel, tm=tm, tc=tc, tiles_per_seq=seq // tm, nf=nf)
    cb = conv_b.reshape(1, 2 * d_ff)
    return pl.pallas_call(
        kern,
        grid=(t // tm, nf),
        in_specs=[pl.BlockSpec((tm, d), lambda i, f: (i, 0), pipeline_mode=pl.Buffered(1)),
                  pl.BlockSpec((HALO, d), lambda i, f: (jnp.maximum(i * hb - 1, 0), 0)),
                  pl.BlockSpec((HALO, d), lambda i, f: (jnp.minimum((i + 1) * hb, last_hb), 0)),
                  pl.BlockSpec((1, d), lambda i, f: (0, 0)),
                  pl.BlockSpec((None, d, tf), lambda i, f: (layer, 0, f)),
                  pl.BlockSpec((None, d, tf), lambda i, f: (layer, 0, nf + f)),
                  pl.BlockSpec((CONV_W, tf), lambda i, f: (0, f)),
                  pl.BlockSpec((CONV_W, tf), lambda i, f: (0, nf + f)),
                  pl.BlockSpec((1, tf), lambda i, f: (0, f)),
                  pl.BlockSpec((1, tf), lambda i, f: (0, nf + f)),
                  pl.BlockSpec((None, tf, d), lambda i, f: (layer, f, 0)),
                  pl.BlockSpec((1, d), lambda i, f: (0, 0))],
        out_specs=pl.BlockSpec((tm, d), lambda i, f: (i, 0)),
        out_shape=jax.ShapeDtypeStruct((t, d), F32),
        scratch_shapes=[pltpu.VMEM((tm + 2 * HALO, d), BF16),
                        pltpu.VMEM((tf // tc, tm + 2 * HALO, tc), F32),
                        pltpu.VMEM((tf // tc, tm + 2 * HALO, tc), F32)],
        compiler_params=_params(("parallel", "arbitrary"), 58),
        name="conv_ffn",
    )(x, x, x, g4.reshape(1, d), w_up, w_up, conv_w, conv_w, cb, cb, w_down, g5.reshape(1, d))


def _trunk(x3, mem3, p):
    batch, seq, d = x3.shape
    x = x3.reshape(batch * seq, d)
    mem = mem3.reshape(-1, d)
    depth = p["w_in"].shape[0]
    a_w = A_HEADS * HEAD_DIM
    c_w = C_HEADS * 2 * C_HALF
    for layer in range(depth):
        g = p["norm_gains"][layer]
        proj = _norm_matmul(x, g[0], p["w_in"], layer, BF16)
        if layer % 2 == 0:
            e = layer // 2
            qkvs = _even_prep(proj, p["rope"][HEAD_DIM], batch, seq, HEAD_DIM ** -0.5)
            seqs = [qkv.reshape(batch * dil, seq // dil, 3 * a_w) for (_, dil), qkv in zip(DILATED_CFGS, qkvs)]
            dilated = [_band_attention(s) for s in seqs[1:]]
            ya = _band_attention(seqs[0], dilated).reshape(batch * seq, a_w)
            yb = _spatial_gating(proj, p["b_w_spatial"][e], p["b_b_spatial"][e], p["b_ln_gain"][e],
                                 3 * A_HEADS, 3 * A_HEADS + B_GROUPS)
            mix = (ya, yb)
        else:
            o = layer // 2
            qt, kk, vt = _diff_prep(proj, p["rope"][C_HALF], batch, seq, C_HALF ** -0.5 * math.log2(math.e),
                                    DIFF_BK)
            yc = _diff_attention(qt, kk, vt, p["c_lambda"][o], p["c_subln_gain"][o], batch, seq, layer,
                                 bk=DIFF_BK)
            m_c = _fourier_weights(p["d_w_fourier"][o], seq)
            u = _fourier_channel_mix(proj, m_c, 3 * c_w // LANES)
            yd = _seq_dft_real(u, batch, seq, D_GROUPS)
            mix = (yc, yd)
        kv = _norm_matmul(mem, p["mem_norm_gain"][layer], p["ca_w_kv"], layer, BF16)
        x = _mix_out_cross_attn(*mix, x, p["w_out"], p["ca_w_q"], p["ca_w_o"], layer, g[1], g[2], g[3], kv,
                                batch, seq)
        x = _conv_ffn(x, seq, g[4], p["ffn_w_up"], p["ffn_conv_w"][layer], p["ffn_conv_b"][layer],
                      p["ffn_w_down"], layer, g[5])
    return x.reshape(batch, seq, d)


def kernel(x_prompt, x_sample, mem_prompt, mem_sample, norm_gains, w_in, w_out, b_w_spatial, b_b_spatial,
           b_ln_gain, c_lambda, c_subln_gain, d_w_fourier, mem_norm_gain, ca_w_q, ca_w_kv, ca_w_o, ffn_w_up,
           ffn_conv_w, ffn_conv_b, ffn_w_down):
    p = dict(norm_gains=norm_gains, w_in=w_in.astype(BF16), w_out=w_out.astype(BF16), b_w_spatial=b_w_spatial,
             b_b_spatial=b_b_spatial, b_ln_gain=b_ln_gain, c_lambda=c_lambda, c_subln_gain=c_subln_gain,
             d_w_fourier=d_w_fourier, mem_norm_gain=mem_norm_gain, ca_w_q=ca_w_q.astype(BF16),
             ca_w_kv=ca_w_kv.astype(BF16), ca_w_o=ca_w_o.astype(BF16), ffn_w_up=ffn_w_up.astype(BF16),
             ffn_conv_w=ffn_conv_w, ffn_conv_b=ffn_conv_b, ffn_w_down=ffn_w_down.astype(BF16))
    seq_max = max(x_prompt.shape[1], x_sample.shape[1])
    p["rope"] = {w: _rope_tables(seq_max, w) for w in (HEAD_DIM, C_HALF)}
    return _trunk(x_prompt, mem_prompt, p), _trunk(x_sample, mem_sample, p)
```

```python
import functools
import math

import jax
import jax.numpy as jnp
from jax import lax
from jax.experimental import pallas as pl
from jax.experimental.pallas import tpu as pltpu

F32 = jnp.float32
BF16 = jnp.bfloat16

EPS = 1e-6
ROPE_THETA = 500000.0
ROPE_FRACTION = 4
LANES = 128
SUBLANES = 8
HEAD_DIM = 128
A_HEADS = 8
DILATED_CFGS = ((128, 1), (512, 4), (2048, 16))
BAND_RADIUS = 64
assert all(window // (2 * dil) == BAND_RADIUS for window, dil in DILATED_CFGS)
B_GROUPS = 8
CHUNK = 128
C_HEADS = 12
C_HALF = 64
D_GROUPS = 4
CA_HEADS = 4
CONV_W = 3
NEG_BIG = -1e30
MIB = 1024 * 1024


def _params(semantics, vmem_mib):
    return pltpu.CompilerParams(dimension_semantics=semantics, vmem_limit_bytes=vmem_mib * MIB)


def _rms(x, g):
    return x * lax.rsqrt(jnp.mean(x * x, axis=-1, keepdims=True) + EPS) * g


def _dot_nt(a, b):
    return lax.dot_general(a, b, (((1,), (1,)), ((), ())), preferred_element_type=F32)


def _norm_mm_kernel(x_ref, g_ref, w_ref, o_ref, xn_ref):
    j = pl.program_id(1)

    @pl.when(j == 0)
    def _():
        half = x_ref.shape[0] // 2
        for r in (slice(0, half), slice(half, 2 * half)):
            xn_ref[r, :] = _rms(x_ref[r, :], g_ref[...]).astype(BF16)
            o_ref[r, :] = jnp.dot(xn_ref[r, :], w_ref[...], preferred_element_type=F32).astype(o_ref.dtype)

    @pl.when(j > 0)
    def _():
        o_ref[...] = jnp.dot(xn_ref[...], w_ref[...], preferred_element_type=F32).astype(o_ref.dtype)


def _norm_matmul(x, g, w, layer, out_dtype, tm=1024, tn=1024):
    t, k = x.shape
    n = w.shape[2]
    tm = min(tm, t)
    return pl.pallas_call(
        _norm_mm_kernel,
        grid=(t // tm, n // tn),
        in_specs=[pl.BlockSpec((tm, k), lambda i, j: (i, 0)),
                  pl.BlockSpec((1, k), lambda i, j: (0, 0)),
                  pl.BlockSpec((None, k, tn), lambda i, j: (layer, 0, j))],
        out_specs=pl.BlockSpec((tm, tn), lambda i, j: (i, j)),
        out_shape=jax.ShapeDtypeStruct((t, n), out_dtype),
        scratch_shapes=[pltpu.VMEM((tm, k), BF16)],
        compiler_params=_params(("parallel", "arbitrary"), 40),
        name="norm_matmul",
    )(x, g.reshape(1, k), w)


def _rope_tables(seq, head_w):
    rd = head_w // ROPE_FRACTION
    half = rd // 2
    inv = ROPE_THETA ** (-jnp.arange(half, dtype=F32) / half)
    ang = jnp.arange(seq, dtype=F32)[:, None] * inv[None, :]
    reps = LANES // half
    cos = jnp.broadcast_to(jnp.cos(ang)[:, None, :], (seq, reps, half)).reshape(seq, LANES)
    sin = jnp.broadcast_to(jnp.sin(ang)[:, None, :], (seq, reps, half)).reshape(seq, LANES)
    lane = (jnp.arange(LANES, dtype=jnp.int32) & (head_w - 1))[None, :]
    return jnp.where(lane < rd, cos, 1.0), jnp.where(lane < rd, jnp.where(lane < half, -sin, sin), 0.0)


def _rope_block(x, c, s, head_w):
    half = head_w // ROPE_FRACTION // 2
    first = (lax.broadcasted_iota(jnp.int32, x.shape, 1) & (head_w - 1)) < half
    partner = jnp.where(first, pltpu.roll(x, LANES - half, axis=1), pltpu.roll(x, half, axis=1))
    return x * c + partner * s


def _even_prep_kernel(x_ref, c_ref, s_ref, *refs, q_scale, n_blocks):
    out_refs, y_ref = refs[:-1], refs[-1]
    part = pl.program_id(1)
    tm = x_ref.shape[0]

    def rope(scale):
        c = c_ref[...]
        s = s_ref[...]
        for b in range(n_blocks):
            cols = slice(b * LANES, (b + 1) * LANES)
            y_ref[b] = _rope_block(x_ref[:, cols].astype(F32), c, s, HEAD_DIM) * scale

    @pl.when(part == 0)
    def _():
        rope(q_scale)

    @pl.when(part == 1)
    def _():
        rope(1.0)

    @pl.when(part == 2)
    def _():
        for b in range(n_blocks):
            y_ref[b] = x_ref[:, b * LANES:(b + 1) * LANES].astype(F32)

    for (_, dil), o_ref in zip(DILATED_CFGS, out_refs):
        for r in range(dil):
            for b in range(n_blocks):
                rows = y_ref[b, pl.ds(r, tm // dil, stride=dil), :]
                o_ref[0, r, :, b * LANES:(b + 1) * LANES] = rows.astype(BF16)


def _even_prep(proj, tables, batch, seq, q_scale, tm=1024):
    a_w = A_HEADS * HEAD_DIM
    cos_t, sin_t = tables
    spb = seq // tm
    kern = functools.partial(_even_prep_kernel, q_scale=q_scale, n_blocks=a_w // LANES)
    return pl.pallas_call(
        kern,
        grid=(batch * spb, 3),
        in_specs=[pl.BlockSpec((tm, a_w), lambda i, p: (i, p)),
                  pl.BlockSpec((tm, LANES), lambda i, p: (i % spb, 0)),
                  pl.BlockSpec((tm, LANES), lambda i, p: (i % spb, 0))],
        out_specs=[pl.BlockSpec((1, dil, tm // dil, a_w), lambda i, p: (i // spb, 0, i % spb, p))
                   for _, dil in DILATED_CFGS],
        out_shape=[jax.ShapeDtypeStruct((batch, dil, seq // dil, 3 * a_w), BF16) for _, dil in DILATED_CFGS],
        scratch_shapes=[pltpu.VMEM((a_w // LANES, tm, LANES), F32)],
        compiler_params=_params(("parallel", "arbitrary"), 32),
        name="even_prep",
    )(proj, cos_t, sin_t)


def _band_attn_kernel(q_ref, kp_ref, kc_ref, kn_ref, vp_ref, vc_ref, vn_ref, *refs, bq, sub_len, merge):
    if merge:
        o4, l4, o16, l16, y_ref, so4, sl4, so16, sl16 = refs
        for dil, src, dst in ((4, o4, so4), (4, l4, sl4), (16, o16, so16), (16, l16, sl16)):
            for r in range(dil):
                for h in range(A_HEADS):
                    dst[h, pl.ds(r, bq // dil, stride=dil), :] = (
                        src[0, r, :, h * HEAD_DIM:(h + 1) * HEAD_DIM].astype(F32))
    else:
        o_ref, lse_ref = refs
    q0 = pl.program_id(1) * bq
    sq = 2 * BAND_RADIUS
    win = sq + 2 * BAND_RADIUS
    n_sub = bq // sq
    rel = (lax.broadcasted_iota(jnp.int32, (sq, win), 1) - BAND_RADIUS
           - lax.broadcasted_iota(jnp.int32, (sq, win), 0))
    band = (rel <= BAND_RADIUS) & (rel >= -BAND_RADIUS)
    biases = []
    for u in range(n_sub):
        kpos = q0 + u * sq - BAND_RADIUS + lax.broadcasted_iota(jnp.int32, (sq, win), 1)
        ok = band & (kpos >= 0) & (kpos < sub_len)
        biases.append(jnp.where(ok, 0.0, NEG_BIG))

    def window(p_ref, c_ref, n_ref, cols, u):
        parts = []
        if u == 0:
            parts.append(p_ref[0, :, cols])
        lo = max(u * sq - BAND_RADIUS, 0)
        hi = min(u * sq + sq + BAND_RADIUS, bq)
        parts.append(c_ref[0, lo:hi, cols])
        if u == n_sub - 1:
            parts.append(n_ref[0, :, cols])
        return jnp.concatenate(parts, axis=0)

    tiles = [(h, u) for h in range(A_HEADS) for u in range(n_sub)]

    def scores(h, u):
        cols = slice(h * HEAD_DIM, (h + 1) * HEAD_DIM)
        q = q_ref[0, u * sq:(u + 1) * sq, cols]
        return _dot_nt(q, window(kp_ref, kc_ref, kn_ref, cols, u)) + biases[u]

    s_next = scores(*tiles[0])
    for n, (h, u) in enumerate(tiles):
        cols = slice(h * HEAD_DIM, (h + 1) * HEAD_DIM)
        rows = slice(u * sq, (u + 1) * sq)
        s = s_next
        if n + 1 < len(tiles):
            s_next = scores(*tiles[n + 1])
        m = jnp.max(s, axis=-1, keepdims=True)
        p = jnp.exp(s - m)
        l = jnp.sum(p, axis=-1, keepdims=True)
        o = jnp.dot(p.astype(BF16), window(vp_ref, vc_ref, vn_ref, cols, u), preferred_element_type=F32) / l
        lse = jnp.broadcast_to(m + jnp.log(l), (sq, HEAD_DIM))
        if merge:
            b, c = sl4[h, rows, :], sl16[h, rows, :]
            top = jnp.maximum(jnp.maximum(lse, b), c)
            wa, wb, wc = jnp.exp(lse - top), jnp.exp(b - top), jnp.exp(c - top)
            num = wa * o + wb * so4[h, rows, :] + wc * so16[h, rows, :]
            y_ref[0, rows, cols] = (num / (wa + wb + wc)).astype(y_ref.dtype)
        else:
            o_ref[0, rows, cols] = o.astype(o_ref.dtype)
            lse_ref[0, rows, cols] = lse


def _band_attention(qkv, dilated=None, bq=256):
    n_seq, sub_len, _ = qkv.shape
    a_w = A_HEADS * HEAD_DIM
    hb = bq // BAND_RADIUS
    last = sub_len // BAND_RADIUS - 1
    merge = dilated is not None
    kern = functools.partial(_band_attn_kernel, bq=bq, sub_len=sub_len, merge=merge)

    def halo_specs(col):
        return [pl.BlockSpec((1, BAND_RADIUS, a_w), lambda s, i: (s, jnp.maximum(i * hb - 1, 0), col)),
                pl.BlockSpec((1, bq, a_w), lambda s, i: (s, i, col)),
                pl.BlockSpec((1, BAND_RADIUS, a_w), lambda s, i: (s, jnp.minimum((i + 1) * hb, last), col))]

    out_spec = pl.BlockSpec((1, bq, a_w), lambda s, i: (s, i, 0))
    in_specs = [pl.BlockSpec((1, bq, a_w), lambda s, i: (s, i, 0))] + halo_specs(1) + halo_specs(2)
    operands = [qkv] * 7
    if merge:
        for (_, dil), pair in zip(DILATED_CFGS[1:], dilated):
            for arr in pair:
                operands.append(arr.reshape(n_seq, dil, sub_len // dil, a_w))
                in_specs.append(pl.BlockSpec((1, dil, bq // dil, a_w), lambda s, i: (s, 0, i, 0)))
        out_specs, out_shape = out_spec, jax.ShapeDtypeStruct((n_seq, sub_len, a_w), BF16)
        scratch = [pltpu.VMEM((A_HEADS, bq, HEAD_DIM), F32)] * 4
    else:
        out_specs = [out_spec, out_spec]
        out_shape = [jax.ShapeDtypeStruct((n_seq, sub_len, a_w), BF16),
                     jax.ShapeDtypeStruct((n_seq, sub_len, a_w), F32)]
        scratch = []
    return pl.pallas_call(
        kern,
        grid=(n_seq, sub_len // bq),
        in_specs=in_specs,
        out_specs=out_specs,
        out_shape=out_shape,
        scratch_shapes=scratch,
        compiler_params=_params(("parallel", "parallel"), 40),
        name="band_attn_merge" if merge else "band_attn",
    )(*operands)


def _gating_kernel(u_ref, v_ref, w_ref, b_ref, g_ref, o_ref, *, n_chunks):
    w = w_ref[0]
    bias = b_ref[0]
    gain = g_ref[0]
    for c in range(n_chunks):
        rows = pl.ds(c * CHUNK, CHUNK)
        v = jax.nn.gelu(v_ref[rows, :].astype(F32))
        mu = jnp.mean(v, axis=-1, keepdims=True)
        d = v - mu
        var = jnp.mean(d * d, axis=-1, keepdims=True)
        vn = d * lax.rsqrt(var + EPS) * gain
        sv = jnp.dot(w, vn.astype(BF16), preferred_element_type=F32) + bias
        o_ref[rows, :] = (jax.nn.gelu(u_ref[rows, :].astype(F32)) * sv).astype(o_ref.dtype)


def _spatial_gating(proj, w_s, b_s, g_ln, u_col, v_col, tm=2048):
    t = proj.shape[0]
    tm = min(tm, t)
    groups = w_s.shape[0]
    bias = jnp.broadcast_to(b_s[:, :, None], (groups, CHUNK, LANES))
    kern = functools.partial(_gating_kernel, n_chunks=tm // CHUNK)
    return pl.pallas_call(
        kern,
        grid=(t // tm, groups),
        in_specs=[pl.BlockSpec((tm, LANES), lambda i, g: (i, u_col + g)),
                  pl.BlockSpec((tm, LANES), lambda i, g: (i, v_col + g)),
                  pl.BlockSpec((1, CHUNK, CHUNK), lambda i, g: (g, 0, 0)),
                  pl.BlockSpec((1, CHUNK, LANES), lambda i, g: (g, 0, 0)),
                  pl.BlockSpec((1, 1, LANES), lambda i, g: (g, 0, 0))],
        out_specs=pl.BlockSpec((tm, LANES), lambda i, g: (i, g)),
        out_shape=jax.ShapeDtypeStruct((t, groups * LANES), BF16),
        compiler_params=_params(("parallel", "parallel"), 32),
        name="spatial_gating",
    )(proj, proj, w_s.astype(BF16), bias, g_ln.reshape(groups, 1, LANES))


V_ROWS = 2 * C_HALF + 16
DIFF_BK = 512


def _diff_prep_kernel(x_ref, c_ref, s_ref, qt_ref, k_ref, vt_ref, *, q_scale, n_blocks):
    part = pl.program_id(1)
    tm = x_ref.shape[0]

    @pl.when(part == 0)
    def _():
        c = c_ref[...]
        s = s_ref[...]
        for h in range(n_blocks):
            cols = slice(h * LANES, (h + 1) * LANES)
            qt_ref[0, h] = (_rope_block(x_ref[:, cols].astype(F32), c, s, C_HALF) * q_scale).T.astype(BF16)

    @pl.when(part == 1)
    def _():
        c = c_ref[...]
        s = s_ref[...]
        for h in range(n_blocks):
            cols = slice(h * LANES, (h + 1) * LANES)
            k_ref[:, cols] = _rope_block(x_ref[:, cols].astype(F32), c, s, C_HALF).astype(BF16)

    @pl.when(part == 2)
    def _():
        ones = jnp.ones((V_ROWS - LANES, tm), BF16)
        for h in range(n_blocks):
            vt_ref[0, h, 0, 0:LANES, :] = x_ref[:, h * LANES:(h + 1) * LANES].astype(F32).T.astype(BF16)
            vt_ref[0, h, 0, LANES:, :] = ones


def _diff_prep(proj, tables, batch, seq, q_scale, bk, tm=512):
    c_w = C_HEADS * 2 * C_HALF
    cos_t, sin_t = tables
    spb = seq // tm
    per_chunk = bk // tm
    kern = functools.partial(_diff_prep_kernel, q_scale=q_scale, n_blocks=C_HEADS)
    return pl.pallas_call(
        kern,
        grid=(batch * spb, 3),
        in_specs=[pl.BlockSpec((tm, c_w), lambda i, p: (i, p)),
                  pl.BlockSpec((tm, LANES), lambda i, p: (i % spb, 0)),
                  pl.BlockSpec((tm, LANES), lambda i, p: (i % spb, 0))],
        out_specs=[pl.BlockSpec((1, C_HEADS, LANES, tm), lambda i, p: (i // spb, 0, 0, i % spb)),
                   pl.BlockSpec((tm, c_w), lambda i, p: (i, 0)),
                   pl.BlockSpec((1, C_HEADS, 1, V_ROWS, tm),
                                lambda i, p: (i // spb, 0, (i % spb) // per_chunk, 0, (i % spb) % per_chunk))],
        out_shape=[jax.ShapeDtypeStruct((batch, C_HEADS, LANES, seq), BF16),
                   jax.ShapeDtypeStruct((batch * seq, c_w), BF16),
                   jax.ShapeDtypeStruct((batch, C_HEADS, seq // bk, V_ROWS, bk), BF16)],
        compiler_params=_params(("parallel", "arbitrary"), 40),
        name="diff_prep",
    )(proj, cos_t, sin_t)


def _diff_attn_kernel(qt_ref, k_ref, vt_ref, lam_ref, g_ref, o_ref, s_ref, mc_ref, m_ref, acc_ref,
                      *, bq, bk, seq, lam_init, unroll):
    n = seq // bk
    sub = SUBLANES
    qt = qt_ref[0, 0]
    row = lax.broadcasted_iota(jnp.int32, qt.shape, 0)
    zero = jnp.zeros_like(qt)
    qs = (jnp.where(row < C_HALF, qt, zero), jnp.where(row < C_HALF, zero, qt))

    def scores(j, slot):
        kc = k_ref[0, pl.ds(pl.multiple_of(j * bk, bk), bk), :]
        for br in range(2):
            s = jnp.dot(kc, qs[br], preferred_element_type=F32)
            s_ref[slot, br] = s
            mc = jnp.max(s.reshape(bk // sub, sub, bq), axis=0)
            mc_ref[slot, br] = jnp.broadcast_to(jnp.max(mc, axis=0, keepdims=True), (sub, bq))

    def accumulate(j, slot):
        vt = vt_ref[0, 0, j]
        for br in range(2):
            m_old = m_ref[br]
            m_new = jnp.maximum(m_old, mc_ref[slot, br])
            alpha = jnp.exp2(m_old - m_new)
            m_ref[br] = m_new
            s3 = s_ref[slot, br].reshape(bk // sub, sub, bq)
            p = jnp.exp2(s3 - m_new[None]).reshape(bk, bq).astype(BF16)
            pv = jnp.dot(vt, p, preferred_element_type=F32)
            acc3 = acc_ref[br].reshape(V_ROWS // sub, sub, bq)
            acc_ref[br] = (alpha[None] * acc3).reshape(V_ROWS, bq) + pv

    m_ref[...] = jnp.full(m_ref.shape, NEG_BIG, F32)
    acc_ref[...] = jnp.zeros(acc_ref.shape, F32)
    scores(0, 0)

    def trip(j, last):
        for u in range(unroll):
            if not (last and u == unroll - 1):
                scores(j + u + 1, (u + 1) % 2)
            accumulate(j + u, u % 2)

    def body(i, carry):
        trip(unroll * i, False)
        return carry

    lax.fori_loop(0, n // unroll - 1, body, 0)
    trip(n - unroll, True)

    lam = lam_ref[...]
    lam_full = (jnp.exp(jnp.sum(lam[0:1] * lam[1:2], keepdims=True))
                - jnp.exp(jnp.sum(lam[2:3] * lam[3:4], keepdims=True)) + lam_init)

    def normalised(a):
        num = a[0:LANES].reshape(LANES // sub, sub, bq)
        return (num / a[LANES:LANES + sub][None]).reshape(LANES, bq)

    o = normalised(acc_ref[0]) - lam_full * normalised(acc_ref[1])
    y = o * lax.rsqrt(jnp.mean(o * o, axis=0, keepdims=True) + EPS) * g_ref[...] * (1.0 - lam_init)
    o_ref[0] = y.T.astype(o_ref.dtype)


def _diff_attention(qt, k, vt, lam, subln_g, batch, seq, layer_idx, bq=512, bk=512, unroll=8):
    c_w = C_HEADS * 2 * C_HALF
    lam_init = 0.8 - 0.6 * math.exp(-0.3 * layer_idx)
    n = seq // bk
    unroll = min(unroll, n)
    assert unroll % 2 == 0 and n % unroll == 0 and vt.shape[2] == n
    kern = functools.partial(_diff_attn_kernel, bq=bq, bk=bk, seq=seq, lam_init=lam_init, unroll=unroll)
    out = pl.pallas_call(
        kern,
        grid=(batch, C_HEADS, seq // bq),
        in_specs=[pl.BlockSpec((1, 1, LANES, bq), lambda b, h, i: (b, h, 0, i)),
                  pl.BlockSpec((1, seq, LANES), lambda b, h, i: (b, 0, h)),
                  pl.BlockSpec((1, 1, n, V_ROWS, bk), lambda b, h, i: (b, h, 0, 0, 0)),
                  pl.BlockSpec((4, C_HALF), lambda b, h, i: (0, 0)),
                  pl.BlockSpec((LANES, bq), lambda b, h, i: (0, 0))],
        out_specs=pl.BlockSpec((1, bq, LANES), lambda b, h, i: (b, i, h)),
        out_shape=jax.ShapeDtypeStruct((batch, seq, c_w), BF16),
        scratch_shapes=[pltpu.VMEM((2, 2, bk, bq), F32),
                        pltpu.VMEM((2, 2, SUBLANES, bq), F32),
                        pltpu.VMEM((2, SUBLANES, bq), F32),
                        pltpu.VMEM((2, V_ROWS, bq), F32)],
        compiler_params=_params(("parallel", "parallel", "arbitrary"), 48),
        name="diff_attn",
    )(qt, k.reshape(batch, seq, c_w), vt, lam, jnp.broadcast_to(subln_g[:, None], (LANES, bq)))
    return out.reshape(batch * seq, c_w)


def _fourier_wprep_kernel(c_ref, s_ref, w_ref, o_ref, *, scale):
    w = w_ref[0]
    mr = jnp.dot(c_ref[...], w, preferred_element_type=F32, precision=lax.Precision.HIGHEST)
    mi = jnp.dot(s_ref[...], w, preferred_element_type=F32, precision=lax.Precision.HIGHEST)
    o_ref[0, :, :LANES] = (mr * scale).astype(BF16)
    o_ref[0, :, LANES:] = (-mi * scale).astype(BF16)


def _fourier_weights(w_f, seq):
    groups, c, _ = w_f.shape
    idx = jnp.arange(c, dtype=jnp.int32)
    ang = (2.0 * math.pi / c) * ((idx[:, None] * idx[None, :]) % c).astype(F32)
    kern = functools.partial(_fourier_wprep_kernel, scale=1.0 / math.sqrt(seq * c))
    return pl.pallas_call(
        kern,
        grid=(groups,),
        in_specs=[pl.BlockSpec((c, c), lambda g: (0, 0)),
                  pl.BlockSpec((c, c), lambda g: (0, 0)),
                  pl.BlockSpec((1, c, c), lambda g: (g, 0, 0))],
        out_specs=pl.BlockSpec((1, c, 2 * c), lambda g: (g, 0, 0)),
        out_shape=jax.ShapeDtypeStruct((groups, c, 2 * c), BF16),
        compiler_params=_params(("parallel",), 32),
        name="fourier_wprep",
    )(jnp.cos(ang), jnp.sin(ang), w_f)


def _dft_stage1_kernel(z_ref, m_ref, fr_ref, fi_ref, y_ref, *, groups):
    pairs = z_ref.shape[2] // LANES
    u = jnp.concatenate(
        [jnp.dot(z_ref[0, :, j * LANES:(j + 1) * LANES], m_ref[j % groups], preferred_element_type=F32).astype(BF16)
         for j in range(pairs)], axis=1)
    p = jnp.dot(fr_ref[...], u, preferred_element_type=F32)
    q = jnp.dot(fi_ref[...], u, preferred_element_type=F32)
    for j in range(pairs):
        re = slice(2 * j * LANES, (2 * j + 1) * LANES)
        im = slice((2 * j + 1) * LANES, (2 * j + 2) * LANES)
        y_ref[0, :, re] = (p[:, re] - q[:, im]).astype(BF16)
        y_ref[0, :, im] = (q[:, re] + p[:, im]).astype(BF16)


def _dft_stage2_kernel(y_ref, gc_ref, gs_ref, o_ref, *, groups):
    for t in range(y_ref.shape[1]):
        y = y_ref[0, t]
        a = jnp.dot(gc_ref[t], y, preferred_element_type=F32)
        b = jnp.dot(gs_ref[t], y, preferred_element_type=F32)
        for g in range(groups):
            re = slice(2 * g * LANES, (2 * g + 1) * LANES)
            im = slice((2 * g + 1) * LANES, (2 * g + 2) * LANES)
            out_cols = slice((t * groups + g) * LANES, (t * groups + g + 1) * LANES)
            o_ref[0, :, out_cols] = (a[:, re] + b[:, im]).astype(o_ref.dtype)


def _seq_dft_real(z, m_c, batch, seq, tc=4096, kb=4):
    groups = m_c.shape[0]
    n1 = 128 if seq >= 16384 else 64
    n2 = seq // n1
    wc = groups * 2 * LANES
    i1 = jnp.arange(n1, dtype=jnp.int32)
    ang1 = (2.0 * math.pi / n1) * ((i1[:, None] * i1[None, :]) % n1).astype(F32)
    fr = jnp.cos(ang1).astype(BF16)
    fi = (-jnp.sin(ang1)).astype(BF16)
    cols = n2 * wc
    tc = min(tc, cols)
    y = pl.pallas_call(
        functools.partial(_dft_stage1_kernel, groups=groups),
        grid=(batch, cols // tc),
        in_specs=[pl.BlockSpec((1, n1, tc // 2), lambda b, j: (b, 0, j)),
                  pl.BlockSpec((groups, LANES, 2 * LANES), lambda b, j: (0, 0, 0)),
                  pl.BlockSpec((n1, n1), lambda b, j: (0, 0)),
                  pl.BlockSpec((n1, n1), lambda b, j: (0, 0))],
        out_specs=pl.BlockSpec((1, n1, tc), lambda b, j: (b, 0, j)),
        out_shape=jax.ShapeDtypeStruct((batch, n1, cols), BF16),
        compiler_params=_params(("parallel", "parallel"), 32),
        name="dft_stage1",
    )(z.reshape(batch, n1, cols // 2), m_c, fr, fi)
    i2 = jnp.arange(n2, dtype=jnp.int32)
    tw = (i2[None, None, :] * (i1[:, None, None] + n1 * i2[None, :, None])) % seq
    ang2 = (2.0 * math.pi / seq) * tw.astype(F32)
    gc = jnp.cos(ang2).astype(BF16)
    gs = jnp.sin(ang2).astype(BF16)
    out = pl.pallas_call(
        functools.partial(_dft_stage2_kernel, groups=groups),
        grid=(batch, n1 // kb),
        in_specs=[pl.BlockSpec((1, kb, n2, wc), lambda b, k: (b, k, 0, 0)),
                  pl.BlockSpec((kb, n2, n2), lambda b, k: (k, 0, 0)),
                  pl.BlockSpec((kb, n2, n2), lambda b, k: (k, 0, 0))],
        out_specs=pl.BlockSpec((1, n2, kb * groups * LANES), lambda b, k: (b, 0, k)),
        out_shape=jax.ShapeDtypeStruct((batch, n2, n1 * groups * LANES), BF16),
        compiler_params=_params(("parallel", "parallel"), 32),
        name="dft_stage2",
    )(y.reshape(batch, n1, n2, wc), gc, gs)
    return out.reshape(batch * seq, groups * LANES)


def _mix_out_cross_attn_kernel(mixa_ref, mixb_ref, x_ref, wout_ref, g1_ref, g2_ref, wq_ref, kv_ref, wo_ref, g3_ref,
                               o_ref, *, heads, scale, halves):
    ca_w = heads * HEAD_DIM
    hr = x_ref.shape[0] // halves
    ka = mixa_ref.shape[1]
    rows = [slice(r * hr, (r + 1) * hr) for r in range(halves)]
    ys = [jnp.dot(mixa_ref[r, :], wout_ref[0:ka, :], preferred_element_type=F32)
          + jnp.dot(mixb_ref[r, :], wout_ref[ka:, :], preferred_element_type=F32) for r in rows]
    x1s = [x_ref[r, :] + _rms(y, g1_ref[...]) for r, y in zip(rows, ys)]
    qs = [jnp.dot(_rms(x1, g2_ref[...]).astype(BF16), wq_ref[...], preferred_element_type=F32).astype(BF16)
          for x1 in x1s]
    cas = []
    for q in qs:
        outs = []
        for h in range(heads):
            cols = slice(h * HEAD_DIM, (h + 1) * HEAD_DIM)
            s = _dot_nt(q[:, cols], kv_ref[0, :, cols]) * scale
            m = jnp.max(s, axis=-1, keepdims=True)
            p = jnp.exp(s - m)
            l = jnp.sum(p, axis=-1, keepdims=True)
            v = kv_ref[0, :, ca_w + h * HEAD_DIM:ca_w + (h + 1) * HEAD_DIM]
            outs.append((jnp.dot(p.astype(BF16), v, preferred_element_type=F32) / l).astype(BF16))
        cas.append(jnp.concatenate(outs, axis=1))
    zs = [jnp.dot(ca, wo_ref[...], preferred_element_type=F32) for ca in cas]
    for r, x1, z in zip(rows, x1s, zs):
        o_ref[r, :] = x1 + _rms(z, g3_ref[...])


def _mix_out_cross_attn(mix_a, mix_b, x, w_out, w_q, w_o, layer, g1, g2, g3, kv, batch, seq, tm=512):
    t, d = x.shape
    ka, kb = mix_a.shape[1], mix_b.shape[1]
    assert ka + kb == w_out.shape[1]
    ca_w = CA_HEADS * HEAD_DIM
    mem_len = kv.shape[0] // batch
    spb = seq // tm
    kern = functools.partial(_mix_out_cross_attn_kernel, heads=CA_HEADS, scale=HEAD_DIM ** -0.5, halves=2)
    once = pl.Buffered(1)
    vec = lambda: pl.BlockSpec((1, d), lambda i: (0, 0))
    return pl.pallas_call(
        kern,
        grid=(t // tm,),
        in_specs=[pl.BlockSpec((tm, ka), lambda i: (i, 0)),
                  pl.BlockSpec((tm, kb), lambda i: (i, 0)),
                  pl.BlockSpec((tm, d), lambda i: (i, 0)),
                  pl.BlockSpec((None, d, d), lambda i: (layer, 0, 0), pipeline_mode=once),
                  vec(), vec(),
                  pl.BlockSpec((None, d, ca_w), lambda i: (layer, 0, 0), pipeline_mode=once),
                  pl.BlockSpec((1, mem_len, 2 * ca_w), lambda i: (i // spb, 0, 0)),
                  pl.BlockSpec((None, ca_w, d), lambda i: (layer, 0, 0), pipeline_mode=once),
                  vec()],
        out_specs=pl.BlockSpec((tm, d), lambda i: (i, 0)),
        out_shape=jax.ShapeDtypeStruct((t, d), F32),
        compiler_params=_params(("parallel",), 56),
        name="mix_out_cross_attn",
    )(mix_a, mix_b, x, w_out, g1.reshape(1, d), g2.reshape(1, d), w_q, kv.reshape(batch, mem_len, 2 * ca_w), w_o,
      g3.reshape(1, d))


HALO = 16


def _ffn_kernel(x_ref, xp_ref, xn_ref, g4_ref, wg_ref, wv_ref, cwg_ref, cwv_ref, cbg_ref, cbv_ref,
                wd_ref, g5_ref, o_ref, xe_ref, zg_ref, zv_ref, *, tm, tc, tiles_per_seq, nf):
    acc_ref = o_ref
    i = pl.program_id(0)
    f = pl.program_id(1)

    @pl.when(f == 0)
    def _():
        g4 = g4_ref[...]
        pos = i % tiles_per_seq
        prev = _rms(xp_ref[...], g4) * jnp.where(pos == 0, 0.0, 1.0)
        nxt = _rms(xn_ref[...], g4) * jnp.where(pos == tiles_per_seq - 1, 0.0, 1.0)
        xe_ref[0:HALO, :] = prev.astype(BF16)
        xe_ref[HALO:HALO + tm, :] = _rms(x_ref[...], g4).astype(BF16)
        xe_ref[HALO + tm:, :] = nxt.astype(BF16)
        acc_ref[...] = jnp.zeros(acc_ref.shape, F32)

    def conv(z_ref, c, cw, cb, lo, rows):
        lo = HALO + lo
        return (z_ref[c, lo - 1:lo - 1 + rows, :] * cw[0:1] + z_ref[c, lo:lo + rows, :] * cw[1:2]
                + z_ref[c, lo + 1:lo + 1 + rows, :] * cw[2:3] + cb)

    xe = xe_ref[...]
    n_sub = zg_ref.shape[0]
    for c in range(n_sub):
        cols = slice(c * tc, (c + 1) * tc)
        zg_ref[c] = jnp.dot(xe, wg_ref[:, cols], preferred_element_type=F32)
        zv_ref[c] = jnp.dot(xe, wv_ref[:, cols], preferred_element_type=F32)
    for c in range(n_sub):
        cols = slice(c * tc, (c + 1) * tc)
        pieces = 2 if c == n_sub - 1 else 1
        rows = tm // pieces
        for r in range(pieces):
            lo = r * rows
            h = (jax.nn.gelu(conv(zg_ref, c, cwg_ref[:, cols], cbg_ref[:, cols], lo, rows))
                 * conv(zv_ref, c, cwv_ref[:, cols], cbv_ref[:, cols], lo, rows))
            acc_ref[lo:lo + rows, :] += jnp.dot(h.astype(BF16), wd_ref[cols, :], preferred_element_type=F32)

    @pl.when(f == nf - 1)
    def _():
        o_ref[...] = x_ref[...] + _rms(acc_ref[...], g5_ref[...])


def _conv_ffn(x, seq, g4, w_up, conv_w, conv_b, w_down, layer, g5, tm=1024, tf=512, tc=256):
    t, d = x.shape
    d_ff = w_down.shape[1]
    nf = d_ff // tf
    hb = tm // HALO
    last_hb = t // HALO - 1
    kern = functools.partial(_ffn_kernel, tm=tm, tc=tc, tiles_per_seq=seq // tm, nf=nf)
    cb = conv_b.reshape(1, 2 * d_ff)
    return pl.pallas_call(
        kern,
        grid=(t // tm, nf),
        in_specs=[pl.BlockSpec((tm, d), lambda i, f: (i, 0), pipeline_mode=pl.Buffered(1)),
                  pl.BlockSpec((HALO, d), lambda i, f: (jnp.maximum(i * hb - 1, 0), 0)),
                  pl.BlockSpec((HALO, d), lambda i, f: (jnp.minimum((i + 1) * hb, last_hb), 0)),
                  pl.BlockSpec((1, d), lambda i, f: (0, 0)),
                  pl.BlockSpec((None, d, tf), lambda i, f: (layer, 0, f)),
                  pl.BlockSpec((None, d, tf), lambda i, f: (layer, 0, nf + f)),
                  pl.BlockSpec((CONV_W, tf), lambda i, f: (0, f)),
                  pl.BlockSpec((CONV_W, tf), lambda i, f: (0, nf + f)),
                  pl.BlockSpec((1, tf), lambda i, f: (0, f)),
                  pl.BlockSpec((1, tf), lambda i, f: (0, nf + f)),
                  pl.BlockSpec((None, tf, d), lambda i, f: (layer, f, 0)),
                  pl.BlockSpec((1, d), lambda i, f: (0, 0))],
        out_specs=pl.BlockSpec((tm, d), lambda i, f: (i, 0)),
        out_shape=jax.ShapeDtypeStruct((t, d), F32),
        scratch_shapes=[pltpu.VMEM((tm + 2 * HALO, d), BF16),
                        pltpu.VMEM((tf // tc, tm + 2 * HALO, tc), F32),
                        pltpu.VMEM((tf // tc, tm + 2 * HALO, tc), F32)],
        compiler_params=_params(("parallel", "arbitrary"), 58),
        name="conv_ffn",
    )(x, x, x, g4.reshape(1, d), w_up, w_up, conv_w, conv_w, cb, cb, w_down, g5.reshape(1, d))


def _trunk(x3, mem3, p):
    batch, seq, d = x3.shape
    x = x3.reshape(batch * seq, d)
    mem = mem3.reshape(-1, d)
    depth = p["w_in"].shape[0]
    a_w = A_HEADS * HEAD_DIM
    c_w = C_HEADS * 2 * C_HALF
    for layer in range(depth):
        g = p["norm_gains"][layer]
        proj = _norm_matmul(x, g[0], p["w_in"], layer, BF16)
        if layer % 2 == 0:
            e = layer // 2
            qkvs = _even_prep(proj, p["rope"][HEAD_DIM], batch, seq, HEAD_DIM ** -0.5)
            seqs = [qkv.reshape(batch * dil, seq // dil, 3 * a_w) for (_, dil), qkv in zip(DILATED_CFGS, qkvs)]
            dilated = [_band_attention(s) for s in seqs[1:]]
            ya = _band_attention(seqs[0], dilated).reshape(batch * seq, a_w)
            yb = _spatial_gating(proj, p["b_w_spatial"][e], p["b_b_spatial"][e], p["b_ln_gain"][e],
                                 3 * A_HEADS, 3 * A_HEADS + B_GROUPS)
            mix = (ya, yb)
        else:
            o = layer // 2
            qt, kk, vt = _diff_prep(proj, p["rope"][C_HALF], batch, seq, C_HALF ** -0.5 * math.log2(math.e),
                                    DIFF_BK)
            yc = _diff_attention(qt, kk, vt, p["c_lambda"][o], p["c_subln_gain"][o], batch, seq, layer,
                                 bk=DIFF_BK)
            m_c = _fourier_weights(p["d_w_fourier"][o], seq)
            yd = _seq_dft_real(proj[:, 3 * c_w:], m_c, batch, seq)
            mix = (yc, yd)
        kv = _norm_matmul(mem, p["mem_norm_gain"][layer], p["ca_w_kv"], layer, BF16)
        x = _mix_out_cross_attn(*mix, x, p["w_out"], p["ca_w_q"], p["ca_w_o"], layer, g[1], g[2], g[3], kv,
                                batch, seq)
        x = _conv_ffn(x, seq, g[4], p["ffn_w_up"], p["ffn_conv_w"][layer], p["ffn_conv_b"][layer],
                      p["ffn_w_down"], layer, g[5])
    return x.reshape(batch, seq, d)


def kernel(x_prompt, x_sample, mem_prompt, mem_sample, norm_gains, w_in, w_out, b_w_spatial, b_b_spatial,
           b_ln_gain, c_lambda, c_subln_gain, d_w_fourier, mem_norm_gain, ca_w_q, ca_w_kv, ca_w_o, ffn_w_up,
           ffn_conv_w, ffn_conv_b, ffn_w_down):
    p = dict(norm_gains=norm_gains, w_in=w_in.astype(BF16), w_out=w_out.astype(BF16), b_w_spatial=b_w_spatial,
             b_b_spatial=b_b_spatial, b_ln_gain=b_ln_gain, c_lambda=c_lambda, c_subln_gain=c_subln_gain,
             d_w_fourier=d_w_fourier, mem_norm_gain=mem_norm_gain, ca_w_q=ca_w_q.astype(BF16),
             ca_w_kv=ca_w_kv.astype(BF16), ca_w_o=ca_w_o.astype(BF16), ffn_w_up=ffn_w_up.astype(BF16),
             ffn_conv_w=ffn_conv_w, ffn_conv_b=ffn_conv_b, ffn_w_down=ffn_w_down.astype(BF16))
    seq_max = max(x_prompt.shape[1], x_sample.shape[1])
    p["rope"] = {w: _rope_tables(seq_max, w) for w in (HEAD_DIM, C_HALF)}
    return _trunk(x_prompt, mem_prompt, p), _trunk(x_sample, mem_sample, p)
```

```python
import functools
import math

import jax
import jax.numpy as jnp
from jax import lax
from jax.experimental import pallas as pl
from jax.experimental.pallas import tpu as pltpu

F32 = jnp.float32
BF16 = jnp.bfloat16

EPS = 1e-6
ROPE_THETA = 500000.0
ROPE_FRACTION = 4
LANES = 128
SUBLANES = 8
HEAD_DIM = 128
A_HEADS = 8
DILATED_CFGS = ((128, 1), (512, 4), (2048, 16))
BAND_RADIUS = 64
assert all(window // (2 * dil) == BAND_RADIUS for window, dil in DILATED_CFGS)
B_GROUPS = 8
CHUNK = 128
C_HEADS = 12
C_HALF = 64
CA_HEADS = 4
CONV_W = 3
NEG_BIG = -1e30
MIB = 1024 * 1024


def _params(semantics, vmem_mib):
    return pltpu.CompilerParams(dimension_semantics=semantics, vmem_limit_bytes=vmem_mib * MIB)


def _rms(x, g):
    return x * lax.rsqrt(jnp.mean(x * x, axis=-1, keepdims=True) + EPS) * g


def _dot_nt(a, b):
    return lax.dot_general(a, b, (((1,), (1,)), ((), ())), preferred_element_type=F32)


def _norm_mm_kernel(x_ref, g_ref, w_ref, o_ref, xn_ref):
    j = pl.program_id(1)

    @pl.when(j == 0)
    def _():
        half = x_ref.shape[0] // 2
        for r in (slice(0, half), slice(half, 2 * half)):
            xn_ref[r, :] = _rms(x_ref[r, :], g_ref[...]).astype(BF16)
            o_ref[r, :] = jnp.dot(xn_ref[r, :], w_ref[...], preferred_element_type=F32).astype(o_ref.dtype)

    @pl.when(j > 0)
    def _():
        o_ref[...] = jnp.dot(xn_ref[...], w_ref[...], preferred_element_type=F32).astype(o_ref.dtype)


def _norm_matmul(x, g, w, layer, out_dtype, tm=1024, tn=1024):
    t, k = x.shape
    n = w.shape[2]
    tm = min(tm, t)
    return pl.pallas_call(
        _norm_mm_kernel,
        grid=(t // tm, n // tn),
        in_specs=[pl.BlockSpec((tm, k), lambda i, j: (i, 0)),
                  pl.BlockSpec((1, k), lambda i, j: (0, 0)),
                  pl.BlockSpec((None, k, tn), lambda i, j: (layer, 0, j))],
        out_specs=pl.BlockSpec((tm, tn), lambda i, j: (i, j)),
        out_shape=jax.ShapeDtypeStruct((t, n), out_dtype),
        scratch_shapes=[pltpu.VMEM((tm, k), BF16)],
        compiler_params=_params(("parallel", "arbitrary"), 40),
        name="norm_matmul",
    )(x, g.reshape(1, k), w)


def _rope_tables(seq, head_w):
    rd = head_w // ROPE_FRACTION
    half = rd // 2
    inv = ROPE_THETA ** (-jnp.arange(half, dtype=F32) / half)
    ang = jnp.arange(seq, dtype=F32)[:, None] * inv[None, :]
    reps = LANES // half
    cos = jnp.broadcast_to(jnp.cos(ang)[:, None, :], (seq, reps, half)).reshape(seq, LANES)
    sin = jnp.broadcast_to(jnp.sin(ang)[:, None, :], (seq, reps, half)).reshape(seq, LANES)
    lane = (jnp.arange(LANES, dtype=jnp.int32) & (head_w - 1))[None, :]
    return jnp.where(lane < rd, cos, 1.0), jnp.where(lane < rd, jnp.where(lane < half, -sin, sin), 0.0)


def _rope_block(x, c, s, head_w):
    half = head_w // ROPE_FRACTION // 2
    first = (lax.broadcasted_iota(jnp.int32, x.shape, 1) & (head_w - 1)) < half
    partner = jnp.where(first, pltpu.roll(x, LANES - half, axis=1), pltpu.roll(x, half, axis=1))
    return x * c + partner * s


def _even_prep_kernel(x_ref, c_ref, s_ref, *refs, q_scale, n_blocks):
    out_refs, y_ref = refs[:-1], refs[-1]
    part = pl.program_id(1)
    tm = x_ref.shape[0]

    def rope(scale):
        c = c_ref[...]
        s = s_ref[...]
        for b in range(n_blocks):
            cols = slice(b * LANES, (b + 1) * LANES)
            y_ref[b] = _rope_block(x_ref[:, cols].astype(F32), c, s, HEAD_DIM) * scale

    @pl.when(part == 0)
    def _():
        rope(q_scale)

    @pl.when(part == 1)
    def _():
        rope(1.0)

    @pl.when(part == 2)
    def _():
        for b in range(n_blocks):
            y_ref[b] = x_ref[:, b * LANES:(b + 1) * LANES].astype(F32)

    for (_, dil), o_ref in zip(DILATED_CFGS, out_refs):
        for r in range(dil):
            for b in range(n_blocks):
                rows = y_ref[b, pl.ds(r, tm // dil, stride=dil), :]
                o_ref[0, r, :, b * LANES:(b + 1) * LANES] = rows.astype(BF16)


def _even_prep(proj, tables, batch, seq, q_scale, tm=1024):
    a_w = A_HEADS * HEAD_DIM
    cos_t, sin_t = tables
    spb = seq // tm
    kern = functools.partial(_even_prep_kernel, q_scale=q_scale, n_blocks=a_w // LANES)
    return pl.pallas_call(
        kern,
        grid=(batch * spb, 3),
        in_specs=[pl.BlockSpec((tm, a_w), lambda i, p: (i, p)),
                  pl.BlockSpec((tm, LANES), lambda i, p: (i % spb, 0)),
                  pl.BlockSpec((tm, LANES), lambda i, p: (i % spb, 0))],
        out_specs=[pl.BlockSpec((1, dil, tm // dil, a_w), lambda i, p: (i // spb, 0, i % spb, p))
                   for _, dil in DILATED_CFGS],
        out_shape=[jax.ShapeDtypeStruct((batch, dil, seq // dil, 3 * a_w), BF16) for _, dil in DILATED_CFGS],
        scratch_shapes=[pltpu.VMEM((a_w // LANES, tm, LANES), F32)],
        compiler_params=_params(("parallel", "arbitrary"), 32),
        name="even_prep",
    )(proj, cos_t, sin_t)


def _band_attn_kernel(q_ref, kp_ref, kc_ref, kn_ref, vp_ref, vc_ref, vn_ref, *refs, bq, sub_len, merge):
    if merge:
        o4, l4, o16, l16, y_ref, so4, sl4, so16, sl16 = refs
        for dil, src, dst in ((4, o4, so4), (4, l4, sl4), (16, o16, so16), (16, l16, sl16)):
            for r in range(dil):
                for h in range(A_HEADS):
                    dst[h, pl.ds(r, bq // dil, stride=dil), :] = (
                        src[0, r, :, h * HEAD_DIM:(h + 1) * HEAD_DIM].astype(F32))
    else:
        o_ref, lse_ref = refs
    q0 = pl.program_id(1) * bq
    sq = 2 * BAND_RADIUS
    win = sq + 2 * BAND_RADIUS
    n_sub = bq // sq
    rel = (lax.broadcasted_iota(jnp.int32, (sq, win), 1) - BAND_RADIUS
           - lax.broadcasted_iota(jnp.int32, (sq, win), 0))
    band = (rel <= BAND_RADIUS) & (rel >= -BAND_RADIUS)
    biases = []
    for u in range(n_sub):
        kpos = q0 + u * sq - BAND_RADIUS + lax.broadcasted_iota(jnp.int32, (sq, win), 1)
        ok = band & (kpos >= 0) & (kpos < sub_len)
        biases.append(jnp.where(ok, 0.0, NEG_BIG))

    def window(p_ref, c_ref, n_ref, cols, u):
        parts = []
        if u == 0:
            parts.append(p_ref[0, :, cols])
        lo = max(u * sq - BAND_RADIUS, 0)
        hi = min(u * sq + sq + BAND_RADIUS, bq)
        parts.append(c_ref[0, lo:hi, cols])
        if u == n_sub - 1:
            parts.append(n_ref[0, :, cols])
        return jnp.concatenate(parts, axis=0)

    tiles = [(h, u) for h in range(A_HEADS) for u in range(n_sub)]

    def scores(h, u):
        cols = slice(h * HEAD_DIM, (h + 1) * HEAD_DIM)
        q = q_ref[0, u * sq:(u + 1) * sq, cols]
        return _dot_nt(q, window(kp_ref, kc_ref, kn_ref, cols, u)) + biases[u]

    s_next = scores(*tiles[0])
    for n, (h, u) in enumerate(tiles):
        cols = slice(h * HEAD_DIM, (h + 1) * HEAD_DIM)
        rows = slice(u * sq, (u + 1) * sq)
        s = s_next
        if n + 1 < len(tiles):
            s_next = scores(*tiles[n + 1])
        m = jnp.max(s, axis=-1, keepdims=True)
        p = jnp.exp(s - m)
        l = jnp.sum(p, axis=-1, keepdims=True)
        o = jnp.dot(p.astype(BF16), window(vp_ref, vc_ref, vn_ref, cols, u), preferred_element_type=F32) / l
        lse = jnp.broadcast_to(m + jnp.log(l), (sq, HEAD_DIM))
        if merge:
            b, c = sl4[h, rows, :], sl16[h, rows, :]
            top = jnp.maximum(jnp.maximum(lse, b), c)
            wa, wb, wc = jnp.exp(lse - top), jnp.exp(b - top), jnp.exp(c - top)
            num = wa * o + wb * so4[h, rows, :] + wc * so16[h, rows, :]
            y_ref[0, rows, cols] = (num / (wa + wb + wc)).astype(y_ref.dtype)
        else:
            o_ref[0, rows, cols] = o.astype(o_ref.dtype)
            lse_ref[0, rows, cols] = lse


def _band_attention(qkv, dilated=None, bq=256):
    n_seq, sub_len, _ = qkv.shape
    a_w = A_HEADS * HEAD_DIM
    hb = bq // BAND_RADIUS
    last = sub_len // BAND_RADIUS - 1
    merge = dilated is not None
    kern = functools.partial(_band_attn_kernel, bq=bq, sub_len=sub_len, merge=merge)

    def halo_specs(col):
        return [pl.BlockSpec((1, BAND_RADIUS, a_w), lambda s, i: (s, jnp.maximum(i * hb - 1, 0), col)),
                pl.BlockSpec((1, bq, a_w), lambda s, i: (s, i, col)),
                pl.BlockSpec((1, BAND_RADIUS, a_w), lambda s, i: (s, jnp.minimum((i + 1) * hb, last), col))]

    out_spec = pl.BlockSpec((1, bq, a_w), lambda s, i: (s, i, 0))
    in_specs = [pl.BlockSpec((1, bq, a_w), lambda s, i: (s, i, 0))] + halo_specs(1) + halo_specs(2)
    operands = [qkv] * 7
    if merge:
        for (_, dil), pair in zip(DILATED_CFGS[1:], dilated):
            for arr in pair:
                operands.append(arr.reshape(n_seq, dil, sub_len // dil, a_w))
                in_specs.append(pl.BlockSpec((1, dil, bq // dil, a_w), lambda s, i: (s, 0, i, 0)))
        out_specs, out_shape = out_spec, jax.ShapeDtypeStruct((n_seq, sub_len, a_w), BF16)
        scratch = [pltpu.VMEM((A_HEADS, bq, HEAD_DIM), F32)] * 4
    else:
        out_specs = [out_spec, out_spec]
        out_shape = [jax.ShapeDtypeStruct((n_seq, sub_len, a_w), BF16),
                     jax.ShapeDtypeStruct((n_seq, sub_len, a_w), F32)]
        scratch = []
    return pl.pallas_call(
        kern,
        grid=(n_seq, sub_len // bq),
        in_specs=in_specs,
        out_specs=out_specs,
        out_shape=out_shape,
        scratch_shapes=scratch,
        compiler_params=_params(("parallel", "parallel"), 40),
        name="band_attn_merge" if merge else "band_attn",
    )(*operands)


def _gating_kernel(u_ref, v_ref, w_ref, b_ref, g_ref, o_ref, *, n_chunks):
    w = w_ref[0]
    bias = b_ref[0]
    gain = g_ref[0]
    for c in range(n_chunks):
        rows = pl.ds(c * CHUNK, CHUNK)
        v = jax.nn.gelu(v_ref[rows, :].astype(F32))
        mu = jnp.mean(v, axis=-1, keepdims=True)
        d = v - mu
        var = jnp.mean(d * d, axis=-1, keepdims=True)
        vn = d * lax.rsqrt(var + EPS) * gain
        sv = jnp.dot(w, vn.astype(BF16), preferred_element_type=F32) + bias
        o_ref[rows, :] = (jax.nn.gelu(u_ref[rows, :].astype(F32)) * sv).astype(o_ref.dtype)


def _spatial_gating(proj, w_s, b_s, g_ln, u_col, v_col, tm=2048):
    t = proj.shape[0]
    tm = min(tm, t)
    groups = w_s.shape[0]
    bias = jnp.broadcast_to(b_s[:, :, None], (groups, CHUNK, LANES))
    kern = functools.partial(_gating_kernel, n_chunks=tm // CHUNK)
    return pl.pallas_call(
        kern,
        grid=(t // tm, groups),
        in_specs=[pl.BlockSpec((tm, LANES), lambda i, g: (i, u_col + g)),
                  pl.BlockSpec((tm, LANES), lambda i, g: (i, v_col + g)),
                  pl.BlockSpec((1, CHUNK, CHUNK), lambda i, g: (g, 0, 0)),
                  pl.BlockSpec((1, CHUNK, LANES), lambda i, g: (g, 0, 0)),
                  pl.BlockSpec((1, 1, LANES), lambda i, g: (g, 0, 0))],
        out_specs=pl.BlockSpec((tm, LANES), lambda i, g: (i, g)),
        out_shape=jax.ShapeDtypeStruct((t, groups * LANES), BF16),
        compiler_params=_params(("parallel", "parallel"), 32),
        name="spatial_gating",
    )(proj, proj, w_s.astype(BF16), bias, g_ln.reshape(groups, 1, LANES))


V_ROWS = 2 * C_HALF + 16
DIFF_BK = 512


def _diff_prep_kernel(x_ref, c_ref, s_ref, qt_ref, k_ref, vt_ref, *, q_scale, n_blocks):
    part = pl.program_id(1)
    tm = x_ref.shape[0]

    @pl.when(part == 0)
    def _():
        c = c_ref[...]
        s = s_ref[...]
        for h in range(n_blocks):
            cols = slice(h * LANES, (h + 1) * LANES)
            qt_ref[0, h] = (_rope_block(x_ref[:, cols].astype(F32), c, s, C_HALF) * q_scale).T.astype(BF16)

    @pl.when(part == 1)
    def _():
        c = c_ref[...]
        s = s_ref[...]
        for h in range(n_blocks):
            cols = slice(h * LANES, (h + 1) * LANES)
            k_ref[:, cols] = _rope_block(x_ref[:, cols].astype(F32), c, s, C_HALF).astype(BF16)

    @pl.when(part == 2)
    def _():
        ones = jnp.ones((V_ROWS - LANES, tm), BF16)
        for h in range(n_blocks):
            vt_ref[0, h, 0, 0:LANES, :] = x_ref[:, h * LANES:(h + 1) * LANES].astype(F32).T.astype(BF16)
            vt_ref[0, h, 0, LANES:, :] = ones


def _diff_prep(proj, tables, batch, seq, q_scale, bk, tm=512):
    c_w = C_HEADS * 2 * C_HALF
    cos_t, sin_t = tables
    spb = seq // tm
    per_chunk = bk // tm
    kern = functools.partial(_diff_prep_kernel, q_scale=q_scale, n_blocks=C_HEADS)
    return pl.pallas_call(
        kern,
        grid=(batch * spb, 3),
        in_specs=[pl.BlockSpec((tm, c_w), lambda i, p: (i, p)),
                  pl.BlockSpec((tm, LANES), lambda i, p: (i % spb, 0)),
                  pl.BlockSpec((tm, LANES), lambda i, p: (i % spb, 0))],
        out_specs=[pl.BlockSpec((1, C_HEADS, LANES, tm), lambda i, p: (i // spb, 0, 0, i % spb)),
                   pl.BlockSpec((tm, c_w), lambda i, p: (i, 0)),
                   pl.BlockSpec((1, C_HEADS, 1, V_ROWS, tm),
                                lambda i, p: (i // spb, 0, (i % spb) // per_chunk, 0, (i % spb) % per_chunk))],
        out_shape=[jax.ShapeDtypeStruct((batch, C_HEADS, LANES, seq), BF16),
                   jax.ShapeDtypeStruct((batch * seq, c_w), BF16),
                   jax.ShapeDtypeStruct((batch, C_HEADS, seq // bk, V_ROWS, bk), BF16)],
        compiler_params=_params(("parallel", "arbitrary"), 40),
        name="diff_prep",
    )(proj, cos_t, sin_t)


def _diff_attn_kernel(qt_ref, k_ref, vt_ref, lam_ref, g_ref, o_ref, s_ref, mc_ref, m_ref, acc_ref,
                      *, bq, bk, seq, lam_init, unroll):
    n = seq // bk
    sub = SUBLANES
    qt = qt_ref[0, 0]
    row = lax.broadcasted_iota(jnp.int32, qt.shape, 0)
    zero = jnp.zeros_like(qt)
    qs = (jnp.where(row < C_HALF, qt, zero), jnp.where(row < C_HALF, zero, qt))

    def scores(j, slot):
        kc = k_ref[0, pl.ds(pl.multiple_of(j * bk, bk), bk), :]
        for br in range(2):
            s = jnp.dot(kc, qs[br], preferred_element_type=F32)
            s_ref[slot, br] = s
            mc = jnp.max(s.reshape(bk // sub, sub, bq), axis=0)
            mc_ref[slot, br] = jnp.broadcast_to(jnp.max(mc, axis=0, keepdims=True), (sub, bq))

    def accumulate(j, slot):
        vt = vt_ref[0, 0, j]
        for br in range(2):
            m_old = m_ref[br]
            m_new = jnp.maximum(m_old, mc_ref[slot, br])
            alpha = jnp.exp2(m_old - m_new)
            m_ref[br] = m_new
            s3 = s_ref[slot, br].reshape(bk // sub, sub, bq)
            p = jnp.exp2(s3 - m_new[None]).reshape(bk, bq).astype(BF16)
            pv = jnp.dot(vt, p, preferred_element_type=F32)
            acc3 = acc_ref[br].reshape(V_ROWS // sub, sub, bq)
            acc_ref[br] = (alpha[None] * acc3).reshape(V_ROWS, bq) + pv

    m_ref[...] = jnp.full(m_ref.shape, NEG_BIG, F32)
    acc_ref[...] = jnp.zeros(acc_ref.shape, F32)
    scores(0, 0)

    def trip(j, last):
        for u in range(unroll):
            if not (last and u == unroll - 1):
                scores(j + u + 1, (u + 1) % 2)
            accumulate(j + u, u % 2)

    def body(i, carry):
        trip(unroll * i, False)
        return carry

    lax.fori_loop(0, n // unroll - 1, body, 0)
    trip(n - unroll, True)

    lam = lam_ref[...]
    lam_full = (jnp.exp(jnp.sum(lam[0:1] * lam[1:2], keepdims=True))
                - jnp.exp(jnp.sum(lam[2:3] * lam[3:4], keepdims=True)) + lam_init)

    def normalised(a):
        num = a[0:LANES].reshape(LANES // sub, sub, bq)
        return (num / a[LANES:LANES + sub][None]).reshape(LANES, bq)

    o = normalised(acc_ref[0]) - lam_full * normalised(acc_ref[1])
    y = o * lax.rsqrt(jnp.mean(o * o, axis=0, keepdims=True) + EPS) * g_ref[...] * (1.0 - lam_init)
    o_ref[0] = y.T.astype(o_ref.dtype)


def _diff_attention(qt, k, vt, lam, subln_g, batch, seq, layer_idx, bq=512, bk=512, unroll=8):
    c_w = C_HEADS * 2 * C_HALF
    lam_init = 0.8 - 0.6 * math.exp(-0.3 * layer_idx)
    n = seq // bk
    unroll = min(unroll, n)
    assert unroll % 2 == 0 and n % unroll == 0 and vt.shape[2] == n
    kern = functools.partial(_diff_attn_kernel, bq=bq, bk=bk, seq=seq, lam_init=lam_init, unroll=unroll)
    out = pl.pallas_call(
        kern,
        grid=(batch, C_HEADS, seq // bq),
        in_specs=[pl.BlockSpec((1, 1, LANES, bq), lambda b, h, i: (b, h, 0, i)),
                  pl.BlockSpec((1, seq, LANES), lambda b, h, i: (b, 0, h)),
                  pl.BlockSpec((1, 1, n, V_ROWS, bk), lambda b, h, i: (b, h, 0, 0, 0)),
                  pl.BlockSpec((4, C_HALF), lambda b, h, i: (0, 0)),
                  pl.BlockSpec((LANES, bq), lambda b, h, i: (0, 0))],
        out_specs=pl.BlockSpec((1, bq, LANES), lambda b, h, i: (b, i, h)),
        out_shape=jax.ShapeDtypeStruct((batch, seq, c_w), BF16),
        scratch_shapes=[pltpu.VMEM((2, 2, bk, bq), F32),
                        pltpu.VMEM((2, 2, SUBLANES, bq), F32),
                        pltpu.VMEM((2, SUBLANES, bq), F32),
                        pltpu.VMEM((2, V_ROWS, bq), F32)],
        compiler_params=_params(("parallel", "parallel", "arbitrary"), 48),
        name="diff_attn",
    )(qt, k.reshape(batch, seq, c_w), vt, lam, jnp.broadcast_to(subln_g[:, None], (LANES, bq)))
    return out.reshape(batch * seq, c_w)


def _fourier_wprep_kernel(c_ref, s_ref, w_ref, o_ref, *, scale):
    w = w_ref[0]
    mr = jnp.dot(c_ref[...], w, preferred_element_type=F32, precision=lax.Precision.HIGHEST)
    mi = jnp.dot(s_ref[...], w, preferred_element_type=F32, precision=lax.Precision.HIGHEST)
    o_ref[0, :, :LANES] = (mr * scale).astype(BF16)
    o_ref[0, :, LANES:] = (-mi * scale).astype(BF16)


def _fourier_weights(w_f, seq):
    groups, c, _ = w_f.shape
    idx = jnp.arange(c, dtype=jnp.int32)
    ang = (2.0 * math.pi / c) * ((idx[:, None] * idx[None, :]) % c).astype(F32)
    kern = functools.partial(_fourier_wprep_kernel, scale=1.0 / math.sqrt(seq * c))
    return pl.pallas_call(
        kern,
        grid=(groups,),
        in_specs=[pl.BlockSpec((c, c), lambda g: (0, 0)),
                  pl.BlockSpec((c, c), lambda g: (0, 0)),
                  pl.BlockSpec((1, c, c), lambda g: (g, 0, 0))],
        out_specs=pl.BlockSpec((1, c, 2 * c), lambda g: (g, 0, 0)),
        out_shape=jax.ShapeDtypeStruct((groups, c, 2 * c), BF16),
        compiler_params=_params(("parallel",), 32),
        name="fourier_wprep",
    )(jnp.cos(ang), jnp.sin(ang), w_f)


def _dft_stage1_kernel(z_ref, m_ref, fr_ref, fi_ref, y_ref, *, groups):
    pairs = z_ref.shape[2] // LANES
    u = jnp.concatenate(
        [jnp.dot(z_ref[0, :, j * LANES:(j + 1) * LANES], m_ref[j % groups], preferred_element_type=F32).astype(BF16)
         for j in range(pairs)], axis=1)
    p = jnp.dot(fr_ref[...], u, preferred_element_type=F32)
    q = jnp.dot(fi_ref[...], u, preferred_element_type=F32)
    for j in range(pairs):
        re = slice(2 * j * LANES, (2 * j + 1) * LANES)
        im = slice((2 * j + 1) * LANES, (2 * j + 2) * LANES)
        y_ref[0, :, re] = (p[:, re] - q[:, im]).astype(BF16)
        y_ref[0, :, im] = (q[:, re] + p[:, im]).astype(BF16)


def _dft_stage2_kernel(y_ref, gc_ref, gs_ref, o_ref, *, groups):
    for t in range(y_ref.shape[1]):
        y = y_ref[0, t]
        a = jnp.dot(gc_ref[t], y, preferred_element_type=F32)
        b = jnp.dot(gs_ref[t], y, preferred_element_type=F32)
        for g in range(groups):
            re = slice(2 * g * LANES, (2 * g + 1) * LANES)
            im = slice((2 * g + 1) * LANES, (2 * g + 2) * LANES)
            out_cols = slice((t * groups + g) * LANES, (t * groups + g + 1) * LANES)
            o_ref[0, :, out_cols] = (a[:, re] + b[:, im]).astype(o_ref.dtype)


def _seq_dft_real(z, m_c, batch, seq, tc=4096, kb=4):
    groups = m_c.shape[0]
    n1 = 128 if seq >= 16384 else 64
    n2 = seq // n1
    wc = groups * 2 * LANES
    i1 = jnp.arange(n1, dtype=jnp.int32)
    ang1 = (2.0 * math.pi / n1) * ((i1[:, None] * i1[None, :]) % n1).astype(F32)
    fr = jnp.cos(ang1).astype(BF16)
    fi = (-jnp.sin(ang1)).astype(BF16)
    cols = n2 * wc
    tc = min(tc, cols)
    y = pl.pallas_call(
        functools.partial(_dft_stage1_kernel, groups=groups),
        grid=(batch, cols // tc),
        in_specs=[pl.BlockSpec((1, n1, tc // 2), lambda b, j: (b, 0, j)),
                  pl.BlockSpec((groups, LANES, 2 * LANES), lambda b, j: (0, 0, 0)),
                  pl.BlockSpec((n1, n1), lambda b, j: (0, 0)),
                  pl.BlockSpec((n1, n1), lambda b, j: (0, 0))],
        out_specs=pl.BlockSpec((1, n1, tc), lambda b, j: (b, 0, j)),
        out_shape=jax.ShapeDtypeStruct((batch, n1, cols), BF16),
        compiler_params=_params(("parallel", "parallel"), 32),
        name="dft_stage1",
    )(z.reshape(batch, n1, cols // 2), m_c, fr, fi)
    i2 = jnp.arange(n2, dtype=jnp.int32)
    tw = (i2[None, None, :] * (i1[:, None, None] + n1 * i2[None, :, None])) % seq
    ang2 = (2.0 * math.pi / seq) * tw.astype(F32)
    gc = jnp.cos(ang2).astype(BF16)
    gs = jnp.sin(ang2).astype(BF16)
    out = pl.pallas_call(
        functools.partial(_dft_stage2_kernel, groups=groups),
        grid=(batch, n1 // kb),
        in_specs=[pl.BlockSpec((1, kb, n2, wc), lambda b, k: (b, k, 0, 0)),
                  pl.BlockSpec((kb, n2, n2), lambda b, k: (k, 0, 0)),
                  pl.BlockSpec((kb, n2, n2), lambda b, k: (k, 0, 0))],
        out_specs=pl.BlockSpec((1, n2, kb * groups * LANES), lambda b, k: (b, 0, k)),
        out_shape=jax.ShapeDtypeStruct((batch, n2, n1 * groups * LANES), BF16),
        compiler_params=_params(("parallel", "parallel"), 32),
        name="dft_stage2",
    )(y.reshape(batch, n1, n2, wc), gc, gs)
    return out.reshape(batch * seq, groups * LANES)


def _mix_out_cross_attn_kernel(mixa_ref, mixb_ref, x_ref, wout_ref, g1_ref, g2_ref, wq_ref, kv_ref, wo_ref, g3_ref,
                               o_ref, *, heads, scale, halves):
    ca_w = heads * HEAD_DIM
    hr = x_ref.shape[0] // halves
    ka = mixa_ref.shape[1]
    rows = [slice(r * hr, (r + 1) * hr) for r in range(halves)]
    ys = [jnp.dot(mixa_ref[r, :], wout_ref[0:ka, :], preferred_element_type=F32)
          + jnp.dot(mixb_ref[r, :], wout_ref[ka:, :], preferred_element_type=F32) for r in rows]
    x1s = [x_ref[r, :] + _rms(y, g1_ref[...]) for r, y in zip(rows, ys)]
    qs = [jnp.dot(_rms(x1, g2_ref[...]).astype(BF16), wq_ref[...], preferred_element_type=F32).astype(BF16)
          for x1 in x1s]
    cas = []
    for q in qs:
        outs = []
        for h in range(heads):
            cols = slice(h * HEAD_DIM, (h + 1) * HEAD_DIM)
            s = _dot_nt(q[:, cols], kv_ref[0, :, cols]) * scale
            m = jnp.max(s, axis=-1, keepdims=True)
            p = jnp.exp(s - m)
            l = jnp.sum(p, axis=-1, keepdims=True)
            v = kv_ref[0, :, ca_w + h * HEAD_DIM:ca_w + (h + 1) * HEAD_DIM]
            outs.append((jnp.dot(p.astype(BF16), v, preferred_element_type=F32) / l).astype(BF16))
        cas.append(jnp.concatenate(outs, axis=1))
    zs = [jnp.dot(ca, wo_ref[...], preferred_element_type=F32) for ca in cas]
    for r, x1, z in zip(rows, x1s, zs):
        o_ref[r, :] = x1 + _rms(z, g3_ref[...])


def _mix_out_cross_attn(mix_a, mix_b, x, w_out, w_q, w_o, layer, g1, g2, g3, kv, batch, seq, tm=512):
    t, d = x.shape
    ka, kb = mix_a.shape[1], mix_b.shape[1]
    assert ka + kb == w_out.shape[1]
    ca_w = CA_HEADS * HEAD_DIM
    mem_len = kv.shape[0] // batch
    spb = seq // tm
    kern = functools.partial(_mix_out_cross_attn_kernel, heads=CA_HEADS, scale=HEAD_DIM ** -0.5, halves=2)
    once = pl.Buffered(1)
    vec = lambda: pl.BlockSpec((1, d), lambda i: (0, 0))
    return pl.pallas_call(
        kern,
        grid=(t // tm,),
        in_specs=[pl.BlockSpec((tm, ka), lambda i: (i, 0)),
                  pl.BlockSpec((tm, kb), lambda i: (i, 0)),
                  pl.BlockSpec((tm, d), lambda i: (i, 0)),
                  pl.BlockSpec((None, d, d), lambda i: (layer, 0, 0), pipeline_mode=once),
                  vec(), vec(),
                  pl.BlockSpec((None, d, ca_w), lambda i: (layer, 0, 0), pipeline_mode=once),
                  pl.BlockSpec((1, mem_len, 2 * ca_w), lambda i: (i // spb, 0, 0)),
                  pl.BlockSpec((None, ca_w, d), lambda i: (layer, 0, 0), pipeline_mode=once),
                  vec()],
        out_specs=pl.BlockSpec((tm, d), lambda i: (i, 0)),
        out_shape=jax.ShapeDtypeStruct((t, d), F32),
        compiler_params=_params(("parallel",), 56),
        name="mix_out_cross_attn",
    )(mix_a, mix_b, x, w_out, g1.reshape(1, d), g2.reshape(1, d), w_q, kv.reshape(batch, mem_len, 2 * ca_w), w_o,
      g3.reshape(1, d))


HALO = 16


def _ffn_kernel(x_ref, xp_ref, xn_ref, g4_ref, wg_ref, wv_ref, cwg_ref, cwv_ref, cbg_ref, cbv_ref,
                wd_ref, g5_ref, o_ref, xe_ref, zg_ref, zv_ref, *, tm, tc, tiles_per_seq, nf):
    acc_ref = o_ref
    i = pl.program_id(0)
    f = pl.program_id(1)

    @pl.when(f == 0)
    def _():
        g4 = g4_ref[...]
        pos = i % tiles_per_seq
        prev = _rms(xp_ref[...], g4) * jnp.where(pos == 0, 0.0, 1.0)
        nxt = _rms(xn_ref[...], g4) * jnp.where(pos == tiles_per_seq - 1, 0.0, 1.0)
        xe_ref[0:HALO, :] = prev.astype(BF16)
        xe_ref[HALO:HALO + tm, :] = _rms(x_ref[...], g4).astype(BF16)
        xe_ref[HALO + tm:, :] = nxt.astype(BF16)
        acc_ref[...] = jnp.zeros(acc_ref.shape, F32)

    def conv(z_ref, c, cw, cb, lo, rows):
        lo = HALO + lo
        return (z_ref[c, lo - 1:lo - 1 + rows, :] * cw[0:1] + z_ref[c, lo:lo + rows, :] * cw[1:2]
                + z_ref[c, lo + 1:lo + 1 + rows, :] * cw[2:3] + cb)

    xe = xe_ref[...]
    n_sub = zg_ref.shape[0]
    for c in range(n_sub):
        cols = slice(c * tc, (c + 1) * tc)
        zg_ref[c] = jnp.dot(xe, wg_ref[:, cols], preferred_element_type=F32)
        zv_ref[c] = jnp.dot(xe, wv_ref[:, cols], preferred_element_type=F32)
    for c in range(n_sub):
        cols = slice(c * tc, (c + 1) * tc)
        pieces = 2 if c == n_sub - 1 else 1
        rows = tm // pieces
        for r in range(pieces):
            lo = r * rows
            h = (jax.nn.gelu(conv(zg_ref, c, cwg_ref[:, cols], cbg_ref[:, cols], lo, rows))
                 * conv(zv_ref, c, cwv_ref[:, cols], cbv_ref[:, cols], lo, rows))
            acc_ref[lo:lo + rows, :] += jnp.dot(h.astype(BF16), wd_ref[cols, :], preferred_element_type=F32)

    @pl.when(f == nf - 1)
    def _():
        o_ref[...] = x_ref[...] + _rms(acc_ref[...], g5_ref[...])


def _conv_ffn(x, seq, g4, w_up, conv_w, conv_b, w_down, layer, g5, tm=1024, tf=512, tc=256):
    t, d = x.shape
    d_ff = w_down.shape[1]
    nf = d_ff // tf
    hb = tm // HALO
    last_hb = t // HALO - 1
    kern = functools.partial(_ffn_kernel, tm=tm, tc=tc, tiles_per_seq=seq // tm, nf=nf)
    cb = conv_b.reshape(1, 2 * d_ff)
    return pl.pallas_call(
        kern,
        grid=(t // tm, nf),
        in_specs=[pl.BlockSpec((tm, d), lambda i, f: (i, 0), pipeline_mode=pl.Buffered(1)),
                  pl.BlockSpec((HALO, d), lambda i, f: (jnp.maximum(i * hb - 1, 0), 0)),
                  pl.BlockSpec((HALO, d), lambda i, f: (jnp.minimum((i + 1) * hb, last_hb), 0)),
                  pl.BlockSpec((1, d), lambda i, f: (0, 0)),
                  pl.BlockSpec((None, d, tf), lambda i, f: (layer, 0, f)),
                  pl.BlockSpec((None, d, tf), lambda i, f: (layer, 0, nf + f)),
                  pl.BlockSpec((CONV_W, tf), lambda i, f: (0, f)),
                  pl.BlockSpec((CONV_W, tf), lambda i, f: (0, nf + f)),
                  pl.BlockSpec((1, tf), lambda i, f: (0, f)),
                  pl.BlockSpec((1, tf), lambda i, f: (0, nf + f)),
                  pl.BlockSpec((None, tf, d), lambda i, f: (layer, f, 0)),
                  pl.BlockSpec((1, d), lambda i, f: (0, 0))],
        out_specs=pl.BlockSpec((tm, d), lambda i, f: (i, 0)),
        out_shape=jax.ShapeDtypeStruct((t, d), F32),
        scratch_shapes=[pltpu.VMEM((tm + 2 * HALO, d), BF16),
                        pltpu.VMEM((tf // tc, tm + 2 * HALO, tc), F32),
                        pltpu.VMEM((tf // tc, tm + 2 * HALO, tc), F32)],
        compiler_params=_params(("parallel", "arbitrary"), 58),
        name="conv_ffn",
    )(x, x, x, g4.reshape(1, d), w_up, w_up, conv_w, conv_w, cb, cb, w_down, g5.reshape(1, d))


def _trunk(x3, mem3, p):
    batch, seq, d = x3.shape
    x = x3.reshape(batch * seq, d)
    mem = mem3.reshape(-1, d)
    depth = p["w_in"].shape[0]
    a_w = A_HEADS * HEAD_DIM
    c_w = C_HEADS * 2 * C_HALF
    for layer in range(depth):
        g = p["norm_gains"][layer]
        proj = _norm_matmul(x, g[0], p["w_in"], layer, BF16)
        if layer % 2 == 0:
            e = layer // 2
            qkvs = _even_prep(proj, p["rope"][HEAD_DIM], batch, seq, HEAD_DIM ** -0.5)
            seqs = [qkv.reshape(batch * dil, seq // dil, 3 * a_w) for (_, dil), qkv in zip(DILATED_CFGS, qkvs)]
            dilated = [_band_attention(s) for s in seqs[1:]]
            ya = _band_attention(seqs[0], dilated).reshape(batch * seq, a_w)
            yb = _spatial_gating(proj, p["b_w_spatial"][e], p["b_b_spatial"][e], p["b_ln_gain"][e],
                                 3 * A_HEADS, 3 * A_HEADS + B_GROUPS)
            mix = (ya, yb)
        else:
            o = layer // 2
            qt, kk, vt = _diff_prep(proj, p["rope"][C_HALF], batch, seq, C_HALF ** -0.5 * math.log2(math.e),
                                    DIFF_BK)
            yc = _diff_attention(qt, kk, vt, p["c_lambda"][o], p["c_subln_gain"][o], batch, seq, layer,
                                 bk=DIFF_BK)
            m_c = _fourier_weights(p["d_w_fourier"][o], seq)
            yd = _seq_dft_real(proj[:, 3 * c_w:], m_c, batch, seq)
            mix = (yc, yd)
        kv = _norm_matmul(mem, p["mem_norm_gain"][layer], p["ca_w_kv"], layer, BF16)
        x = _mix_out_cross_attn(*mix, x, p["w_out"], p["ca_w_q"], p["ca_w_o"], layer, g[1], g[2], g[3], kv,
                                batch, seq)
        x = _conv_ffn(x, seq, g[4], p["ffn_w_up"], p["ffn_conv_w"][layer], p["ffn_conv_b"][layer],
                      p["ffn_w_down"], layer, g[5])
    return x.reshape(batch, seq, d)


def kernel(x_prompt, x_sample, mem_prompt, mem_sample, norm_gains, w_in, w_out, b_w_spatial, b_b_spatial,
           b_ln_gain, c_lambda, c_subln_gain, d_w_fourier, mem_norm_gain, ca_w_q, ca_w_kv, ca_w_o, ffn_w_up,
           ffn_conv_w, ffn_conv_b, ffn_w_down):
    p = dict(norm_gains=norm_gains, w_in=w_in.astype(BF16), w_out=w_out.astype(BF16), b_w_spatial=b_w_spatial,
             b_b_spatial=b_b_spatial, b_ln_gain=b_ln_gain, c_lambda=c_lambda, c_subln_gain=c_subln_gain,
             d_w_fourier=d_w_fourier, mem_norm_gain=mem_norm_gain, ca_w_q=ca_w_q.astype(BF16),
             ca_w_kv=ca_w_kv.astype(BF16), ca_w_o=ca_w_o.astype(BF16), ffn_w_up=ffn_w_up.astype(BF16),
             ffn_conv_w=ffn_conv_w, ffn_conv_b=ffn_conv_b, ffn_w_down=ffn_w_down.astype(BF16))
    seq_max = max(x_prompt.shape[1], x_sample.shape[1])
    p["rope"] = {w: _rope_tables(seq_max, w) for w in (HEAD_DIM, C_HALF)}
    return _trunk(x_prompt, mem_prompt, p), _trunk(x_sample, mem_sample, p)
```

```python
import functools
import math

import jax
import jax.numpy as jnp
from jax import lax
from jax.experimental import pallas as pl
from jax.experimental.pallas import tpu as pltpu

F32 = jnp.float32
BF16 = jnp.bfloat16

EPS = 1e-6
ROPE_THETA = 500000.0
ROPE_FRACTION = 4
LANES = 128
SUBLANES = 8
HEAD_DIM = 128
A_HEADS = 8
DILATED_CFGS = ((128, 1), (512, 4), (2048, 16))
BAND_RADIUS = 64
assert all(window // (2 * dil) == BAND_RADIUS for window, dil in DILATED_CFGS)
B_GROUPS = 8
CHUNK = 128
C_HEADS = 12
C_HALF = 64
CA_HEADS = 4
CONV_W = 3
NEG_BIG = -1e30
MIB = 1024 * 1024


def _params(semantics, vmem_mib):
    return pltpu.CompilerParams(dimension_semantics=semantics, vmem_limit_bytes=vmem_mib * MIB)


def _rms(x, g):
    return x * lax.rsqrt(jnp.mean(x * x, axis=-1, keepdims=True) + EPS) * g


def _dot_nt(a, b):
    return lax.dot_general(a, b, (((1,), (1,)), ((), ())), preferred_element_type=F32)


def _norm_mm_kernel(x_ref, g_ref, w_ref, o_ref, xn_ref):
    j = pl.program_id(1)

    @pl.when(j == 0)
    def _():
        half = x_ref.shape[0] // 2
        for r in (slice(0, half), slice(half, 2 * half)):
            xn_ref[r, :] = _rms(x_ref[r, :], g_ref[...]).astype(BF16)
            o_ref[r, :] = jnp.dot(xn_ref[r, :], w_ref[...], preferred_element_type=F32).astype(o_ref.dtype)

    @pl.when(j > 0)
    def _():
        o_ref[...] = jnp.dot(xn_ref[...], w_ref[...], preferred_element_type=F32).astype(o_ref.dtype)


def _norm_matmul(x, g, w, layer, out_dtype, tm=1024, tn=1024):
    t, k = x.shape
    n = w.shape[2]
    tm = min(tm, t)
    return pl.pallas_call(
        _norm_mm_kernel,
        grid=(t // tm, n // tn),
        in_specs=[pl.BlockSpec((tm, k), lambda i, j: (i, 0)),
                  pl.BlockSpec((1, k), lambda i, j: (0, 0)),
                  pl.BlockSpec((None, k, tn), lambda i, j: (layer, 0, j))],
        out_specs=pl.BlockSpec((tm, tn), lambda i, j: (i, j)),
        out_shape=jax.ShapeDtypeStruct((t, n), out_dtype),
        scratch_shapes=[pltpu.VMEM((tm, k), BF16)],
        compiler_params=_params(("parallel", "arbitrary"), 40),
        name="norm_matmul",
    )(x, g.reshape(1, k), w)


def _rope_tables(seq, head_w):
    rd = head_w // ROPE_FRACTION
    half = rd // 2
    inv = ROPE_THETA ** (-jnp.arange(half, dtype=F32) / half)
    ang = jnp.arange(seq, dtype=F32)[:, None] * inv[None, :]
    reps = LANES // half
    cos = jnp.broadcast_to(jnp.cos(ang)[:, None, :], (seq, reps, half)).reshape(seq, LANES)
    sin = jnp.broadcast_to(jnp.sin(ang)[:, None, :], (seq, reps, half)).reshape(seq, LANES)
    lane = (jnp.arange(LANES, dtype=jnp.int32) & (head_w - 1))[None, :]
    return jnp.where(lane < rd, cos, 1.0), jnp.where(lane < rd, jnp.where(lane < half, -sin, sin), 0.0)


def _rope_block(x, c, s, head_w):
    half = head_w // ROPE_FRACTION // 2
    first = (lax.broadcasted_iota(jnp.int32, x.shape, 1) & (head_w - 1)) < half
    partner = jnp.where(first, pltpu.roll(x, LANES - half, axis=1), pltpu.roll(x, half, axis=1))
    return x * c + partner * s


def _even_prep_kernel(x_ref, c_ref, s_ref, *refs, q_scale, n_blocks):
    out_refs, y_ref = refs[:-1], refs[-1]
    part = pl.program_id(1)
    tm = x_ref.shape[0]

    def rope(scale):
        c = c_ref[...]
        s = s_ref[...]
        for b in range(n_blocks):
            cols = slice(b * LANES, (b + 1) * LANES)
            y_ref[b] = _rope_block(x_ref[:, cols].astype(F32), c, s, HEAD_DIM) * scale

    @pl.when(part == 0)
    def _():
        rope(q_scale)

    @pl.when(part == 1)
    def _():
        rope(1.0)

    @pl.when(part == 2)
    def _():
        for b in range(n_blocks):
            y_ref[b] = x_ref[:, b * LANES:(b + 1) * LANES].astype(F32)

    for (_, dil), o_ref in zip(DILATED_CFGS, out_refs):
        for r in range(dil):
            for b in range(n_blocks):
                rows = y_ref[b, pl.ds(r, tm // dil, stride=dil), :]
                o_ref[0, r, :, b * LANES:(b + 1) * LANES] = rows.astype(BF16)


def _even_prep(proj, tables, batch, seq, q_scale, tm=1024):
    a_w = A_HEADS * HEAD_DIM
    cos_t, sin_t = tables
    spb = seq // tm
    kern = functools.partial(_even_prep_kernel, q_scale=q_scale, n_blocks=a_w // LANES)
    return pl.pallas_call(
        kern,
        grid=(batch * spb, 3),
        in_specs=[pl.BlockSpec((tm, a_w), lambda i, p: (i, p)),
                  pl.BlockSpec((tm, LANES), lambda i, p: (i % spb, 0)),
                  pl.BlockSpec((tm, LANES), lambda i, p: (i % spb, 0))],
        out_specs=[pl.BlockSpec((1, dil, tm // dil, a_w), lambda i, p: (i // spb, 0, i % spb, p))
                   for _, dil in DILATED_CFGS],
        out_shape=[jax.ShapeDtypeStruct((batch, dil, seq // dil, 3 * a_w), BF16) for _, dil in DILATED_CFGS],
        scratch_shapes=[pltpu.VMEM((a_w // LANES, tm, LANES), F32)],
        compiler_params=_params(("parallel", "arbitrary"), 32),
        name="even_prep",
    )(proj, cos_t, sin_t)


def _band_attn_kernel(q_ref, kp_ref, kc_ref, kn_ref, vp_ref, vc_ref, vn_ref, *refs, bq, sub_len, merge):
    if merge:
        o4, l4, o16, l16, y_ref, so4, sl4, so16, sl16 = refs
        for dil, src, dst in ((4, o4, so4), (4, l4, sl4), (16, o16, so16), (16, l16, sl16)):
            for r in range(dil):
                for h in range(A_HEADS):
                    dst[h, pl.ds(r, bq // dil, stride=dil), :] = (
                        src[0, r, :, h * HEAD_DIM:(h + 1) * HEAD_DIM].astype(F32))
    else:
        o_ref, lse_ref = refs
    q0 = pl.program_id(1) * bq
    sq = 2 * BAND_RADIUS
    win = sq + 2 * BAND_RADIUS
    n_sub = bq // sq
    rel = (lax.broadcasted_iota(jnp.int32, (sq, win), 1) - BAND_RADIUS
           - lax.broadcasted_iota(jnp.int32, (sq, win), 0))
    band = (rel <= BAND_RADIUS) & (rel >= -BAND_RADIUS)
    biases = []
    for u in range(n_sub):
        kpos = q0 + u * sq - BAND_RADIUS + lax.broadcasted_iota(jnp.int32, (sq, win), 1)
        ok = band & (kpos >= 0) & (kpos < sub_len)
        biases.append(jnp.where(ok, 0.0, NEG_BIG))

    def window(p_ref, c_ref, n_ref, cols, u):
        parts = []
        if u == 0:
            parts.append(p_ref[0, :, cols])
        lo = max(u * sq - BAND_RADIUS, 0)
        hi = min(u * sq + sq + BAND_RADIUS, bq)
        parts.append(c_ref[0, lo:hi, cols])
        if u == n_sub - 1:
            parts.append(n_ref[0, :, cols])
        return jnp.concatenate(parts, axis=0)

    tiles = [(h, u) for h in range(A_HEADS) for u in range(n_sub)]

    def scores(h, u):
        cols = slice(h * HEAD_DIM, (h + 1) * HEAD_DIM)
        q = q_ref[0, u * sq:(u + 1) * sq, cols]
        return _dot_nt(q, window(kp_ref, kc_ref, kn_ref, cols, u)) + biases[u]

    s_next = scores(*tiles[0])
    for n, (h, u) in enumerate(tiles):
        cols = slice(h * HEAD_DIM, (h + 1) * HEAD_DIM)
        rows = slice(u * sq, (u + 1) * sq)
        s = s_next
        if n + 1 < len(tiles):
            s_next = scores(*tiles[n + 1])
        m = jnp.max(s, axis=-1, keepdims=True)
        p = jnp.exp(s - m)
        l = jnp.sum(p, axis=-1, keepdims=True)
        o = jnp.dot(p.astype(BF16), window(vp_ref, vc_ref, vn_ref, cols, u), preferred_element_type=F32) / l
        lse = jnp.broadcast_to(m + jnp.log(l), (sq, HEAD_DIM))
        if merge:
            b, c = sl4[h, rows, :], sl16[h, rows, :]
            top = jnp.maximum(jnp.maximum(lse, b), c)
            wa, wb, wc = jnp.exp(lse - top), jnp.exp(b - top), jnp.exp(c - top)
            num = wa * o + wb * so4[h, rows, :] + wc * so16[h, rows, :]
            y_ref[0, rows, cols] = (num / (wa + wb + wc)).astype(y_ref.dtype)
        else:
            o_ref[0, rows, cols] = o.astype(o_ref.dtype)
            lse_ref[0, rows, cols] = lse


def _band_attention(qkv, dilated=None, bq=512):
    n_seq, sub_len, _ = qkv.shape
    a_w = A_HEADS * HEAD_DIM
    bq = min(bq, sub_len)
    hb = bq // BAND_RADIUS
    last = sub_len // BAND_RADIUS - 1
    merge = dilated is not None
    kern = functools.partial(_band_attn_kernel, bq=bq, sub_len=sub_len, merge=merge)

    def halo_specs(col):
        return [pl.BlockSpec((1, BAND_RADIUS, a_w), lambda s, i: (s, jnp.maximum(i * hb - 1, 0), col)),
                pl.BlockSpec((1, bq, a_w), lambda s, i: (s, i, col)),
                pl.BlockSpec((1, BAND_RADIUS, a_w), lambda s, i: (s, jnp.minimum((i + 1) * hb, last), col))]

    out_spec = pl.BlockSpec((1, bq, a_w), lambda s, i: (s, i, 0))
    in_specs = [pl.BlockSpec((1, bq, a_w), lambda s, i: (s, i, 0))] + halo_specs(1) + halo_specs(2)
    operands = [qkv] * 7
    if merge:
        for (_, dil), pair in zip(DILATED_CFGS[1:], dilated):
            for arr in pair:
                operands.append(arr.reshape(n_seq, dil, sub_len // dil, a_w))
                in_specs.append(pl.BlockSpec((1, dil, bq // dil, a_w), lambda s, i: (s, 0, i, 0)))
        out_specs, out_shape = out_spec, jax.ShapeDtypeStruct((n_seq, sub_len, a_w), BF16)
        scratch = [pltpu.VMEM((A_HEADS, bq, HEAD_DIM), F32)] * 4
    else:
        out_specs = [out_spec, out_spec]
        out_shape = [jax.ShapeDtypeStruct((n_seq, sub_len, a_w), BF16),
                     jax.ShapeDtypeStruct((n_seq, sub_len, a_w), F32)]
        scratch = []
    return pl.pallas_call(
        kern,
        grid=(n_seq, sub_len // bq),
        in_specs=in_specs,
        out_specs=out_specs,
        out_shape=out_shape,
        scratch_shapes=scratch,
        compiler_params=_params(("parallel", "parallel"), 40),
        name="band_attn_merge" if merge else "band_attn",
    )(*operands)


def _gating_kernel(u_ref, v_ref, w_ref, b_ref, g_ref, o_ref, *, n_chunks):
    w = w_ref[0]
    bias = b_ref[0]
    gain = g_ref[0]
    for c in range(n_chunks):
        rows = pl.ds(c * CHUNK, CHUNK)
        v = jax.nn.gelu(v_ref[rows, :].astype(F32))
        mu = jnp.mean(v, axis=-1, keepdims=True)
        d = v - mu
        var = jnp.mean(d * d, axis=-1, keepdims=True)
        vn = d * lax.rsqrt(var + EPS) * gain
        sv = jnp.dot(w, vn.astype(BF16), preferred_element_type=F32) + bias
        o_ref[rows, :] = (jax.nn.gelu(u_ref[rows, :].astype(F32)) * sv).astype(o_ref.dtype)


def _spatial_gating(proj, w_s, b_s, g_ln, u_col, v_col, tm=2048):
    t = proj.shape[0]
    tm = min(tm, t)
    groups = w_s.shape[0]
    bias = jnp.broadcast_to(b_s[:, :, None], (groups, CHUNK, LANES))
    kern = functools.partial(_gating_kernel, n_chunks=tm // CHUNK)
    return pl.pallas_call(
        kern,
        grid=(t // tm, groups),
        in_specs=[pl.BlockSpec((tm, LANES), lambda i, g: (i, u_col + g)),
                  pl.BlockSpec((tm, LANES), lambda i, g: (i, v_col + g)),
                  pl.BlockSpec((1, CHUNK, CHUNK), lambda i, g: (g, 0, 0)),
                  pl.BlockSpec((1, CHUNK, LANES), lambda i, g: (g, 0, 0)),
                  pl.BlockSpec((1, 1, LANES), lambda i, g: (g, 0, 0))],
        out_specs=pl.BlockSpec((tm, LANES), lambda i, g: (i, g)),
        out_shape=jax.ShapeDtypeStruct((t, groups * LANES), BF16),
        compiler_params=_params(("parallel", "parallel"), 32),
        name="spatial_gating",
    )(proj, proj, w_s.astype(BF16), bias, g_ln.reshape(groups, 1, LANES))


V_ROWS = 2 * C_HALF + 16
DIFF_BK = 512


def _diff_prep_kernel(x_ref, c_ref, s_ref, qt_ref, k_ref, vt_ref, *, q_scale, n_blocks):
    part = pl.program_id(1)
    tm = x_ref.shape[0]

    @pl.when(part == 0)
    def _():
        c = c_ref[...]
        s = s_ref[...]
        for h in range(n_blocks):
            cols = slice(h * LANES, (h + 1) * LANES)
            qt_ref[0, h] = (_rope_block(x_ref[:, cols].astype(F32), c, s, C_HALF) * q_scale).T.astype(BF16)

    @pl.when(part == 1)
    def _():
        c = c_ref[...]
        s = s_ref[...]
        for h in range(n_blocks):
            cols = slice(h * LANES, (h + 1) * LANES)
            k_ref[:, cols] = _rope_block(x_ref[:, cols].astype(F32), c, s, C_HALF).astype(BF16)

    @pl.when(part == 2)
    def _():
        ones = jnp.ones((V_ROWS - LANES, tm), BF16)
        for h in range(n_blocks):
            vt_ref[0, h, 0, 0:LANES, :] = x_ref[:, h * LANES:(h + 1) * LANES].astype(F32).T.astype(BF16)
            vt_ref[0, h, 0, LANES:, :] = ones


def _diff_prep(proj, tables, batch, seq, q_scale, bk, tm=512):
    c_w = C_HEADS * 2 * C_HALF
    cos_t, sin_t = tables
    spb = seq // tm
    per_chunk = bk // tm
    kern = functools.partial(_diff_prep_kernel, q_scale=q_scale, n_blocks=C_HEADS)
    return pl.pallas_call(
        kern,
        grid=(batch * spb, 3),
        in_specs=[pl.BlockSpec((tm, c_w), lambda i, p: (i, p)),
                  pl.BlockSpec((tm, LANES), lambda i, p: (i % spb, 0)),
                  pl.BlockSpec((tm, LANES), lambda i, p: (i % spb, 0))],
        out_specs=[pl.BlockSpec((1, C_HEADS, LANES, tm), lambda i, p: (i // spb, 0, 0, i % spb)),
                   pl.BlockSpec((tm, c_w), lambda i, p: (i, 0)),
                   pl.BlockSpec((1, C_HEADS, 1, V_ROWS, tm),
                                lambda i, p: (i // spb, 0, (i % spb) // per_chunk, 0, (i % spb) % per_chunk))],
        out_shape=[jax.ShapeDtypeStruct((batch, C_HEADS, LANES, seq), BF16),
                   jax.ShapeDtypeStruct((batch * seq, c_w), BF16),
                   jax.ShapeDtypeStruct((batch, C_HEADS, seq // bk, V_ROWS, bk), BF16)],
        compiler_params=_params(("parallel", "arbitrary"), 40),
        name="diff_prep",
    )(proj, cos_t, sin_t)


def _diff_attn_kernel(qt_ref, k_ref, vt_ref, lam_ref, g_ref, o_ref, s_ref, mc_ref, m_ref, acc_ref,
                      *, bq, bk, seq, lam_init, unroll):
    n = seq // bk
    sub = SUBLANES
    qt = qt_ref[0, 0]
    row = lax.broadcasted_iota(jnp.int32, qt.shape, 0)
    zero = jnp.zeros_like(qt)
    qs = (jnp.where(row < C_HALF, qt, zero), jnp.where(row < C_HALF, zero, qt))

    def scores(j, slot):
        kc = k_ref[0, pl.ds(pl.multiple_of(j * bk, bk), bk), :]
        for br in range(2):
            s = jnp.dot(kc, qs[br], preferred_element_type=F32)
            s_ref[slot, br] = s
            mc = jnp.max(s.reshape(bk // sub, sub, bq), axis=0)
            mc_ref[slot, br] = jnp.broadcast_to(jnp.max(mc, axis=0, keepdims=True), (sub, bq))

    def accumulate(j, slot):
        vt = vt_ref[0, 0, j]
        for br in range(2):
            m_old = m_ref[br]
            m_new = jnp.maximum(m_old, mc_ref[slot, br])
            alpha = jnp.exp2(m_old - m_new)
            m_ref[br] = m_new
            s3 = s_ref[slot, br].reshape(bk // sub, sub, bq)
            p = jnp.exp2(s3 - m_new[None]).reshape(bk, bq).astype(BF16)
            pv = jnp.dot(vt, p, preferred_element_type=F32)
            acc3 = acc_ref[br].reshape(V_ROWS // sub, sub, bq)
            acc_ref[br] = (alpha[None] * acc3).reshape(V_ROWS, bq) + pv

    m_ref[...] = jnp.full(m_ref.shape, NEG_BIG, F32)
    acc_ref[...] = jnp.zeros(acc_ref.shape, F32)
    scores(0, 0)

    def trip(j, last):
        for u in range(unroll):
            if not (last and u == unroll - 1):
                scores(j + u + 1, (u + 1) % 2)
            accumulate(j + u, u % 2)

    def body(i, carry):
        trip(unroll * i, False)
        return carry

    lax.fori_loop(0, n // unroll - 1, body, 0)
    trip(n - unroll, True)

    lam = lam_ref[...]
    lam_full = (jnp.exp(jnp.sum(lam[0:1] * lam[1:2], keepdims=True))
                - jnp.exp(jnp.sum(lam[2:3] * lam[3:4], keepdims=True)) + lam_init)

    def normalised(a):
        num = a[0:LANES].reshape(LANES // sub, sub, bq)
        return (num / a[LANES:LANES + sub][None]).reshape(LANES, bq)

    o = normalised(acc_ref[0]) - lam_full * normalised(acc_ref[1])
    y = o * lax.rsqrt(jnp.mean(o * o, axis=0, keepdims=True) + EPS) * g_ref[...] * (1.0 - lam_init)
    o_ref[0] = y.T.astype(o_ref.dtype)


def _diff_attention(qt, k, vt, lam, subln_g, batch, seq, layer_idx, bq=512, bk=512, unroll=8):
    c_w = C_HEADS * 2 * C_HALF
    lam_init = 0.8 - 0.6 * math.exp(-0.3 * layer_idx)
    n = seq // bk
    unroll = min(unroll, n)
    assert unroll % 2 == 0 and n % unroll == 0 and vt.shape[2] == n
    kern = functools.partial(_diff_attn_kernel, bq=bq, bk=bk, seq=seq, lam_init=lam_init, unroll=unroll)
    out = pl.pallas_call(
        kern,
        grid=(batch, C_HEADS, seq // bq),
        in_specs=[pl.BlockSpec((1, 1, LANES, bq), lambda b, h, i: (b, h, 0, i)),
                  pl.BlockSpec((1, seq, LANES), lambda b, h, i: (b, 0, h)),
                  pl.BlockSpec((1, 1, n, V_ROWS, bk), lambda b, h, i: (b, h, 0, 0, 0)),
                  pl.BlockSpec((4, C_HALF), lambda b, h, i: (0, 0)),
                  pl.BlockSpec((LANES, bq), lambda b, h, i: (0, 0))],
        out_specs=pl.BlockSpec((1, bq, LANES), lambda b, h, i: (b, i, h)),
        out_shape=jax.ShapeDtypeStruct((batch, seq, c_w), BF16),
        scratch_shapes=[pltpu.VMEM((2, 2, bk, bq), F32),
                        pltpu.VMEM((2, 2, SUBLANES, bq), F32),
                        pltpu.VMEM((2, SUBLANES, bq), F32),
                        pltpu.VMEM((2, V_ROWS, bq), F32)],
        compiler_params=_params(("parallel", "parallel", "arbitrary"), 48),
        name="diff_attn",
    )(qt, k.reshape(batch, seq, c_w), vt, lam, jnp.broadcast_to(subln_g[:, None], (LANES, bq)))
    return out.reshape(batch * seq, c_w)


def _fourier_wprep_kernel(c_ref, s_ref, w_ref, o_ref, *, scale):
    w = w_ref[0]
    mr = jnp.dot(c_ref[...], w, preferred_element_type=F32, precision=lax.Precision.HIGHEST)
    mi = jnp.dot(s_ref[...], w, preferred_element_type=F32, precision=lax.Precision.HIGHEST)
    o_ref[0, :, :LANES] = (mr * scale).astype(BF16)
    o_ref[0, :, LANES:] = (-mi * scale).astype(BF16)


def _fourier_weights(w_f, seq):
    groups, c, _ = w_f.shape
    idx = jnp.arange(c, dtype=jnp.int32)
    ang = (2.0 * math.pi / c) * ((idx[:, None] * idx[None, :]) % c).astype(F32)
    kern = functools.partial(_fourier_wprep_kernel, scale=1.0 / math.sqrt(seq * c))
    return pl.pallas_call(
        kern,
        grid=(groups,),
        in_specs=[pl.BlockSpec((c, c), lambda g: (0, 0)),
                  pl.BlockSpec((c, c), lambda g: (0, 0)),
                  pl.BlockSpec((1, c, c), lambda g: (g, 0, 0))],
        out_specs=pl.BlockSpec((1, c, 2 * c), lambda g: (g, 0, 0)),
        out_shape=jax.ShapeDtypeStruct((groups, c, 2 * c), BF16),
        compiler_params=_params(("parallel",), 32),
        name="fourier_wprep",
    )(jnp.cos(ang), jnp.sin(ang), w_f)


def _dft_stage1_kernel(z_ref, m_ref, fr_ref, fi_ref, y_ref, *, groups):
    pairs = z_ref.shape[2] // LANES
    u = jnp.concatenate(
        [jnp.dot(z_ref[0, :, j * LANES:(j + 1) * LANES], m_ref[j % groups], preferred_element_type=F32).astype(BF16)
         for j in range(pairs)], axis=1)
    p = jnp.dot(fr_ref[...], u, preferred_element_type=F32)
    q = jnp.dot(fi_ref[...], u, preferred_element_type=F32)
    for j in range(pairs):
        re = slice(2 * j * LANES, (2 * j + 1) * LANES)
        im = slice((2 * j + 1) * LANES, (2 * j + 2) * LANES)
        y_ref[0, :, re] = (p[:, re] - q[:, im]).astype(BF16)
        y_ref[0, :, im] = (q[:, re] + p[:, im]).astype(BF16)


def _dft_stage2_kernel(y_ref, gc_ref, gs_ref, o_ref, *, groups):
    for t in range(y_ref.shape[1]):
        y = y_ref[0, t]
        a = jnp.dot(gc_ref[t], y, preferred_element_type=F32)
        b = jnp.dot(gs_ref[t], y, preferred_element_type=F32)
        for g in range(groups):
            re = slice(2 * g * LANES, (2 * g + 1) * LANES)
            im = slice((2 * g + 1) * LANES, (2 * g + 2) * LANES)
            out_cols = slice((t * groups + g) * LANES, (t * groups + g + 1) * LANES)
            o_ref[0, :, out_cols] = (a[:, re] + b[:, im]).astype(o_ref.dtype)


def _seq_dft_real(z, m_c, batch, seq, tc=4096, kb=4):
    groups = m_c.shape[0]
    n1 = 128 if seq >= 16384 else 64
    n2 = seq // n1
    wc = groups * 2 * LANES
    i1 = jnp.arange(n1, dtype=jnp.int32)
    ang1 = (2.0 * math.pi / n1) * ((i1[:, None] * i1[None, :]) % n1).astype(F32)
    fr = jnp.cos(ang1).astype(BF16)
    fi = (-jnp.sin(ang1)).astype(BF16)
    cols = n2 * wc
    tc = min(tc, cols)
    y = pl.pallas_call(
        functools.partial(_dft_stage1_kernel, groups=groups),
        grid=(batch, cols // tc),
        in_specs=[pl.BlockSpec((1, n1, tc // 2), lambda b, j: (b, 0, j)),
                  pl.BlockSpec((groups, LANES, 2 * LANES), lambda b, j: (0, 0, 0)),
                  pl.BlockSpec((n1, n1), lambda b, j: (0, 0)),
                  pl.BlockSpec((n1, n1), lambda b, j: (0, 0))],
        out_specs=pl.BlockSpec((1, n1, tc), lambda b, j: (b, 0, j)),
        out_shape=jax.ShapeDtypeStruct((batch, n1, cols), BF16),
        compiler_params=_params(("parallel", "parallel"), 32),
        name="dft_stage1",
    )(z.reshape(batch, n1, cols // 2), m_c, fr, fi)
    i2 = jnp.arange(n2, dtype=jnp.int32)
    tw = (i2[None, None, :] * (i1[:, None, None] + n1 * i2[None, :, None])) % seq
    ang2 = (2.0 * math.pi / seq) * tw.astype(F32)
    gc = jnp.cos(ang2).astype(BF16)
    gs = jnp.sin(ang2).astype(BF16)
    out = pl.pallas_call(
        functools.partial(_dft_stage2_kernel, groups=groups),
        grid=(batch, n1 // kb),
        in_specs=[pl.BlockSpec((1, kb, n2, wc), lambda b, k: (b, k, 0, 0)),
                  pl.BlockSpec((kb, n2, n2), lambda b, k: (k, 0, 0)),
                  pl.BlockSpec((kb, n2, n2), lambda b, k: (k, 0, 0))],
        out_specs=pl.BlockSpec((1, n2, kb * groups * LANES), lambda b, k: (b, 0, k)),
        out_shape=jax.ShapeDtypeStruct((batch, n2, n1 * groups * LANES), BF16),
        compiler_params=_params(("parallel", "parallel"), 32),
        name="dft_stage2",
    )(y.reshape(batch, n1, n2, wc), gc, gs)
    return out.reshape(batch * seq, groups * LANES)


def _mix_out_cross_attn_kernel(mixa_ref, mixb_ref, x_ref, wout_ref, g1_ref, g2_ref, wq_ref, kv_ref, wo_ref, g3_ref,
                               o_ref, *, heads, scale, halves):
    ca_w = heads * HEAD_DIM
    hr = x_ref.shape[0] // halves
    ka = mixa_ref.shape[1]
    rows = [slice(r * hr, (r + 1) * hr) for r in range(halves)]
    ys = [jnp.dot(mixa_ref[r, :], wout_ref[0:ka, :], preferred_element_type=F32)
          + jnp.dot(mixb_ref[r, :], wout_ref[ka:, :], preferred_element_type=F32) for r in rows]
    x1s = [x_ref[r, :] + _rms(y, g1_ref[...]) for r, y in zip(rows, ys)]
    qs = [jnp.dot(_rms(x1, g2_ref[...]).astype(BF16), wq_ref[...], preferred_element_type=F32).astype(BF16)
          for x1 in x1s]
    cas = []
    for q in qs:
        outs = []
        for h in range(heads):
            cols = slice(h * HEAD_DIM, (h + 1) * HEAD_DIM)
            s = _dot_nt(q[:, cols], kv_ref[0, :, cols]) * scale
            m = jnp.max(s, axis=-1, keepdims=True)
            p = jnp.exp(s - m)
            l = jnp.sum(p, axis=-1, keepdims=True)
            v = kv_ref[0, :, ca_w + h * HEAD_DIM:ca_w + (h + 1) * HEAD_DIM]
            outs.append((jnp.dot(p.astype(BF16), v, preferred_element_type=F32) / l).astype(BF16))
        cas.append(jnp.concatenate(outs, axis=1))
    zs = [jnp.dot(ca, wo_ref[...], preferred_element_type=F32) for ca in cas]
    for r, x1, z in zip(rows, x1s, zs):
        o_ref[r, :] = x1 + _rms(z, g3_ref[...])


def _mix_out_cross_attn(mix_a, mix_b, x, w_out, w_q, w_o, layer, g1, g2, g3, kv, batch, seq, tm=512):
    t, d = x.shape
    ka, kb = mix_a.shape[1], mix_b.shape[1]
    assert ka + kb == w_out.shape[1]
    ca_w = CA_HEADS * HEAD_DIM
    mem_len = kv.shape[0] // batch
    spb = seq // tm
    kern = functools.partial(_mix_out_cross_attn_kernel, heads=CA_HEADS, scale=HEAD_DIM ** -0.5, halves=2)
    once = pl.Buffered(1)
    vec = lambda: pl.BlockSpec((1, d), lambda i: (0, 0))
    return pl.pallas_call(
        kern,
        grid=(t // tm,),
        in_specs=[pl.BlockSpec((tm, ka), lambda i: (i, 0)),
                  pl.BlockSpec((tm, kb), lambda i: (i, 0)),
                  pl.BlockSpec((tm, d), lambda i: (i, 0)),
                  pl.BlockSpec((None, d, d), lambda i: (layer, 0, 0), pipeline_mode=once),
                  vec(), vec(),
                  pl.BlockSpec((None, d, ca_w), lambda i: (layer, 0, 0), pipeline_mode=once),
                  pl.BlockSpec((1, mem_len, 2 * ca_w), lambda i: (i // spb, 0, 0)),
                  pl.BlockSpec((None, ca_w, d), lambda i: (layer, 0, 0), pipeline_mode=once),
                  vec()],
        out_specs=pl.BlockSpec((tm, d), lambda i: (i, 0)),
        out_shape=jax.ShapeDtypeStruct((t, d), F32),
        compiler_params=_params(("parallel",), 56),
        name="mix_out_cross_attn",
    )(mix_a, mix_b, x, w_out, g1.reshape(1, d), g2.reshape(1, d), w_q, kv.reshape(batch, mem_len, 2 * ca_w), w_o,
      g3.reshape(1, d))


HALO = 16


def _ffn_kernel(x_ref, xp_ref, xn_ref, g4_ref, wg_ref, wv_ref, cwg_ref, cwv_ref, cbg_ref, cbv_ref,
                wd_ref, g5_ref, o_ref, xe_ref, zg_ref, zv_ref, *, tm, tc, tiles_per_seq, nf):
    acc_ref = o_ref
    i = pl.program_id(0)
    f = pl.program_id(1)

    @pl.when(f == 0)
    def _():
        g4 = g4_ref[...]
        pos = i % tiles_per_seq
        prev = _rms(xp_ref[...], g4) * jnp.where(pos == 0, 0.0, 1.0)
        nxt = _rms(xn_ref[...], g4) * jnp.where(pos == tiles_per_seq - 1, 0.0, 1.0)
        xe_ref[0:HALO, :] = prev.astype(BF16)
        xe_ref[HALO:HALO + tm, :] = _rms(x_ref[...], g4).astype(BF16)
        xe_ref[HALO + tm:, :] = nxt.astype(BF16)
        acc_ref[...] = jnp.zeros(acc_ref.shape, F32)

    def conv(z_ref, c, cw, cb, lo, rows):
        lo = HALO + lo
        return (z_ref[c, lo - 1:lo - 1 + rows, :] * cw[0:1] + z_ref[c, lo:lo + rows, :] * cw[1:2]
                + z_ref[c, lo + 1:lo + 1 + rows, :] * cw[2:3] + cb)

    xe = xe_ref[...]
    n_sub = zg_ref.shape[0]
    for c in range(n_sub):
        cols = slice(c * tc, (c + 1) * tc)
        zg_ref[c] = jnp.dot(xe, wg_ref[:, cols], preferred_element_type=F32)
        zv_ref[c] = jnp.dot(xe, wv_ref[:, cols], preferred_element_type=F32)
    for c in range(n_sub):
        cols = slice(c * tc, (c + 1) * tc)
        pieces = 2 if c == n_sub - 1 else 1
        rows = tm // pieces
        for r in range(pieces):
            lo = r * rows
            h = (jax.nn.gelu(conv(zg_ref, c, cwg_ref[:, cols], cbg_ref[:, cols], lo, rows))
                 * conv(zv_ref, c, cwv_ref[:, cols], cbv_ref[:, cols], lo, rows))
            acc_ref[lo:lo + rows, :] += jnp.dot(h.astype(BF16), wd_ref[cols, :], preferred_element_type=F32)

    @pl.when(f == nf - 1)
    def _():
        o_ref[...] = x_ref[...] + _rms(acc_ref[...], g5_ref[...])


def _conv_ffn(x, seq, g4, w_up, conv_w, conv_b, w_down, layer, g5, tm=1024, tf=512, tc=256):
    t, d = x.shape
    d_ff = w_down.shape[1]
    nf = d_ff // tf
    hb = tm // HALO
    last_hb = t // HALO - 1
    kern = functools.partial(_ffn_kernel, tm=tm, tc=tc, tiles_per_seq=seq // tm, nf=nf)
    cb = conv_b.reshape(1, 2 * d_ff)
    return pl.pallas_call(
        kern,
        grid=(t // tm, nf),
        in_specs=[pl.BlockSpec((tm, d), lambda i, f: (i, 0), pipeline_mode=pl.Buffered(1)),
                  pl.BlockSpec((HALO, d), lambda i, f: (jnp.maximum(i * hb - 1, 0), 0)),
                  pl.BlockSpec((HALO, d), lambda i, f: (jnp.minimum((i + 1) * hb, last_hb), 0)),
                  pl.BlockSpec((1, d), lambda i, f: (0, 0)),
                  pl.BlockSpec((None, d, tf), lambda i, f: (layer, 0, f)),
                  pl.BlockSpec((None, d, tf), lambda i, f: (layer, 0, nf + f)),
                  pl.BlockSpec((CONV_W, tf), lambda i, f: (0, f)),
                  pl.BlockSpec((CONV_W, tf), lambda i, f: (0, nf + f)),
                  pl.BlockSpec((1, tf), lambda i, f: (0, f)),
                  pl.BlockSpec((1, tf), lambda i, f: (0, nf + f)),
                  pl.BlockSpec((None, tf, d), lambda i, f: (layer, f, 0)),
                  pl.BlockSpec((1, d), lambda i, f: (0, 0))],
        out_specs=pl.BlockSpec((tm, d), lambda i, f: (i, 0)),
        out_shape=jax.ShapeDtypeStruct((t, d), F32),
        scratch_shapes=[pltpu.VMEM((tm + 2 * HALO, d), BF16),
                        pltpu.VMEM((tf // tc, tm + 2 * HALO, tc), F32),
                        pltpu.VMEM((tf // tc, tm + 2 * HALO, tc), F32)],
        compiler_params=_params(("parallel", "arbitrary"), 58),
        name="conv_ffn",
    )(x, x, x, g4.reshape(1, d), w_up, w_up, conv_w, conv_w, cb, cb, w_down, g5.reshape(1, d))


def _trunk(x3, mem3, p):
    batch, seq, d = x3.shape
    x = x3.reshape(batch * seq, d)
    mem = mem3.reshape(-1, d)
    depth = p["w_in"].shape[0]
    a_w = A_HEADS * HEAD_DIM
    c_w = C_HEADS * 2 * C_HALF
    for layer in range(depth):
        g = p["norm_gains"][layer]
        proj = _norm_matmul(x, g[0], p["w_in"], layer, BF16)
        if layer % 2 == 0:
            e = layer // 2
            qkvs = _even_prep(proj, p["rope"][HEAD_DIM], batch, seq, HEAD_DIM ** -0.5)
            seqs = [qkv.reshape(batch * dil, seq // dil, 3 * a_w) for (_, dil), qkv in zip(DILATED_CFGS, qkvs)]
            dilated = [_band_attention(s) for s in seqs[1:]]
            ya = _band_attention(seqs[0], dilated).reshape(batch * seq, a_w)
            yb = _spatial_gating(proj, p["b_w_spatial"][e], p["b_b_spatial"][e], p["b_ln_gain"][e],
                                 3 * A_HEADS, 3 * A_HEADS + B_GROUPS)
            mix = (ya, yb)
        else:
            o = layer // 2
            qt, kk, vt = _diff_prep(proj, p["rope"][C_HALF], batch, seq, C_HALF ** -0.5 * math.log2(math.e),
                                    DIFF_BK)
            yc = _diff_attention(qt, kk, vt, p["c_lambda"][o], p["c_subln_gain"][o], batch, seq, layer,
                                 bk=DIFF_BK)
            m_c = _fourier_weights(p["d_w_fourier"][o], seq)
            yd = _seq_dft_real(proj[:, 3 * c_w:], m_c, batch, seq)
            mix = (yc, yd)
        kv = _norm_matmul(mem, p["mem_norm_gain"][layer], p["ca_w_kv"], layer, BF16)
        x = _mix_out_cross_attn(*mix, x, p["w_out"], p["ca_w_q"], p["ca_w_o"], layer, g[1], g[2], g[3], kv,
                                batch, seq)
        x = _conv_ffn(x, seq, g[4], p["ffn_w_up"], p["ffn_conv_w"][layer], p["ffn_conv_b"][layer],
                      p["ffn_w_down"], layer, g[5])
    return x.reshape(batch, seq, d)


def kernel(x_prompt, x_sample, mem_prompt, mem_sample, norm_gains, w_in, w_out, b_w_spatial, b_b_spatial,
           b_ln_gain, c_lambda, c_subln_gain, d_w_fourier, mem_norm_gain, ca_w_q, ca_w_kv, ca_w_o, ffn_w_up,
           ffn_conv_w, ffn_conv_b, ffn_w_down):
    p = dict(norm_gains=norm_gains, w_in=w_in.astype(BF16), w_out=w_out.astype(BF16), b_w_spatial=b_w_spatial,
             b_b_spatial=b_b_spatial, b_ln_gain=b_ln_gain, c_lambda=c_lambda, c_subln_gain=c_subln_gain,
             d_w_fourier=d_w_fourier, mem_norm_gain=mem_norm_gain, ca_w_q=ca_w_q.astype(BF16),
             ca_w_kv=ca_w_kv.astype(BF16), ca_w_o=ca_w_o.astype(BF16), ffn_w_up=ffn_w_up.astype(BF16),
             ffn_conv_w=ffn_conv_w, ffn_conv_b=ffn_conv_b, ffn_w_down=ffn_w_down.astype(BF16))
    seq_max = max(x_prompt.shape[1], x_sample.shape[1])
    p["rope"] = {w: _rope_tables(seq_max, w) for w in (HEAD_DIM, C_HALF)}
    return _trunk(x_prompt, mem_prompt, p), _trunk(x_sample, mem_sample, p)
```
